```python
import math
import jax, jax.numpy as jnp
from jax import lax
import numpy as np

D_MODEL = 1024
BATCH = 8
SEQ = 2048
DEPTH = 2
DEC_BATCH = 128
DEC_SEQ = 8
PAST_LEN = 16384
PAGE_SIZE = 128

N_AB_LAYERS = (DEPTH + 1) // 2
N_CD_LAYERS = DEPTH // 2
PLE_DIM = 256
D_FF = 4 * D_MODEL
CHUNK = 64
EPS = 1e-6
NEG = -1e30
A_HEADS = 4
A_DH = D_MODEL // (2 * A_HEADS)
A_W = A_HEADS * A_DH
CONV_W = 4
B_HEADS = 4
B_DH = D_MODEL // (2 * B_HEADS)
B_W = B_HEADS * B_DH
ROPE_BASE = 10000.0
C_HEADS = 4
C_DK = 128
C_DV = D_MODEL // (2 * C_HEADS)
C_KW = C_HEADS * C_DK
C_W = C_HEADS * C_DV
D_W = D_MODEL - C_W
D_GROUP = 16
D_GROUPS = D_W // D_GROUP
D_STATE = 64
AB_COLS = 4 * A_W + 2 * A_HEADS + 4 * B_W
CD_COLS = 2 * C_KW + 2 * C_W + D_W

kernel_name = 'hybrid_mlstm_retnet_hgrn2_s5_step'


def rmsnorm(x, g):
    xf = x.astype(jnp.float32)
    y = xf * lax.rsqrt(jnp.mean(xf * xf, axis=-1, keepdims=True) + EPS)
    return (y * g.astype(jnp.float32)).astype(x.dtype)


def head_rmsnorm(x):
    return x * lax.rsqrt(jnp.mean(x * x, axis=-1, keepdims=True) + EPS)


def split_cols(z, sizes):
    out, start = [], 0
    for s in sizes:
        out.append(z[..., start:start + s])
        start += s
    return out


def to_heads(x, h):
    b, l, _ = x.shape
    return x.reshape(b, l, h, -1).transpose(0, 2, 1, 3)


def from_heads(x):
    b, h, l, d = x.shape
    return x.transpose(0, 2, 1, 3).reshape(b, l, h * d)


def chunk_len(length):
    return CHUNK if length % CHUNK == 0 else length


def causal_mask(c):
    return jnp.tril(jnp.ones((c, c), dtype=bool))


def to_chunks(a, c):
    b, h, l = a.shape[:3]
    return jnp.moveaxis(a.reshape((b, h, l // c, c) + a.shape[3:]), 2, 0)


def from_chunks(a):
    a = jnp.moveaxis(a, 0, 2)
    b, h, nc, c = a.shape[:4]
    return a.reshape((b, h, nc * c) + a.shape[4:])


def rope(x, pos):
    half = x.shape[-1] // 2
    inv = ROPE_BASE ** (-jnp.arange(half, dtype=jnp.float32) / half)
    ang = pos.astype(jnp.float32)[:, None] * inv[None, :]
    cos, sin = jnp.cos(ang), jnp.sin(ang)
    x1, x2 = x[..., :half], x[..., half:]
    return jnp.concatenate([x1 * cos - x2 * sin, x1 * sin + x2 * cos], axis=-1)


def mlstm_chunkwise(q, k, v, itil, logf, C0, n0, m0):
    c = chunk_len(q.shape[2])
    mask = causal_mask(c)

    def step(carry, xs):
        C, n, m = carry
        qc, kc, vc, ic, fc = xs
        b = jnp.cumsum(fc, axis=-1)
        dmat = jnp.where(mask, b[..., :, None] - b[..., None, :] + ic[..., None, :], NEG)
        inter = b + m[..., None]
        m_t = jnp.maximum(inter, jnp.max(dmat, axis=-1))
        w_intra = jnp.exp(dmat - m_t[..., None])
        w_inter = jnp.exp(inter - m_t)
        s = jnp.einsum('bhtd,bhsd->bhts', qc, kc) * w_intra
        num = (w_inter[..., None] * jnp.einsum('bhtd,bhde->bhte', qc, C)
               + jnp.einsum('bhts,bhse->bhte', s, vc))
        den = w_inter * jnp.einsum('bhtd,bhd->bht', qc, n) + jnp.sum(s, axis=-1)
        h = num / jnp.maximum(jnp.abs(den), jnp.exp(-m_t))[..., None]
        m_new = m_t[..., -1]
        w_last = w_intra[..., -1, :]
        decay = jnp.exp(inter[..., -1] - m_new)
        C_new = decay[..., None, None] * C + jnp.einsum('bhs,bhsd,bhse->bhde', w_last, kc, vc)
        n_new = decay[..., None] * n + jnp.einsum('bhs,bhsd->bhd', w_last, kc)
        return (C_new, n_new, m_new), h

    xs = (to_chunks(q, c), to_chunks(k, c), to_chunks(v, c), to_chunks(itil, c), to_chunks(logf, c))
    (C, n, m), h = lax.scan(step, (C0, n0, m0), xs)
    return from_chunks(h), C, n, m


def retention_chunkwise(q, k, v, log_gamma, S0):
    c = chunk_len(q.shape[2])
    j = jnp.arange(c, dtype=jnp.float32)
    rel = jnp.maximum(j[:, None] - j[None, :], 0.0)
    decay = jnp.where(causal_mask(c), jnp.exp(rel[None] * log_gamma[:, None, None]), 0.0)
    inter = jnp.exp((j + 1.0)[None, :] * log_gamma[:, None])[:, :, None]
    kdecay = jnp.exp((c - 1.0 - j)[None, :] * log_gamma[:, None])[:, :, None]
    cdecay = jnp.exp(c * log_gamma)[:, None, None]

    def step(S, xs):
        qc, kc, vc = xs
        o = (jnp.einsum('bhtd,bhde->bhte', qc, S) * inter
             + jnp.einsum('bhts,bhse->bhte', jnp.einsum('bhtd,bhsd->bhts', qc, kc) * decay, vc))
        S = cdecay * S + jnp.einsum('bhsd,bhse->bhde', kc * kdecay, vc)
        return S, o

    S, o = lax.scan(step, S0, (to_chunks(q, c), to_chunks(k, c), to_chunks(v, c)))
    return from_chunks(o), S


def gla_chunkwise(q, k, v, logf, S0):
    c = chunk_len(q.shape[2])
    mask = causal_mask(c)[:, :, None]

    def step(S, xs):
        qc, kc, vc, fc = xs
        b = jnp.cumsum(fc, axis=2)
        diff = jnp.where(mask, b[:, :, :, None, :] - b[:, :, None, :, :], NEG)
        attn = jnp.einsum('bhtk,bhtsk,bhsk->bhts', qc, jnp.exp(diff), kc)
        o = (jnp.einsum('bhtk,bhkv->bhtv', qc * jnp.exp(b), S)
             + jnp.einsum('bhts,bhsv->bhtv', attn, vc))
        b_last = b[:, :, -1]
        S = (jnp.exp(b_last)[..., None] * S
             + jnp.einsum('bhsk,bhsv->bhkv', kc * jnp.exp(b_last[:, :, None] - b), vc))
        return S, o

    xs = (to_chunks(q, c), to_chunks(k, c), to_chunks(v, c), to_chunks(logf, c))
    S, o = lax.scan(step, S0, xs)
    return from_chunks(o), S


def s5_scan(u, A_re, A_im, log_dt, B_re, B_im, C_re, C_im, x0_re, x0_im):
    bsz, length = u.shape[:2]
    dt = jnp.exp(log_dt)[:, None]
    mag = jnp.exp(dt * A_re)
    ar, ai = mag * jnp.cos(dt * A_im), mag * jnp.sin(dt * A_im)
    den = A_re * A_re + A_im * A_im
    nr, ni = ar - 1.0, ai
    zr = (nr * A_re + ni * A_im) / den
    zi = (ni * A_re - nr * A_im) / den
    bbr = zr[..., None] * B_re - zi[..., None] * B_im
    bbi = zr[..., None] * B_im + zi[..., None] * B_re
    bu_r = jnp.einsum('gph,blgh->blgp', bbr, u)
    bu_i = jnp.einsum('gph,blgh->blgp', bbi, u)
    a_r = jnp.broadcast_to(ar[None, None], (1, length) + ar.shape)
    a_i = jnp.broadcast_to(ai[None, None], (1, length) + ai.shape)

    def combine(e1, e2):
        a1r, a1i, b1r, b1i = e1
        a2r, a2i, b2r, b2i = e2
        return (a1r * a2r - a1i * a2i, a1r * a2i + a1i * a2r,
                a2r * b1r - a2i * b1i + b2r, a2r * b1i + a2i * b1r + b2i)

    pr, pi, sr, si = lax.associative_scan(combine, (a_r, a_i, bu_r, bu_i), axis=1)
    xr = sr + pr * x0_re[:, None] - pi * x0_im[:, None]
    xi = si + pr * x0_im[:, None] + pi * x0_re[:, None]
    y = jnp.einsum('ghp,blgp->blgh', C_re, xr) - jnp.einsum('ghp,blgp->blgh', C_im, xi)
    return y.reshape(bsz, length, -1), xr[:, -1], xi[:, -1]


def ab_mixer(h, pos, conv_buf, C0, n0, m0, S0, w_in, b_gate, conv_w, conv_b, gn_a, w_out):
    f32 = jnp.float32
    length = h.shape[1]
    z = jnp.matmul(h, w_in).astype(f32)
    mq, mk, mv, mo, mi, mf, rq, rk, rv, rg = split_cols(
        z, (A_W, A_W, A_W, A_W, A_HEADS, A_HEADS, B_W, B_W, B_W, B_W))
    qk_ext = jnp.concatenate([conv_buf.astype(f32), jnp.concatenate([mq, mk], axis=-1)], axis=1)
    cw = conv_w.astype(f32)
    conv = conv_b.astype(f32)
    for j in range(CONV_W):
        conv = conv + cw[j] * qk_ext[:, j:j + length]
    qk = jax.nn.silu(conv)
    q = to_heads(qk[..., :A_W], A_HEADS)
    k = to_heads(qk[..., A_W:], A_HEADS) * (A_DH ** -0.5)
    v = to_heads(mv, A_HEADS)
    bg = b_gate.astype(f32)
    itil = jnp.transpose(mi + bg[:A_HEADS], (0, 2, 1))
    logf = jnp.transpose(jax.nn.log_sigmoid(mf + bg[A_HEADS:]), (0, 2, 1))
    hm, C, n, m = mlstm_chunkwise(q, k, v, itil, logf, C0.astype(f32), n0.astype(f32), m0.astype(f32))
    hm = jax.nn.sigmoid(to_heads(mo, A_HEADS)) * hm
    hm = from_heads(head_rmsnorm(hm) * gn_a.astype(f32).reshape(A_HEADS, 1, A_DH))
    log_gamma = jnp.log1p(-jnp.exp2(-5.0 - jnp.arange(B_HEADS, dtype=f32)))
    qr = rope(to_heads(rq, B_HEADS), pos)
    kr = rope(to_heads(rk, B_HEADS), pos) * (B_DH ** -0.5)
    hr, S = retention_chunkwise(qr, kr, to_heads(rv, B_HEADS), log_gamma, S0.astype(f32))
    hr = from_heads(head_rmsnorm(hr)) * jax.nn.silu(rg)
    out = jnp.matmul(jnp.concatenate([hm, hr], axis=-1).astype(h.dtype), w_out)
    return out, qk_ext[:, -(CONV_W - 1):], C, n, m, S


def cd_mixer(h, lb, S0, x0_re, x0_im, w_in, gn_c, A_re, A_im, log_dt, B_re, B_im, C_re, C_im,
             D_skip, w_glu, b_glu, w_out):
    f32 = jnp.float32
    bsz, length, _ = h.shape
    z = jnp.matmul(h, w_in).astype(f32)
    hq, hf, hi, hg, su = split_cols(z, (C_KW, C_KW, C_W, C_W, D_W))
    logf = jnp.logaddexp(jnp.log(lb), jnp.log1p(-lb) + jax.nn.log_sigmoid(hf))
    q = to_heads(hq, C_HEADS) * (C_DK ** -0.5)
    k = to_heads(-jnp.expm1(logf), C_HEADS)
    o, S = gla_chunkwise(q, k, to_heads(hi, C_HEADS), to_heads(logf, C_HEADS), S0.astype(f32))
    o = from_heads(head_rmsnorm(o) * gn_c.astype(f32).reshape(C_HEADS, 1, C_DV)) * jax.nn.silu(hg)
    u = su.reshape(bsz, length, D_GROUPS, D_GROUP)
    y, xr, xi = s5_scan(u, A_re.astype(f32), A_im.astype(f32), log_dt.astype(f32),
                        B_re.astype(f32), B_im.astype(f32), C_re.astype(f32), C_im.astype(f32),
                        x0_re.astype(f32), x0_im.astype(f32))
    y = y + D_skip.astype(f32) * su
    a = jax.nn.gelu(y)
    s = a * jax.nn.sigmoid(jnp.matmul(a, w_glu.astype(f32)) + b_glu.astype(f32))
    out = jnp.matmul(jnp.concatenate([o, s], axis=-1).astype(h.dtype), w_out)
    return out, S, xr, xi


def trunk(x, p, pos, ab_state, cd_state, prm):
    conv_s, C_s, n_s, m_s, ret_s = ab_state
    hgrn_s, s5r_s, s5i_s = cd_state
    lb_all = jnp.cumsum(jax.nn.softmax(prm['lb_logits'].astype(jnp.float32), axis=0), axis=0)
    lb_all = lb_all - lb_all[0]
    new_ab = ([], [], [], [], [])
    new_cd = ([], [], [])
    h = x
    for i in range(DEPTH):
        j = i // 2
        hn = rmsnorm(h, prm['norm_mix'][i])
        if i % 2 == 0:
            out, *st = ab_mixer(hn, pos, conv_s[j], C_s[j], n_s[j], m_s[j], ret_s[j],
                                prm['w_in_ab'][j], prm['b_gate_ab'][j], prm['conv_w_ab'][j],
                                prm['conv_b_ab'][j], prm['gn_a'][j], prm['w_out_ab'][j])
            for lst, s in zip(new_ab, st):
                lst.append(s)
        else:
            out, *st = cd_mixer(hn, lb_all[i], hgrn_s[j], s5r_s[j], s5i_s[j],
                                prm['w_in_cd'][j], prm['gn_c'][j], prm['s5_A_re'][j], prm['s5_A_im'][j],
                                prm['s5_log_dt'][j], prm['s5_B_re'][j], prm['s5_B_im'][j],
                                prm['s5_C_re'][j], prm['s5_C_im'][j], prm['s5_D'][j],
                                prm['w_glu'][j], prm['b_glu'][j], prm['w_out_cd'][j])
            for lst, s in zip(new_cd, st):
                lst.append(s)
        h = h + out
        hn = rmsnorm(h, prm['norm_ff'][i])
        h = h + jnp.matmul(jnp.square(jax.nn.relu(jnp.matmul(hn, prm['w_ff1'][i]))), prm['w_ff2'][i])
        gate = jax.nn.sigmoid(jnp.matmul(rmsnorm(h, prm['norm_ple'][i]), prm['w_ple_gate'][i]))
        h = h + gate * jnp.matmul(p[i], prm['w_ple_proj'][i])
    y = rmsnorm(h, prm['norm_final'])
    return y, [jnp.stack(l) for l in new_ab], [jnp.stack(l) for l in new_cd]


def setup_inputs(seed: int = 0) -> dict:
    key = jax.random.key(seed)
    keys = jax.random.split(key, 48)
    counter = [0]

    def nk():
        kk = keys[counter[0]]
        counter[0] += 1
        return kk

    def nrm(shape, scale):
        return scale * jax.random.normal(nk(), shape, jnp.float32)

    f32 = jnp.float32
    d = {}
    d['x_prompt'] = nrm((BATCH, SEQ, D_MODEL), 1.0)
    d['x_sample'] = nrm((DEC_BATCH, DEC_SEQ, D_MODEL), 1.0)
    d['state_mlstm_conv'] = nrm((N_AB_LAYERS, DEC_BATCH, CONV_W - 1, 2 * A_W), 1.0)
    d['state_mlstm_C'] = nrm((N_AB_LAYERS, DEC_BATCH, A_HEADS, A_DH, A_DH), 0.1)
    d['state_mlstm_n'] = nrm((N_AB_LAYERS, DEC_BATCH, A_HEADS, A_DH), 0.1)
    d['state_mlstm_m'] = nrm((N_AB_LAYERS, DEC_BATCH, A_HEADS), 1.0)
    d['state_ret'] = nrm((N_AB_LAYERS, DEC_BATCH, B_HEADS, B_DH, B_DH), 0.3)
    d['state_hgrn'] = nrm((N_CD_LAYERS, DEC_BATCH, C_HEADS, C_DK, C_DV), 0.3)
    d['state_s5_re'] = nrm((N_CD_LAYERS, DEC_BATCH, D_GROUPS, D_STATE), 0.1)
    d['state_s5_im'] = nrm((N_CD_LAYERS, DEC_BATCH, D_GROUPS, D_STATE), 0.1)
    d['p_prompt'] = nrm((DEPTH, BATCH, SEQ, PLE_DIM), 1.0)
    d['p_sample'] = nrm((DEPTH, DEC_BATCH, DEC_SEQ, PLE_DIM), 1.0)
    d['norm_mix'] = 1.0 + nrm((DEPTH, D_MODEL), 0.02)
    d['norm_ff'] = 1.0 + nrm((DEPTH, D_MODEL), 0.02)
    d['norm_ple'] = 1.0 + nrm((DEPTH, D_MODEL), 0.02)
    d['norm_final'] = 1.0 + nrm((D_MODEL,), 0.02)
    d['w_in_ab'] = nrm((N_AB_LAYERS, D_MODEL, AB_COLS), D_MODEL ** -0.5)
    f_bias = jnp.linspace(3.0, 6.0, A_HEADS, dtype=f32)
    d['b_gate_ab'] = jnp.concatenate([nrm((N_AB_LAYERS, A_HEADS), 0.1),
                                      f_bias + nrm((N_AB_LAYERS, A_HEADS), 0.1)], axis=-1)
    d['conv_w_ab'] = nrm((N_AB_LAYERS, CONV_W, 2 * A_W), CONV_W ** -0.5)
    d['conv_b_ab'] = nrm((N_AB_LAYERS, 2 * A_W), 0.02)
    d['gn_a'] = 1.0 + nrm((N_AB_LAYERS, A_W), 0.02)
    d['w_out_ab'] = nrm((N_AB_LAYERS, D_MODEL, D_MODEL), 0.5 * D_MODEL ** -0.5)
    d['w_in_cd'] = nrm((N_CD_LAYERS, D_MODEL, CD_COLS), D_MODEL ** -0.5)
    d['lb_logits'] = nrm((DEPTH, C_KW), 0.1)
    d['gn_c'] = 1.0 + nrm((N_CD_LAYERS, C_W), 0.02)
    d['s5_A_re'] = -0.5 + nrm((N_CD_LAYERS, D_GROUPS, D_STATE), 0.01)
    d['s5_A_im'] = math.pi * jnp.arange(D_STATE, dtype=f32) + nrm((N_CD_LAYERS, D_GROUPS, D_STATE), 0.01)
    d['s5_log_dt'] = jax.random.uniform(nk(), (N_CD_LAYERS, D_GROUPS), f32, math.log(1e-3), math.log(1e-1))
    d['s5_B_re'] = nrm((N_CD_LAYERS, D_GROUPS, D_STATE, D_GROUP), (2 * D_GROUP) ** -0.5)
    d['s5_B_im'] = nrm((N_CD_LAYERS, D_GROUPS, D_STATE, D_GROUP), (2 * D_GROUP) ** -0.5)
    d['s5_C_re'] = nrm((N_CD_LAYERS, D_GROUPS, D_GROUP, D_STATE), (2 * D_STATE) ** -0.5)
    d['s5_C_im'] = nrm((N_CD_LAYERS, D_GROUPS, D_GROUP, D_STATE), (2 * D_STATE) ** -0.5)
    d['s5_D'] = nrm((N_CD_LAYERS, D_W), 1.0)
    d['w_glu'] = nrm((N_CD_LAYERS, D_W, D_W), D_W ** -0.5)
    d['b_glu'] = nrm((N_CD_LAYERS, D_W), 0.02)
    d['w_out_cd'] = nrm((N_CD_LAYERS, D_MODEL, D_MODEL), 0.5 * D_MODEL ** -0.5)
    d['w_ff1'] = nrm((DEPTH, D_MODEL, D_FF), D_MODEL ** -0.5)
    d['w_ff2'] = nrm((DEPTH, D_FF, D_MODEL), 0.5 * D_FF ** -0.5)
    d['w_ple_proj'] = nrm((DEPTH, PLE_DIM, D_MODEL), PLE_DIM ** -0.5)
    d['w_ple_gate'] = nrm((DEPTH, D_MODEL, D_MODEL), D_MODEL ** -0.5)
    return d


def reference(x_prompt, x_sample, state_mlstm_conv, state_mlstm_C, state_mlstm_n, state_mlstm_m,
              state_ret, state_hgrn, state_s5_re, state_s5_im, p_prompt, p_sample,
              norm_mix, norm_ff, norm_ple, norm_final, w_in_ab, b_gate_ab, conv_w_ab, conv_b_ab, gn_a,
              w_out_ab, w_in_cd, lb_logits, gn_c, s5_A_re, s5_A_im, s5_log_dt, s5_B_re, s5_B_im,
              s5_C_re, s5_C_im, s5_D, w_glu, b_glu, w_out_cd, w_ff1, w_ff2, w_ple_proj, w_ple_gate):
    f32 = jnp.float32
    prm = dict(norm_mix=norm_mix, norm_ff=norm_ff, norm_ple=norm_ple, norm_final=norm_final,
               w_in_ab=w_in_ab, b_gate_ab=b_gate_ab, conv_w_ab=conv_w_ab, conv_b_ab=conv_b_ab,
               gn_a=gn_a, w_out_ab=w_out_ab, w_in_cd=w_in_cd, lb_logits=lb_logits, gn_c=gn_c,
               s5_A_re=s5_A_re, s5_A_im=s5_A_im, s5_log_dt=s5_log_dt, s5_B_re=s5_B_re,
               s5_B_im=s5_B_im, s5_C_re=s5_C_re, s5_C_im=s5_C_im, s5_D=s5_D, w_glu=w_glu,
               b_glu=b_glu, w_out_cd=w_out_cd, w_ff1=w_ff1, w_ff2=w_ff2,
               w_ple_proj=w_ple_proj, w_ple_gate=w_ple_gate)
    bp, lp, _ = x_prompt.shape
    zero_ab = (jnp.zeros((N_AB_LAYERS, bp, CONV_W - 1, 2 * A_W), f32),
               jnp.zeros((N_AB_LAYERS, bp, A_HEADS, A_DH, A_DH), f32),
               jnp.zeros((N_AB_LAYERS, bp, A_HEADS, A_DH), f32),
               jnp.zeros((N_AB_LAYERS, bp, A_HEADS), f32),
               jnp.zeros((N_AB_LAYERS, bp, B_HEADS, B_DH, B_DH), f32))
    zero_cd = (jnp.zeros((N_CD_LAYERS, bp, C_HEADS, C_DK, C_DV), f32),
               jnp.zeros((N_CD_LAYERS, bp, D_GROUPS, D_STATE), f32),
               jnp.zeros((N_CD_LAYERS, bp, D_GROUPS, D_STATE), f32))
    y_prompt, ab_p, cd_p = trunk(x_prompt, p_prompt, jnp.arange(lp), zero_ab, zero_cd, prm)
    ls = x_sample.shape[1]
    y_sample, ab_s, cd_s = trunk(
        x_sample, p_sample, PAST_LEN + jnp.arange(ls),
        (state_mlstm_conv, state_mlstm_C, state_mlstm_n, state_mlstm_m, state_ret),
        (state_hgrn, state_s5_re, state_s5_im), prm)
    return (y_prompt, y_sample,
            ab_p[0], ab_s[0], ab_p[1], ab_s[1], ab_p[2], ab_s[2], ab_p[3], ab_s[3], ab_p[4], ab_s[4],
            cd_p[0], cd_s[0], cd_p[1], cd_s[1], cd_p[2], cd_s[2])
```

```python
import functools
import math

import jax
import jax.numpy as jnp
from jax import lax
from jax.experimental import pallas as pl
from jax.experimental.pallas import tpu as pltpu

F32 = jnp.float32
BF16 = jnp.bfloat16

EPS = 1e-6
NEG = -1e30
ROPE_BASE = 10000.0
PAST_LEN = 16384
CHUNK = 64
HEADS = 4
DH = 128
HW = HEADS * DH
CONV_W = 4
S5_GROUP = 16
S5_STATE = 64
SUBLANES = 8
LANES = 128
MXU_DIM = 256
VMEM_LIMIT = 56 * 1024 * 1024


def _params(n_axes, vmem=None):
    return pltpu.CompilerParams(dimension_semantics=("arbitrary",) * n_axes, vmem_limit_bytes=vmem)


def _full(shape):
    return pl.BlockSpec(shape, lambda *_: (0,) * len(shape))


def _bdot(a, b):
    return jnp.dot(a.astype(BF16), b.astype(BF16), preferred_element_type=F32)


def _bdot_nt(a, b):
    return lax.dot_general(a.astype(BF16), b.astype(BF16), (((1,), (1,)), ((), ())),
                           preferred_element_type=F32)


def _bdot_tn(a, b):
    return lax.dot_general(a.astype(BF16), b.astype(BF16), (((0,), (0,)), ((), ())),
                           preferred_element_type=F32)


def _sigmoid(x):
    return 1.0 / (1.0 + jnp.exp(-x))


def _log_sigmoid(x):
    return jnp.minimum(x, 0.0) - jnp.log1p(jnp.exp(-jnp.abs(x)))


def _rms(x, g):
    return x * lax.rsqrt(jnp.mean(x * x, axis=-1, keepdims=True) + EPS) * g


def _head_rms(x):
    return x * lax.rsqrt(jnp.mean(x * x, axis=-1, keepdims=True) + EPS)


def _chunk_len(length):
    return CHUNK if length % CHUNK == 0 else length


def _inproj_kernel(x_ref, g_ref, w_ref, *out_refs):
    hn = _rms(x_ref[...], g_ref[...]).astype(BF16)
    off = 0
    for o_ref in out_refs:
        n = o_ref.shape[-1]
        for n0 in range(0, n, HW):
            nn = min(HW, n - n0)
            o_ref[:, n0:n0 + nn] = jnp.dot(hn, w_ref[:, off + n0:off + n0 + nn],
                                           preferred_element_type=F32)
        off += n


def _inproj(h, g, w, widths, tm, time_major_last=None):
    t, d = h.shape
    n = w.shape[1]
    assert sum(widths) == n and t % tm == 0
    out_shape = [jax.ShapeDtypeStruct((t, wd), F32) for wd in widths]
    out_specs = [pl.BlockSpec((tm, wd), lambda i: (i, 0)) for wd in widths]
    if time_major_last is not None:
        b, l = time_major_last
        assert l % tm == 0
        nl = l // tm
        wd = widths[-1]
        out_shape[-1] = jax.ShapeDtypeStruct((l, b * wd), F32)
        out_specs[-1] = pl.BlockSpec((tm, wd), lambda i: (i % nl, i // nl))
    return pl.pallas_call(
        _inproj_kernel,
        grid=(t // tm,),
        in_specs=[pl.BlockSpec((tm, d), lambda i: (i, 0)), _full((1, d)), _full((d, n))],
        out_specs=out_specs,
        out_shape=out_shape,
        compiler_params=_params(1, VMEM_LIMIT),
        name="inproj",
    )(h, g.reshape(1, d), w)


def _post_kernel(h_ref, ma_ref, mb_ref, p_ref, wo_ref, gff_ref, w1_ref, w2_ref, gple_ref, wg_ref,
                 wp_ref, gfin_ref, o_ref, *, final, ff_chunk):
    half = ma_ref.shape[-1]
    h = h_ref[...]
    h = h + (jnp.dot(ma_ref[...].astype(BF16), wo_ref[0:half, :], preferred_element_type=F32)
             + jnp.dot(mb_ref[...].astype(BF16), wo_ref[half:2 * half, :], preferred_element_type=F32))
    hn = _rms(h, gff_ref[...]).astype(BF16)
    d_ff = w1_ref.shape[1]
    acc = jnp.zeros_like(h)
    for f0 in range(0, d_ff, ff_chunk):
        a = jnp.dot(hn, w1_ref[:, f0:f0 + ff_chunk], preferred_element_type=F32)
        a = jnp.square(jnp.maximum(a, 0.0))
        acc = acc + jnp.dot(a.astype(BF16), w2_ref[f0:f0 + ff_chunk, :], preferred_element_type=F32)
    h = h + acc
    gate = _sigmoid(jnp.dot(_rms(h, gple_ref[...]).astype(BF16), wg_ref[...],
                            preferred_element_type=F32))
    h = h + gate * jnp.dot(p_ref[...].astype(BF16), wp_ref[...], preferred_element_type=F32)
    o_ref[...] = _rms(h, gfin_ref[...]) if final else h


def _post(h, mix_a, mix_b, p, wo, gff, w1, w2, gple, wg, wp, gfin, *, final, tm, b_time_major=None):
    t, d = h.shape
    half = mix_a.shape[-1]
    pd = p.shape[-1]
    d_ff = w1.shape[1]
    assert t % tm == 0
    row = lambda i: (i, 0)
    mb_spec = pl.BlockSpec((tm, half), row)
    if b_time_major is not None:
        _, l = b_time_major
        assert l % tm == 0
        nl = l // tm
        mb_spec = pl.BlockSpec((tm, half), lambda i: (i % nl, i // nl))
    return pl.pallas_call(
        functools.partial(_post_kernel, final=final, ff_chunk=1024),
        grid=(t // tm,),
        in_specs=[pl.BlockSpec((tm, d), row), pl.BlockSpec((tm, half), row), mb_spec,
                  pl.BlockSpec((tm, pd), row),
                  _full((d, d)), _full((1, d)), _full((d, d_ff)), _full((d_ff, d)), _full((1, d)),
                  _full((d, d)), _full((pd, d)), _full((1, d))],
        out_specs=pl.BlockSpec((tm, d), row),
        out_shape=jax.ShapeDtypeStruct((t, d), F32),
        compiler_params=_params(1, VMEM_LIMIT),
        name="post",
    )(h, mix_a, mix_b, p, wo, gff.reshape(1, d), w1, w2, gple.reshape(1, d), wg, wp,
      gfin.reshape(1, d))


def _mlstm_kernel(zq_ref, zk_ref, zv_ref, zo_ref, zg_ref, cw_ref, cb_ref, bg_ref, gn_ref,
                  conv0_ref, c0_ref, n0_ref, m0_ref,
                  out_ref, c_ref, n_ref, m_ref, ext_ref, *, c):
    j = pl.program_id(1)
    tail = CONV_W - 1

    @pl.when(j == 0)
    def _():
        c_ref[...] = c0_ref[...]
        n_ref[...] = n0_ref[...]
        m_ref[...] = m0_ref[...]
        ext_ref[0:SUBLANES, :] = jnp.zeros((SUBLANES, 2 * HW), F32)
        ext_ref[SUBLANES - tail:SUBLANES, :] = conv0_ref[...]

    ext_ref[SUBLANES:SUBLANES + c, 0:HW] = zq_ref[...]
    ext_ref[SUBLANES:SUBLANES + c, HW:2 * HW] = zk_ref[...]
    conv = cb_ref[...]
    for jj in range(CONV_W):
        r0 = SUBLANES - tail + jj
        conv = conv + cw_ref[jj:jj + 1, :] * ext_ref[r0:r0 + c, :]
    ext_ref[0:SUBLANES, :] = ext_ref[c:c + SUBLANES, :]
    qk = conv * _sigmoid(conv)

    ri = lax.broadcasted_iota(jnp.int32, (c, c), 0)
    ci = lax.broadcasted_iota(jnp.int32, (c, c), 1)
    eye = ri == ci
    tril = ri >= ci
    g = zg_ref[...]
    bg = bg_ref[...]
    for h in range(HEADS):
        sl = slice(h * DH, (h + 1) * DH)
        qh = qk[:, sl]
        kh = qk[:, HW + h * DH:HW + (h + 1) * DH] * (DH ** -0.5)
        vh = zv_ref[:, sl]
        i_col = g[:, h:h + 1] + bg[:, h:h + 1]
        f_col = _log_sigmoid(g[:, HEADS + h:HEADS + h + 1] + bg[:, HEADS + h:HEADS + h + 1])
        b_row = jnp.sum(jnp.where(ri <= ci, f_col, 0.0), axis=0, keepdims=True)
        b_col = jnp.sum(jnp.where(eye, b_row, 0.0), axis=1, keepdims=True)
        i_row = jnp.sum(jnp.where(eye, i_col, 0.0), axis=0, keepdims=True)
        m_prev = m_ref[h:h + 1, 0:1]
        dmat = jnp.where(tril, b_col - b_row + i_row, NEG)
        inter = b_col + m_prev
        m_t = jnp.maximum(inter, jnp.max(dmat, axis=1, keepdims=True))
        w_intra = jnp.exp(dmat - m_t)
        w_inter = jnp.exp(inter - m_t)
        c_h = c_ref[h]
        n_h = n_ref[h:h + 1, :]
        s = _bdot_nt(qh, kh) * w_intra
        num = w_inter * _bdot(qh, c_h) + _bdot(s, vh)
        den = (w_inter * jnp.sum(qh * n_h, axis=1, keepdims=True)
               + jnp.sum(s, axis=1, keepdims=True))
        hh = num / jnp.maximum(jnp.abs(den), jnp.exp(-m_t))
        m_new = m_t[c - 1:c, :]
        b_last = b_col[c - 1:c, :]
        w_last = jnp.exp(b_last - b_col + i_col - m_new)
        decay = jnp.exp(b_last + m_prev - m_new)
        kw = w_last * kh
        c_ref[h] = decay * c_h + _bdot_tn(kw, vh)
        n_ref[h:h + 1, :] = decay * n_h + jnp.sum(kw, axis=0, keepdims=True)
        m_ref[h:h + 1, :] = jnp.broadcast_to(m_new, (1, DH))
        hh = _sigmoid(zo_ref[:, sl]) * hh
        out_ref[:, sl] = _head_rms(hh) * gn_ref[:, sl]


def _mlstm(z_main, z_gate, conv_w, conv_b, b_gate, gn_a, conv0, c0, n0, m0, b, l):
    c = _chunk_len(l)
    nc = l // c
    t = b * l
    zspec = lambda col: pl.BlockSpec((c, HW), lambda bi, j: (bi * nc + j, col))
    st = lambda shape: pl.BlockSpec((None,) + shape, lambda bi, j: (bi,) + (0,) * len(shape))
    m0b = jnp.broadcast_to(m0[:, :, None], (b, HEADS, DH))
    out, c_new, n_new, m_new = pl.pallas_call(
        functools.partial(_mlstm_kernel, c=c),
        grid=(b, nc),
        in_specs=[zspec(0), zspec(1), zspec(2), zspec(3),
                  pl.BlockSpec((c, LANES), lambda bi, j: (bi * nc + j, 0)),
                  _full((CONV_W, 2 * HW)), _full((1, 2 * HW)), _full((1, 2 * HEADS)), _full((1, HW)),
                  st((CONV_W - 1, 2 * HW)), st((HEADS, DH, DH)), st((HEADS, DH)), st((HEADS, DH))],
        out_specs=[pl.BlockSpec((c, HW), lambda bi, j: (bi * nc + j, 0)),
                   st((HEADS, DH, DH)), st((HEADS, DH)), st((HEADS, DH))],
        out_shape=[jax.ShapeDtypeStruct((t, HW), F32),
                   jax.ShapeDtypeStruct((b, HEADS, DH, DH), F32),
                   jax.ShapeDtypeStruct((b, HEADS, DH), F32),
                   jax.ShapeDtypeStruct((b, HEADS, DH), F32)],
        scratch_shapes=[pltpu.VMEM((c + SUBLANES, 2 * HW), F32)],
        compiler_params=_params(2),
        name="mlstm",
    )(z_main, z_main, z_main, z_main, z_gate, conv_w, conv_b.reshape(1, -1), b_gate.reshape(1, -1),
      gn_a.reshape(1, -1), conv0, c0, n0, m0b)
    return out, c_new, n_new, m_new[:, :, 0]


def _rope_table_kernel(inv_ref, sign_ref, cos_ref, sin_ref, *, pos0, rows):
    i = pl.program_id(0)
    pos = (pos0 + i * rows + lax.broadcasted_iota(jnp.int32, (rows, LANES), 0)).astype(F32)
    ang = pos * inv_ref[...]
    cos_ref[...] = jnp.cos(ang)
    sin_ref[...] = jnp.sin(ang) * sign_ref[...]


def _rope_tables(l, pos0):
    half = DH // 2
    inv = ROPE_BASE ** (-jnp.arange(half, dtype=F32) / half)
    inv2 = jnp.concatenate([inv, inv]).reshape(1, DH)
    sign = jnp.concatenate([-jnp.ones((half,), F32), jnp.ones((half,), F32)]).reshape(1, DH)
    rows = min(l, 512)
    assert l % rows == 0
    return pl.pallas_call(
        functools.partial(_rope_table_kernel, pos0=pos0, rows=rows),
        grid=(l // rows,),
        in_specs=[_full((1, DH)), _full((1, DH))],
        out_specs=[pl.BlockSpec((rows, DH), lambda i: (i, 0))] * 2,
        out_shape=[jax.ShapeDtypeStruct((l, DH), F32)] * 2,
        compiler_params=_params(1),
        name="rope_table",
    )(inv2, sign)


def _ret_kernel(zq_ref, zk_ref, zv_ref, zg_ref, cos_ref, sin_ref, s0_ref, out_ref, s_ref, *, c):
    j = pl.program_id(1)

    @pl.when(j == 0)
    def _():
        s_ref[...] = s0_ref[...]

    cosf = cos_ref[...]
    sinf = sin_ref[...]
    ti = lax.broadcasted_iota(jnp.int32, (c, c), 0)
    si = lax.broadcasted_iota(jnp.int32, (c, c), 1)
    rel = jnp.maximum(ti - si, 0).astype(F32)
    tcol = lax.broadcasted_iota(jnp.int32, (c, 1), 0).astype(F32)

    def rope(x):
        return x * cosf + pltpu.roll(x, DH // 2, axis=1) * sinf

    for h in range(HEADS):
        sl = slice(h * DH, (h + 1) * DH)
        lg = math.log1p(-(2.0 ** (-5.0 - h)))
        decay = jnp.where(ti >= si, jnp.exp(rel * lg), 0.0)
        inter = jnp.exp((tcol + 1.0) * lg)
        kdecay = jnp.exp((c - 1.0 - tcol) * lg)
        cdecay = math.exp(c * lg)
        qr = rope(zq_ref[:, sl])
        kr = rope(zk_ref[:, sl]) * (DH ** -0.5)
        vh = zv_ref[:, sl]
        s_h = s_ref[h]
        o = _bdot(qr, s_h) * inter + _bdot(_bdot_nt(qr, kr) * decay, vh)
        s_ref[h] = cdecay * s_h + _bdot_tn(kr * kdecay, vh)
        gate = zg_ref[:, sl]
        out_ref[:, sl] = _head_rms(o) * (gate * _sigmoid(gate))


def _retention(z_main, cos_t, sin_t, s0, b, l):
    c = _chunk_len(l)
    nc = l // c
    t = b * l
    zspec = lambda col: pl.BlockSpec((c, HW), lambda bi, j: (bi * nc + j, col))
    st = pl.BlockSpec((None, HEADS, DH, DH), lambda bi, j: (bi, 0, 0, 0))
    tab = pl.BlockSpec((c, DH), lambda bi, j: (j, 0))
    return pl.pallas_call(
        functools.partial(_ret_kernel, c=c),
        grid=(b, nc),
        in_specs=[zspec(4), zspec(5), zspec(6), zspec(7), tab, tab, st],
        out_specs=[pl.BlockSpec((c, HW), lambda bi, j: (bi * nc + j, 0)), st],
        out_shape=[jax.ShapeDtypeStruct((t, HW), F32),
                   jax.ShapeDtypeStruct((b, HEADS, DH, DH), F32)],
        compiler_params=_params(2),
        name="retention",
    )(z_main, z_main, z_main, z_main, cos_t, sin_t, s0)


def _hgrn_kernel(zq_ref, zf_ref, zi_ref, zg_ref, lbl_ref, gn_ref, s0_ref, out_ref, s_ref,
                 *, c, sc, layer):
    j = pl.program_id(1)

    @pl.when(j == 0)
    def _():
        s_ref[...] = s0_ref[...]

    lbl = lbl_ref[...]
    e = jnp.exp(lbl - jnp.max(lbl, axis=0, keepdims=True))
    sm = e / jnp.sum(e, axis=0, keepdims=True)
    cum = sm[0:1, :]
    for r in range(1, layer + 1):
        cum = cum + sm[r:r + 1, :]
    lb = cum - sm[0:1, :]

    la = jnp.log(lb)
    zf = zf_ref[...]
    ls = _log_sigmoid(zf)
    l1m = jnp.log1p(-lb)
    lbb = l1m + ls
    logf = jnp.maximum(la, lbb) + jnp.log1p(jnp.exp(-jnp.abs(la - lbb)))
    kk = jnp.exp(l1m + ls - zf)

    ri = lax.broadcasted_iota(jnp.int32, (c, c), 0)
    ci = lax.broadcasted_iota(jnp.int32, (c, c), 1)
    tril = jnp.where(ri >= ci, 1.0, 0.0).astype(BF16)
    p0 = logf.astype(BF16)
    r1 = logf - p0.astype(F32)
    p1 = r1.astype(BF16)
    p2 = (r1 - p1.astype(F32)).astype(BF16)
    bcum = (jnp.dot(tril, p0, preferred_element_type=F32)
            + jnp.dot(tril, p1, preferred_element_type=F32)
            + jnp.dot(tril, p2, preferred_element_type=F32))

    rowi = lax.broadcasted_iota(jnp.int32, (sc, 1), 0)
    e_r = lax.broadcasted_iota(jnp.int32, (DH, DH), 0)
    e_c = lax.broadcasted_iota(jnp.int32, (DH, DH), 1)
    eye = e_r == e_c
    for h in range(HEADS):
        sl = slice(h * DH, (h + 1) * DH)
        bh = bcum[:, sl]
        qh = zq_ref[:, sl] * (DH ** -0.5)
        kh = kk[:, sl]
        vh = zi_ref[:, sl]
        s_h = s_ref[h]
        o_inter = _bdot(qh * jnp.exp(bh), s_h)
        blocks = []
        for blk in range(c // sc):
            r0 = blk * sc
            b_i = bh[r0:r0 + sc]
            q_i = qh[r0:r0 + sc]
            k_i = kh[r0:r0 + sc]
            v_i = vh[r0:r0 + sc]
            o_i = jnp.zeros((sc, DH), F32)
            if blk > 0:
                ref_row = bh[r0 - 1:r0, :]
                a_i = q_i * jnp.exp(b_i - ref_row)
                k_prev = kh[0:r0] * jnp.exp(ref_row - bh[0:r0])
                o_i = _bdot(_bdot_nt(a_i, k_prev), vh[0:r0])
            for s in range(sc):
                dec = jnp.exp(jnp.where(rowi >= s, b_i - b_i[s:s + 1, :], NEG))
                col = jnp.sum(q_i * dec * k_i[s:s + 1, :], axis=1, keepdims=True)
                o_i = o_i + col * v_i[s:s + 1, :]
            blocks.append(o_i)
        o = o_inter + (jnp.concatenate(blocks, axis=0) if len(blocks) > 1 else blocks[0])
        b_last = bh[c - 1:c, :]
        dec_col = jnp.sum(jnp.where(eye, jnp.exp(b_last), 0.0), axis=1, keepdims=True)
        s_ref[h] = dec_col * s_h + _bdot_tn(kh * jnp.exp(b_last - bh), vh)
        gate = zg_ref[:, sl]
        out_ref[:, sl] = _head_rms(o) * gn_ref[:, sl] * (gate * _sigmoid(gate))


def _hgrn(z_cd, lb_logits, gn_c, s0, b, l, layer):
    c = _chunk_len(l)
    sc = min(c, 16)
    nc = l // c
    t = b * l
    depth = lb_logits.shape[0]
    zspec = lambda col: pl.BlockSpec((c, HW), lambda bi, j: (bi * nc + j, col))
    st = pl.BlockSpec((None, HEADS, DH, DH), lambda bi, j: (bi, 0, 0, 0))
    return pl.pallas_call(
        functools.partial(_hgrn_kernel, c=c, sc=sc, layer=layer),
        grid=(b, nc),
        in_specs=[zspec(0), zspec(1), zspec(2), zspec(3), _full((depth, HW)), _full((1, HW)), st],
        out_specs=[pl.BlockSpec((c, HW), lambda bi, j: (bi * nc + j, 0)), st],
        out_shape=[jax.ShapeDtypeStruct((t, HW), F32),
                   jax.ShapeDtypeStruct((b, HEADS, DH, DH), F32)],
        compiler_params=_params(2),
        name="hgrn2",
    )(z_cd, z_cd, z_cd, z_cd, lb_logits, gn_c.reshape(1, -1), s0)


def _s5_kernel(u_ref, are_ref, aim_ref, ldt_ref, bre_ref, bim_ref, cre_ref, cim_ref, d_ref,
               wglu_ref, bglu_ref, x0r_ref, x0i_ref,
               s_ref, xr_ref, xi_ref, ar_sc, ai_sc, bbr_sc, bbi_sc, bur_sc, bui_sc, *, ct):
    j = pl.program_id(1)
    ns = are_ref.shape[-1]
    nu = u_ref.shape[-1]
    hs = ns // 2
    hu = nu // 2
    rows = ct * SUBLANES

    @pl.when(j == 0)
    def _():
        a_re = are_ref[...]
        a_im = aim_ref[...]
        dt = jnp.exp(ldt_ref[...])
        mag = jnp.exp(dt * a_re)
        ar = mag * jnp.cos(dt * a_im)
        ai = mag * jnp.sin(dt * a_im)
        ar_sc[...] = jnp.broadcast_to(ar, (SUBLANES, ns))
        ai_sc[...] = jnp.broadcast_to(ai, (SUBLANES, ns))
        den = a_re * a_re + a_im * a_im
        nr = ar - 1.0
        zr = (nr * a_re + ai * a_im) / den
        zi = (ai * a_re - nr * a_im) / den
        for hg in range(2):
            us = slice(hg * hu, (hg + 1) * hu)
            ss = slice(hg * hs, (hg + 1) * hs)
            bbr_sc[us, :] = (zr[:, ss] * bre_ref[us, :] - zi[:, ss] * bim_ref[us, :]).astype(BF16)
            bbi_sc[us, :] = (zr[:, ss] * bim_ref[us, :] + zi[:, ss] * bre_ref[us, :]).astype(BF16)
        xr_ref[...] = x0r_ref[...]
        xi_ref[...] = x0i_ref[...]

    u = u_ref[...].reshape(rows, nu)
    ub = u.astype(BF16)
    for hg in range(2):
        us = slice(hg * hu, (hg + 1) * hu)
        ss = slice(hg * hs, (hg + 1) * hs)
        bur_sc[:, ss] = jnp.dot(ub[:, us], bbr_sc[us, :], preferred_element_type=F32)
        bui_sc[:, ss] = jnp.dot(ub[:, us], bbi_sc[us, :], preferred_element_type=F32)

    lane_chunk = 4 * LANES
    for lc in range(ns // lane_chunk):
        ls = slice(lc * lane_chunk, (lc + 1) * lane_chunk)
        ar = ar_sc[:, ls]
        ai = ai_sc[:, ls]

        def step(t, carry):
            xr, xi = carry
            r0 = pl.multiple_of(t * SUBLANES, SUBLANES)
            nxr = ar * xr - ai * xi + bur_sc[pl.ds(r0, SUBLANES), ls]
            nxi = ar * xi + ai * xr + bui_sc[pl.ds(r0, SUBLANES), ls]
            bur_sc[pl.ds(r0, SUBLANES), ls] = nxr
            bui_sc[pl.ds(r0, SUBLANES), ls] = nxi
            return nxr, nxi

        xr, xi = lax.fori_loop(0, ct, step, (xr_ref[:, ls], xi_ref[:, ls]))
        xr_ref[:, ls] = xr
        xi_ref[:, ls] = xi

    ys = []
    for hg in range(2):
        ss = slice(hg * hs, (hg + 1) * hs)
        ys.append(jnp.dot(bur_sc[:, ss].astype(BF16), cre_ref[ss, :], preferred_element_type=F32)
                  - jnp.dot(bui_sc[:, ss].astype(BF16), cim_ref[ss, :], preferred_element_type=F32))
    y = jnp.concatenate(ys, axis=1) + d_ref[...] * u
    a = 0.5 * y * (1.0 + jnp.tanh(math.sqrt(2.0 / math.pi) * (y + 0.044715 * (y * y * y))))
    s = a * _sigmoid(jnp.dot(a.astype(BF16), wglu_ref[...], preferred_element_type=F32) + bglu_ref[...])
    s_ref[...] = s.reshape(ct, SUBLANES, nu)


def _s5_block_diag(bmat, cmat):
    g, p, hgrp = bmat.shape
    gh = g // 2
    eye = jnp.eye(gh, dtype=F32)
    b4 = bmat.reshape(2, gh, p, hgrp)
    bc = jnp.einsum('agph,gk->aghkp', b4, eye).reshape(2 * gh * hgrp, gh * p)
    c4 = cmat.reshape(2, gh, hgrp, p)
    cc = jnp.einsum('aghp,gk->agpkh', c4, eye).reshape(2 * gh * p, gh * hgrp)
    return bc, cc


def _s5(u_tm, a_re, a_im, log_dt, b_re, b_im, c_re, c_im, d_skip, w_glu, b_glu, x0r, x0i):
    l, b, nu = u_tm.shape
    g, p = a_re.shape
    ns = g * p
    assert b % SUBLANES == 0 and nu == g * S5_GROUP and (g // 2) * S5_GROUP == MXU_DIM
    ct = _chunk_len(l)
    nct = l // ct
    bre_c, cre_c = _s5_block_diag(b_re, c_re)
    bim_c, cim_c = _s5_block_diag(b_im, c_im)
    ldt = jnp.broadcast_to(log_dt[:, None], (g, p)).reshape(1, ns)
    rows = ct * SUBLANES
    xspec = pl.BlockSpec((SUBLANES, ns), lambda bb, j: (bb, 0))
    s, xr, xi = pl.pallas_call(
        functools.partial(_s5_kernel, ct=ct),
        grid=(b // SUBLANES, nct),
        in_specs=[pl.BlockSpec((ct, SUBLANES, nu), lambda bb, j: (j, bb, 0)),
                  _full((1, ns)), _full((1, ns)), _full((1, ns)),
                  _full((nu, ns // 2)), _full((nu, ns // 2)),
                  _full((ns, nu // 2)), _full((ns, nu // 2)),
                  _full((1, nu)), _full((nu, nu)), _full((1, nu)), xspec, xspec],
        out_specs=[pl.BlockSpec((ct, SUBLANES, nu), lambda bb, j: (j, bb, 0)), xspec, xspec],
        out_shape=[jax.ShapeDtypeStruct((l, b, nu), F32),
                   jax.ShapeDtypeStruct((b, ns), F32), jax.ShapeDtypeStruct((b, ns), F32)],
        scratch_shapes=[pltpu.VMEM((SUBLANES, ns), F32), pltpu.VMEM((SUBLANES, ns), F32),
                        pltpu.VMEM((nu, ns // 2), BF16), pltpu.VMEM((nu, ns // 2), BF16),
                        pltpu.VMEM((rows, ns), F32), pltpu.VMEM((rows, ns), F32)],
        compiler_params=_params(2, VMEM_LIMIT),
        name="s5",
    )(u_tm, a_re.reshape(1, ns), a_im.reshape(1, ns), ldt, bre_c, bim_c,
      cre_c.astype(BF16), cim_c.astype(BF16), d_skip.reshape(1, nu), w_glu.astype(BF16),
      b_glu.reshape(1, nu), x0r.reshape(b, ns), x0i.reshape(b, ns))
    return s, xr.reshape(b, g, p), xi.reshape(b, g, p)


def _trunk(x, p, pos0, ab_state, cd_state, prm):
    conv0, c0, n0, m0, ret0 = ab_state
    hg0, x0r, x0i = cd_state
    b, l, d = x.shape
    assert l >= CONV_W - 1
    t = b * l
    tm = min(512, t)
    h = x.reshape(t, d)
    pp = p.reshape(p.shape[0], t, p.shape[-1])

    z_main, z_gate = _inproj(h, prm['norm_mix'][0], prm['w_ab'], (8 * HW, LANES), tm)
    conv_new = z_main.reshape(b, l, 8 * HW)[:, l - (CONV_W - 1):, :2 * HW]
    hm, c_new, n_new, m_new = _mlstm(z_main, z_gate, prm['conv_w_ab'][0], prm['conv_b_ab'][0],
                                     prm['b_gate_ab'][0], prm['gn_a'][0], conv0[0], c0[0], n0[0],
                                     m0[0], b, l)
    cos_t, sin_t = _rope_tables(l, pos0)
    hr, ret_new = _retention(z_main, cos_t, sin_t, ret0[0], b, l)
    h = _post(h, hm, hr, pp[0], prm['w_out_ab'], prm['norm_ff'][0], prm['w_ff1'][0], prm['w_ff2'][0],
              prm['norm_ple'][0], prm['w_ple_gate'][0], prm['w_ple_proj'][0], prm['norm_final'],
              final=False, tm=tm)

    direct_tm = b == SUBLANES and l % tm == 0
    z_cd, su = _inproj(h, prm['norm_mix'][1], prm['w_cd'], (4 * HW, HW), tm,
                       time_major_last=(b, l) if direct_tm else None)
    o, hg_new = _hgrn(z_cd, prm['lb_logits'], prm['gn_c'][0], hg0[0], b, l, layer=1)
    u_tm = su.reshape(l, b, HW) if direct_tm else jnp.transpose(su.reshape(b, l, HW), (1, 0, 2))
    s_tm, xr, xi = _s5(u_tm, prm['s5_A_re'][0], prm['s5_A_im'][0], prm['s5_log_dt'][0],
                       prm['s5_B_re'][0], prm['s5_B_im'][0], prm['s5_C_re'][0], prm['s5_C_im'][0],
                       prm['s5_D'][0], prm['w_glu'][0], prm['b_glu'][0], x0r[0], x0i[0])
    s_in = s_tm.reshape(l, b * HW) if direct_tm else jnp.transpose(s_tm, (1, 0, 2)).reshape(t, HW)
    y = _post(h, o, s_in, pp[1], prm['w_out_cd'], prm['norm_ff'][1], prm['w_ff1'][1], prm['w_ff2'][1],
              prm['norm_ple'][1], prm['w_ple_gate'][1], prm['w_ple_proj'][1], prm['norm_final'],
              final=True, tm=tm, b_time_major=(b, l) if direct_tm else None)

    ab_new = (conv_new[None], c_new[None], n_new[None], m_new[None], ret_new[None])
    cd_new = (hg_new[None], xr[None], xi[None])
    return y.reshape(b, l, d), ab_new, cd_new


def kernel(x_prompt, x_sample, state_mlstm_conv, state_mlstm_C, state_mlstm_n, state_mlstm_m, state_ret, state_hgrn, state_s5_re, state_s5_im, p_prompt, p_sample, norm_mix, norm_ff, norm_ple, norm_final, w_in_ab, b_gate_ab, conv_w_ab, conv_b_ab, gn_a, w_out_ab, w_in_cd, lb_logits, gn_c, s5_A_re, s5_A_im, s5_log_dt, s5_B_re, s5_B_im, s5_C_re, s5_C_im, s5_D, w_glu, b_glu, w_out_cd, w_ff1, w_ff2, w_ple_proj, w_ple_gate):
    assert norm_mix.shape[0] == 2, "two layers: (mLSTM || retention), (HGRN2 || S5)"
    w_ab = w_in_ab[0]
    gate0 = 4 * HW
    w_ab = jnp.concatenate([w_ab[:, :gate0], w_ab[:, gate0 + 2 * HEADS:],
                            w_ab[:, gate0:gate0 + 2 * HEADS],
                            jnp.zeros((w_ab.shape[0], LANES - 2 * HEADS), w_ab.dtype)], axis=1)
    prm = dict(norm_mix=norm_mix, norm_ff=norm_ff, norm_ple=norm_ple, norm_final=norm_final,
               w_ab=w_ab.astype(BF16), b_gate_ab=b_gate_ab, conv_w_ab=conv_w_ab, conv_b_ab=conv_b_ab,
               gn_a=gn_a, w_out_ab=w_out_ab[0].astype(BF16), w_cd=w_in_cd[0].astype(BF16),
               lb_logits=lb_logits, gn_c=gn_c, s5_A_re=s5_A_re, s5_A_im=s5_A_im,
               s5_log_dt=s5_log_dt, s5_B_re=s5_B_re, s5_B_im=s5_B_im, s5_C_re=s5_C_re,
               s5_C_im=s5_C_im, s5_D=s5_D, w_glu=w_glu, b_glu=b_glu,
               w_out_cd=w_out_cd[0].astype(BF16), w_ff1=w_ff1.astype(BF16), w_ff2=w_ff2.astype(BF16),
               w_ple_proj=w_ple_proj.astype(BF16), w_ple_gate=w_ple_gate.astype(BF16))

    bp, lp, _ = x_prompt.shape
    z = lambda *s: jnp.zeros(s, F32)
    zero_ab = (z(1, bp, CONV_W - 1, 2 * HW), z(1, bp, HEADS, DH, DH), z(1, bp, HEADS, DH),
               z(1, bp, HEADS), z(1, bp, HEADS, DH, DH))
    zero_cd = (z(1, bp, HEADS, DH, DH),) + (z(*((1, bp) + s5_A_re.shape[1:])),) * 2
    y_p, ab_p, cd_p = _trunk(x_prompt, p_prompt, 0, zero_ab, zero_cd, prm)
    y_s, ab_s, cd_s = _trunk(x_sample, p_sample, PAST_LEN,
                             (state_mlstm_conv, state_mlstm_C, state_mlstm_n, state_mlstm_m, state_ret),
                             (state_hgrn, state_s5_re, state_s5_im), prm)
    return (y_p, y_s,
            ab_p[0], ab_s[0], ab_p[1], ab_s[1], ab_p[2], ab_s[2], ab_p[3], ab_s[3], ab_p[4], ab_s[4],
            cd_p[0], cd_s[0], cd_p[1], cd_s[1], cd_p[2], cd_s[2])
```

```python
import functools
import math

import jax
import jax.numpy as jnp
from jax import lax
from jax.experimental import pallas as pl
from jax.experimental.pallas import tpu as pltpu

F32 = jnp.float32
BF16 = jnp.bfloat16

EPS = 1e-6
NEG = -1e30
ROPE_BASE = 10000.0
PAST_LEN = 16384
CHUNK = 64
HEADS = 4
DH = 128
HW = HEADS * DH
CONV_W = 4
S5_GROUP = 16
S5_STATE = 64
SUBLANES = 8
LANES = 128
MXU_DIM = 256
VMEM_LIMIT = 56 * 1024 * 1024


def _params(n_axes, vmem=None):
    return pltpu.CompilerParams(dimension_semantics=("arbitrary",) * n_axes, vmem_limit_bytes=vmem)


def _full(shape):
    return pl.BlockSpec(shape, lambda *_: (0,) * len(shape))


def _bdot(a, b):
    return jnp.dot(a.astype(BF16), b.astype(BF16), preferred_element_type=F32)


def _bdot_nt(a, b):
    return lax.dot_general(a.astype(BF16), b.astype(BF16), (((1,), (1,)), ((), ())),
                           preferred_element_type=F32)


def _bdot_tn(a, b):
    return lax.dot_general(a.astype(BF16), b.astype(BF16), (((0,), (0,)), ((), ())),
                           preferred_element_type=F32)


def _sigmoid(x):
    return 1.0 / (1.0 + jnp.exp(-x))


def _log_sigmoid(x):
    return jnp.minimum(x, 0.0) - jnp.log1p(jnp.exp(-jnp.abs(x)))


def _rms(x, g):
    return x * lax.rsqrt(jnp.mean(x * x, axis=-1, keepdims=True) + EPS) * g


def _head_rms(x):
    return x * lax.rsqrt(jnp.mean(x * x, axis=-1, keepdims=True) + EPS)


def _chunk_len(length):
    return CHUNK if length % CHUNK == 0 else length


def _inproj_kernel(x_ref, g_ref, w_ref, *out_refs):
    hn = _rms(x_ref[...], g_ref[...]).astype(BF16)
    off = 0
    for o_ref in out_refs:
        n = o_ref.shape[-1]
        for n0 in range(0, n, HW):
            nn = min(HW, n - n0)
            o_ref[:, n0:n0 + nn] = jnp.dot(hn, w_ref[:, off + n0:off + n0 + nn],
                                           preferred_element_type=F32)
        off += n


def _inproj(h, g, w, widths, tm, time_major_last=None):
    t, d = h.shape
    n = w.shape[1]
    assert sum(widths) == n and t % tm == 0
    out_shape = [jax.ShapeDtypeStruct((t, wd), F32) for wd in widths]
    out_specs = [pl.BlockSpec((tm, wd), lambda i: (i, 0)) for wd in widths]
    if time_major_last is not None:
        b, l = time_major_last
        assert l % tm == 0
        nl = l // tm
        wd = widths[-1]
        out_shape[-1] = jax.ShapeDtypeStruct((l, b * wd), F32)
        out_specs[-1] = pl.BlockSpec((tm, wd), lambda i: (i % nl, i // nl))
    return pl.pallas_call(
        _inproj_kernel,
        grid=(t // tm,),
        in_specs=[pl.BlockSpec((tm, d), lambda i: (i, 0)), _full((1, d)), _full((d, n))],
        out_specs=out_specs,
        out_shape=out_shape,
        compiler_params=_params(1, VMEM_LIMIT),
        name="inproj",
    )(h, g.reshape(1, d), w)


def _post_kernel(h_ref, ma_ref, mb_ref, p_ref, wo_ref, gff_ref, w1_ref, w2_ref, gple_ref, wg_ref,
                 wp_ref, gfin_ref, o_ref, *, final, ff_chunk):
    half = ma_ref.shape[-1]
    h = h_ref[...]
    h = h + (jnp.dot(ma_ref[...].astype(BF16), wo_ref[0:half, :], preferred_element_type=F32)
             + jnp.dot(mb_ref[...].astype(BF16), wo_ref[half:2 * half, :], preferred_element_type=F32))
    hn = _rms(h, gff_ref[...]).astype(BF16)
    d_ff = w1_ref.shape[1]
    acc = jnp.zeros_like(h)
    for f0 in range(0, d_ff, ff_chunk):
        a = jnp.dot(hn, w1_ref[:, f0:f0 + ff_chunk], preferred_element_type=F32)
        a = jnp.square(jnp.maximum(a, 0.0))
        acc = acc + jnp.dot(a.astype(BF16), w2_ref[f0:f0 + ff_chunk, :], preferred_element_type=F32)
    h = h + acc
    gate = _sigmoid(jnp.dot(_rms(h, gple_ref[...]).astype(BF16), wg_ref[...],
                            preferred_element_type=F32))
    h = h + gate * jnp.dot(p_ref[...].astype(BF16), wp_ref[...], preferred_element_type=F32)
    o_ref[...] = _rms(h, gfin_ref[...]) if final else h


def _post(h, mix_a, mix_b, p, wo, gff, w1, w2, gple, wg, wp, gfin, *, final, tm, b_time_major=None):
    t, d = h.shape
    half = mix_a.shape[-1]
    pd = p.shape[-1]
    d_ff = w1.shape[1]
    assert t % tm == 0
    row = lambda i: (i, 0)
    mb_spec = pl.BlockSpec((tm, half), row)
    if b_time_major is not None:
        _, l = b_time_major
        assert l % tm == 0
        nl = l // tm
        mb_spec = pl.BlockSpec((tm, half), lambda i: (i % nl, i // nl))
    return pl.pallas_call(
        functools.partial(_post_kernel, final=final, ff_chunk=1024),
        grid=(t // tm,),
        in_specs=[pl.BlockSpec((tm, d), row), pl.BlockSpec((tm, half), row), mb_spec,
                  pl.BlockSpec((tm, pd), row),
                  _full((d, d)), _full((1, d)), _full((d, d_ff)), _full((d_ff, d)), _full((1, d)),
                  _full((d, d)), _full((pd, d)), _full((1, d))],
        out_specs=pl.BlockSpec((tm, d), row),
        out_shape=jax.ShapeDtypeStruct((t, d), F32),
        compiler_params=_params(1, VMEM_LIMIT),
        name="post",
    )(h, mix_a, mix_b, p, wo, gff.reshape(1, d), w1, w2, gple.reshape(1, d), wg, wp,
      gfin.reshape(1, d))


def _mlstm_kernel(zq_ref, zk_ref, zv_ref, zo_ref, zg_ref, cw_ref, cb_ref, bg_ref, gn_ref,
                  conv0_ref, c0_ref, n0_ref, m0_ref,
                  out_ref, c_ref, n_ref, m_ref, ext_ref, *, c, group):
    j = pl.program_id(1)
    tail = CONV_W - 1

    @pl.when(j == 0)
    def _():
        c_ref[...] = c0_ref[...]
        n_ref[...] = n0_ref[...]
        m_ref[...] = m0_ref[...]
        for gi in range(group):
            ext_ref[gi, 0:SUBLANES, :] = jnp.zeros((SUBLANES, 2 * HW), F32)
            ext_ref[gi, SUBLANES - tail:SUBLANES, :] = conv0_ref[gi]

    ri = lax.broadcasted_iota(jnp.int32, (c, c), 0)
    ci = lax.broadcasted_iota(jnp.int32, (c, c), 1)
    eye = ri == ci
    tril = ri >= ci
    lane = lax.broadcasted_iota(jnp.int32, (c, LANES), 1)
    bg = bg_ref[...]

    def lsum(x):
        return jnp.broadcast_to(jnp.sum(x, axis=1, keepdims=True), (c, DH))

    for gi in range(group):
        ext_ref[gi, SUBLANES:SUBLANES + c, 0:HW] = zq_ref[gi]
        ext_ref[gi, SUBLANES:SUBLANES + c, HW:2 * HW] = zk_ref[gi]
        conv = cb_ref[...]
        for jj in range(CONV_W):
            r0 = SUBLANES - tail + jj
            conv = conv + cw_ref[jj:jj + 1, :] * ext_ref[gi, r0:r0 + c, :]
        ext_ref[gi, 0:SUBLANES, :] = ext_ref[gi, c:c + SUBLANES, :]
        qk = conv * _sigmoid(conv)
        gb = zg_ref[gi] + bg
        gates = jnp.where(lane < HEADS, gb, _log_sigmoid(gb))
        for h in range(HEADS):
            sl = slice(h * DH, (h + 1) * DH)
            qh = qk[:, sl]
            kh = qk[:, HW + h * DH:HW + (h + 1) * DH] * (DH ** -0.5)
            vh = zv_ref[gi, :, sl]
            i_col = lsum(jnp.where(lane == h, gates, 0.0))
            f_col = lsum(jnp.where(lane == HEADS + h, gates, 0.0))
            b_row = jnp.sum(jnp.where(ri <= ci, f_col[:, :c], 0.0), axis=0, keepdims=True)
            b_col = lsum(jnp.where(eye, b_row, 0.0))
            i_row = jnp.sum(jnp.where(eye, i_col[:, :c], 0.0), axis=0, keepdims=True)
            m_prev = m_ref[gi, h:h + 1, :]
            dmat = jnp.where(tril, b_col[:, :c] - b_row + i_row, NEG)
            inter = b_col + m_prev
            m_t = jnp.maximum(inter, jnp.broadcast_to(jnp.max(dmat, axis=1, keepdims=True), (c, DH)))
            w_intra = jnp.exp(dmat - m_t[:, :c])
            w_inter = jnp.exp(inter - m_t)
            c_h = c_ref[gi, h]
            n_h = n_ref[gi, h:h + 1, :]
            s = _bdot_nt(qh, kh) * w_intra
            num = w_inter * _bdot(qh, c_h) + _bdot(s, vh)
            den = w_inter * lsum(qh * n_h) + lsum(s)
            hh = num / jnp.maximum(jnp.abs(den), jnp.exp(-m_t))
            m_new = m_t[c - 1:c, :]
            b_last = b_col[c - 1:c, :]
            w_last = jnp.exp(b_last - b_col + i_col - m_new)
            decay = jnp.exp(b_last + m_prev - m_new)
            kw = w_last * kh
            c_ref[gi, h] = decay * c_h + _bdot_tn(kw, vh)
            n_ref[gi, h:h + 1, :] = decay * n_h + jnp.sum(kw, axis=0, keepdims=True)
            m_ref[gi, h:h + 1, :] = m_new
            hh = _sigmoid(zo_ref[gi, :, sl]) * hh
            out_ref[gi, :, sl] = _head_rms(hh) * gn_ref[:, sl]


def _seq_group(b, c):
    rows = 256
    group = max(1, min(b, rows // c, SUBLANES))
    assert b % group == 0
    return group


def _mlstm(z_main, z_gate, conv_w, conv_b, b_gate, gn_a, conv0, c0, n0, m0, b, l):
    c = _chunk_len(l)
    nc = l // c
    grp = _seq_group(b, c)
    z3 = z_main.reshape(b, l, z_main.shape[-1])
    zspec = lambda col: pl.BlockSpec((grp, c, HW), lambda bi, j: (bi, j, col))
    st = lambda shape: pl.BlockSpec((grp,) + shape, lambda bi, j: (bi,) + (0,) * len(shape))
    m0b = jnp.broadcast_to(m0[:, :, None], (b, HEADS, DH))
    out, c_new, n_new, m_new = pl.pallas_call(
        functools.partial(_mlstm_kernel, c=c, group=grp),
        grid=(b // grp, nc),
        in_specs=[zspec(0), zspec(1), zspec(2), zspec(3),
                  pl.BlockSpec((grp, c, LANES), lambda bi, j: (bi, j, 0)),
                  _full((CONV_W, 2 * HW)), _full((1, 2 * HW)), _full((1, LANES)), _full((1, HW)),
                  st((CONV_W - 1, 2 * HW)), st((HEADS, DH, DH)), st((HEADS, DH)), st((HEADS, DH))],
        out_specs=[pl.BlockSpec((grp, c, HW), lambda bi, j: (bi, j, 0)),
                   st((HEADS, DH, DH)), st((HEADS, DH)), st((HEADS, DH))],
        out_shape=[jax.ShapeDtypeStruct((b, l, HW), F32),
                   jax.ShapeDtypeStruct((b, HEADS, DH, DH), F32),
                   jax.ShapeDtypeStruct((b, HEADS, DH), F32),
                   jax.ShapeDtypeStruct((b, HEADS, DH), F32)],
        scratch_shapes=[pltpu.VMEM((grp, c + SUBLANES, 2 * HW), F32)],
        compiler_params=_params(2),
        name="mlstm",
    )(z3, z3, z3, z3, z_gate.reshape(b, l, LANES), conv_w, conv_b.reshape(1, -1),
      jnp.pad(b_gate, (0, LANES - 2 * HEADS)).reshape(1, LANES), gn_a.reshape(1, -1),
      conv0, c0, n0, m0b)
    return out.reshape(b * l, HW), c_new, n_new, m_new[:, :, 0]


def _rope_table_kernel(inv_ref, sign_ref, cos_ref, sin_ref, *, pos0, rows):
    i = pl.program_id(0)
    pos = (pos0 + i * rows + lax.broadcasted_iota(jnp.int32, (rows, LANES), 0)).astype(F32)
    ang = pos * inv_ref[...]
    cos_ref[...] = jnp.cos(ang)
    sin_ref[...] = jnp.sin(ang) * sign_ref[...]


def _rope_tables(l, pos0):
    half = DH // 2
    inv = ROPE_BASE ** (-jnp.arange(half, dtype=F32) / half)
    inv2 = jnp.concatenate([inv, inv]).reshape(1, DH)
    sign = jnp.concatenate([-jnp.ones((half,), F32), jnp.ones((half,), F32)]).reshape(1, DH)
    rows = min(l, 512)
    assert l % rows == 0
    return pl.pallas_call(
        functools.partial(_rope_table_kernel, pos0=pos0, rows=rows),
        grid=(l // rows,),
        in_specs=[_full((1, DH)), _full((1, DH))],
        out_specs=[pl.BlockSpec((rows, DH), lambda i: (i, 0))] * 2,
        out_shape=[jax.ShapeDtypeStruct((l, DH), F32)] * 2,
        compiler_params=_params(1),
        name="rope_table",
    )(inv2, sign)


def _ret_kernel(zq_ref, zk_ref, zv_ref, zg_ref, cos_ref, sin_ref, s0_ref, out_ref, s_ref,
                *, c, group):
    j = pl.program_id(1)

    @pl.when(j == 0)
    def _():
        s_ref[...] = s0_ref[...]

    cosf = cos_ref[...]
    sinf = sin_ref[...]
    ti = lax.broadcasted_iota(jnp.int32, (c, c), 0)
    si = lax.broadcasted_iota(jnp.int32, (c, c), 1)
    rel = jnp.maximum(ti - si, 0).astype(F32)
    tcol = lax.broadcasted_iota(jnp.int32, (c, 1), 0).astype(F32)

    def rope(x):
        return x * cosf + pltpu.roll(x, DH // 2, axis=1) * sinf

    for h in range(HEADS):
        sl = slice(h * DH, (h + 1) * DH)
        lg = math.log1p(-(2.0 ** (-5.0 - h)))
        decay = jnp.where(ti >= si, jnp.exp(rel * lg), 0.0)
        inter = jnp.exp((tcol + 1.0) * lg)
        kdecay = jnp.exp((c - 1.0 - tcol) * lg)
        cdecay = math.exp(c * lg)
        for gi in range(group):
            qr = rope(zq_ref[gi, :, sl])
            kr = rope(zk_ref[gi, :, sl]) * (DH ** -0.5)
            vh = zv_ref[gi, :, sl]
            s_h = s_ref[gi, h]
            o = _bdot(qr, s_h) * inter + _bdot(_bdot_nt(qr, kr) * decay, vh)
            s_ref[gi, h] = cdecay * s_h + _bdot_tn(kr * kdecay, vh)
            gate = zg_ref[gi, :, sl]
            out_ref[gi, :, sl] = _head_rms(o) * (gate * _sigmoid(gate))


def _retention(z_main, cos_t, sin_t, s0, b, l):
    c = _chunk_len(l)
    nc = l // c
    grp = _seq_group(b, c)
    z3 = z_main.reshape(b, l, z_main.shape[-1])
    zspec = lambda col: pl.BlockSpec((grp, c, HW), lambda bi, j: (bi, j, col))
    st = pl.BlockSpec((grp, HEADS, DH, DH), lambda bi, j: (bi, 0, 0, 0))
    tab = pl.BlockSpec((c, DH), lambda bi, j: (j, 0))
    out, s_new = pl.pallas_call(
        functools.partial(_ret_kernel, c=c, group=grp),
        grid=(b // grp, nc),
        in_specs=[zspec(4), zspec(5), zspec(6), zspec(7), tab, tab, st],
        out_specs=[pl.BlockSpec((grp, c, HW), lambda bi, j: (bi, j, 0)), st],
        out_shape=[jax.ShapeDtypeStruct((b, l, HW), F32),
                   jax.ShapeDtypeStruct((b, HEADS, DH, DH), F32)],
        compiler_params=_params(2),
        name="retention",
    )(z3, z3, z3, z3, cos_t, sin_t, s0)
    return out.reshape(b * l, HW), s_new


def _hgrn_kernel(zq_ref, zf_ref, zi_ref, zg_ref, lbl_ref, gn_ref, s0_ref, out_ref, s_ref,
                 *, c, sc, layer, group):
    j = pl.program_id(1)

    @pl.when(j == 0)
    def _():
        s_ref[...] = s0_ref[...]

    lbl = lbl_ref[...]
    e = jnp.exp(lbl - jnp.max(lbl, axis=0, keepdims=True))
    sm = e / jnp.sum(e, axis=0, keepdims=True)
    cum = sm[0:1, :]
    for r in range(1, layer + 1):
        cum = cum + sm[r:r + 1, :]
    lb = cum - sm[0:1, :]

    la = jnp.log(lb)
    l1m = jnp.log1p(-lb)
    ri = lax.broadcasted_iota(jnp.int32, (c, c), 0)
    ci = lax.broadcasted_iota(jnp.int32, (c, c), 1)
    tril = jnp.where(ri >= ci, 1.0, 0.0).astype(BF16)
    rowi = lax.broadcasted_iota(jnp.int32, (sc, 1), 0)
    e_r = lax.broadcasted_iota(jnp.int32, (DH, DH), 0)
    e_c = lax.broadcasted_iota(jnp.int32, (DH, DH), 1)
    eye = e_r == e_c
    for gi in range(group):
        zf = zf_ref[gi]
        ls = _log_sigmoid(zf)
        lbb = l1m + ls
        logf = jnp.maximum(la, lbb) + jnp.log1p(jnp.exp(-jnp.abs(la - lbb)))
        kk = jnp.exp(l1m + ls - zf)

        p0 = logf.astype(BF16)
        r1 = logf - p0.astype(F32)
        p1 = r1.astype(BF16)
        p2 = (r1 - p1.astype(F32)).astype(BF16)
        bcum = (jnp.dot(tril, p0, preferred_element_type=F32)
                + jnp.dot(tril, p1, preferred_element_type=F32)
                + jnp.dot(tril, p2, preferred_element_type=F32))

        for h in range(HEADS):
            sl = slice(h * DH, (h + 1) * DH)
            bh = bcum[:, sl]
            qh = zq_ref[gi, :, sl] * (DH ** -0.5)
            kh = kk[:, sl]
            vh = zi_ref[gi, :, sl]
            s_h = s_ref[gi, h]
            o_inter = _bdot(qh * jnp.exp(bh), s_h)
            blocks = []
            for blk in range(c // sc):
                r0 = blk * sc
                b_i = bh[r0:r0 + sc]
                q_i = qh[r0:r0 + sc]
                k_i = kh[r0:r0 + sc]
                v_i = vh[r0:r0 + sc]
                o_i = jnp.zeros((sc, DH), F32)
                if blk > 0:
                    ref_row = bh[r0 - 1:r0, :]
                    a_i = q_i * jnp.exp(b_i - ref_row)
                    k_prev = kh[0:r0] * jnp.exp(ref_row - bh[0:r0])
                    o_i = _bdot(_bdot_nt(a_i, k_prev), vh[0:r0])
                for s in range(sc):
                    dec = jnp.exp(jnp.where(rowi >= s, b_i - b_i[s:s + 1, :], NEG))
                    col = jnp.sum(q_i * dec * k_i[s:s + 1, :], axis=1, keepdims=True)
                    o_i = o_i + col * v_i[s:s + 1, :]
                blocks.append(o_i)
            o = o_inter + (jnp.concatenate(blocks, axis=0) if len(blocks) > 1 else blocks[0])
            b_last = bh[c - 1:c, :]
            dec_col = jnp.sum(jnp.where(eye, jnp.exp(b_last), 0.0), axis=1, keepdims=True)
            s_ref[gi, h] = dec_col * s_h + _bdot_tn(kh * jnp.exp(b_last - bh), vh)
            gate = zg_ref[gi, :, sl]
            out_ref[gi, :, sl] = _head_rms(o) * gn_ref[:, sl] * (gate * _sigmoid(gate))


def _hgrn(z_cd, lb_logits, gn_c, s0, b, l, layer):
    c = _chunk_len(l)
    sc = min(c, 16)
    nc = l // c
    grp = _seq_group(b, c)
    depth = lb_logits.shape[0]
    z3 = z_cd.reshape(b, l, z_cd.shape[-1])
    zspec = lambda col: pl.BlockSpec((grp, c, HW), lambda bi, j: (bi, j, col))
    st = pl.BlockSpec((grp, HEADS, DH, DH), lambda bi, j: (bi, 0, 0, 0))
    out, s_new = pl.pallas_call(
        functools.partial(_hgrn_kernel, c=c, sc=sc, layer=layer, group=grp),
        grid=(b // grp, nc),
        in_specs=[zspec(0), zspec(1), zspec(2), zspec(3), _full((depth, HW)), _full((1, HW)), st],
        out_specs=[pl.BlockSpec((grp, c, HW), lambda bi, j: (bi, j, 0)), st],
        out_shape=[jax.ShapeDtypeStruct((b, l, HW), F32),
                   jax.ShapeDtypeStruct((b, HEADS, DH, DH), F32)],
        compiler_params=_params(2),
        name="hgrn2",
    )(z3, z3, z3, z3, lb_logits, gn_c.reshape(1, -1), s0)
    return out.reshape(b * l, HW), s_new


def _s5_kernel(u_ref, are_ref, aim_ref, ldt_ref, bre_ref, bim_ref, cre_ref, cim_ref, d_ref,
               wglu_ref, bglu_ref, x0r_ref, x0i_ref,
               s_ref, xr_ref, xi_ref, ar_sc, ai_sc, bbr_sc, bbi_sc, bur_sc, bui_sc, *, ct):
    j = pl.program_id(1)
    ns = are_ref.shape[-1]
    nu = u_ref.shape[-1]
    hs = ns // 2
    hu = nu // 2
    rows = ct * SUBLANES

    @pl.when(j == 0)
    def _():
        a_re = are_ref[...]
        a_im = aim_ref[...]
        dt = jnp.exp(ldt_ref[...])
        mag = jnp.exp(dt * a_re)
        ar = mag * jnp.cos(dt * a_im)
        ai = mag * jnp.sin(dt * a_im)
        ar_sc[...] = jnp.broadcast_to(ar, (SUBLANES, ns))
        ai_sc[...] = jnp.broadcast_to(ai, (SUBLANES, ns))
        den = a_re * a_re + a_im * a_im
        nr = ar - 1.0
        zr = (nr * a_re + ai * a_im) / den
        zi = (ai * a_re - nr * a_im) / den
        for hg in range(2):
            us = slice(hg * hu, (hg + 1) * hu)
            ss = slice(hg * hs, (hg + 1) * hs)
            bbr_sc[us, :] = (zr[:, ss] * bre_ref[us, :] - zi[:, ss] * bim_ref[us, :]).astype(BF16)
            bbi_sc[us, :] = (zr[:, ss] * bim_ref[us, :] + zi[:, ss] * bre_ref[us, :]).astype(BF16)
        xr_ref[...] = x0r_ref[...]
        xi_ref[...] = x0i_ref[...]

    u = u_ref[...].reshape(rows, nu)
    ub = u.astype(BF16)
    for hg in range(2):
        us = slice(hg * hu, (hg + 1) * hu)
        ss = slice(hg * hs, (hg + 1) * hs)
        bur_sc[:, ss] = jnp.dot(ub[:, us], bbr_sc[us, :], preferred_element_type=F32)
        bui_sc[:, ss] = jnp.dot(ub[:, us], bbi_sc[us, :], preferred_element_type=F32)

    lane_chunk = 4 * LANES
    for lc in range(ns // lane_chunk):
        ls = slice(lc * lane_chunk, (lc + 1) * lane_chunk)
        ar = ar_sc[:, ls]
        ai = ai_sc[:, ls]

        def step(t, carry):
            xr, xi = carry
            r0 = pl.multiple_of(t * SUBLANES, SUBLANES)
            nxr = ar * xr - ai * xi + bur_sc[pl.ds(r0, SUBLANES), ls]
            nxi = ar * xi + ai * xr + bui_sc[pl.ds(r0, SUBLANES), ls]
            bur_sc[pl.ds(r0, SUBLANES), ls] = nxr
            bui_sc[pl.ds(r0, SUBLANES), ls] = nxi
            return nxr, nxi

        xr, xi = lax.fori_loop(0, ct, step, (xr_ref[:, ls], xi_ref[:, ls]))
        xr_ref[:, ls] = xr
        xi_ref[:, ls] = xi

    ys = []
    for hg in range(2):
        ss = slice(hg * hs, (hg + 1) * hs)
        ys.append(jnp.dot(bur_sc[:, ss].astype(BF16), cre_ref[ss, :], preferred_element_type=F32)
                  - jnp.dot(bui_sc[:, ss].astype(BF16), cim_ref[ss, :], preferred_element_type=F32))
    y = jnp.concatenate(ys, axis=1) + d_ref[...] * u
    a = 0.5 * y * (1.0 + jnp.tanh(math.sqrt(2.0 / math.pi) * (y + 0.044715 * (y * y * y))))
    s = a * _sigmoid(jnp.dot(a.astype(BF16), wglu_ref[...], preferred_element_type=F32) + bglu_ref[...])
    s_ref[...] = s.reshape(ct, SUBLANES, nu)


def _s5_block_diag(bmat, cmat):
    g, p, hgrp = bmat.shape
    gh = g // 2
    eye = jnp.eye(gh, dtype=F32)
    b4 = bmat.reshape(2, gh, p, hgrp)
    bc = jnp.einsum('agph,gk->aghkp', b4, eye).reshape(2 * gh * hgrp, gh * p)
    c4 = cmat.reshape(2, gh, hgrp, p)
    cc = jnp.einsum('aghp,gk->agpkh', c4, eye).reshape(2 * gh * p, gh * hgrp)
    return bc, cc


def _s5(u_tm, a_re, a_im, log_dt, b_re, b_im, c_re, c_im, d_skip, w_glu, b_glu, x0r, x0i):
    l, b, nu = u_tm.shape
    g, p = a_re.shape
    ns = g * p
    assert b % SUBLANES == 0 and nu == g * S5_GROUP and (g // 2) * S5_GROUP == MXU_DIM
    ct = _chunk_len(l)
    nct = l // ct
    bre_c, cre_c = _s5_block_diag(b_re, c_re)
    bim_c, cim_c = _s5_block_diag(b_im, c_im)
    ldt = jnp.broadcast_to(log_dt[:, None], (g, p)).reshape(1, ns)
    rows = ct * SUBLANES
    xspec = pl.BlockSpec((SUBLANES, ns), lambda bb, j: (bb, 0))
    s, xr, xi = pl.pallas_call(
        functools.partial(_s5_kernel, ct=ct),
        grid=(b // SUBLANES, nct),
        in_specs=[pl.BlockSpec((ct, SUBLANES, nu), lambda bb, j: (j, bb, 0)),
                  _full((1, ns)), _full((1, ns)), _full((1, ns)),
                  _full((nu, ns // 2)), _full((nu, ns // 2)),
                  _full((ns, nu // 2)), _full((ns, nu // 2)),
                  _full((1, nu)), _full((nu, nu)), _full((1, nu)), xspec, xspec],
        out_specs=[pl.BlockSpec((ct, SUBLANES, nu), lambda bb, j: (j, bb, 0)), xspec, xspec],
        out_shape=[jax.ShapeDtypeStruct((l, b, nu), F32),
                   jax.ShapeDtypeStruct((b, ns), F32), jax.ShapeDtypeStruct((b, ns), F32)],
        scratch_shapes=[pltpu.VMEM((SUBLANES, ns), F32), pltpu.VMEM((SUBLANES, ns), F32),
                        pltpu.VMEM((nu, ns // 2), BF16), pltpu.VMEM((nu, ns // 2), BF16),
                        pltpu.VMEM((rows, ns), F32), pltpu.VMEM((rows, ns), F32)],
        compiler_params=_params(2, VMEM_LIMIT),
        name="s5",
    )(u_tm, a_re.reshape(1, ns), a_im.reshape(1, ns), ldt, bre_c, bim_c,
      cre_c.astype(BF16), cim_c.astype(BF16), d_skip.reshape(1, nu), w_glu.astype(BF16),
      b_glu.reshape(1, nu), x0r.reshape(b, ns), x0i.reshape(b, ns))
    return s, xr.reshape(b, g, p), xi.reshape(b, g, p)


def _trunk(x, p, pos0, ab_state, cd_state, prm):
    conv0, c0, n0, m0, ret0 = ab_state
    hg0, x0r, x0i = cd_state
    b, l, d = x.shape
    assert l >= CONV_W - 1
    t = b * l
    tm = min(512, t)
    h = x.reshape(t, d)
    pp = p.reshape(p.shape[0], t, p.shape[-1])

    z_main, z_gate = _inproj(h, prm['norm_mix'][0], prm['w_ab'], (8 * HW, LANES), tm)
    conv_new = z_main.reshape(b, l, 8 * HW)[:, l - (CONV_W - 1):, :2 * HW]
    hm, c_new, n_new, m_new = _mlstm(z_main, z_gate, prm['conv_w_ab'][0], prm['conv_b_ab'][0],
                                     prm['b_gate_ab'][0], prm['gn_a'][0], conv0[0], c0[0], n0[0],
                                     m0[0], b, l)
    cos_t, sin_t = _rope_tables(l, pos0)
    hr, ret_new = _retention(z_main, cos_t, sin_t, ret0[0], b, l)
    h = _post(h, hm, hr, pp[0], prm['w_out_ab'], prm['norm_ff'][0], prm['w_ff1'][0], prm['w_ff2'][0],
              prm['norm_ple'][0], prm['w_ple_gate'][0], prm['w_ple_proj'][0], prm['norm_final'],
              final=False, tm=tm)

    direct_tm = b == SUBLANES and l % tm == 0
    z_cd, su = _inproj(h, prm['norm_mix'][1], prm['w_cd'], (4 * HW, HW), tm,
                       time_major_last=(b, l) if direct_tm else None)
    o, hg_new = _hgrn(z_cd, prm['lb_logits'], prm['gn_c'][0], hg0[0], b, l, layer=1)
    u_tm = su.reshape(l, b, HW) if direct_tm else jnp.transpose(su.reshape(b, l, HW), (1, 0, 2))
    s_tm, xr, xi = _s5(u_tm, prm['s5_A_re'][0], prm['s5_A_im'][0], prm['s5_log_dt'][0],
                       prm['s5_B_re'][0], prm['s5_B_im'][0], prm['s5_C_re'][0], prm['s5_C_im'][0],
                       prm['s5_D'][0], prm['w_glu'][0], prm['b_glu'][0], x0r[0], x0i[0])
    s_in = s_tm.reshape(l, b * HW) if direct_tm else jnp.transpose(s_tm, (1, 0, 2)).reshape(t, HW)
    y = _post(h, o, s_in, pp[1], prm['w_out_cd'], prm['norm_ff'][1], prm['w_ff1'][1], prm['w_ff2'][1],
              prm['norm_ple'][1], prm['w_ple_gate'][1], prm['w_ple_proj'][1], prm['norm_final'],
              final=True, tm=tm, b_time_major=(b, l) if direct_tm else None)

    ab_new = (conv_new[None], c_new[None], n_new[None], m_new[None], ret_new[None])
    cd_new = (hg_new[None], xr[None], xi[None])
    return y.reshape(b, l, d), ab_new, cd_new


def kernel(x_prompt, x_sample, state_mlstm_conv, state_mlstm_C, state_mlstm_n, state_mlstm_m, state_ret, state_hgrn, state_s5_re, state_s5_im, p_prompt, p_sample, norm_mix, norm_ff, norm_ple, norm_final, w_in_ab, b_gate_ab, conv_w_ab, conv_b_ab, gn_a, w_out_ab, w_in_cd, lb_logits, gn_c, s5_A_re, s5_A_im, s5_log_dt, s5_B_re, s5_B_im, s5_C_re, s5_C_im, s5_D, w_glu, b_glu, w_out_cd, w_ff1, w_ff2, w_ple_proj, w_ple_gate):
    assert norm_mix.shape[0] == 2, "two layers: (mLSTM || retention), (HGRN2 || S5)"
    w_ab = w_in_ab[0]
    gate0 = 4 * HW
    w_ab = jnp.concatenate([w_ab[:, :gate0], w_ab[:, gate0 + 2 * HEADS:],
                            w_ab[:, gate0:gate0 + 2 * HEADS],
                            jnp.zeros((w_ab.shape[0], LANES - 2 * HEADS), w_ab.dtype)], axis=1)
    prm = dict(norm_mix=norm_mix, norm_ff=norm_ff, norm_ple=norm_ple, norm_final=norm_final,
               w_ab=w_ab.astype(BF16), b_gate_ab=b_gate_ab, conv_w_ab=conv_w_ab, conv_b_ab=conv_b_ab,
               gn_a=gn_a, w_out_ab=w_out_ab[0].astype(BF16), w_cd=w_in_cd[0].astype(BF16),
               lb_logits=lb_logits, gn_c=gn_c, s5_A_re=s5_A_re, s5_A_im=s5_A_im,
               s5_log_dt=s5_log_dt, s5_B_re=s5_B_re, s5_B_im=s5_B_im, s5_C_re=s5_C_re,
               s5_C_im=s5_C_im, s5_D=s5_D, w_glu=w_glu, b_glu=b_glu,
               w_out_cd=w_out_cd[0].astype(BF16), w_ff1=w_ff1.astype(BF16), w_ff2=w_ff2.astype(BF16),
               w_ple_proj=w_ple_proj.astype(BF16), w_ple_gate=w_ple_gate.astype(BF16))

    bp, lp, _ = x_prompt.shape
    z = lambda *s: jnp.zeros(s, F32)
    zero_ab = (z(1, bp, CONV_W - 1, 2 * HW), z(1, bp, HEADS, DH, DH), z(1, bp, HEADS, DH),
               z(1, bp, HEADS), z(1, bp, HEADS, DH, DH))
    zero_cd = (z(1, bp, HEADS, DH, DH),) + (z(*((1, bp) + s5_A_re.shape[1:])),) * 2
    y_p, ab_p, cd_p = _trunk(x_prompt, p_prompt, 0, zero_ab, zero_cd, prm)
    y_s, ab_s, cd_s = _trunk(x_sample, p_sample, PAST_LEN,
                             (state_mlstm_conv, state_mlstm_C, state_mlstm_n, state_mlstm_m, state_ret),
                             (state_hgrn, state_s5_re, state_s5_im), prm)
    return (y_p, y_s,
            ab_p[0], ab_s[0], ab_p[1], ab_s[1], ab_p[2], ab_s[2], ab_p[3], ab_s[3], ab_p[4], ab_s[4],
            cd_p[0], cd_s[0], cd_p[1], cd_s[1], cd_p[2], cd_s[2])
```

```python
import functools
import math

import jax
import jax.numpy as jnp
from jax import lax
from jax.experimental import pallas as pl
from jax.experimental.pallas import tpu as pltpu

F32 = jnp.float32
BF16 = jnp.bfloat16

EPS = 1e-6
NEG = -1e30
LOG2_E = math.log2(math.e)
ROPE_BASE = 10000.0
PAST_LEN = 16384
CHUNK = 64
HEADS = 4
DH = 128
HW = HEADS * DH
CONV_W = 4
S5_GROUP = 16
S5_STATE = 64
SUBLANES = 8
LANES = 128
MXU_DIM = 256
VMEM_LIMIT = 56 * 1024 * 1024


def _params(n_axes, vmem=None):
    return pltpu.CompilerParams(dimension_semantics=("arbitrary",) * n_axes, vmem_limit_bytes=vmem)


def _full(shape):
    return pl.BlockSpec(shape, lambda *_: (0,) * len(shape))


def _bdot(a, b):
    return jnp.dot(a.astype(BF16), b.astype(BF16), preferred_element_type=F32)


def _bdot_nt(a, b):
    return lax.dot_general(a.astype(BF16), b.astype(BF16), (((1,), (1,)), ((), ())),
                           preferred_element_type=F32)


def _bdot_tn(a, b):
    return lax.dot_general(a.astype(BF16), b.astype(BF16), (((0,), (0,)), ((), ())),
                           preferred_element_type=F32)


def _sigmoid(x):
    return 1.0 / (1.0 + jnp.exp(-x))


def _log_sigmoid(x):
    return jnp.minimum(x, 0.0) - jnp.log(1.0 + jnp.exp(-jnp.abs(x)))


def _rms(x, g):
    return x * lax.rsqrt(jnp.mean(x * x, axis=-1, keepdims=True) + EPS) * g


def _head_rms(x):
    return x * lax.rsqrt(jnp.mean(x * x, axis=-1, keepdims=True) + EPS)


def _chunk_len(length):
    return CHUNK if length % CHUNK == 0 else length


def _inproj_kernel(x_ref, g_ref, w_ref, *out_refs):
    hn = _rms(x_ref[...], g_ref[...]).astype(BF16)
    off = 0
    for o_ref in out_refs:
        n = o_ref.shape[-1]
        for n0 in range(0, n, HW):
            nn = min(HW, n - n0)
            o_ref[:, n0:n0 + nn] = jnp.dot(hn, w_ref[:, off + n0:off + n0 + nn],
                                           preferred_element_type=F32)
        off += n


def _inproj(h, g, w, widths, tm, time_major_last=None):
    t, d = h.shape
    n = w.shape[1]
    assert sum(widths) == n and t % tm == 0
    out_shape = [jax.ShapeDtypeStruct((t, wd), F32) for wd in widths]
    out_specs = [pl.BlockSpec((tm, wd), lambda i: (i, 0)) for wd in widths]
    if time_major_last is not None:
        b, l = time_major_last
        assert l % tm == 0
        nl = l // tm
        wd = widths[-1]
        out_shape[-1] = jax.ShapeDtypeStruct((l, b * wd), F32)
        out_specs[-1] = pl.BlockSpec((tm, wd), lambda i: (i % nl, i // nl))
    return pl.pallas_call(
        _inproj_kernel,
        grid=(t // tm,),
        in_specs=[pl.BlockSpec((tm, d), lambda i: (i, 0)), _full((1, d)), _full((d, n))],
        out_specs=out_specs,
        out_shape=out_shape,
        compiler_params=_params(1, VMEM_LIMIT),
        name="inproj",
    )(h, g.reshape(1, d), w)


def _post_kernel(h_ref, ma_ref, mb_ref, p_ref, wo_ref, gff_ref, w1_ref, w2_ref, gple_ref, wg_ref,
                 wp_ref, gfin_ref, o_ref, *, final, ff_chunk):
    half = ma_ref.shape[-1]
    h = h_ref[...]
    h = h + (jnp.dot(ma_ref[...].astype(BF16), wo_ref[0:half, :], preferred_element_type=F32)
             + jnp.dot(mb_ref[...].astype(BF16), wo_ref[half:2 * half, :], preferred_element_type=F32))
    hn = _rms(h, gff_ref[...]).astype(BF16)
    d_ff = w1_ref.shape[1]
    acc = jnp.zeros_like(h)
    for f0 in range(0, d_ff, ff_chunk):
        a = jnp.dot(hn, w1_ref[:, f0:f0 + ff_chunk], preferred_element_type=F32)
        a = jnp.square(jnp.maximum(a, 0.0))
        acc = acc + jnp.dot(a.astype(BF16), w2_ref[f0:f0 + ff_chunk, :], preferred_element_type=F32)
    h = h + acc
    gate = _sigmoid(jnp.dot(_rms(h, gple_ref[...]).astype(BF16), wg_ref[...],
                            preferred_element_type=F32))
    h = h + gate * jnp.dot(p_ref[...].astype(BF16), wp_ref[...], preferred_element_type=F32)
    o_ref[...] = _rms(h, gfin_ref[...]) if final else h


def _post(h, mix_a, mix_b, p, wo, gff, w1, w2, gple, wg, wp, gfin, *, layer, final, tm,
          b_time_major=None):
    t, d = h.shape
    half = mix_a.shape[-1]
    pd = p.shape[-1]
    d_ff = w1.shape[-1]
    assert t % tm == 0
    row = lambda i: (i, 0)
    mb_spec = pl.BlockSpec((tm, half), row)
    if b_time_major is not None:
        _, l = b_time_major
        assert l % tm == 0
        nl = l // tm
        mb_spec = pl.BlockSpec((tm, half), lambda i: (i % nl, i // nl))
    lw = lambda r, cdim: pl.BlockSpec((None, r, cdim), lambda i: (layer, 0, 0))
    return pl.pallas_call(
        functools.partial(_post_kernel, final=final, ff_chunk=1024),
        grid=(t // tm,),
        in_specs=[pl.BlockSpec((tm, d), row), pl.BlockSpec((tm, half), row), mb_spec,
                  pl.BlockSpec((None, tm, pd), lambda i: (layer, i, 0)),
                  _full((d, d)), _full((1, d)), lw(d, d_ff), lw(d_ff, d), _full((1, d)),
                  lw(d, d), lw(pd, d), _full((1, d))],
        out_specs=pl.BlockSpec((tm, d), row),
        out_shape=jax.ShapeDtypeStruct((t, d), F32),
        compiler_params=_params(1, VMEM_LIMIT),
        name="post",
    )(h, mix_a, mix_b, p, wo, gff.reshape(1, d), w1, w2, gple.reshape(1, d), wg, wp,
      gfin.reshape(1, d))


def _mlstm_kernel(zq_ref, zk_ref, zv_ref, zo_ref, zg_ref, cw_ref, cb_ref, bg_ref, gn_ref,
                  conv0_ref, c0_ref, n0_ref, m0_ref,
                  out_ref, c_ref, n_ref, m_ref, ext_ref, *, c, group):
    j = pl.program_id(1)
    tail = CONV_W - 1

    @pl.when(j == 0)
    def _():
        c_ref[...] = c0_ref[...]
        n_ref[...] = n0_ref[...]
        m_ref[...] = m0_ref[...]
        for gi in range(group):
            ext_ref[gi, 0:SUBLANES, :] = jnp.zeros((SUBLANES, 2 * HW), F32)
            ext_ref[gi, SUBLANES - tail:SUBLANES, :] = conv0_ref[gi]

    ri = lax.broadcasted_iota(jnp.int32, (c, c), 0)
    ci = lax.broadcasted_iota(jnp.int32, (c, c), 1)
    eye = ri == ci
    tril = ri >= ci
    lane = lax.broadcasted_iota(jnp.int32, (c, LANES), 1)
    bg = bg_ref[...]

    def lsum(x):
        return jnp.broadcast_to(jnp.sum(x, axis=1, keepdims=True), (c, DH))

    for gi in range(group):
        ext_ref[gi, SUBLANES:SUBLANES + c, 0:HW] = zq_ref[gi]
        ext_ref[gi, SUBLANES:SUBLANES + c, HW:2 * HW] = zk_ref[gi]
        conv = cb_ref[...]
        for jj in range(CONV_W):
            r0 = SUBLANES - tail + jj
            conv = conv + cw_ref[jj:jj + 1, :] * ext_ref[gi, r0:r0 + c, :]
        ext_ref[gi, 0:SUBLANES, :] = ext_ref[gi, c:c + SUBLANES, :]
        qk = conv * _sigmoid(conv)
        gb = zg_ref[gi] + bg
        gates = jnp.where(lane < HEADS, gb, _log_sigmoid(gb))
        for h in range(HEADS):
            sl = slice(h * DH, (h + 1) * DH)
            qh = qk[:, sl]
            kh = qk[:, HW + h * DH:HW + (h + 1) * DH] * (DH ** -0.5)
            vh = zv_ref[gi, :, sl]
            i_col = lsum(jnp.where(lane == h, gates, 0.0))
            f_col = lsum(jnp.where(lane == HEADS + h, gates, 0.0))
            b_row = jnp.sum(jnp.where(ri <= ci, f_col[:, :c], 0.0), axis=0, keepdims=True)
            b_col = lsum(jnp.where(eye, b_row, 0.0))
            i_row = jnp.sum(jnp.where(eye, i_col[:, :c], 0.0), axis=0, keepdims=True)
            m_prev = m_ref[gi, h:h + 1, :]
            dmat = jnp.where(tril, b_col[:, :c] - b_row + i_row, NEG)
            inter = b_col + m_prev
            m_t = jnp.maximum(inter, jnp.broadcast_to(jnp.max(dmat, axis=1, keepdims=True), (c, DH)))
            w_intra = jnp.exp(dmat - m_t[:, :c])
            w_inter = jnp.exp(inter - m_t)
            c_h = c_ref[gi, h]
            n_h = n_ref[gi, h:h + 1, :]
            s = _bdot_nt(qh, kh) * w_intra
            num = w_inter * _bdot(qh, c_h) + _bdot(s, vh)
            den = w_inter * lsum(qh * n_h) + lsum(s)
            hh = num / jnp.maximum(jnp.abs(den), jnp.exp(-m_t))
            m_new = m_t[c - 1:c, :]
            b_last = b_col[c - 1:c, :]
            w_last = jnp.exp(b_last - b_col + i_col - m_new)
            decay = jnp.exp(b_last + m_prev - m_new)
            kw = w_last * kh
            c_ref[gi, h] = decay * c_h + _bdot_tn(kw, vh)
            n_ref[gi, h:h + 1, :] = decay * n_h + jnp.sum(kw, axis=0, keepdims=True)
            m_ref[gi, h:h + 1, :] = m_new
            hh = _sigmoid(zo_ref[gi, :, sl]) * hh
            out_ref[gi, :, sl] = _head_rms(hh) * gn_ref[:, sl]


def _seq_group(b, c):
    rows = 256
    group = max(1, min(b, rows // c, 2 * SUBLANES))
    assert b % group == 0
    return group


def _mlstm(z_main, z_gate, conv_w, conv_b, b_gate, gn_a, conv0, c0, n0, m0, b, l):
    c = _chunk_len(l)
    nc = l // c
    grp = _seq_group(b, c)
    z3 = z_main.reshape(b, l, z_main.shape[-1])
    zspec = lambda col: pl.BlockSpec((grp, c, HW), lambda bi, j: (bi, j, col))
    st = lambda shape: pl.BlockSpec((grp,) + shape, lambda bi, j: (bi,) + (0,) * len(shape))
    m0b = jnp.broadcast_to(m0[:, :, None], (b, HEADS, DH))
    out, c_new, n_new, m_new = pl.pallas_call(
        functools.partial(_mlstm_kernel, c=c, group=grp),
        grid=(b // grp, nc),
        in_specs=[zspec(0), zspec(1), zspec(2), zspec(3),
                  pl.BlockSpec((grp, c, LANES), lambda bi, j: (bi, j, 0)),
                  _full((CONV_W, 2 * HW)), _full((1, 2 * HW)), _full((1, LANES)), _full((1, HW)),
                  st((CONV_W - 1, 2 * HW)), st((HEADS, DH, DH)), st((HEADS, DH)), st((HEADS, DH))],
        out_specs=[pl.BlockSpec((grp, c, HW), lambda bi, j: (bi, j, 0)),
                   st((HEADS, DH, DH)), st((HEADS, DH)), st((HEADS, DH))],
        out_shape=[jax.ShapeDtypeStruct((b, l, HW), F32),
                   jax.ShapeDtypeStruct((b, HEADS, DH, DH), F32),
                   jax.ShapeDtypeStruct((b, HEADS, DH), F32),
                   jax.ShapeDtypeStruct((b, HEADS, DH), F32)],
        scratch_shapes=[pltpu.VMEM((grp, c + SUBLANES, 2 * HW), F32)],
        compiler_params=_params(2),
        name="mlstm",
    )(z3, z3, z3, z3, z_gate.reshape(b, l, LANES), conv_w, conv_b.reshape(1, -1),
      jnp.pad(b_gate, (0, LANES - 2 * HEADS)).reshape(1, LANES), gn_a.reshape(1, -1),
      conv0, c0, n0, m0b)
    return out.reshape(b * l, HW), c_new, n_new, m_new[:, :, 0]


def _rope_table_kernel(inv_ref, sign_ref, cos_ref, sin_ref, *, pos0, rows):
    i = pl.program_id(0)
    pos = (pos0 + i * rows + lax.broadcasted_iota(jnp.int32, (rows, LANES), 0)).astype(F32)
    ang = pos * inv_ref[...]
    cos_ref[...] = jnp.cos(ang)
    sin_ref[...] = jnp.sin(ang) * sign_ref[...]


def _rope_tables(l, pos0):
    half = DH // 2
    inv = ROPE_BASE ** (-jnp.arange(half, dtype=F32) / half)
    inv2 = jnp.concatenate([inv, inv]).reshape(1, DH)
    sign = jnp.concatenate([-jnp.ones((half,), F32), jnp.ones((half,), F32)]).reshape(1, DH)
    rows = min(l, 512)
    assert l % rows == 0
    return pl.pallas_call(
        functools.partial(_rope_table_kernel, pos0=pos0, rows=rows),
        grid=(l // rows,),
        in_specs=[_full((1, DH)), _full((1, DH))],
        out_specs=[pl.BlockSpec((rows, DH), lambda i: (i, 0))] * 2,
        out_shape=[jax.ShapeDtypeStruct((l, DH), F32)] * 2,
        compiler_params=_params(1),
        name="rope_table",
    )(inv2, sign)


def _ret_kernel(zq_ref, zk_ref, zv_ref, zg_ref, cos_ref, sin_ref, s0_ref, out_ref, s_ref,
                *, c, group):
    j = pl.program_id(1)

    @pl.when(j == 0)
    def _():
        s_ref[...] = s0_ref[...]

    cosf = cos_ref[...]
    sinf = sin_ref[...]
    ti = lax.broadcasted_iota(jnp.int32, (c, c), 0)
    si = lax.broadcasted_iota(jnp.int32, (c, c), 1)
    rel = jnp.maximum(ti - si, 0).astype(F32)
    tcol = lax.broadcasted_iota(jnp.int32, (c, 1), 0).astype(F32)

    def rope(x):
        return x * cosf + pltpu.roll(x, DH // 2, axis=1) * sinf

    for h in range(HEADS):
        sl = slice(h * DH, (h + 1) * DH)
        lg = math.log1p(-(2.0 ** (-5.0 - h)))
        decay = jnp.where(ti >= si, jnp.exp(rel * lg), 0.0)
        inter = jnp.exp((tcol + 1.0) * lg)
        kdecay = jnp.exp((c - 1.0 - tcol) * lg)
        cdecay = math.exp(c * lg)
        for gi in range(group):
            qr = rope(zq_ref[gi, :, sl])
            kr = rope(zk_ref[gi, :, sl]) * (DH ** -0.5)
            vh = zv_ref[gi, :, sl]
            s_h = s_ref[gi, h]
            o = _bdot(qr, s_h) * inter + _bdot(_bdot_nt(qr, kr) * decay, vh)
            s_ref[gi, h] = cdecay * s_h + _bdot_tn(kr * kdecay, vh)
            gate = zg_ref[gi, :, sl]
            out_ref[gi, :, sl] = _head_rms(o) * (gate * _sigmoid(gate))


def _retention(z_main, cos_t, sin_t, s0, b, l):
    c = _chunk_len(l)
    nc = l // c
    grp = _seq_group(b, c)
    z3 = z_main.reshape(b, l, z_main.shape[-1])
    zspec = lambda col: pl.BlockSpec((grp, c, HW), lambda bi, j: (bi, j, col))
    st = pl.BlockSpec((grp, HEADS, DH, DH), lambda bi, j: (bi, 0, 0, 0))
    tab = pl.BlockSpec((c, DH), lambda bi, j: (j, 0))
    out, s_new = pl.pallas_call(
        functools.partial(_ret_kernel, c=c, group=grp),
        grid=(b // grp, nc),
        in_specs=[zspec(4), zspec(5), zspec(6), zspec(7), tab, tab, st],
        out_specs=[pl.BlockSpec((grp, c, HW), lambda bi, j: (bi, j, 0)), st],
        out_shape=[jax.ShapeDtypeStruct((b, l, HW), F32),
                   jax.ShapeDtypeStruct((b, HEADS, DH, DH), F32)],
        compiler_params=_params(2),
        name="retention",
    )(z3, z3, z3, z3, cos_t, sin_t, s0)
    return out.reshape(b * l, HW), s_new


def _hgrn_kernel(zq_ref, zf_ref, zi_ref, zg_ref, lbl_ref, gn_ref, s0_ref, out_ref, s_ref,
                 *, c, sc, layer, group):
    j = pl.program_id(1)

    @pl.when(j == 0)
    def _():
        s_ref[...] = s0_ref[...]

    lbl = lbl_ref[...]
    e = jnp.exp(lbl - jnp.max(lbl, axis=0, keepdims=True))
    sm = e / jnp.sum(e, axis=0, keepdims=True)
    cum = sm[0:1, :]
    for r in range(1, layer + 1):
        cum = cum + sm[r:r + 1, :]
    lb = cum - sm[0:1, :]

    oml = 1.0 - lb
    ri = lax.broadcasted_iota(jnp.int32, (c, c), 0)
    ci = lax.broadcasted_iota(jnp.int32, (c, c), 1)
    tril = jnp.where(ri >= ci, 1.0, 0.0).astype(BF16)
    rowi = lax.broadcasted_iota(jnp.int32, (sc, 1), 0)
    lane_s = lax.broadcasted_iota(jnp.int32, (SUBLANES, sc), 1)
    e_r = lax.broadcasted_iota(jnp.int32, (DH, DH), 0)
    e_c = lax.broadcasted_iota(jnp.int32, (DH, DH), 1)
    eye = e_r == e_c
    for gi in range(group):
        zf = zf_ref[gi]
        ez = jnp.exp(-jnp.abs(zf))
        big = 1.0 / (1.0 + ez)
        small = ez * big
        pos = zf >= 0.0
        logf = jnp.log(lb + oml * jnp.where(pos, big, small))
        kk = oml * jnp.where(pos, small, big)

        p0 = logf.astype(BF16)
        r1 = logf - p0.astype(F32)
        p1 = r1.astype(BF16)
        p2 = (r1 - p1.astype(F32)).astype(BF16)
        bcum = (jnp.dot(tril, p0, preferred_element_type=F32)
                + jnp.dot(tril, p1, preferred_element_type=F32)
                + jnp.dot(tril, p2, preferred_element_type=F32))

        for h in range(HEADS):
            sl = slice(h * DH, (h + 1) * DH)
            bh = bcum[:, sl]
            qh = zq_ref[gi, :, sl] * (DH ** -0.5)
            kh = kk[:, sl]
            vh = zi_ref[gi, :, sl]
            s_h = s_ref[gi, h]
            o_inter = _bdot(qh * jnp.exp(bh), s_h)
            blocks = []
            for blk in range(c // sc):
                r0 = blk * sc
                b_i = bh[r0:r0 + sc]
                q_i = qh[r0:r0 + sc]
                k_i = kh[r0:r0 + sc]
                v_i = vh[r0:r0 + sc]
                o_i = jnp.zeros((sc, DH), F32)
                if blk > 0:
                    ref_row = bh[r0 - 1:r0, :]
                    a_i = q_i * jnp.exp(b_i - ref_row)
                    k_prev = kh[0:r0] * jnp.exp(ref_row - bh[0:r0])
                    o_i = _bdot(_bdot_nt(a_i, k_prev), vh[0:r0])
                b2_i = b_i * LOG2_E
                att = [jnp.zeros((SUBLANES, sc), F32) for _ in range(sc // SUBLANES)]
                for s in range(sc):
                    lo = (s // SUBLANES) * SUBLANES
                    dec = jnp.exp2(jnp.where(rowi[lo:] >= s, b2_i[lo:] - b2_i[s:s + 1, :], NEG))
                    col = jnp.sum(q_i[lo:] * dec * k_i[s:s + 1, :], axis=1, keepdims=True)
                    for pi in range(lo // SUBLANES, sc // SUBLANES):
                        piece = col[pi * SUBLANES - lo:(pi + 1) * SUBLANES - lo]
                        att[pi] = jnp.where(lane_s == s, piece, att[pi])
                att = jnp.concatenate(att, axis=0) if len(att) > 1 else att[0]
                o_i = o_i + _bdot(att, v_i)
                blocks.append(o_i)
            o = o_inter + (jnp.concatenate(blocks, axis=0) if len(blocks) > 1 else blocks[0])
            b_last = bh[c - 1:c, :]
            dec_col = jnp.sum(jnp.where(eye, jnp.exp(b_last), 0.0), axis=1, keepdims=True)
            s_ref[gi, h] = dec_col * s_h + _bdot_tn(kh * jnp.exp(b_last - bh), vh)
            gate = zg_ref[gi, :, sl]
            out_ref[gi, :, sl] = _head_rms(o) * gn_ref[:, sl] * (gate * _sigmoid(gate))


def _hgrn(z_cd, lb_logits, gn_c, s0, b, l, layer):
    c = _chunk_len(l)
    sc = min(c, 16)
    nc = l // c
    grp = _seq_group(b, c)
    depth = lb_logits.shape[0]
    z3 = z_cd.reshape(b, l, z_cd.shape[-1])
    zspec = lambda col: pl.BlockSpec((grp, c, HW), lambda bi, j: (bi, j, col))
    st = pl.BlockSpec((grp, HEADS, DH, DH), lambda bi, j: (bi, 0, 0, 0))
    out, s_new = pl.pallas_call(
        functools.partial(_hgrn_kernel, c=c, sc=sc, layer=layer, group=grp),
        grid=(b // grp, nc),
        in_specs=[zspec(0), zspec(1), zspec(2), zspec(3), _full((depth, HW)), _full((1, HW)), st],
        out_specs=[pl.BlockSpec((grp, c, HW), lambda bi, j: (bi, j, 0)), st],
        out_shape=[jax.ShapeDtypeStruct((b, l, HW), F32),
                   jax.ShapeDtypeStruct((b, HEADS, DH, DH), F32)],
        compiler_params=_params(2),
        name="hgrn2",
    )(z3, z3, z3, z3, lb_logits, gn_c.reshape(1, -1), s0)
    return out.reshape(b * l, HW), s_new


def _s5_kernel(u_ref, are_ref, aim_ref, ldt_ref, bre_ref, bim_ref, cre_ref, cim_ref, d_ref,
               wglu_ref, bglu_ref, x0r_ref, x0i_ref,
               s_ref, xr_ref, xi_ref, ar_sc, ai_sc, bbr_sc, bbi_sc, bur_sc, bui_sc, *, ct):
    j = pl.program_id(1)
    ns = are_ref.shape[-1]
    nu = u_ref.shape[-1]
    hs = ns // 2
    hu = nu // 2
    rows = ct * SUBLANES

    @pl.when(j == 0)
    def _():
        a_re = are_ref[...]
        a_im = aim_ref[...]
        dt = jnp.exp(ldt_ref[...])
        mag = jnp.exp(dt * a_re)
        ar = mag * jnp.cos(dt * a_im)
        ai = mag * jnp.sin(dt * a_im)
        ar_sc[...] = jnp.broadcast_to(ar, (SUBLANES, ns))
        ai_sc[...] = jnp.broadcast_to(ai, (SUBLANES, ns))
        den = a_re * a_re + a_im * a_im
        nr = ar - 1.0
        zr = (nr * a_re + ai * a_im) / den
        zi = (ai * a_re - nr * a_im) / den
        for hg in range(2):
            us = slice(hg * hu, (hg + 1) * hu)
            ss = slice(hg * hs, (hg + 1) * hs)
            bbr_sc[us, :] = (zr[:, ss] * bre_ref[us, :] - zi[:, ss] * bim_ref[us, :]).astype(BF16)
            bbi_sc[us, :] = (zr[:, ss] * bim_ref[us, :] + zi[:, ss] * bre_ref[us, :]).astype(BF16)
        xr_ref[...] = x0r_ref[...]
        xi_ref[...] = x0i_ref[...]

    u = u_ref[...].reshape(rows, nu)
    ub = u.astype(BF16)
    for hg in range(2):
        us = slice(hg * hu, (hg + 1) * hu)
        ss = slice(hg * hs, (hg + 1) * hs)
        bur_sc[:, ss] = jnp.dot(ub[:, us], bbr_sc[us, :], preferred_element_type=F32)
        bui_sc[:, ss] = jnp.dot(ub[:, us], bbi_sc[us, :], preferred_element_type=F32)

    lane_chunk = 4 * LANES
    for lc in range(ns // lane_chunk):
        ls = slice(lc * lane_chunk, (lc + 1) * lane_chunk)
        ar = ar_sc[:, ls]
        ai = ai_sc[:, ls]

        def step(t, carry):
            xr, xi = carry
            r0 = pl.multiple_of(t * SUBLANES, SUBLANES)
            nxr = ar * xr - ai * xi + bur_sc[pl.ds(r0, SUBLANES), ls]
            nxi = ar * xi + ai * xr + bui_sc[pl.ds(r0, SUBLANES), ls]
            bur_sc[pl.ds(r0, SUBLANES), ls] = nxr
            bui_sc[pl.ds(r0, SUBLANES), ls] = nxi
            return nxr, nxi

        xr, xi = lax.fori_loop(0, ct, step, (xr_ref[:, ls], xi_ref[:, ls]))
        xr_ref[:, ls] = xr
        xi_ref[:, ls] = xi

    ys = []
    for hg in range(2):
        ss = slice(hg * hs, (hg + 1) * hs)
        ys.append(jnp.dot(bur_sc[:, ss].astype(BF16), cre_ref[ss, :], preferred_element_type=F32)
                  - jnp.dot(bui_sc[:, ss].astype(BF16), cim_ref[ss, :], preferred_element_type=F32))
    y = jnp.concatenate(ys, axis=1) + d_ref[...] * u
    a = 0.5 * y * (1.0 + jnp.tanh(math.sqrt(2.0 / math.pi) * (y + 0.044715 * (y * y * y))))
    s = a * _sigmoid(jnp.dot(a.astype(BF16), wglu_ref[...], preferred_element_type=F32) + bglu_ref[...])
    s_ref[...] = s.reshape(ct, SUBLANES, nu)


def _s5_block_diag(bmat, cmat):
    g, p, hgrp = bmat.shape
    gh = g // 2
    eye = jnp.eye(gh, dtype=F32)
    b4 = bmat.reshape(2, gh, p, hgrp)
    bc = jnp.einsum('agph,gk->aghkp', b4, eye).reshape(2 * gh * hgrp, gh * p)
    c4 = cmat.reshape(2, gh, hgrp, p)
    cc = jnp.einsum('aghp,gk->agpkh', c4, eye).reshape(2 * gh * p, gh * hgrp)
    return bc, cc


def _s5(u_tm, a_re, a_im, log_dt, b_re, b_im, c_re, c_im, d_skip, w_glu, b_glu, x0r, x0i):
    l, b, nu = u_tm.shape
    g, p = a_re.shape
    ns = g * p
    assert b % SUBLANES == 0 and nu == g * S5_GROUP and (g // 2) * S5_GROUP == MXU_DIM
    ct = _chunk_len(l)
    nct = l // ct
    bre_c, cre_c = _s5_block_diag(b_re, c_re)
    bim_c, cim_c = _s5_block_diag(b_im, c_im)
    ldt = jnp.broadcast_to(log_dt[:, None], (g, p)).reshape(1, ns)
    rows = ct * SUBLANES
    xspec = pl.BlockSpec((SUBLANES, ns), lambda bb, j: (bb, 0))
    s, xr, xi = pl.pallas_call(
        functools.partial(_s5_kernel, ct=ct),
        grid=(b // SUBLANES, nct),
        in_specs=[pl.BlockSpec((ct, SUBLANES, nu), lambda bb, j: (j, bb, 0)),
                  _full((1, ns)), _full((1, ns)), _full((1, ns)),
                  _full((nu, ns // 2)), _full((nu, ns // 2)),
                  _full((ns, nu // 2)), _full((ns, nu // 2)),
                  _full((1, nu)), _full((nu, nu)), _full((1, nu)), xspec, xspec],
        out_specs=[pl.BlockSpec((ct, SUBLANES, nu), lambda bb, j: (j, bb, 0)), xspec, xspec],
        out_shape=[jax.ShapeDtypeStruct((l, b, nu), F32),
                   jax.ShapeDtypeStruct((b, ns), F32), jax.ShapeDtypeStruct((b, ns), F32)],
        scratch_shapes=[pltpu.VMEM((SUBLANES, ns), F32), pltpu.VMEM((SUBLANES, ns), F32),
                        pltpu.VMEM((nu, ns // 2), BF16), pltpu.VMEM((nu, ns // 2), BF16),
                        pltpu.VMEM((rows, ns), F32), pltpu.VMEM((rows, ns), F32)],
        compiler_params=_params(2, VMEM_LIMIT),
        name="s5",
    )(u_tm, a_re.reshape(1, ns), a_im.reshape(1, ns), ldt, bre_c, bim_c,
      cre_c.astype(BF16), cim_c.astype(BF16), d_skip.reshape(1, nu), w_glu.astype(BF16),
      b_glu.reshape(1, nu), x0r.reshape(b, ns), x0i.reshape(b, ns))
    return s, xr.reshape(b, g, p), xi.reshape(b, g, p)


def _trunk(x, p, pos0, ab_state, cd_state, prm):
    conv0, c0, n0, m0, ret0 = ab_state
    hg0, x0r, x0i = cd_state
    b, l, d = x.shape
    assert l >= CONV_W - 1
    t = b * l
    tm = min(512, t)
    h = x.reshape(t, d)
    pp = p.reshape(p.shape[0], t, p.shape[-1])

    z_main, z_gate = _inproj(h, prm['norm_mix'][0], prm['w_ab'], (8 * HW, LANES), tm)
    conv_new = z_main.reshape(b, l, 8 * HW)[:, l - (CONV_W - 1):, :2 * HW]
    hm, c_new, n_new, m_new = _mlstm(z_main, z_gate, prm['conv_w_ab'][0], prm['conv_b_ab'][0],
                                     prm['b_gate_ab'][0], prm['gn_a'][0], conv0[0], c0[0], n0[0],
                                     m0[0], b, l)
    cos_t, sin_t = _rope_tables(l, pos0)
    hr, ret_new = _retention(z_main, cos_t, sin_t, ret0[0], b, l)
    h = _post(h, hm, hr, pp, prm['w_out_ab'], prm['norm_ff'][0], prm['w_ff1'], prm['w_ff2'],
              prm['norm_ple'][0], prm['w_ple_gate'], prm['w_ple_proj'], prm['norm_final'],
              layer=0, final=False, tm=tm)

    direct_tm = b == SUBLANES and l % tm == 0
    z_cd, su = _inproj(h, prm['norm_mix'][1], prm['w_cd'], (4 * HW, HW), tm,
                       time_major_last=(b, l) if direct_tm else None)
    o, hg_new = _hgrn(z_cd, prm['lb_logits'], prm['gn_c'][0], hg0[0], b, l, layer=1)
    u_tm = su.reshape(l, b, HW) if direct_tm else jnp.transpose(su.reshape(b, l, HW), (1, 0, 2))
    s_tm, xr, xi = _s5(u_tm, prm['s5_A_re'][0], prm['s5_A_im'][0], prm['s5_log_dt'][0],
                       prm['s5_B_re'][0], prm['s5_B_im'][0], prm['s5_C_re'][0], prm['s5_C_im'][0],
                       prm['s5_D'][0], prm['w_glu'][0], prm['b_glu'][0], x0r[0], x0i[0])
    s_in = s_tm.reshape(l, b * HW) if direct_tm else jnp.transpose(s_tm, (1, 0, 2)).reshape(t, HW)
    y = _post(h, o, s_in, pp, prm['w_out_cd'], prm['norm_ff'][1], prm['w_ff1'], prm['w_ff2'],
              prm['norm_ple'][1], prm['w_ple_gate'], prm['w_ple_proj'], prm['norm_final'],
              layer=1, final=True, tm=tm, b_time_major=(b, l) if direct_tm else None)

    ab_new = (conv_new[None], c_new[None], n_new[None], m_new[None], ret_new[None])
    cd_new = (hg_new[None], xr[None], xi[None])
    return y.reshape(b, l, d), ab_new, cd_new


def kernel(x_prompt, x_sample, state_mlstm_conv, state_mlstm_C, state_mlstm_n, state_mlstm_m, state_ret, state_hgrn, state_s5_re, state_s5_im, p_prompt, p_sample, norm_mix, norm_ff, norm_ple, norm_final, w_in_ab, b_gate_ab, conv_w_ab, conv_b_ab, gn_a, w_out_ab, w_in_cd, lb_logits, gn_c, s5_A_re, s5_A_im, s5_log_dt, s5_B_re, s5_B_im, s5_C_re, s5_C_im, s5_D, w_glu, b_glu, w_out_cd, w_ff1, w_ff2, w_ple_proj, w_ple_gate):
    assert norm_mix.shape[0] == 2, "two layers: (mLSTM || retention), (HGRN2 || S5)"
    w_ab = w_in_ab[0]
    gate0 = 4 * HW
    w_ab = jnp.concatenate([w_ab[:, :gate0], w_ab[:, gate0 + 2 * HEADS:],
                            w_ab[:, gate0:gate0 + 2 * HEADS],
                            jnp.zeros((w_ab.shape[0], LANES - 2 * HEADS), w_ab.dtype)], axis=1)
    prm = dict(norm_mix=norm_mix, norm_ff=norm_ff, norm_ple=norm_ple, norm_final=norm_final,
               w_ab=w_ab.astype(BF16), b_gate_ab=b_gate_ab, conv_w_ab=conv_w_ab, conv_b_ab=conv_b_ab,
               gn_a=gn_a, w_out_ab=w_out_ab[0].astype(BF16), w_cd=w_in_cd[0].astype(BF16),
               lb_logits=lb_logits, gn_c=gn_c, s5_A_re=s5_A_re, s5_A_im=s5_A_im,
               s5_log_dt=s5_log_dt, s5_B_re=s5_B_re, s5_B_im=s5_B_im, s5_C_re=s5_C_re,
               s5_C_im=s5_C_im, s5_D=s5_D, w_glu=w_glu, b_glu=b_glu,
               w_out_cd=w_out_cd[0].astype(BF16), w_ff1=w_ff1.astype(BF16), w_ff2=w_ff2.astype(BF16),
               w_ple_proj=w_ple_proj.astype(BF16), w_ple_gate=w_ple_gate.astype(BF16))

    bp, lp, _ = x_prompt.shape
    z = lambda *s: jnp.zeros(s, F32)
    zero_ab = (z(1, bp, CONV_W - 1, 2 * HW), z(1, bp, HEADS, DH, DH), z(1, bp, HEADS, DH),
               z(1, bp, HEADS), z(1, bp, HEADS, DH, DH))
    zero_cd = (z(1, bp, HEADS, DH, DH),) + (z(*((1, bp) + s5_A_re.shape[1:])),) * 2
    y_p, ab_p, cd_p = _trunk(x_prompt, p_prompt, 0, zero_ab, zero_cd, prm)
    y_s, ab_s, cd_s = _trunk(x_sample, p_sample, PAST_LEN,
                             (state_mlstm_conv, state_mlstm_C, state_mlstm_n, state_mlstm_m, state_ret),
                             (state_hgrn, state_s5_re, state_s5_im), prm)
    return (y_p, y_s,
            ab_p[0], ab_s[0], ab_p[1], ab_s[1], ab_p[2], ab_s[2], ab_p[3], ab_s[3], ab_p[4], ab_s[4],
            cd_p[0], cd_s[0], cd_p[1], cd_s[1], cd_p[2], cd_s[2])
```

```python
import functools
import math

import jax
import jax.numpy as jnp
from jax import lax
from jax.experimental import pallas as pl
from jax.experimental.pallas import tpu as pltpu

F32 = jnp.float32
BF16 = jnp.bfloat16

EPS = 1e-6
NEG = -1e30
LOG2_E = math.log2(math.e)
ROPE_BASE = 10000.0
PAST_LEN = 16384
CHUNK = 64
HEADS = 4
DH = 128
HW = HEADS * DH
CONV_W = 4
S5_GROUP = 16
S5_STATE = 64
SUBLANES = 8
LANES = 128
MXU_DIM = 256
VMEM_LIMIT = 56 * 1024 * 1024


def _params(n_axes, vmem=None):
    return pltpu.CompilerParams(dimension_semantics=("arbitrary",) * n_axes, vmem_limit_bytes=vmem)


def _full(shape):
    return pl.BlockSpec(shape, lambda *_: (0,) * len(shape))


def _bdot(a, b):
    return jnp.dot(a.astype(BF16), b.astype(BF16), preferred_element_type=F32)


def _bdot_nt(a, b):
    return lax.dot_general(a.astype(BF16), b.astype(BF16), (((1,), (1,)), ((), ())),
                           preferred_element_type=F32)


def _bdot_tn(a, b):
    return lax.dot_general(a.astype(BF16), b.astype(BF16), (((0,), (0,)), ((), ())),
                           preferred_element_type=F32)


def _sigmoid(x):
    return 1.0 / (1.0 + jnp.exp(-x))


def _log_sigmoid(x):
    return jnp.minimum(x, 0.0) - jnp.log(1.0 + jnp.exp(-jnp.abs(x)))


def _rms(x, g):
    return x * lax.rsqrt(jnp.mean(x * x, axis=-1, keepdims=True) + EPS) * g


def _head_rms(x):
    return x * lax.rsqrt(jnp.mean(x * x, axis=-1, keepdims=True) + EPS)


def _chunk_len(length):
    return CHUNK if length % CHUNK == 0 else length


def _round_robin(gens):
    gens = list(gens)
    while gens:
        alive = []
        for g in gens:
            try:
                next(g)
                alive.append(g)
            except StopIteration:
                pass
        gens = alive


def _inproj_kernel(x_ref, g_ref, w_ref, *out_refs):
    hn = _rms(x_ref[...], g_ref[...]).astype(BF16)
    off = 0
    for o_ref in out_refs:
        n = o_ref.shape[-1]
        for n0 in range(0, n, HW):
            nn = min(HW, n - n0)
            o_ref[:, n0:n0 + nn] = jnp.dot(hn, w_ref[:, off + n0:off + n0 + nn],
                                           preferred_element_type=F32)
        off += n


def _inproj(h, g, w, widths, tm, time_major_last=None):
    t, d = h.shape
    n = w.shape[1]
    assert sum(widths) == n and t % tm == 0
    out_shape = [jax.ShapeDtypeStruct((t, wd), F32) for wd in widths]
    out_specs = [pl.BlockSpec((tm, wd), lambda i: (i, 0)) for wd in widths]
    if time_major_last is not None:
        b, l = time_major_last
        assert l % tm == 0
        nl = l // tm
        wd = widths[-1]
        out_shape[-1] = jax.ShapeDtypeStruct((l, b * wd), F32)
        out_specs[-1] = pl.BlockSpec((tm, wd), lambda i: (i % nl, i // nl))
    return pl.pallas_call(
        _inproj_kernel,
        grid=(t // tm,),
        in_specs=[pl.BlockSpec((tm, d), lambda i: (i, 0)), _full((1, d)), _full((d, n))],
        out_specs=out_specs,
        out_shape=out_shape,
        compiler_params=_params(1, VMEM_LIMIT),
        name="inproj",
    )(h, g.reshape(1, d), w)


def _post_kernel(h_ref, ma_ref, mb_ref, p_ref, wo_ref, gff_ref, w1_ref, w2_ref, gple_ref, wg_ref,
                 wp_ref, gfin_ref, o_ref, *, final, ff_chunk):
    half = ma_ref.shape[-1]
    h = h_ref[...]
    h = h + (jnp.dot(ma_ref[...].astype(BF16), wo_ref[0:half, :], preferred_element_type=F32)
             + jnp.dot(mb_ref[...].astype(BF16), wo_ref[half:2 * half, :], preferred_element_type=F32))
    hn = _rms(h, gff_ref[...]).astype(BF16)
    d_ff = w1_ref.shape[1]
    acc = jnp.zeros_like(h)
    for f0 in range(0, d_ff, ff_chunk):
        a = jnp.dot(hn, w1_ref[:, f0:f0 + ff_chunk], preferred_element_type=F32)
        a = jnp.square(jnp.maximum(a, 0.0))
        acc = acc + jnp.dot(a.astype(BF16), w2_ref[f0:f0 + ff_chunk, :], preferred_element_type=F32)
    h = h + acc
    gate = _sigmoid(jnp.dot(_rms(h, gple_ref[...]).astype(BF16), wg_ref[...],
                            preferred_element_type=F32))
    h = h + gate * jnp.dot(p_ref[...].astype(BF16), wp_ref[...], preferred_element_type=F32)
    o_ref[...] = _rms(h, gfin_ref[...]) if final else h


def _post(h, mix_a, mix_b, p, wo, gff, w1, w2, gple, wg, wp, gfin, *, layer, final, tm,
          b_time_major=None):
    t, d = h.shape
    half = mix_a.shape[-1]
    pd = p.shape[-1]
    d_ff = w1.shape[-1]
    assert t % tm == 0
    row = lambda i: (i, 0)
    mb_spec = pl.BlockSpec((tm, half), row)
    if b_time_major is not None:
        _, l = b_time_major
        assert l % tm == 0
        nl = l // tm
        mb_spec = pl.BlockSpec((tm, half), lambda i: (i % nl, i // nl))
    lw = lambda r, cdim: pl.BlockSpec((None, r, cdim), lambda i: (layer, 0, 0))
    return pl.pallas_call(
        functools.partial(_post_kernel, final=final, ff_chunk=1024),
        grid=(t // tm,),
        in_specs=[pl.BlockSpec((tm, d), row), pl.BlockSpec((tm, half), row), mb_spec,
                  pl.BlockSpec((None, tm, pd), lambda i: (layer, i, 0)),
                  _full((d, d)), _full((1, d)), lw(d, d_ff), lw(d_ff, d), _full((1, d)),
                  lw(d, d), lw(pd, d), _full((1, d))],
        out_specs=pl.BlockSpec((tm, d), row),
        out_shape=jax.ShapeDtypeStruct((t, d), F32),
        compiler_params=_params(1, VMEM_LIMIT),
        name="post",
    )(h, mix_a, mix_b, p, wo, gff.reshape(1, d), w1, w2, gple.reshape(1, d), wg, wp,
      gfin.reshape(1, d))


def _mlstm_kernel(zq_ref, zk_ref, zv_ref, zo_ref, zg_ref, cw_ref, cb_ref, bg_ref, gn_ref,
                  conv0_ref, c0_ref, n0_ref, m0_ref,
                  out_ref, c_ref, n_ref, m_ref, ext_ref, qk_ref, gates_ref, *, c, group):
    j = pl.program_id(1)
    tail = CONV_W - 1

    @pl.when(j == 0)
    def _():
        c_ref[...] = c0_ref[...]
        n_ref[...] = n0_ref[...]
        m_ref[...] = m0_ref[...]
        for gi in range(group):
            ext_ref[gi, 0:SUBLANES, :] = jnp.zeros((SUBLANES, 2 * HW), F32)
            ext_ref[gi, SUBLANES - tail:SUBLANES, :] = conv0_ref[gi]

    ri = lax.broadcasted_iota(jnp.int32, (c, c), 0)
    ci = lax.broadcasted_iota(jnp.int32, (c, c), 1)
    eye = ri == ci
    tril = ri >= ci
    lane = lax.broadcasted_iota(jnp.int32, (c, LANES), 1)
    bg = bg_ref[...]

    def lsum(x):
        return jnp.broadcast_to(jnp.sum(x, axis=1, keepdims=True), (c, DH))

    def unit(gi, h):
        sl = slice(h * DH, (h + 1) * DH)
        gates = gates_ref[gi]
        i_col = lsum(jnp.where(lane == h, gates, 0.0))
        f_col = lsum(jnp.where(lane == HEADS + h, gates, 0.0))
        yield
        b_row = jnp.sum(jnp.where(ri <= ci, f_col[:, :c], 0.0), axis=0, keepdims=True)
        b_col = lsum(jnp.where(eye, b_row, 0.0))
        i_row = jnp.sum(jnp.where(eye, i_col[:, :c], 0.0), axis=0, keepdims=True)
        yield
        m_prev = m_ref[gi, h:h + 1, :]
        dmat = jnp.where(tril, b_col[:, :c] - b_row + i_row, NEG)
        inter = b_col + m_prev
        row_max = jnp.broadcast_to(jnp.max(dmat, axis=1, keepdims=True), (c, DH))
        qh = qk_ref[gi, :, sl]
        kh = qk_ref[gi, :, HW + h * DH:HW + (h + 1) * DH] * (DH ** -0.5)
        vh = zv_ref[gi, :, sl]
        c_h = c_ref[gi, h]
        n_h = n_ref[gi, h:h + 1, :]
        s_raw = _bdot_nt(qh, kh)
        q_c = _bdot(qh, c_h)
        q_n = lsum(qh * n_h)
        yield
        m_t = jnp.maximum(inter, row_max)
        w_intra = jnp.exp(dmat - m_t[:, :c])
        w_inter = jnp.exp(inter - m_t)
        s = s_raw * w_intra
        s_v = _bdot(s, vh)
        s_sum = lsum(s)
        m_new = m_t[c - 1:c, :]
        b_last = b_col[c - 1:c, :]
        w_last = jnp.exp(b_last - b_col + i_col - m_new)
        decay = jnp.exp(b_last + m_prev - m_new)
        kw = w_last * kh
        kw_v = _bdot_tn(kw, vh)
        yield
        num = w_inter * q_c + s_v
        den = w_inter * q_n + s_sum
        hh = num / jnp.maximum(jnp.abs(den), jnp.exp(-m_t))
        c_ref[gi, h] = decay * c_h + kw_v
        n_ref[gi, h:h + 1, :] = decay * n_h + jnp.sum(kw, axis=0, keepdims=True)
        m_ref[gi, h:h + 1, :] = m_new
        hh = _sigmoid(zo_ref[gi, :, sl]) * hh
        out_ref[gi, :, sl] = _head_rms(hh) * gn_ref[:, sl]

    for gi in range(group):
        ext_ref[gi, SUBLANES:SUBLANES + c, 0:HW] = zq_ref[gi]
        ext_ref[gi, SUBLANES:SUBLANES + c, HW:2 * HW] = zk_ref[gi]
        conv = cb_ref[...]
        for jj in range(CONV_W):
            r0 = SUBLANES - tail + jj
            conv = conv + cw_ref[jj:jj + 1, :] * ext_ref[gi, r0:r0 + c, :]
        ext_ref[gi, 0:SUBLANES, :] = ext_ref[gi, c:c + SUBLANES, :]
        qk_ref[gi] = conv * _sigmoid(conv)
        gb = zg_ref[gi] + bg
        gates_ref[gi] = jnp.where(lane < HEADS, gb, _log_sigmoid(gb))
    _round_robin([unit(gi, h) for gi in range(group) for h in range(HEADS)])


def _seq_group(b, c):
    rows = 512
    group = max(1, min(b, rows // c, 2 * SUBLANES))
    assert b % group == 0
    return group


def _mlstm(z_main, z_gate, conv_w, conv_b, b_gate, gn_a, conv0, c0, n0, m0, b, l):
    c = _chunk_len(l)
    nc = l // c
    grp = _seq_group(b, c)
    z3 = z_main.reshape(b, l, z_main.shape[-1])
    zspec = lambda col: pl.BlockSpec((grp, c, HW), lambda bi, j: (bi, j, col))
    st = lambda shape: pl.BlockSpec((grp,) + shape, lambda bi, j: (bi,) + (0,) * len(shape))
    m0b = jnp.broadcast_to(m0[:, :, None], (b, HEADS, DH))
    out, c_new, n_new, m_new = pl.pallas_call(
        functools.partial(_mlstm_kernel, c=c, group=grp),
        grid=(b // grp, nc),
        in_specs=[zspec(0), zspec(1), zspec(2), zspec(3),
                  pl.BlockSpec((grp, c, LANES), lambda bi, j: (bi, j, 0)),
                  _full((CONV_W, 2 * HW)), _full((1, 2 * HW)), _full((1, LANES)), _full((1, HW)),
                  st((CONV_W - 1, 2 * HW)), st((HEADS, DH, DH)), st((HEADS, DH)), st((HEADS, DH))],
        out_specs=[pl.BlockSpec((grp, c, HW), lambda bi, j: (bi, j, 0)),
                   st((HEADS, DH, DH)), st((HEADS, DH)), st((HEADS, DH))],
        out_shape=[jax.ShapeDtypeStruct((b, l, HW), F32),
                   jax.ShapeDtypeStruct((b, HEADS, DH, DH), F32),
                   jax.ShapeDtypeStruct((b, HEADS, DH), F32),
                   jax.ShapeDtypeStruct((b, HEADS, DH), F32)],
        scratch_shapes=[pltpu.VMEM((grp, c + SUBLANES, 2 * HW), F32),
                        pltpu.VMEM((grp, c, 2 * HW), F32), pltpu.VMEM((grp, c, LANES), F32)],
        compiler_params=_params(2),
        name="mlstm",
    )(z3, z3, z3, z3, z_gate.reshape(b, l, LANES), conv_w, conv_b.reshape(1, -1),
      jnp.pad(b_gate, (0, LANES - 2 * HEADS)).reshape(1, LANES), gn_a.reshape(1, -1),
      conv0, c0, n0, m0b)
    return out.reshape(b * l, HW), c_new, n_new, m_new[:, :, 0]


def _rope_table_kernel(inv_ref, sign_ref, cos_ref, sin_ref, *, pos0, rows):
    i = pl.program_id(0)
    pos = (pos0 + i * rows + lax.broadcasted_iota(jnp.int32, (rows, LANES), 0)).astype(F32)
    ang = pos * inv_ref[...]
    cos_ref[...] = jnp.cos(ang)
    sin_ref[...] = jnp.sin(ang) * sign_ref[...]


def _rope_tables(l, pos0):
    half = DH // 2
    inv = ROPE_BASE ** (-jnp.arange(half, dtype=F32) / half)
    inv2 = jnp.concatenate([inv, inv]).reshape(1, DH)
    sign = jnp.concatenate([-jnp.ones((half,), F32), jnp.ones((half,), F32)]).reshape(1, DH)
    rows = min(l, 512)
    assert l % rows == 0
    return pl.pallas_call(
        functools.partial(_rope_table_kernel, pos0=pos0, rows=rows),
        grid=(l // rows,),
        in_specs=[_full((1, DH)), _full((1, DH))],
        out_specs=[pl.BlockSpec((rows, DH), lambda i: (i, 0))] * 2,
        out_shape=[jax.ShapeDtypeStruct((l, DH), F32)] * 2,
        compiler_params=_params(1),
        name="rope_table",
    )(inv2, sign)


def _ret_kernel(zq_ref, zk_ref, zv_ref, zg_ref, cos_ref, sin_ref, s0_ref, out_ref, s_ref,
                *, c, group):
    j = pl.program_id(1)

    @pl.when(j == 0)
    def _():
        s_ref[...] = s0_ref[...]

    cosf = cos_ref[...]
    sinf = sin_ref[...]
    ti = lax.broadcasted_iota(jnp.int32, (c, c), 0)
    si = lax.broadcasted_iota(jnp.int32, (c, c), 1)
    rel = jnp.maximum(ti - si, 0).astype(F32)
    tcol = lax.broadcasted_iota(jnp.int32, (c, 1), 0).astype(F32)

    def rope(x):
        return x * cosf + pltpu.roll(x, DH // 2, axis=1) * sinf

    def unit(gi, h, decay, inter, kdecay, cdecay):
        sl = slice(h * DH, (h + 1) * DH)
        qr = rope(zq_ref[gi, :, sl])
        kr = rope(zk_ref[gi, :, sl]) * (DH ** -0.5)
        yield
        vh = zv_ref[gi, :, sl]
        s_h = s_ref[gi, h]
        qk = _bdot_nt(qr, kr)
        q_s = _bdot(qr, s_h)
        k_v = _bdot_tn(kr * kdecay, vh)
        yield
        o = q_s * inter + _bdot(qk * decay, vh)
        s_ref[gi, h] = cdecay * s_h + k_v
        yield
        gate = zg_ref[gi, :, sl]
        out_ref[gi, :, sl] = _head_rms(o) * (gate * _sigmoid(gate))

    units = []
    for h in range(HEADS):
        lg = math.log1p(-(2.0 ** (-5.0 - h)))
        decay = jnp.where(ti >= si, jnp.exp(rel * lg), 0.0)
        inter = jnp.exp((tcol + 1.0) * lg)
        kdecay = jnp.exp((c - 1.0 - tcol) * lg)
        cdecay = math.exp(c * lg)
        units += [unit(gi, h, decay, inter, kdecay, cdecay) for gi in range(group)]
    _round_robin(units)


def _retention(z_main, cos_t, sin_t, s0, b, l):
    c = _chunk_len(l)
    nc = l // c
    grp = _seq_group(b, c)
    z3 = z_main.reshape(b, l, z_main.shape[-1])
    zspec = lambda col: pl.BlockSpec((grp, c, HW), lambda bi, j: (bi, j, col))
    st = pl.BlockSpec((grp, HEADS, DH, DH), lambda bi, j: (bi, 0, 0, 0))
    tab = pl.BlockSpec((c, DH), lambda bi, j: (j, 0))
    out, s_new = pl.pallas_call(
        functools.partial(_ret_kernel, c=c, group=grp),
        grid=(b // grp, nc),
        in_specs=[zspec(4), zspec(5), zspec(6), zspec(7), tab, tab, st],
        out_specs=[pl.BlockSpec((grp, c, HW), lambda bi, j: (bi, j, 0)), st],
        out_shape=[jax.ShapeDtypeStruct((b, l, HW), F32),
                   jax.ShapeDtypeStruct((b, HEADS, DH, DH), F32)],
        compiler_params=_params(2),
        name="retention",
    )(z3, z3, z3, z3, cos_t, sin_t, s0)
    return out.reshape(b * l, HW), s_new


def _hgrn_kernel(zq_ref, zf_ref, zi_ref, zg_ref, lbl_ref, gn_ref, s0_ref, out_ref, s_ref,
                 kk_ref, bcum_ref, *, c, sc, layer, group):
    j = pl.program_id(1)

    @pl.when(j == 0)
    def _():
        s_ref[...] = s0_ref[...]

    lbl = lbl_ref[...]
    e = jnp.exp(lbl - jnp.max(lbl, axis=0, keepdims=True))
    sm = e / jnp.sum(e, axis=0, keepdims=True)
    cum = sm[0:1, :]
    for r in range(1, layer + 1):
        cum = cum + sm[r:r + 1, :]
    lb = cum - sm[0:1, :]

    oml = 1.0 - lb
    ri = lax.broadcasted_iota(jnp.int32, (c, c), 0)
    ci = lax.broadcasted_iota(jnp.int32, (c, c), 1)
    tril = jnp.where(ri >= ci, 1.0, 0.0).astype(BF16)
    rowi = lax.broadcasted_iota(jnp.int32, (sc, 1), 0)
    lane_s = lax.broadcasted_iota(jnp.int32, (SUBLANES, sc), 1)
    e_r = lax.broadcasted_iota(jnp.int32, (DH, DH), 0)
    e_c = lax.broadcasted_iota(jnp.int32, (DH, DH), 1)
    eye = e_r == e_c
    def unit(gi, h):
        sl = slice(h * DH, (h + 1) * DH)
        bh = bcum_ref[gi, :, sl]
        qh = zq_ref[gi, :, sl] * (DH ** -0.5)
        kh = kk_ref[gi, :, sl]
        vh = zi_ref[gi, :, sl]
        s_h = s_ref[gi, h]
        b_last = bh[c - 1:c, :]
        o_inter = _bdot(qh * jnp.exp(bh), s_h)
        k_v = _bdot_tn(kh * jnp.exp(b_last - bh), vh)
        dec_col = jnp.sum(jnp.where(eye, jnp.exp(b_last), 0.0), axis=1, keepdims=True)
        yield
        s_ref[gi, h] = dec_col * s_h + k_v
        blocks = []
        for blk in range(c // sc):
            r0 = blk * sc
            b_i = bh[r0:r0 + sc]
            q_i = qh[r0:r0 + sc]
            k_i = kh[r0:r0 + sc]
            v_i = vh[r0:r0 + sc]
            att_prev = None
            if blk > 0:
                ref_row = bh[r0 - 1:r0, :]
                a_i = q_i * jnp.exp(b_i - ref_row)
                k_prev = kh[0:r0] * jnp.exp(ref_row - bh[0:r0])
                att_prev = _bdot_nt(a_i, k_prev)
            b2_i = b_i * LOG2_E
            cols = []
            for s in range(sc):
                lo = (s // SUBLANES) * SUBLANES
                dec = jnp.exp2(jnp.where(rowi[lo:] >= s, b2_i[lo:] - b2_i[s:s + 1, :], NEG))
                cols.append(jnp.sum(q_i[lo:] * dec * k_i[s:s + 1, :], axis=1, keepdims=True))
            yield
            att = [jnp.zeros((SUBLANES, sc), F32) for _ in range(sc // SUBLANES)]
            for s in range(sc):
                lo = (s // SUBLANES) * SUBLANES
                for pi in range(lo // SUBLANES, sc // SUBLANES):
                    piece = cols[s][pi * SUBLANES - lo:(pi + 1) * SUBLANES - lo]
                    att[pi] = jnp.where(lane_s == s, piece, att[pi])
            att = jnp.concatenate(att, axis=0) if len(att) > 1 else att[0]
            o_i = _bdot(att, v_i)
            if att_prev is not None:
                o_i = o_i + _bdot(att_prev, vh[0:r0])
            blocks.append(o_i)
        yield
        o = o_inter + (jnp.concatenate(blocks, axis=0) if len(blocks) > 1 else blocks[0])
        gate = zg_ref[gi, :, sl]
        out_ref[gi, :, sl] = _head_rms(o) * gn_ref[:, sl] * (gate * _sigmoid(gate))

    for gi in range(group):
        zf = zf_ref[gi]
        ez = jnp.exp(-jnp.abs(zf))
        big = 1.0 / (1.0 + ez)
        small = ez * big
        pos = zf >= 0.0
        logf = jnp.log(lb + oml * jnp.where(pos, big, small))
        kk_ref[gi] = oml * jnp.where(pos, small, big)

        p0 = logf.astype(BF16)
        r1 = logf - p0.astype(F32)
        p1 = r1.astype(BF16)
        p2 = (r1 - p1.astype(F32)).astype(BF16)
        bcum_ref[gi] = (jnp.dot(tril, p0, preferred_element_type=F32)
                        + jnp.dot(tril, p1, preferred_element_type=F32)
                        + jnp.dot(tril, p2, preferred_element_type=F32))
    _round_robin([unit(gi, h) for gi in range(group) for h in range(HEADS)])


def _hgrn(z_cd, lb_logits, gn_c, s0, b, l, layer):
    c = _chunk_len(l)
    sc = min(c, 16)
    nc = l // c
    grp = _seq_group(b, c)
    depth = lb_logits.shape[0]
    z3 = z_cd.reshape(b, l, z_cd.shape[-1])
    zspec = lambda col: pl.BlockSpec((grp, c, HW), lambda bi, j: (bi, j, col))
    st = pl.BlockSpec((grp, HEADS, DH, DH), lambda bi, j: (bi, 0, 0, 0))
    out, s_new = pl.pallas_call(
        functools.partial(_hgrn_kernel, c=c, sc=sc, layer=layer, group=grp),
        grid=(b // grp, nc),
        in_specs=[zspec(0), zspec(1), zspec(2), zspec(3), _full((depth, HW)), _full((1, HW)), st],
        out_specs=[pl.BlockSpec((grp, c, HW), lambda bi, j: (bi, j, 0)), st],
        out_shape=[jax.ShapeDtypeStruct((b, l, HW), F32),
                   jax.ShapeDtypeStruct((b, HEADS, DH, DH), F32)],
        scratch_shapes=[pltpu.VMEM((grp, c, HW), F32), pltpu.VMEM((grp, c, HW), F32)],
        compiler_params=_params(2),
        name="hgrn2",
    )(z3, z3, z3, z3, lb_logits, gn_c.reshape(1, -1), s0)
    return out.reshape(b * l, HW), s_new


def _s5_kernel(u_ref, are_ref, aim_ref, ldt_ref, bre_ref, bim_ref, cre_ref, cim_ref, d_ref,
               wglu_ref, bglu_ref, x0r_ref, x0i_ref,
               s_ref, xr_ref, xi_ref, ar_sc, ai_sc, bbr_sc, bbi_sc, bur_sc, bui_sc, *, ct):
    j = pl.program_id(1)
    ns = are_ref.shape[-1]
    nu = u_ref.shape[-1]
    hs = ns // 2
    hu = nu // 2
    rows = ct * SUBLANES

    @pl.when(j == 0)
    def _():
        a_re = are_ref[...]
        a_im = aim_ref[...]
        dt = jnp.exp(ldt_ref[...])
        mag = jnp.exp(dt * a_re)
        ar = mag * jnp.cos(dt * a_im)
        ai = mag * jnp.sin(dt * a_im)
        ar_sc[...] = jnp.broadcast_to(ar, (SUBLANES, ns))
        ai_sc[...] = jnp.broadcast_to(ai, (SUBLANES, ns))
        den = a_re * a_re + a_im * a_im
        nr = ar - 1.0
        zr = (nr * a_re + ai * a_im) / den
        zi = (ai * a_re - nr * a_im) / den
        for hg in range(2):
            us = slice(hg * hu, (hg + 1) * hu)
            ss = slice(hg * hs, (hg + 1) * hs)
            bbr_sc[us, :] = (zr[:, ss] * bre_ref[us, :] - zi[:, ss] * bim_ref[us, :]).astype(BF16)
            bbi_sc[us, :] = (zr[:, ss] * bim_ref[us, :] + zi[:, ss] * bre_ref[us, :]).astype(BF16)
        xr_ref[...] = x0r_ref[...]
        xi_ref[...] = x0i_ref[...]

    u = u_ref[...].reshape(rows, nu)
    ub = u.astype(BF16)
    for hg in range(2):
        us = slice(hg * hu, (hg + 1) * hu)
        ss = slice(hg * hs, (hg + 1) * hs)
        bur_sc[:, ss] = jnp.dot(ub[:, us], bbr_sc[us, :], preferred_element_type=F32)
        bui_sc[:, ss] = jnp.dot(ub[:, us], bbi_sc[us, :], preferred_element_type=F32)

    lane_chunk = 4 * LANES
    for lc in range(ns // lane_chunk):
        ls = slice(lc * lane_chunk, (lc + 1) * lane_chunk)
        ar = ar_sc[:, ls]
        ai = ai_sc[:, ls]

        def step(t, carry):
            xr, xi = carry
            r0 = pl.multiple_of(t * SUBLANES, SUBLANES)
            nxr = ar * xr - ai * xi + bur_sc[pl.ds(r0, SUBLANES), ls]
            nxi = ar * xi + ai * xr + bui_sc[pl.ds(r0, SUBLANES), ls]
            bur_sc[pl.ds(r0, SUBLANES), ls] = nxr
            bui_sc[pl.ds(r0, SUBLANES), ls] = nxi
            return nxr, nxi

        xr, xi = lax.fori_loop(0, ct, step, (xr_ref[:, ls], xi_ref[:, ls]))
        xr_ref[:, ls] = xr
        xi_ref[:, ls] = xi

    ys = []
    for hg in range(2):
        ss = slice(hg * hs, (hg + 1) * hs)
        ys.append(jnp.dot(bur_sc[:, ss].astype(BF16), cre_ref[ss, :], preferred_element_type=F32)
                  - jnp.dot(bui_sc[:, ss].astype(BF16), cim_ref[ss, :], preferred_element_type=F32))
    y = jnp.concatenate(ys, axis=1) + d_ref[...] * u
    a = 0.5 * y * (1.0 + jnp.tanh(math.sqrt(2.0 / math.pi) * (y + 0.044715 * (y * y * y))))
    s = a * _sigmoid(jnp.dot(a.astype(BF16), wglu_ref[...], preferred_element_type=F32) + bglu_ref[...])
    s_ref[...] = s.reshape(ct, SUBLANES, nu)


def _s5_block_diag(bmat, cmat):
    g, p, hgrp = bmat.shape
    gh = g // 2
    eye = jnp.eye(gh, dtype=F32)
    b4 = bmat.reshape(2, gh, p, hgrp)
    bc = jnp.einsum('agph,gk->aghkp', b4, eye).reshape(2 * gh * hgrp, gh * p)
    c4 = cmat.reshape(2, gh, hgrp, p)
    cc = jnp.einsum('aghp,gk->agpkh', c4, eye).reshape(2 * gh * p, gh * hgrp)
    return bc, cc


def _s5(u_tm, a_re, a_im, log_dt, b_re, b_im, c_re, c_im, d_skip, w_glu, b_glu, x0r, x0i):
    l, b, nu = u_tm.shape
    g, p = a_re.shape
    ns = g * p
    assert b % SUBLANES == 0 and nu == g * S5_GROUP and (g // 2) * S5_GROUP == MXU_DIM
    ct = _chunk_len(l)
    nct = l // ct
    bre_c, cre_c = _s5_block_diag(b_re, c_re)
    bim_c, cim_c = _s5_block_diag(b_im, c_im)
    ldt = jnp.broadcast_to(log_dt[:, None], (g, p)).reshape(1, ns)
    rows = ct * SUBLANES
    xspec = pl.BlockSpec((SUBLANES, ns), lambda bb, j: (bb, 0))
    s, xr, xi = pl.pallas_call(
        functools.partial(_s5_kernel, ct=ct),
        grid=(b // SUBLANES, nct),
        in_specs=[pl.BlockSpec((ct, SUBLANES, nu), lambda bb, j: (j, bb, 0)),
                  _full((1, ns)), _full((1, ns)), _full((1, ns)),
                  _full((nu, ns // 2)), _full((nu, ns // 2)),
                  _full((ns, nu // 2)), _full((ns, nu // 2)),
                  _full((1, nu)), _full((nu, nu)), _full((1, nu)), xspec, xspec],
        out_specs=[pl.BlockSpec((ct, SUBLANES, nu), lambda bb, j: (j, bb, 0)), xspec, xspec],
        out_shape=[jax.ShapeDtypeStruct((l, b, nu), F32),
                   jax.ShapeDtypeStruct((b, ns), F32), jax.ShapeDtypeStruct((b, ns), F32)],
        scratch_shapes=[pltpu.VMEM((SUBLANES, ns), F32), pltpu.VMEM((SUBLANES, ns), F32),
                        pltpu.VMEM((nu, ns // 2), BF16), pltpu.VMEM((nu, ns // 2), BF16),
                        pltpu.VMEM((rows, ns), F32), pltpu.VMEM((rows, ns), F32)],
        compiler_params=_params(2, VMEM_LIMIT),
        name="s5",
    )(u_tm, a_re.reshape(1, ns), a_im.reshape(1, ns), ldt, bre_c, bim_c,
      cre_c.astype(BF16), cim_c.astype(BF16), d_skip.reshape(1, nu), w_glu.astype(BF16),
      b_glu.reshape(1, nu), x0r.reshape(b, ns), x0i.reshape(b, ns))
    return s, xr.reshape(b, g, p), xi.reshape(b, g, p)


def _trunk(x, p, pos0, ab_state, cd_state, prm):
    conv0, c0, n0, m0, ret0 = ab_state
    hg0, x0r, x0i = cd_state
    b, l, d = x.shape
    assert l >= CONV_W - 1
    t = b * l
    tm = min(512, t)
    h = x.reshape(t, d)
    pp = p.reshape(p.shape[0], t, p.shape[-1])

    z_main, z_gate = _inproj(h, prm['norm_mix'][0], prm['w_ab'], (8 * HW, LANES), tm)
    conv_new = z_main.reshape(b, l, 8 * HW)[:, l - (CONV_W - 1):, :2 * HW]
    hm, c_new, n_new, m_new = _mlstm(z_main, z_gate, prm['conv_w_ab'][0], prm['conv_b_ab'][0],
                                     prm['b_gate_ab'][0], prm['gn_a'][0], conv0[0], c0[0], n0[0],
                                     m0[0], b, l)
    cos_t, sin_t = _rope_tables(l, pos0)
    hr, ret_new = _retention(z_main, cos_t, sin_t, ret0[0], b, l)
    h = _post(h, hm, hr, pp, prm['w_out_ab'], prm['norm_ff'][0], prm['w_ff1'], prm['w_ff2'],
              prm['norm_ple'][0], prm['w_ple_gate'], prm['w_ple_proj'], prm['norm_final'],
              layer=0, final=False, tm=tm)

    direct_tm = b == SUBLANES and l % tm == 0
    z_cd, su = _inproj(h, prm['norm_mix'][1], prm['w_cd'], (4 * HW, HW), tm,
                       time_major_last=(b, l) if direct_tm else None)
    o, hg_new = _hgrn(z_cd, prm['lb_logits'], prm['gn_c'][0], hg0[0], b, l, layer=1)
    u_tm = su.reshape(l, b, HW) if direct_tm else jnp.transpose(su.reshape(b, l, HW), (1, 0, 2))
    s_tm, xr, xi = _s5(u_tm, prm['s5_A_re'][0], prm['s5_A_im'][0], prm['s5_log_dt'][0],
                       prm['s5_B_re'][0], prm['s5_B_im'][0], prm['s5_C_re'][0], prm['s5_C_im'][0],
                       prm['s5_D'][0], prm['w_glu'][0], prm['b_glu'][0], x0r[0], x0i[0])
    s_in = s_tm.reshape(l, b * HW) if direct_tm else jnp.transpose(s_tm, (1, 0, 2)).reshape(t, HW)
    y = _post(h, o, s_in, pp, prm['w_out_cd'], prm['norm_ff'][1], prm['w_ff1'], prm['w_ff2'],
              prm['norm_ple'][1], prm['w_ple_gate'], prm['w_ple_proj'], prm['norm_final'],
              layer=1, final=True, tm=tm, b_time_major=(b, l) if direct_tm else None)

    ab_new = (conv_new[None], c_new[None], n_new[None], m_new[None], ret_new[None])
    cd_new = (hg_new[None], xr[None], xi[None])
    return y.reshape(b, l, d), ab_new, cd_new


def kernel(x_prompt, x_sample, state_mlstm_conv, state_mlstm_C, state_mlstm_n, state_mlstm_m, state_ret, state_hgrn, state_s5_re, state_s5_im, p_prompt, p_sample, norm_mix, norm_ff, norm_ple, norm_final, w_in_ab, b_gate_ab, conv_w_ab, conv_b_ab, gn_a, w_out_ab, w_in_cd, lb_logits, gn_c, s5_A_re, s5_A_im, s5_log_dt, s5_B_re, s5_B_im, s5_C_re, s5_C_im, s5_D, w_glu, b_glu, w_out_cd, w_ff1, w_ff2, w_ple_proj, w_ple_gate):
    assert norm_mix.shape[0] == 2, "two layers: (mLSTM || retention), (HGRN2 || S5)"
    w_ab = w_in_ab[0]
    gate0 = 4 * HW
    w_ab = jnp.concatenate([w_ab[:, :gate0], w_ab[:, gate0 + 2 * HEADS:],
                            w_ab[:, gate0:gate0 + 2 * HEADS],
                            jnp.zeros((w_ab.shape[0], LANES - 2 * HEADS), w_ab.dtype)], axis=1)
    prm = dict(norm_mix=norm_mix, norm_ff=norm_ff, norm_ple=norm_ple, norm_final=norm_final,
               w_ab=w_ab.astype(BF16), b_gate_ab=b_gate_ab, conv_w_ab=conv_w_ab, conv_b_ab=conv_b_ab,
               gn_a=gn_a, w_out_ab=w_out_ab[0].astype(BF16), w_cd=w_in_cd[0].astype(BF16),
               lb_logits=lb_logits, gn_c=gn_c, s5_A_re=s5_A_re, s5_A_im=s5_A_im,
               s5_log_dt=s5_log_dt, s5_B_re=s5_B_re, s5_B_im=s5_B_im, s5_C_re=s5_C_re,
               s5_C_im=s5_C_im, s5_D=s5_D, w_glu=w_glu, b_glu=b_glu,
               w_out_cd=w_out_cd[0].astype(BF16), w_ff1=w_ff1.astype(BF16), w_ff2=w_ff2.astype(BF16),
               w_ple_proj=w_ple_proj.astype(BF16), w_ple_gate=w_ple_gate.astype(BF16))

    bp, lp, _ = x_prompt.shape
    z = lambda *s: jnp.zeros(s, F32)
    zero_ab = (z(1, bp, CONV_W - 1, 2 * HW), z(1, bp, HEADS, DH, DH), z(1, bp, HEADS, DH),
               z(1, bp, HEADS), z(1, bp, HEADS, DH, DH))
    zero_cd = (z(1, bp, HEADS, DH, DH),) + (z(*((1, bp) + s5_A_re.shape[1:])),) * 2
    y_p, ab_p, cd_p = _trunk(x_prompt, p_prompt, 0, zero_ab, zero_cd, prm)
    y_s, ab_s, cd_s = _trunk(x_sample, p_sample, PAST_LEN,
                             (state_mlstm_conv, state_mlstm_C, state_mlstm_n, state_mlstm_m, state_ret),
                             (state_hgrn, state_s5_re, state_s5_im), prm)
    return (y_p, y_s,
            ab_p[0], ab_s[0], ab_p[1], ab_s[1], ab_p[2], ab_s[2], ab_p[3], ab_s[3], ab_p[4], ab_s[4],
            cd_p[0], cd_s[0], cd_p[1], cd_s[1], cd_p[2], cd_s[2])
```

```python
import functools
import math

import jax
import jax.numpy as jnp
from jax import lax
from jax.experimental import pallas as pl
from jax.experimental.pallas import tpu as pltpu

F32 = jnp.float32
BF16 = jnp.bfloat16

EPS = 1e-6
NEG = -1e30
LOG2_E = math.log2(math.e)
ROPE_BASE = 10000.0
PAST_LEN = 16384
CHUNK = 64
HEADS = 4
DH = 128
HW = HEADS * DH
CONV_W = 4
S5_GROUP = 16
S5_STATE = 64
SUBLANES = 8
LANES = 128
MXU_DIM = 256
VMEM_LIMIT = 56 * 1024 * 1024
FUSED_VMEM_LIMIT = 58 * 1024 * 1024


def _params(n_axes, vmem=None):
    return pltpu.CompilerParams(dimension_semantics=("arbitrary",) * n_axes, vmem_limit_bytes=vmem)


def _full(shape):
    return pl.BlockSpec(shape, lambda *_: (0,) * len(shape))


def _bdot(a, b):
    return jnp.dot(a.astype(BF16), b.astype(BF16), preferred_element_type=F32)


def _bdot_nt(a, b):
    return lax.dot_general(a.astype(BF16), b.astype(BF16), (((1,), (1,)), ((), ())),
                           preferred_element_type=F32)


def _bdot_tn(a, b):
    return lax.dot_general(a.astype(BF16), b.astype(BF16), (((0,), (0,)), ((), ())),
                           preferred_element_type=F32)


def _sigmoid(x):
    return 1.0 / (1.0 + jnp.exp(-x))


def _log_sigmoid(x):
    return jnp.minimum(x, 0.0) - jnp.log(1.0 + jnp.exp(-jnp.abs(x)))


def _rms(x, g):
    return x * lax.rsqrt(jnp.mean(x * x, axis=-1, keepdims=True) + EPS) * g


def _head_rms(x):
    return x * lax.rsqrt(jnp.mean(x * x, axis=-1, keepdims=True) + EPS)


def _chunk_len(length):
    return CHUNK if length % CHUNK == 0 else length


def _round_robin(gens):
    gens = list(gens)
    while gens:
        alive = []
        for g in gens:
            try:
                next(g)
                alive.append(g)
            except StopIteration:
                pass
        gens = alive


def _inproj_kernel(x_ref, g_ref, w_ref, *out_refs):
    hn = _rms(x_ref[...], g_ref[...]).astype(BF16)
    off = 0
    for o_ref in out_refs:
        n = o_ref.shape[-1]
        for n0 in range(0, n, HW):
            nn = min(HW, n - n0)
            o_ref[:, n0:n0 + nn] = jnp.dot(hn, w_ref[:, off + n0:off + n0 + nn],
                                           preferred_element_type=F32)
        off += n


def _inproj(h, g, w, widths, tm, time_major_last=None):
    t, d = h.shape
    n = w.shape[1]
    assert sum(widths) == n and t % tm == 0
    out_shape = [jax.ShapeDtypeStruct((t, wd), F32) for wd in widths]
    out_specs = [pl.BlockSpec((tm, wd), lambda i: (i, 0)) for wd in widths]
    if time_major_last is not None:
        b, l = time_major_last
        assert l % tm == 0
        nl = l // tm
        wd = widths[-1]
        out_shape[-1] = jax.ShapeDtypeStruct((l, b * wd), F32)
        out_specs[-1] = pl.BlockSpec((tm, wd), lambda i: (i % nl, i // nl))
    return pl.pallas_call(
        _inproj_kernel,
        grid=(t // tm,),
        in_specs=[pl.BlockSpec((tm, d), lambda i: (i, 0)), _full((1, d)), _full((d, n))],
        out_specs=out_specs,
        out_shape=out_shape,
        compiler_params=_params(1, VMEM_LIMIT),
        name="inproj",
    )(h, g.reshape(1, d), w)


def _post_kernel(h_ref, ma_ref, mb_ref, p_ref, wo_ref, gff_ref, w1_ref, w2_ref, gple_ref, wg_ref,
                 wp_ref, gfin_ref, o_ref, *, final, ff_chunk):
    half = ma_ref.shape[-1]
    h = h_ref[...]
    h = h + (jnp.dot(ma_ref[...].astype(BF16), wo_ref[0:half, :], preferred_element_type=F32)
             + jnp.dot(mb_ref[...].astype(BF16), wo_ref[half:2 * half, :], preferred_element_type=F32))
    hn = _rms(h, gff_ref[...]).astype(BF16)
    d_ff = w1_ref.shape[1]
    acc = jnp.zeros_like(h)
    for f0 in range(0, d_ff, ff_chunk):
        a = jnp.dot(hn, w1_ref[:, f0:f0 + ff_chunk], preferred_element_type=F32)
        a = jnp.square(jnp.maximum(a, 0.0))
        acc = acc + jnp.dot(a.astype(BF16), w2_ref[f0:f0 + ff_chunk, :], preferred_element_type=F32)
    h = h + acc
    gate = _sigmoid(jnp.dot(_rms(h, gple_ref[...]).astype(BF16), wg_ref[...],
                            preferred_element_type=F32))
    h = h + gate * jnp.dot(p_ref[...].astype(BF16), wp_ref[...], preferred_element_type=F32)
    o_ref[...] = _rms(h, gfin_ref[...]) if final else h


def _post(h, mix_a, mix_b, p, wo, gff, w1, w2, gple, wg, wp, gfin, *, layer, final, tm,
          b_time_major=None):
    t, d = h.shape
    half = mix_a.shape[-1]
    pd = p.shape[-1]
    d_ff = w1.shape[-1]
    assert t % tm == 0
    row = lambda i: (i, 0)
    mb_spec = pl.BlockSpec((tm, half), row)
    if b_time_major is not None:
        _, l = b_time_major
        assert l % tm == 0
        nl = l // tm
        mb_spec = pl.BlockSpec((tm, half), lambda i: (i % nl, i // nl))
    lw = lambda r, cdim: pl.BlockSpec((None, r, cdim), lambda i: (layer, 0, 0))
    return pl.pallas_call(
        functools.partial(_post_kernel, final=final, ff_chunk=1024),
        grid=(t // tm,),
        in_specs=[pl.BlockSpec((tm, d), row), pl.BlockSpec((tm, half), row), mb_spec,
                  pl.BlockSpec((None, tm, pd), lambda i: (layer, i, 0)),
                  _full((d, d)), _full((1, d)), lw(d, d_ff), lw(d_ff, d), _full((1, d)),
                  lw(d, d), lw(pd, d), _full((1, d))],
        out_specs=pl.BlockSpec((tm, d), row),
        out_shape=jax.ShapeDtypeStruct((t, d), F32),
        compiler_params=_params(1, VMEM_LIMIT),
        name="post",
    )(h, mix_a, mix_b, p, wo, gff.reshape(1, d), w1, w2, gple.reshape(1, d), wg, wp,
      gfin.reshape(1, d))


def _mlstm_kernel(zq_ref, zk_ref, zv_ref, zo_ref, zg_ref, cw_ref, cb_ref, bg_ref, gn_ref,
                  conv0_ref, c0_ref, n0_ref, m0_ref,
                  out_ref, c_ref, n_ref, m_ref, ext_ref, qk_ref, gates_ref, *, c, group):
    j = pl.program_id(1)
    tail = CONV_W - 1

    @pl.when(j == 0)
    def _():
        c_ref[...] = c0_ref[...]
        n_ref[...] = n0_ref[...]
        m_ref[...] = m0_ref[...]
        for gi in range(group):
            ext_ref[gi, 0:SUBLANES, :] = jnp.zeros((SUBLANES, 2 * HW), F32)
            ext_ref[gi, SUBLANES - tail:SUBLANES, :] = conv0_ref[gi]

    ri = lax.broadcasted_iota(jnp.int32, (c, c), 0)
    ci = lax.broadcasted_iota(jnp.int32, (c, c), 1)
    eye = ri == ci
    tril = ri >= ci
    lane = lax.broadcasted_iota(jnp.int32, (c, LANES), 1)
    bg = bg_ref[...]

    def lsum(x):
        return jnp.broadcast_to(jnp.sum(x, axis=1, keepdims=True), (c, DH))

    def unit(gi, h):
        sl = slice(h * DH, (h + 1) * DH)
        gates = gates_ref[gi]
        i_col = lsum(jnp.where(lane == h, gates, 0.0))
        f_col = lsum(jnp.where(lane == HEADS + h, gates, 0.0))
        yield
        b_row = jnp.sum(jnp.where(ri <= ci, f_col[:, :c], 0.0), axis=0, keepdims=True)
        b_col = lsum(jnp.where(eye, b_row, 0.0))
        i_row = jnp.sum(jnp.where(eye, i_col[:, :c], 0.0), axis=0, keepdims=True)
        yield
        m_prev = m_ref[gi, h:h + 1, :]
        dmat = jnp.where(tril, b_col[:, :c] - b_row + i_row, NEG)
        inter = b_col + m_prev
        row_max = jnp.broadcast_to(jnp.max(dmat, axis=1, keepdims=True), (c, DH))
        qh = qk_ref[gi, :, sl]
        kh = qk_ref[gi, :, HW + h * DH:HW + (h + 1) * DH] * (DH ** -0.5)
        vh = zv_ref[gi, :, sl]
        c_h = c_ref[gi, h]
        n_h = n_ref[gi, h:h + 1, :]
        s_raw = _bdot_nt(qh, kh)
        q_c = _bdot(qh, c_h)
        q_n = lsum(qh * n_h)
        yield
        m_t = jnp.maximum(inter, row_max)
        w_intra = jnp.exp(dmat - m_t[:, :c])
        w_inter = jnp.exp(inter - m_t)
        s = s_raw * w_intra
        s_v = _bdot(s, vh)
        s_sum = lsum(s)
        m_new = m_t[c - 1:c, :]
        b_last = b_col[c - 1:c, :]
        w_last = jnp.exp(b_last - b_col + i_col - m_new)
        decay = jnp.exp(b_last + m_prev - m_new)
        kw = w_last * kh
        kw_v = _bdot_tn(kw, vh)
        yield
        num = w_inter * q_c + s_v
        den = w_inter * q_n + s_sum
        hh = num / jnp.maximum(jnp.abs(den), jnp.exp(-m_t))
        c_ref[gi, h] = decay * c_h + kw_v
        n_ref[gi, h:h + 1, :] = decay * n_h + jnp.sum(kw, axis=0, keepdims=True)
        m_ref[gi, h:h + 1, :] = m_new
        hh = _sigmoid(zo_ref[gi, :, sl]) * hh
        out_ref[gi, :, sl] = _head_rms(hh) * gn_ref[:, sl]

    for gi in range(group):
        ext_ref[gi, SUBLANES:SUBLANES + c, 0:HW] = zq_ref[gi]
        ext_ref[gi, SUBLANES:SUBLANES + c, HW:2 * HW] = zk_ref[gi]
        conv = cb_ref[...]
        for jj in range(CONV_W):
            r0 = SUBLANES - tail + jj
            conv = conv + cw_ref[jj:jj + 1, :] * ext_ref[gi, r0:r0 + c, :]
        ext_ref[gi, 0:SUBLANES, :] = ext_ref[gi, c:c + SUBLANES, :]
        qk_ref[gi] = conv * _sigmoid(conv)
        gb = zg_ref[gi] + bg
        gates_ref[gi] = jnp.where(lane < HEADS, gb, _log_sigmoid(gb))
    _round_robin([unit(gi, h) for gi in range(group) for h in range(HEADS)])


def _seq_group(b, c):
    rows = 512
    group = max(1, min(b, rows // c, 2 * SUBLANES))
    assert b % group == 0
    return group


def _mlstm(z_main, z_gate, conv_w, conv_b, b_gate, gn_a, conv0, c0, n0, m0, b, l):
    c = _chunk_len(l)
    nc = l // c
    grp = _seq_group(b, c)
    z3 = z_main.reshape(b, l, z_main.shape[-1])
    zspec = lambda col: pl.BlockSpec((grp, c, HW), lambda bi, j: (bi, j, col))
    st = lambda shape: pl.BlockSpec((grp,) + shape, lambda bi, j: (bi,) + (0,) * len(shape))
    m0b = jnp.broadcast_to(m0[:, :, None], (b, HEADS, DH))
    out, c_new, n_new, m_new = pl.pallas_call(
        functools.partial(_mlstm_kernel, c=c, group=grp),
        grid=(b // grp, nc),
        in_specs=[zspec(0), zspec(1), zspec(2), zspec(3),
                  pl.BlockSpec((grp, c, LANES), lambda bi, j: (bi, j, 0)),
                  _full((CONV_W, 2 * HW)), _full((1, 2 * HW)), _full((1, LANES)), _full((1, HW)),
                  st((CONV_W - 1, 2 * HW)), st((HEADS, DH, DH)), st((HEADS, DH)), st((HEADS, DH))],
        out_specs=[pl.BlockSpec((grp, c, HW), lambda bi, j: (bi, j, 0)),
                   st((HEADS, DH, DH)), st((HEADS, DH)), st((HEADS, DH))],
        out_shape=[jax.ShapeDtypeStruct((b, l, HW), F32),
                   jax.ShapeDtypeStruct((b, HEADS, DH, DH), F32),
                   jax.ShapeDtypeStruct((b, HEADS, DH), F32),
                   jax.ShapeDtypeStruct((b, HEADS, DH), F32)],
        scratch_shapes=[pltpu.VMEM((grp, c + SUBLANES, 2 * HW), F32),
                        pltpu.VMEM((grp, c, 2 * HW), F32), pltpu.VMEM((grp, c, LANES), F32)],
        compiler_params=_params(2),
        name="mlstm",
    )(z3, z3, z3, z3, z_gate.reshape(b, l, LANES), conv_w, conv_b.reshape(1, -1),
      jnp.pad(b_gate, (0, LANES - 2 * HEADS)).reshape(1, LANES), gn_a.reshape(1, -1),
      conv0, c0, n0, m0b)
    return out.reshape(b * l, HW), c_new, n_new, m_new[:, :, 0]


def _rope_table_kernel(inv_ref, sign_ref, cos_ref, sin_ref, *, pos0, rows):
    i = pl.program_id(0)
    pos = (pos0 + i * rows + lax.broadcasted_iota(jnp.int32, (rows, LANES), 0)).astype(F32)
    ang = pos * inv_ref[...]
    cos_ref[...] = jnp.cos(ang)
    sin_ref[...] = jnp.sin(ang) * sign_ref[...]


def _rope_tables(l, pos0):
    half = DH // 2
    inv = ROPE_BASE ** (-jnp.arange(half, dtype=F32) / half)
    inv2 = jnp.concatenate([inv, inv]).reshape(1, DH)
    sign = jnp.concatenate([-jnp.ones((half,), F32), jnp.ones((half,), F32)]).reshape(1, DH)
    rows = min(l, 512)
    assert l % rows == 0
    return pl.pallas_call(
        functools.partial(_rope_table_kernel, pos0=pos0, rows=rows),
        grid=(l // rows,),
        in_specs=[_full((1, DH)), _full((1, DH))],
        out_specs=[pl.BlockSpec((rows, DH), lambda i: (i, 0))] * 2,
        out_shape=[jax.ShapeDtypeStruct((l, DH), F32)] * 2,
        compiler_params=_params(1),
        name="rope_table",
    )(inv2, sign)


def _ret_kernel(zq_ref, zk_ref, zv_ref, zg_ref, cos_ref, sin_ref, s0_ref, out_ref, s_ref,
                *, c, group):
    j = pl.program_id(1)

    @pl.when(j == 0)
    def _():
        s_ref[...] = s0_ref[...]

    cosf = cos_ref[...]
    sinf = sin_ref[...]
    ti = lax.broadcasted_iota(jnp.int32, (c, c), 0)
    si = lax.broadcasted_iota(jnp.int32, (c, c), 1)
    rel = jnp.maximum(ti - si, 0).astype(F32)
    tcol = lax.broadcasted_iota(jnp.int32, (c, 1), 0).astype(F32)

    def rope(x):
        return x * cosf + pltpu.roll(x, DH // 2, axis=1) * sinf

    def unit(gi, h, decay, inter, kdecay, cdecay):
        sl = slice(h * DH, (h + 1) * DH)
        qr = rope(zq_ref[gi, :, sl])
        kr = rope(zk_ref[gi, :, sl]) * (DH ** -0.5)
        yield
        vh = zv_ref[gi, :, sl]
        s_h = s_ref[gi, h]
        qk = _bdot_nt(qr, kr)
        q_s = _bdot(qr, s_h)
        k_v = _bdot_tn(kr * kdecay, vh)
        yield
        o = q_s * inter + _bdot(qk * decay, vh)
        s_ref[gi, h] = cdecay * s_h + k_v
        yield
        gate = zg_ref[gi, :, sl]
        out_ref[gi, :, sl] = _head_rms(o) * (gate * _sigmoid(gate))

    units = []
    for h in range(HEADS):
        lg = math.log1p(-(2.0 ** (-5.0 - h)))
        decay = jnp.where(ti >= si, jnp.exp(rel * lg), 0.0)
        inter = jnp.exp((tcol + 1.0) * lg)
        kdecay = jnp.exp((c - 1.0 - tcol) * lg)
        cdecay = math.exp(c * lg)
        units += [unit(gi, h, decay, inter, kdecay, cdecay) for gi in range(group)]
    _round_robin(units)


def _retention(z_main, cos_t, sin_t, s0, b, l):
    c = _chunk_len(l)
    nc = l // c
    grp = _seq_group(b, c)
    z3 = z_main.reshape(b, l, z_main.shape[-1])
    zspec = lambda col: pl.BlockSpec((grp, c, HW), lambda bi, j: (bi, j, col))
    st = pl.BlockSpec((grp, HEADS, DH, DH), lambda bi, j: (bi, 0, 0, 0))
    tab = pl.BlockSpec((c, DH), lambda bi, j: (j, 0))
    out, s_new = pl.pallas_call(
        functools.partial(_ret_kernel, c=c, group=grp),
        grid=(b // grp, nc),
        in_specs=[zspec(4), zspec(5), zspec(6), zspec(7), tab, tab, st],
        out_specs=[pl.BlockSpec((grp, c, HW), lambda bi, j: (bi, j, 0)), st],
        out_shape=[jax.ShapeDtypeStruct((b, l, HW), F32),
                   jax.ShapeDtypeStruct((b, HEADS, DH, DH), F32)],
        compiler_params=_params(2),
        name="retention",
    )(z3, z3, z3, z3, cos_t, sin_t, s0)
    return out.reshape(b * l, HW), s_new


def _hgrn_kernel(zq_ref, zf_ref, zi_ref, zg_ref, lbl_ref, gn_ref, s0_ref, out_ref, s_ref,
                 kk_ref, bcum_ref, *, c, sc, layer, group):
    j = pl.program_id(1)

    @pl.when(j == 0)
    def _():
        s_ref[...] = s0_ref[...]

    lbl = lbl_ref[...]
    e = jnp.exp(lbl - jnp.max(lbl, axis=0, keepdims=True))
    sm = e / jnp.sum(e, axis=0, keepdims=True)
    cum = sm[0:1, :]
    for r in range(1, layer + 1):
        cum = cum + sm[r:r + 1, :]
    lb = cum - sm[0:1, :]

    oml = 1.0 - lb
    ri = lax.broadcasted_iota(jnp.int32, (c, c), 0)
    ci = lax.broadcasted_iota(jnp.int32, (c, c), 1)
    tril = jnp.where(ri >= ci, 1.0, 0.0).astype(BF16)
    rowi = lax.broadcasted_iota(jnp.int32, (sc, 1), 0)
    lane_s = lax.broadcasted_iota(jnp.int32, (SUBLANES, sc), 1)
    e_r = lax.broadcasted_iota(jnp.int32, (DH, DH), 0)
    e_c = lax.broadcasted_iota(jnp.int32, (DH, DH), 1)
    eye = e_r == e_c
    def unit(gi, h):
        sl = slice(h * DH, (h + 1) * DH)
        bh = bcum_ref[gi, :, sl]
        qh = zq_ref[gi, :, sl] * (DH ** -0.5)
        kh = kk_ref[gi, :, sl]
        vh = zi_ref[gi, :, sl]
        s_h = s_ref[gi, h]
        b_last = bh[c - 1:c, :]
        o_inter = _bdot(qh * jnp.exp(bh), s_h)
        k_v = _bdot_tn(kh * jnp.exp(b_last - bh), vh)
        dec_col = jnp.sum(jnp.where(eye, jnp.exp(b_last), 0.0), axis=1, keepdims=True)
        yield
        s_ref[gi, h] = dec_col * s_h + k_v
        blocks = []
        for blk in range(c // sc):
            r0 = blk * sc
            b_i = bh[r0:r0 + sc]
            q_i = qh[r0:r0 + sc]
            k_i = kh[r0:r0 + sc]
            v_i = vh[r0:r0 + sc]
            att_prev = None
            if blk > 0:
                ref_row = bh[r0 - 1:r0, :]
                a_i = q_i * jnp.exp(b_i - ref_row)
                k_prev = kh[0:r0] * jnp.exp(ref_row - bh[0:r0])
                att_prev = _bdot_nt(a_i, k_prev)
            b2_i = b_i * LOG2_E
            cols = []
            for s in range(sc):
                lo = (s // SUBLANES) * SUBLANES
                dec = jnp.exp2(jnp.where(rowi[lo:] >= s, b2_i[lo:] - b2_i[s:s + 1, :], NEG))
                cols.append(jnp.sum(q_i[lo:] * dec * k_i[s:s + 1, :], axis=1, keepdims=True))
            yield
            att = [jnp.zeros((SUBLANES, sc), F32) for _ in range(sc // SUBLANES)]
            for s in range(sc):
                lo = (s // SUBLANES) * SUBLANES
                for pi in range(lo // SUBLANES, sc // SUBLANES):
                    piece = cols[s][pi * SUBLANES - lo:(pi + 1) * SUBLANES - lo]
                    att[pi] = jnp.where(lane_s == s, piece, att[pi])
            att = jnp.concatenate(att, axis=0) if len(att) > 1 else att[0]
            o_i = _bdot(att, v_i)
            if att_prev is not None:
                o_i = o_i + _bdot(att_prev, vh[0:r0])
            blocks.append(o_i)
        yield
        o = o_inter + (jnp.concatenate(blocks, axis=0) if len(blocks) > 1 else blocks[0])
        gate = zg_ref[gi, :, sl]
        out_ref[gi, :, sl] = _head_rms(o) * gn_ref[:, sl] * (gate * _sigmoid(gate))

    for gi in range(group):
        zf = zf_ref[gi]
        ez = jnp.exp(-jnp.abs(zf))
        big = 1.0 / (1.0 + ez)
        small = ez * big
        pos = zf >= 0.0
        logf = jnp.log(lb + oml * jnp.where(pos, big, small))
        kk_ref[gi] = oml * jnp.where(pos, small, big)

        p0 = logf.astype(BF16)
        r1 = logf - p0.astype(F32)
        p1 = r1.astype(BF16)
        p2 = (r1 - p1.astype(F32)).astype(BF16)
        bcum_ref[gi] = (jnp.dot(tril, p0, preferred_element_type=F32)
                        + jnp.dot(tril, p1, preferred_element_type=F32)
                        + jnp.dot(tril, p2, preferred_element_type=F32))
    _round_robin([unit(gi, h) for gi in range(group) for h in range(HEADS)])


def _hgrn(z_cd, lb_logits, gn_c, s0, b, l, layer):
    c = _chunk_len(l)
    sc = min(c, 16)
    nc = l // c
    grp = _seq_group(b, c)
    depth = lb_logits.shape[0]
    z3 = z_cd.reshape(b, l, z_cd.shape[-1])
    zspec = lambda col: pl.BlockSpec((grp, c, HW), lambda bi, j: (bi, j, col))
    st = pl.BlockSpec((grp, HEADS, DH, DH), lambda bi, j: (bi, 0, 0, 0))
    out, s_new = pl.pallas_call(
        functools.partial(_hgrn_kernel, c=c, sc=sc, layer=layer, group=grp),
        grid=(b // grp, nc),
        in_specs=[zspec(0), zspec(1), zspec(2), zspec(3), _full((depth, HW)), _full((1, HW)), st],
        out_specs=[pl.BlockSpec((grp, c, HW), lambda bi, j: (bi, j, 0)), st],
        out_shape=[jax.ShapeDtypeStruct((b, l, HW), F32),
                   jax.ShapeDtypeStruct((b, HEADS, DH, DH), F32)],
        scratch_shapes=[pltpu.VMEM((grp, c, HW), F32), pltpu.VMEM((grp, c, HW), F32)],
        compiler_params=_params(2),
        name="hgrn2",
    )(z3, z3, z3, z3, lb_logits, gn_c.reshape(1, -1), s0)
    return out.reshape(b * l, HW), s_new


def _s5_kernel(u_ref, are_ref, aim_ref, ldt_ref, bre_ref, bim_ref, cre_ref, cim_ref, d_ref,
               wglu_ref, bglu_ref, x0r_ref, x0i_ref,
               s_ref, xr_ref, xi_ref, ar_sc, ai_sc, bbr_sc, bbi_sc, bur_sc, bui_sc, *, ct):
    j = pl.program_id(1)
    ns = are_ref.shape[-1]
    nu = u_ref.shape[-1]
    hs = ns // 2
    hu = nu // 2
    rows = ct * SUBLANES

    @pl.when(j == 0)
    def _():
        a_re = are_ref[...]
        a_im = aim_ref[...]
        dt = jnp.exp(ldt_ref[...])
        mag = jnp.exp(dt * a_re)
        ar = mag * jnp.cos(dt * a_im)
        ai = mag * jnp.sin(dt * a_im)
        ar_sc[...] = jnp.broadcast_to(ar, (SUBLANES, ns))
        ai_sc[...] = jnp.broadcast_to(ai, (SUBLANES, ns))
        den = a_re * a_re + a_im * a_im
        nr = ar - 1.0
        zr = (nr * a_re + ai * a_im) / den
        zi = (ai * a_re - nr * a_im) / den
        for hg in range(2):
            us = slice(hg * hu, (hg + 1) * hu)
            ss = slice(hg * hs, (hg + 1) * hs)
            bbr_sc[us, :] = (zr[:, ss] * bre_ref[us, :] - zi[:, ss] * bim_ref[us, :]).astype(BF16)
            bbi_sc[us, :] = (zr[:, ss] * bim_ref[us, :] + zi[:, ss] * bre_ref[us, :]).astype(BF16)
        xr_ref[...] = x0r_ref[...]
        xi_ref[...] = x0i_ref[...]

    u = u_ref[...].reshape(rows, nu)
    ub = u.astype(BF16)
    for hg in range(2):
        us = slice(hg * hu, (hg + 1) * hu)
        ss = slice(hg * hs, (hg + 1) * hs)
        bur_sc[:, ss] = jnp.dot(ub[:, us], bbr_sc[us, :], preferred_element_type=F32)
        bui_sc[:, ss] = jnp.dot(ub[:, us], bbi_sc[us, :], preferred_element_type=F32)

    lane_chunk = 4 * LANES
    for lc in range(ns // lane_chunk):
        ls = slice(lc * lane_chunk, (lc + 1) * lane_chunk)
        ar = ar_sc[:, ls]
        ai = ai_sc[:, ls]

        def step(t, carry):
            xr, xi = carry
            r0 = pl.multiple_of(t * SUBLANES, SUBLANES)
            nxr = ar * xr - ai * xi + bur_sc[pl.ds(r0, SUBLANES), ls]
            nxi = ar * xi + ai * xr + bui_sc[pl.ds(r0, SUBLANES), ls]
            bur_sc[pl.ds(r0, SUBLANES), ls] = nxr
            bui_sc[pl.ds(r0, SUBLANES), ls] = nxi
            return nxr, nxi

        xr, xi = lax.fori_loop(0, ct, step, (xr_ref[:, ls], xi_ref[:, ls]))
        xr_ref[:, ls] = xr
        xi_ref[:, ls] = xi

    ys = []
    for hg in range(2):
        ss = slice(hg * hs, (hg + 1) * hs)
        ys.append(jnp.dot(bur_sc[:, ss].astype(BF16), cre_ref[ss, :], preferred_element_type=F32)
                  - jnp.dot(bui_sc[:, ss].astype(BF16), cim_ref[ss, :], preferred_element_type=F32))
    y = jnp.concatenate(ys, axis=1) + d_ref[...] * u
    a = 0.5 * y * (1.0 + jnp.tanh(math.sqrt(2.0 / math.pi) * (y + 0.044715 * (y * y * y))))
    s = a * _sigmoid(jnp.dot(a.astype(BF16), wglu_ref[...], preferred_element_type=F32) + bglu_ref[...])
    s_ref[...] = s.reshape(ct, SUBLANES, nu)


def _s5_block_diag(bmat, cmat):
    g, p, hgrp = bmat.shape
    gh = g // 2
    eye = jnp.eye(gh, dtype=F32)
    b4 = bmat.reshape(2, gh, p, hgrp)
    bc = jnp.einsum('agph,gk->aghkp', b4, eye).reshape(2 * gh * hgrp, gh * p)
    c4 = cmat.reshape(2, gh, hgrp, p)
    cc = jnp.einsum('aghp,gk->agpkh', c4, eye).reshape(2 * gh * p, gh * hgrp)
    return bc, cc


def _s5(u_tm, a_re, a_im, log_dt, b_re, b_im, c_re, c_im, d_skip, w_glu, b_glu, x0r, x0i):
    l, b, nu = u_tm.shape
    g, p = a_re.shape
    ns = g * p
    assert b % SUBLANES == 0 and nu == g * S5_GROUP and (g // 2) * S5_GROUP == MXU_DIM
    ct = _chunk_len(l)
    nct = l // ct
    bre_c, cre_c = _s5_block_diag(b_re, c_re)
    bim_c, cim_c = _s5_block_diag(b_im, c_im)
    ldt = jnp.broadcast_to(log_dt[:, None], (g, p)).reshape(1, ns)
    rows = ct * SUBLANES
    xspec = pl.BlockSpec((SUBLANES, ns), lambda bb, j: (bb, 0))
    s, xr, xi = pl.pallas_call(
        functools.partial(_s5_kernel, ct=ct),
        grid=(b // SUBLANES, nct),
        in_specs=[pl.BlockSpec((ct, SUBLANES, nu), lambda bb, j: (j, bb, 0)),
                  _full((1, ns)), _full((1, ns)), _full((1, ns)),
                  _full((nu, ns // 2)), _full((nu, ns // 2)),
                  _full((ns, nu // 2)), _full((ns, nu // 2)),
                  _full((1, nu)), _full((nu, nu)), _full((1, nu)), xspec, xspec],
        out_specs=[pl.BlockSpec((ct, SUBLANES, nu), lambda bb, j: (j, bb, 0)), xspec, xspec],
        out_shape=[jax.ShapeDtypeStruct((l, b, nu), F32),
                   jax.ShapeDtypeStruct((b, ns), F32), jax.ShapeDtypeStruct((b, ns), F32)],
        scratch_shapes=[pltpu.VMEM((SUBLANES, ns), F32), pltpu.VMEM((SUBLANES, ns), F32),
                        pltpu.VMEM((nu, ns // 2), BF16), pltpu.VMEM((nu, ns // 2), BF16),
                        pltpu.VMEM((rows, ns), F32), pltpu.VMEM((rows, ns), F32)],
        compiler_params=_params(2, VMEM_LIMIT),
        name="s5",
    )(u_tm, a_re.reshape(1, ns), a_im.reshape(1, ns), ldt, bre_c, bim_c,
      cre_c.astype(BF16), cim_c.astype(BF16), d_skip.reshape(1, nu), w_glu.astype(BF16),
      b_glu.reshape(1, nu), x0r.reshape(b, ns), x0i.reshape(b, ns))
    return s, xr.reshape(b, g, p), xi.reshape(b, g, p)


def _tail_matmul_pieces(kk, h_ref, p_ref, mix_prev, wo_ref, gff_ref, w1_ref, w2_ref, gple_ref,
                        wg_ref, wp_ref, gfin_ref, out_ref, h1_sc, hn_sc, acc_sc, a_sc, *, final):
    nb, c, d = h_ref.shape
    rows = nb * c
    blk = w1_ref.shape[-1]
    pw = MXU_DIM
    pieces = range(0, d, pw)

    if kk == 0:
        mix = mix_prev()
        hw = mix.shape[-1] // 2
        mix_a = mix[:, :hw].astype(BF16)
        mix_b = mix[:, hw:].astype(BF16)
        for m0 in pieces:
            cs = slice(m0, m0 + pw)
            h1_sc[:, cs] = (h_ref[:, :, cs].reshape(rows, pw)
                            + jnp.dot(mix_a, wo_ref[0:hw, cs], preferred_element_type=F32)
                            + jnp.dot(mix_b, wo_ref[hw:2 * hw, cs], preferred_element_type=F32))
            yield
        hn_sc[...] = _rms(h1_sc[...], gff_ref[...]).astype(BF16)

    for n0 in range(0, blk, pw):
        a = jnp.dot(hn_sc[...], w1_ref[:, n0:n0 + pw], preferred_element_type=F32)
        a_sc[:, n0:n0 + pw] = jnp.square(jnp.maximum(a, 0.0)).astype(BF16)
        yield
    for m0 in pieces:
        cs = slice(m0, m0 + pw)
        part = jnp.dot(a_sc[...], w2_ref[:, cs], preferred_element_type=F32)
        acc_sc[:, cs] = part if kk == 0 else acc_sc[:, cs] + part
        yield

    if kk == HEADS - 1:
        h2 = h1_sc[...] + acc_sc[...]
        h1_sc[...] = h2
        hn_sc[...] = _rms(h2, gple_ref[...]).astype(BF16)
        pe = p_ref[...].reshape(rows, p_ref.shape[-1]).astype(BF16)
        yield
        for m0 in pieces:
            cs = slice(m0, m0 + pw)
            gate = _sigmoid(jnp.dot(hn_sc[...], wg_ref[:, cs], preferred_element_type=F32))
            acc_sc[:, cs] = h1_sc[:, cs] + gate * jnp.dot(pe, wp_ref[:, cs], preferred_element_type=F32)
            yield
        o = acc_sc[...]
        out_ref[...] = (_rms(o, gfin_ref[...]) if final else o).reshape(nb, c, d)


def _interleave(big, units, per_round):
    gap = -(-len(units) // per_round)
    order = []
    for i, u in enumerate(units):
        if i % gap == 0:
            order.append(big)
        order.append(u)
    alive = set(id(g) for g in order)
    while alive:
        for g in order:
            if id(g) in alive:
                try:
                    next(g)
                except StopIteration:
                    alive.discard(id(g))


def _fused_ab_kernel(mq_ref, mk_ref, mv_ref, mo_ref, zg_ref, rq_ref, rk_ref, rv_ref, rg_ref,
                     cwq_ref, cwk_ref, cbq_ref, cbk_ref, bg_ref, gn_ref, cos_ref, sin_ref,
                     h_ref, p_ref, wo_ref, gff_ref, w1_ref, w2_ref, gple_ref, wg_ref, wp_ref, gfin_ref,
                     out_ref, c_ref, n_ref, m_ref, s_ref,
                     mix_sc, h1_sc, acc_sc, hn_sc, a_sc, ext_sc, qk_sc, gates_sc, *, nc, final):
    nb, c, _ = mq_ref.shape
    s_id = pl.program_id(0)
    k_id = pl.program_id(1)
    valid = s_id < nc
    slot = s_id % 2
    tail = CONV_W - 1

    @pl.when((s_id == 0) & (k_id == 0))
    def _():
        c_ref[...] = jnp.zeros(c_ref.shape, F32)
        n_ref[...] = jnp.zeros(n_ref.shape, F32)
        m_ref[...] = jnp.zeros(m_ref.shape, F32)
        s_ref[...] = jnp.zeros(s_ref.shape, F32)
        mix_sc[...] = jnp.zeros(mix_sc.shape, F32)
        ext_sc[...] = jnp.zeros(ext_sc.shape, F32)

    ri = lax.broadcasted_iota(jnp.int32, (c, c), 0)
    ci = lax.broadcasted_iota(jnp.int32, (c, c), 1)
    eye = ri == ci
    tril = ri >= ci
    lane = lax.broadcasted_iota(jnp.int32, (c, LANES), 1)
    rel = jnp.maximum(ri - ci, 0).astype(F32)
    tcol = lax.broadcasted_iota(jnp.int32, (c, 1), 0).astype(F32)

    def lsum(x):
        return jnp.broadcast_to(jnp.sum(x, axis=1, keepdims=True), (c, DH))

    def rope(x):
        return x * cos_ref[...] + pltpu.roll(x, DH // 2, axis=1) * sin_ref[...]

    def mlstm_unit(gi, kk):
        rows = slice(gi * c, (gi + 1) * c)
        gates = gates_sc[gi]
        i_col = lsum(jnp.where(lane == kk, gates, 0.0))
        f_col = lsum(jnp.where(lane == HEADS + kk, gates, 0.0))
        yield
        b_row = jnp.sum(jnp.where(ri <= ci, f_col[:, :c], 0.0), axis=0, keepdims=True)
        b_col = lsum(jnp.where(eye, b_row, 0.0))
        i_row = jnp.sum(jnp.where(eye, i_col[:, :c], 0.0), axis=0, keepdims=True)
        yield
        m_prev = m_ref[gi, kk:kk + 1, :]
        dmat = jnp.where(tril, b_col[:, :c] - b_row + i_row, NEG)
        inter = b_col + m_prev
        row_max = jnp.broadcast_to(jnp.max(dmat, axis=1, keepdims=True), (c, DH))
        qh = qk_sc[gi, :, 0:DH]
        kh = qk_sc[gi, :, DH:2 * DH] * (DH ** -0.5)
        vh = mv_ref[gi]
        c_h = c_ref[gi, kk]
        n_h = n_ref[gi, kk:kk + 1, :]
        s_raw = _bdot_nt(qh, kh)
        q_c = _bdot(qh, c_h)
        q_n = lsum(qh * n_h)
        yield
        m_t = jnp.maximum(inter, row_max)
        w_intra = jnp.exp(dmat - m_t[:, :c])
        w_inter = jnp.exp(inter - m_t)
        sw = s_raw * w_intra
        s_v = _bdot(sw, vh)
        s_sum = lsum(sw)
        m_new = m_t[c - 1:c, :]
        b_last = b_col[c - 1:c, :]
        w_last = jnp.exp(b_last - b_col + i_col - m_new)
        decay = jnp.exp(b_last + m_prev - m_new)
        kw = w_last * kh
        kw_v = _bdot_tn(kw, vh)
        yield
        num = w_inter * q_c + s_v
        den = w_inter * q_n + s_sum
        hh = num / jnp.maximum(jnp.abs(den), jnp.exp(-m_t))
        c_ref[gi, kk] = jnp.where(valid, decay * c_h + kw_v, c_h)
        n_ref[gi, kk:kk + 1, :] = jnp.where(valid, decay * n_h + jnp.sum(kw, axis=0, keepdims=True), n_h)
        m_ref[gi, kk:kk + 1, :] = jnp.where(valid, m_new, m_prev)
        hh = _sigmoid(mo_ref[gi]) * hh
        res = _head_rms(hh) * gn_ref[...]
        mix_sc[slot, rows, kk * DH:(kk + 1) * DH] = res

    def ret_unit(gi, kk, decay, inter, kdecay, cdecay):
        rows = slice(gi * c, (gi + 1) * c)
        qr = rope(rq_ref[gi])
        kr = rope(rk_ref[gi]) * (DH ** -0.5)
        yield
        vh = rv_ref[gi]
        s_h = s_ref[gi, kk]
        qk = _bdot_nt(qr, kr)
        q_s = _bdot(qr, s_h)
        k_v = _bdot_tn(kr * kdecay, vh)
        yield
        o = q_s * inter + _bdot(qk * decay, vh)
        s_ref[gi, kk] = jnp.where(valid, cdecay * s_h + k_v, s_h)
        yield
        gate = rg_ref[gi]
        res = _head_rms(o) * (gate * _sigmoid(gate))
        mix_sc[slot, rows, HW + kk * DH:HW + (kk + 1) * DH] = res

    def prologue(kk, gi):
        ext_sc[kk, gi, SUBLANES:SUBLANES + c, 0:DH] = mq_ref[gi]
        ext_sc[kk, gi, SUBLANES:SUBLANES + c, DH:2 * DH] = mk_ref[gi]
        conv = jnp.concatenate([cbq_ref[...], cbk_ref[...]], axis=1)
        for jj in range(CONV_W):
            r0 = SUBLANES - tail + jj
            cw = jnp.concatenate([cwq_ref[jj:jj + 1, :], cwk_ref[jj:jj + 1, :]], axis=1)
            conv = conv + cw * ext_sc[kk, gi, r0:r0 + c, :]
        ext_sc[kk, gi, 0:SUBLANES, :] = ext_sc[kk, gi, c:c + SUBLANES, :]
        qk_sc[gi] = conv * _sigmoid(conv)
        gb = zg_ref[gi] + bg_ref[...]
        gates_sc[gi] = jnp.where(lane < HEADS, gb, _log_sigmoid(gb))

    for kk in range(HEADS):
        @pl.when(k_id == kk)
        def _(kk=kk):
            lg = math.log1p(-(2.0 ** (-5.0 - kk)))
            ret_consts = (jnp.where(tril, jnp.exp(rel * lg), 0.0), jnp.exp((tcol + 1.0) * lg),
                          jnp.exp((c - 1.0 - tcol) * lg), math.exp(c * lg))
            for gi in range(nb):
                prologue(kk, gi)
            units = [mlstm_unit(gi, kk) for gi in range(nb)]
            units += [ret_unit(gi, kk, *ret_consts) for gi in range(nb)]
            big = _tail_matmul_pieces(kk, h_ref, p_ref, lambda: mix_sc[1 - slot], wo_ref, gff_ref,
                                      w1_ref, w2_ref, gple_ref, wg_ref, wp_ref, gfin_ref, out_ref,
                                      h1_sc, hn_sc, acc_sc, a_sc, final=final)
            _interleave(big, units, per_round=3)


def _tail_specs(nb, c, d, pd, d_ff, nc, layer):
    prev = lambda s: jnp.maximum(s - 1, 0)
    blk = d_ff // HEADS
    in_specs = [pl.BlockSpec((nb, c, d), lambda s, k: (0, prev(s), 0)),
                pl.BlockSpec((None, nb, c, pd), lambda s, k: (layer, 0, prev(s), 0)),
                _full((d, d)), _full((1, d)),
                pl.BlockSpec((None, d, blk), lambda s, k: (layer, 0, k)),
                pl.BlockSpec((None, blk, d), lambda s, k: (layer, k, 0)),
                _full((1, d)),
                pl.BlockSpec((None, d, d), lambda s, k: (layer, 0, 0)),
                pl.BlockSpec((None, pd, d), lambda s, k: (layer, 0, 0)),
                _full((1, d))]
    out_spec = pl.BlockSpec((nb, c, d), lambda s, k: (0, prev(s), 0))
    return in_specs, out_spec


def _fused_ab(x3, z_main, z_gate, p4, cos_t, sin_t, prm, *, layer, final):
    nb, l, d = x3.shape
    c = _chunk_len(l)
    nc = l // c
    pd = p4.shape[-1]
    d_ff = prm['w_ff1'].shape[-1]
    assert c % SUBLANES == 0 and d_ff % (2 * HEADS) == 0
    z3 = z_main.reshape(nb, l, z_main.shape[-1])
    cur = lambda s: jnp.minimum(s, nc - 1)
    zspec = lambda base: pl.BlockSpec((nb, c, DH), lambda s, k: (0, cur(s), base + k))
    hcol = lambda arr_rows, base: pl.BlockSpec((arr_rows, DH), lambda s, k: (0, base + k))
    tab = pl.BlockSpec((c, DH), lambda s, k: (cur(s), 0))
    tail_in, out_spec = _tail_specs(nb, c, d, pd, d_ff, nc, layer)
    st4 = _full((nb, HEADS, DH, DH))
    st3 = _full((nb, HEADS, DH))
    rows = nb * c
    out, c_new, n_new, m_new, s_new = pl.pallas_call(
        functools.partial(_fused_ab_kernel, nc=nc, final=final),
        grid=(nc + 1, HEADS),
        in_specs=[zspec(0), zspec(HEADS), zspec(2 * HEADS), zspec(3 * HEADS),
                  pl.BlockSpec((nb, c, LANES), lambda s, k: (0, cur(s), 0)),
                  zspec(4 * HEADS), zspec(5 * HEADS), zspec(6 * HEADS), zspec(7 * HEADS),
                  hcol(CONV_W, 0), hcol(CONV_W, HEADS), hcol(1, 0), hcol(1, HEADS),
                  _full((1, LANES)), hcol(1, 0), tab, tab] + tail_in,
        out_specs=[out_spec, st4, st3, st3, st4],
        out_shape=[jax.ShapeDtypeStruct((nb, l, d), F32),
                   jax.ShapeDtypeStruct((nb, HEADS, DH, DH), F32),
                   jax.ShapeDtypeStruct((nb, HEADS, DH), F32),
                   jax.ShapeDtypeStruct((nb, HEADS, DH), F32),
                   jax.ShapeDtypeStruct((nb, HEADS, DH, DH), F32)],
        scratch_shapes=[pltpu.VMEM((2, rows, 2 * HW), F32), pltpu.VMEM((rows, d), F32),
                        pltpu.VMEM((rows, d), F32), pltpu.VMEM((rows, d), BF16),
                        pltpu.VMEM((rows, d_ff // HEADS), BF16),
                        pltpu.VMEM((HEADS, nb, c + SUBLANES, 2 * DH), F32),
                        pltpu.VMEM((nb, c, 2 * DH), F32), pltpu.VMEM((nb, c, LANES), F32)],
        compiler_params=_params(2, FUSED_VMEM_LIMIT),
        name="fused_ab",
    )(z3, z3, z3, z3, z_gate.reshape(nb, l, LANES), z3, z3, z3, z3,
      prm['conv_w_ab'][0], prm['conv_w_ab'][0], prm['conv_b_ab'][0].reshape(1, -1),
      prm['conv_b_ab'][0].reshape(1, -1),
      jnp.pad(prm['b_gate_ab'][0], (0, LANES - 2 * HEADS)).reshape(1, LANES),
      prm['gn_a'][0].reshape(1, -1), cos_t, sin_t,
      x3, p4, prm['w_out_ab'], prm['norm_ff'][layer].reshape(1, d), prm['w_ff1'], prm['w_ff2'],
      prm['norm_ple'][layer].reshape(1, d), prm['w_ple_gate'], prm['w_ple_proj'],
      prm['norm_final'].reshape(1, d))
    return out, c_new, n_new, m_new[:, :, 0], s_new


def _trunk(x, p, pos0, ab_state, cd_state, prm):
    b, l, d = x.shape
    assert l >= CONV_W - 1
    t = b * l
    tm = min(512, t)
    h = x.reshape(t, d)
    pp = p.reshape(p.shape[0], t, p.shape[-1])
    c = _chunk_len(l)
    fused = ab_state is None and b * c == tm and c % SUBLANES == 0
    z = lambda *shape: jnp.zeros(shape, F32)
    if ab_state is None and not fused:
        ab_state = (z(1, b, CONV_W - 1, 2 * HW), z(1, b, HEADS, DH, DH), z(1, b, HEADS, DH),
                    z(1, b, HEADS), z(1, b, HEADS, DH, DH))
    if cd_state is None:
        cd_state = (z(1, b, HEADS, DH, DH),) + (z(*((1, b) + prm['s5_A_re'].shape[1:])),) * 2
    hg0, x0r, x0i = cd_state

    z_main, z_gate = _inproj(h, prm['norm_mix'][0], prm['w_ab'], (8 * HW, LANES), tm)
    conv_new = z_main.reshape(b, l, 8 * HW)[:, l - (CONV_W - 1):, :2 * HW]
    cos_t, sin_t = _rope_tables(l, pos0)
    if fused:
        h3, c_new, n_new, m_new, ret_new = _fused_ab(x, z_main, z_gate, p, cos_t, sin_t, prm,
                                                     layer=0, final=False)
        h = h3.reshape(t, d)
    else:
        conv0, c0, n0, m0, ret0 = ab_state
        hm, c_new, n_new, m_new = _mlstm(z_main, z_gate, prm['conv_w_ab'][0], prm['conv_b_ab'][0],
                                         prm['b_gate_ab'][0], prm['gn_a'][0], conv0[0], c0[0], n0[0],
                                         m0[0], b, l)
        hr, ret_new = _retention(z_main, cos_t, sin_t, ret0[0], b, l)
        h = _post(h, hm, hr, pp, prm['w_out_ab'], prm['norm_ff'][0], prm['w_ff1'], prm['w_ff2'],
                  prm['norm_ple'][0], prm['w_ple_gate'], prm['w_ple_proj'], prm['norm_final'],
                  layer=0, final=False, tm=tm)

    direct_tm = b == SUBLANES and l % tm == 0
    z_cd, su = _inproj(h, prm['norm_mix'][1], prm['w_cd'], (4 * HW, HW), tm,
                       time_major_last=(b, l) if direct_tm else None)
    o, hg_new = _hgrn(z_cd, prm['lb_logits'], prm['gn_c'][0], hg0[0], b, l, layer=1)
    u_tm = su.reshape(l, b, HW) if direct_tm else jnp.transpose(su.reshape(b, l, HW), (1, 0, 2))
    s_tm, xr, xi = _s5(u_tm, prm['s5_A_re'][0], prm['s5_A_im'][0], prm['s5_log_dt'][0],
                       prm['s5_B_re'][0], prm['s5_B_im'][0], prm['s5_C_re'][0], prm['s5_C_im'][0],
                       prm['s5_D'][0], prm['w_glu'][0], prm['b_glu'][0], x0r[0], x0i[0])
    s_in = s_tm.reshape(l, b * HW) if direct_tm else jnp.transpose(s_tm, (1, 0, 2)).reshape(t, HW)
    y = _post(h, o, s_in, pp, prm['w_out_cd'], prm['norm_ff'][1], prm['w_ff1'], prm['w_ff2'],
              prm['norm_ple'][1], prm['w_ple_gate'], prm['w_ple_proj'], prm['norm_final'],
              layer=1, final=True, tm=tm, b_time_major=(b, l) if direct_tm else None)

    ab_new = (conv_new[None], c_new[None], n_new[None], m_new[None], ret_new[None])
    cd_new = (hg_new[None], xr[None], xi[None])
    return y.reshape(b, l, d), ab_new, cd_new


def kernel(x_prompt, x_sample, state_mlstm_conv, state_mlstm_C, state_mlstm_n, state_mlstm_m, state_ret, state_hgrn, state_s5_re, state_s5_im, p_prompt, p_sample, norm_mix, norm_ff, norm_ple, norm_final, w_in_ab, b_gate_ab, conv_w_ab, conv_b_ab, gn_a, w_out_ab, w_in_cd, lb_logits, gn_c, s5_A_re, s5_A_im, s5_log_dt, s5_B_re, s5_B_im, s5_C_re, s5_C_im, s5_D, w_glu, b_glu, w_out_cd, w_ff1, w_ff2, w_ple_proj, w_ple_gate):
    assert norm_mix.shape[0] == 2, "two layers: (mLSTM || retention), (HGRN2 || S5)"
    w_ab = w_in_ab[0]
    gate0 = 4 * HW
    w_ab = jnp.concatenate([w_ab[:, :gate0], w_ab[:, gate0 + 2 * HEADS:],
                            w_ab[:, gate0:gate0 + 2 * HEADS],
                            jnp.zeros((w_ab.shape[0], LANES - 2 * HEADS), w_ab.dtype)], axis=1)
    prm = dict(norm_mix=norm_mix, norm_ff=norm_ff, norm_ple=norm_ple, norm_final=norm_final,
               w_ab=w_ab.astype(BF16), b_gate_ab=b_gate_ab, conv_w_ab=conv_w_ab, conv_b_ab=conv_b_ab,
               gn_a=gn_a, w_out_ab=w_out_ab[0].astype(BF16), w_cd=w_in_cd[0].astype(BF16),
               lb_logits=lb_logits, gn_c=gn_c, s5_A_re=s5_A_re, s5_A_im=s5_A_im,
               s5_log_dt=s5_log_dt, s5_B_re=s5_B_re, s5_B_im=s5_B_im, s5_C_re=s5_C_re,
               s5_C_im=s5_C_im, s5_D=s5_D, w_glu=w_glu, b_glu=b_glu,
               w_out_cd=w_out_cd[0].astype(BF16), w_ff1=w_ff1.astype(BF16), w_ff2=w_ff2.astype(BF16),
               w_ple_proj=w_ple_proj.astype(BF16), w_ple_gate=w_ple_gate.astype(BF16))

    y_p, ab_p, cd_p = _trunk(x_prompt, p_prompt, 0, None, None, prm)
    y_s, ab_s, cd_s = _trunk(x_sample, p_sample, PAST_LEN,
                             (state_mlstm_conv, state_mlstm_C, state_mlstm_n, state_mlstm_m, state_ret),
                             (state_hgrn, state_s5_re, state_s5_im), prm)
    return (y_p, y_s,
            ab_p[0], ab_s[0], ab_p[1], ab_s[1], ab_p[2], ab_s[2], ab_p[3], ab_s[3], ab_p[4], ab_s[4],
            cd_p[0], cd_s[0], cd_p[1], cd_s[1], cd_p[2], cd_s[2])
```

```python
import functools
import math

import jax
import jax.numpy as jnp
from jax import lax
from jax.experimental import pallas as pl
from jax.experimental.pallas import tpu as pltpu

F32 = jnp.float32
BF16 = jnp.bfloat16

EPS = 1e-6
NEG = -1e30
LOG2_E = math.log2(math.e)
ROPE_BASE = 10000.0
PAST_LEN = 16384
CHUNK = 64
HEADS = 4
DH = 128
HW = HEADS * DH
CONV_W = 4
S5_GROUP = 16
S5_STATE = 64
SUBLANES = 8
LANES = 128
MXU_DIM = 256
VMEM_LIMIT = 56 * 1024 * 1024


def _params(n_axes, vmem=None):
    return pltpu.CompilerParams(dimension_semantics=("arbitrary",) * n_axes, vmem_limit_bytes=vmem)


def _full(shape):
    return pl.BlockSpec(shape, lambda *_: (0,) * len(shape))


def _bdot(a, b):
    return jnp.dot(a.astype(BF16), b.astype(BF16), preferred_element_type=F32)


def _bdot_nt(a, b):
    return lax.dot_general(a.astype(BF16), b.astype(BF16), (((1,), (1,)), ((), ())),
                           preferred_element_type=F32)


def _bdot_tn(a, b):
    return lax.dot_general(a.astype(BF16), b.astype(BF16), (((0,), (0,)), ((), ())),
                           preferred_element_type=F32)


def _sigmoid(x):
    return 1.0 / (1.0 + jnp.exp(-x))


def _log_sigmoid(x):
    return jnp.minimum(x, 0.0) - jnp.log(1.0 + jnp.exp(-jnp.abs(x)))


def _rms(x, g):
    return x * lax.rsqrt(jnp.mean(x * x, axis=-1, keepdims=True) + EPS) * g


def _head_rms(x):
    return x * lax.rsqrt(jnp.mean(x * x, axis=-1, keepdims=True) + EPS)


def _chunk_len(length):
    return CHUNK if length % CHUNK == 0 else length


def _round_robin(gens):
    gens = list(gens)
    while gens:
        alive = []
        for g in gens:
            try:
                next(g)
                alive.append(g)
            except StopIteration:
                pass
        gens = alive


def _step_ranges(counts):
    starts = [0]
    for n in counts:
        starts.append(starts[-1] + n)
    return starts


def _local(i, start, count):
    return jnp.clip(i - start, 0, count - 1)


def _inproj_kernel(*refs, n_groups, n_out, starts):
    x_refs = refs[:n_groups]
    g_ref, w_ref = refs[n_groups:n_groups + 2]
    out_refs = refs[n_groups + 2:]
    i = pl.program_id(0)

    def run(x_ref, outs):
        hn = _rms(x_ref[...], g_ref[...]).astype(BF16)
        off = 0
        for o_ref in outs:
            n = o_ref.shape[-1]
            for n0 in range(0, n, HW):
                nn = min(HW, n - n0)
                o_ref[:, n0:n0 + nn] = jnp.dot(hn, w_ref[:, off + n0:off + n0 + nn],
                                               preferred_element_type=F32)
            off += n

    for gi in range(n_groups):
        @pl.when((i >= starts[gi]) & (i < starts[gi + 1]))
        def _(gi=gi):
            run(x_refs[gi], out_refs[gi * n_out:(gi + 1) * n_out])


def _inproj(hs, g, w, widths, tm, time_major_last):
    d = hs[0].shape[1]
    n = w.shape[1]
    assert sum(widths) == n
    counts = [h.shape[0] // tm for h in hs]
    starts = _step_ranges(counts)
    in_specs, out_specs, out_shape = [], [], []
    for k, h in enumerate(hs):
        t = h.shape[0]
        assert t % tm == 0
        loc = functools.partial(_local, start=starts[k], count=counts[k])
        in_specs.append(pl.BlockSpec((tm, d), lambda i, loc=loc: (loc(i), 0)))
        for wi, wd in enumerate(widths):
            if wi == len(widths) - 1 and time_major_last[k] is not None:
                b, l = time_major_last[k]
                assert l % tm == 0
                nl = l // tm
                out_shape.append(jax.ShapeDtypeStruct((l, b * wd), F32))
                out_specs.append(pl.BlockSpec((tm, wd), lambda i, loc=loc, nl=nl: (loc(i) % nl, loc(i) // nl)))
            else:
                out_shape.append(jax.ShapeDtypeStruct((t, wd), F32))
                out_specs.append(pl.BlockSpec((tm, wd), lambda i, loc=loc: (loc(i), 0)))
    outs = pl.pallas_call(
        functools.partial(_inproj_kernel, n_groups=len(hs), n_out=len(widths), starts=tuple(starts)),
        grid=(starts[-1],),
        in_specs=in_specs + [_full((1, d)), _full((d, n))],
        out_specs=out_specs,
        out_shape=out_shape,
        compiler_params=_params(1, VMEM_LIMIT),
        name="inproj",
    )(*hs, g.reshape(1, d), w)
    nw = len(widths)
    return [outs[k * nw:(k + 1) * nw] for k in range(len(hs))]


def _post_kernel(*refs, n_groups, starts, final, ff_chunk):
    grp_in = [refs[4 * k:4 * k + 4] for k in range(n_groups)]
    (wo_ref, gff_ref, w1_ref, w2_ref, gple_ref, wg_ref, wp_ref,
     gfin_ref) = refs[4 * n_groups:4 * n_groups + 8]
    out_refs = refs[4 * n_groups + 8:]
    i = pl.program_id(0)

    def run(h_ref, ma_ref, mb_ref, p_ref, o_ref):
        half = ma_ref.shape[-1]
        h = h_ref[...]
        h = h + (jnp.dot(ma_ref[...].astype(BF16), wo_ref[0:half, :], preferred_element_type=F32)
                 + jnp.dot(mb_ref[...].astype(BF16), wo_ref[half:2 * half, :],
                           preferred_element_type=F32))
        hn = _rms(h, gff_ref[...]).astype(BF16)
        d_ff = w1_ref.shape[1]
        acc = jnp.zeros_like(h)
        for f0 in range(0, d_ff, ff_chunk):
            a = jnp.dot(hn, w1_ref[:, f0:f0 + ff_chunk], preferred_element_type=F32)
            a = jnp.square(jnp.maximum(a, 0.0))
            acc = acc + jnp.dot(a.astype(BF16), w2_ref[f0:f0 + ff_chunk, :], preferred_element_type=F32)
        h = h + acc
        gate = _sigmoid(jnp.dot(_rms(h, gple_ref[...]).astype(BF16), wg_ref[...],
                                preferred_element_type=F32))
        h = h + gate * jnp.dot(p_ref[...].astype(BF16), wp_ref[...], preferred_element_type=F32)
        o_ref[...] = _rms(h, gfin_ref[...]) if final else h

    for k in range(n_groups):
        @pl.when((i >= starts[k]) & (i < starts[k + 1]))
        def _(k=k):
            run(*grp_in[k], out_refs[k])


def _post(groups, wo, gff, w1, w2, gple, wg, wp, gfin, *, layer, final, tm):
    d = groups[0][0].shape[1]
    d_ff = w1.shape[-1]
    counts = [grp[0].shape[0] // tm for grp in groups]
    starts = _step_ranges(counts)
    in_specs, out_specs, out_shape, operands = [], [], [], []
    for k, (h, mix_a, mix_b, p, b_time_major) in enumerate(groups):
        t = h.shape[0]
        half = mix_a.shape[-1]
        pd = p.shape[-1]
        assert t % tm == 0
        loc = functools.partial(_local, start=starts[k], count=counts[k])
        row = lambda i, loc=loc: (loc(i), 0)
        mb_spec = pl.BlockSpec((tm, half), row)
        if b_time_major is not None:
            _, l = b_time_major
            assert l % tm == 0
            nl = l // tm
            mb_spec = pl.BlockSpec((tm, half), lambda i, loc=loc, nl=nl: (loc(i) % nl, loc(i) // nl))
        in_specs += [pl.BlockSpec((tm, d), row), pl.BlockSpec((tm, half), row), mb_spec,
                     pl.BlockSpec((None, tm, pd), lambda i, loc=loc: (layer, loc(i), 0))]
        operands += [h, mix_a, mix_b, p]
        out_specs.append(pl.BlockSpec((tm, d), row))
        out_shape.append(jax.ShapeDtypeStruct((t, d), F32))
    pd = groups[0][3].shape[-1]
    lw = lambda r, cdim: pl.BlockSpec((None, r, cdim), lambda i: (layer, 0, 0))
    return pl.pallas_call(
        functools.partial(_post_kernel, n_groups=len(groups), starts=tuple(starts), final=final,
                          ff_chunk=1024),
        grid=(starts[-1],),
        in_specs=in_specs + [_full((d, d)), _full((1, d)), lw(d, d_ff), lw(d_ff, d), _full((1, d)),
                             lw(d, d), lw(pd, d), _full((1, d))],
        out_specs=out_specs,
        out_shape=out_shape,
        compiler_params=_params(1, VMEM_LIMIT),
        name="post",
    )(*operands, wo, gff.reshape(1, d), w1, w2, gple.reshape(1, d), wg, wp, gfin.reshape(1, d))


def _mlstm_kernel(zq_ref, zk_ref, zv_ref, zo_ref, zg_ref, cw_ref, cb_ref, bg_ref, gn_ref,
                  conv0_ref, c0_ref, n0_ref, m0_ref,
                  out_ref, c_ref, n_ref, m_ref, ext_ref, qk_ref, gates_ref, *, c, group):
    j = pl.program_id(1)
    tail = CONV_W - 1

    @pl.when(j == 0)
    def _():
        c_ref[...] = c0_ref[...]
        n_ref[...] = n0_ref[...]
        m_ref[...] = m0_ref[...]
        for gi in range(group):
            ext_ref[gi, 0:SUBLANES, :] = jnp.zeros((SUBLANES, 2 * HW), F32)
            ext_ref[gi, SUBLANES - tail:SUBLANES, :] = conv0_ref[gi]

    ri = lax.broadcasted_iota(jnp.int32, (c, c), 0)
    ci = lax.broadcasted_iota(jnp.int32, (c, c), 1)
    eye = ri == ci
    tril = ri >= ci
    lane = lax.broadcasted_iota(jnp.int32, (c, LANES), 1)
    bg = bg_ref[...]

    def lsum(x):
        return jnp.broadcast_to(jnp.sum(x, axis=1, keepdims=True), (c, DH))

    def unit(gi, h):
        sl = slice(h * DH, (h + 1) * DH)
        gates = gates_ref[gi]
        i_col = lsum(jnp.where(lane == h, gates, 0.0))
        f_col = lsum(jnp.where(lane == HEADS + h, gates, 0.0))
        yield
        b_row = jnp.sum(jnp.where(ri <= ci, f_col[:, :c], 0.0), axis=0, keepdims=True)
        b_col = lsum(jnp.where(eye, b_row, 0.0))
        i_row = jnp.sum(jnp.where(eye, i_col[:, :c], 0.0), axis=0, keepdims=True)
        yield
        m_prev = m_ref[gi, h:h + 1, :]
        dmat = jnp.where(tril, b_col[:, :c] - b_row + i_row, NEG)
        inter = b_col + m_prev
        row_max = jnp.broadcast_to(jnp.max(dmat, axis=1, keepdims=True), (c, DH))
        qh = qk_ref[gi, :, sl]
        kh = qk_ref[gi, :, HW + h * DH:HW + (h + 1) * DH] * (DH ** -0.5)
        vh = zv_ref[gi, :, sl]
        c_h = c_ref[gi, h]
        n_h = n_ref[gi, h:h + 1, :]
        s_raw = _bdot_nt(qh, kh)
        q_c = _bdot(qh, c_h)
        q_n = lsum(qh * n_h)
        yield
        m_t = jnp.maximum(inter, row_max)
        w_intra = jnp.exp(dmat - m_t[:, :c])
        w_inter = jnp.exp(inter - m_t)
        s = s_raw * w_intra
        s_v = _bdot(s, vh)
        s_sum = lsum(s)
        m_new = m_t[c - 1:c, :]
        b_last = b_col[c - 1:c, :]
        w_last = jnp.exp(b_last - b_col + i_col - m_new)
        decay = jnp.exp(b_last + m_prev - m_new)
        kw = w_last * kh
        kw_v = _bdot_tn(kw, vh)
        yield
        num = w_inter * q_c + s_v
        den = w_inter * q_n + s_sum
        hh = num / jnp.maximum(jnp.abs(den), jnp.exp(-m_t))
        c_ref[gi, h] = decay * c_h + kw_v
        n_ref[gi, h:h + 1, :] = decay * n_h + jnp.sum(kw, axis=0, keepdims=True)
        m_ref[gi, h:h + 1, :] = m_new
        hh = _sigmoid(zo_ref[gi, :, sl]) * hh
        out_ref[gi, :, sl] = _head_rms(hh) * gn_ref[:, sl]

    for gi in range(group):
        ext_ref[gi, SUBLANES:SUBLANES + c, 0:HW] = zq_ref[gi]
        ext_ref[gi, SUBLANES:SUBLANES + c, HW:2 * HW] = zk_ref[gi]
        conv = cb_ref[...]
        for jj in range(CONV_W):
            r0 = SUBLANES - tail + jj
            conv = conv + cw_ref[jj:jj + 1, :] * ext_ref[gi, r0:r0 + c, :]
        ext_ref[gi, 0:SUBLANES, :] = ext_ref[gi, c:c + SUBLANES, :]
        qk_ref[gi] = conv * _sigmoid(conv)
        gb = zg_ref[gi] + bg
        gates_ref[gi] = jnp.where(lane < HEADS, gb, _log_sigmoid(gb))
    _round_robin([unit(gi, h) for gi in range(group) for h in range(HEADS)])


def _seq_group(b, c):
    rows = 512
    group = max(1, min(b, rows // c, 2 * SUBLANES))
    assert b % group == 0
    return group


def _mlstm(z_main, z_gate, conv_w, conv_b, b_gate, gn_a, conv0, c0, n0, m0, b, l):
    c = _chunk_len(l)
    nc = l // c
    grp = _seq_group(b, c)
    z3 = z_main.reshape(b, l, z_main.shape[-1])
    zspec = lambda col: pl.BlockSpec((grp, c, HW), lambda bi, j: (bi, j, col))
    st = lambda shape: pl.BlockSpec((grp,) + shape, lambda bi, j: (bi,) + (0,) * len(shape))
    m0b = jnp.broadcast_to(m0[:, :, None], (b, HEADS, DH))
    out, c_new, n_new, m_new = pl.pallas_call(
        functools.partial(_mlstm_kernel, c=c, group=grp),
        grid=(b // grp, nc),
        in_specs=[zspec(0), zspec(1), zspec(2), zspec(3),
                  pl.BlockSpec((grp, c, LANES), lambda bi, j: (bi, j, 0)),
                  _full((CONV_W, 2 * HW)), _full((1, 2 * HW)), _full((1, LANES)), _full((1, HW)),
                  st((CONV_W - 1, 2 * HW)), st((HEADS, DH, DH)), st((HEADS, DH)), st((HEADS, DH))],
        out_specs=[pl.BlockSpec((grp, c, HW), lambda bi, j: (bi, j, 0)),
                   st((HEADS, DH, DH)), st((HEADS, DH)), st((HEADS, DH))],
        out_shape=[jax.ShapeDtypeStruct((b, l, HW), F32),
                   jax.ShapeDtypeStruct((b, HEADS, DH, DH), F32),
                   jax.ShapeDtypeStruct((b, HEADS, DH), F32),
                   jax.ShapeDtypeStruct((b, HEADS, DH), F32)],
        scratch_shapes=[pltpu.VMEM((grp, c + SUBLANES, 2 * HW), F32),
                        pltpu.VMEM((grp, c, 2 * HW), F32), pltpu.VMEM((grp, c, LANES), F32)],
        compiler_params=_params(2),
        name="mlstm",
    )(z3, z3, z3, z3, z_gate.reshape(b, l, LANES), conv_w, conv_b.reshape(1, -1),
      jnp.pad(b_gate, (0, LANES - 2 * HEADS)).reshape(1, LANES), gn_a.reshape(1, -1),
      conv0, c0, n0, m0b)
    return out.reshape(b * l, HW), c_new, n_new, m_new[:, :, 0]


def _rope_table_kernel(inv_ref, sign_ref, cos_ref, sin_ref, *, pos0, rows):
    i = pl.program_id(0)
    pos = (pos0 + i * rows + lax.broadcasted_iota(jnp.int32, (rows, LANES), 0)).astype(F32)
    ang = pos * inv_ref[...]
    cos_ref[...] = jnp.cos(ang)
    sin_ref[...] = jnp.sin(ang) * sign_ref[...]


def _rope_tables(l, pos0):
    half = DH // 2
    inv = ROPE_BASE ** (-jnp.arange(half, dtype=F32) / half)
    inv2 = jnp.concatenate([inv, inv]).reshape(1, DH)
    sign = jnp.concatenate([-jnp.ones((half,), F32), jnp.ones((half,), F32)]).reshape(1, DH)
    rows = min(l, 512)
    assert l % rows == 0
    return pl.pallas_call(
        functools.partial(_rope_table_kernel, pos0=pos0, rows=rows),
        grid=(l // rows,),
        in_specs=[_full((1, DH)), _full((1, DH))],
        out_specs=[pl.BlockSpec((rows, DH), lambda i: (i, 0))] * 2,
        out_shape=[jax.ShapeDtypeStruct((l, DH), F32)] * 2,
        compiler_params=_params(1),
        name="rope_table",
    )(inv2, sign)


def _ret_kernel(zq_ref, zk_ref, zv_ref, zg_ref, cos_ref, sin_ref, s0_ref, out_ref, s_ref,
                *, c, group):
    j = pl.program_id(1)

    @pl.when(j == 0)
    def _():
        s_ref[...] = s0_ref[...]

    cosf = cos_ref[...]
    sinf = sin_ref[...]
    ti = lax.broadcasted_iota(jnp.int32, (c, c), 0)
    si = lax.broadcasted_iota(jnp.int32, (c, c), 1)
    rel = jnp.maximum(ti - si, 0).astype(F32)
    tcol = lax.broadcasted_iota(jnp.int32, (c, 1), 0).astype(F32)

    def rope(x):
        return x * cosf + pltpu.roll(x, DH // 2, axis=1) * sinf

    def unit(gi, h, decay, inter, kdecay, cdecay):
        sl = slice(h * DH, (h + 1) * DH)
        qr = rope(zq_ref[gi, :, sl])
        kr = rope(zk_ref[gi, :, sl]) * (DH ** -0.5)
        yield
        vh = zv_ref[gi, :, sl]
        s_h = s_ref[gi, h]
        qk = _bdot_nt(qr, kr)
        q_s = _bdot(qr, s_h)
        k_v = _bdot_tn(kr * kdecay, vh)
        yield
        o = q_s * inter + _bdot(qk * decay, vh)
        s_ref[gi, h] = cdecay * s_h + k_v
        yield
        gate = zg_ref[gi, :, sl]
        out_ref[gi, :, sl] = _head_rms(o) * (gate * _sigmoid(gate))

    units = []
    for h in range(HEADS):
        lg = math.log1p(-(2.0 ** (-5.0 - h)))
        decay = jnp.where(ti >= si, jnp.exp(rel * lg), 0.0)
        inter = jnp.exp((tcol + 1.0) * lg)
        kdecay = jnp.exp((c - 1.0 - tcol) * lg)
        cdecay = math.exp(c * lg)
        units += [unit(gi, h, decay, inter, kdecay, cdecay) for gi in range(group)]
    _round_robin(units)


def _retention(z_main, cos_t, sin_t, s0, b, l):
    c = _chunk_len(l)
    nc = l // c
    grp = _seq_group(b, c)
    z3 = z_main.reshape(b, l, z_main.shape[-1])
    zspec = lambda col: pl.BlockSpec((grp, c, HW), lambda bi, j: (bi, j, col))
    st = pl.BlockSpec((grp, HEADS, DH, DH), lambda bi, j: (bi, 0, 0, 0))
    tab = pl.BlockSpec((c, DH), lambda bi, j: (j, 0))
    out, s_new = pl.pallas_call(
        functools.partial(_ret_kernel, c=c, group=grp),
        grid=(b // grp, nc),
        in_specs=[zspec(4), zspec(5), zspec(6), zspec(7), tab, tab, st],
        out_specs=[pl.BlockSpec((grp, c, HW), lambda bi, j: (bi, j, 0)), st],
        out_shape=[jax.ShapeDtypeStruct((b, l, HW), F32),
                   jax.ShapeDtypeStruct((b, HEADS, DH, DH), F32)],
        compiler_params=_params(2),
        name="retention",
    )(z3, z3, z3, z3, cos_t, sin_t, s0)
    return out.reshape(b * l, HW), s_new


def _hgrn_kernel(zq_ref, zf_ref, zi_ref, zg_ref, lbl_ref, gn_ref, s0_ref, out_ref, s_ref,
                 kk_ref, bcum_ref, *, c, sc, layer, group):
    j = pl.program_id(1)

    @pl.when(j == 0)
    def _():
        s_ref[...] = s0_ref[...]

    lbl = lbl_ref[...]
    e = jnp.exp(lbl - jnp.max(lbl, axis=0, keepdims=True))
    sm = e / jnp.sum(e, axis=0, keepdims=True)
    cum = sm[0:1, :]
    for r in range(1, layer + 1):
        cum = cum + sm[r:r + 1, :]
    lb = cum - sm[0:1, :]

    oml = 1.0 - lb
    ri = lax.broadcasted_iota(jnp.int32, (c, c), 0)
    ci = lax.broadcasted_iota(jnp.int32, (c, c), 1)
    tril = jnp.where(ri >= ci, 1.0, 0.0).astype(BF16)
    rowi = lax.broadcasted_iota(jnp.int32, (sc, 1), 0)
    lane_s = lax.broadcasted_iota(jnp.int32, (SUBLANES, sc), 1)
    e_r = lax.broadcasted_iota(jnp.int32, (DH, DH), 0)
    e_c = lax.broadcasted_iota(jnp.int32, (DH, DH), 1)
    eye = e_r == e_c

    def unit(gi, h):
        sl = slice(h * DH, (h + 1) * DH)
        bh = bcum_ref[gi, :, sl]
        qh = zq_ref[gi, :, sl] * (DH ** -0.5)
        kh = kk_ref[gi, :, sl]
        vh = zi_ref[gi, :, sl]
        s_h = s_ref[gi, h]
        b_last = bh[c - 1:c, :]
        o_inter = _bdot(qh * jnp.exp(bh), s_h)
        k_v = _bdot_tn(kh * jnp.exp(b_last - bh), vh)
        dec_col = jnp.sum(jnp.where(eye, jnp.exp(b_last), 0.0), axis=1, keepdims=True)
        yield
        s_ref[gi, h] = dec_col * s_h + k_v
        blocks = []
        for blk in range(c // sc):
            r0 = blk * sc
            b_i = bh[r0:r0 + sc]
            q_i = qh[r0:r0 + sc]
            k_i = kh[r0:r0 + sc]
            v_i = vh[r0:r0 + sc]
            att_prev = None
            if blk > 0:
                ref_row = bh[r0 - 1:r0, :]
                a_i = q_i * jnp.exp(b_i - ref_row)
                k_prev = kh[0:r0] * jnp.exp(ref_row - bh[0:r0])
                att_prev = _bdot_nt(a_i, k_prev)
            b2_i = b_i * LOG2_E
            cols = []
            for s in range(sc):
                lo = (s // SUBLANES) * SUBLANES
                dec = jnp.exp2(jnp.where(rowi[lo:] >= s, b2_i[lo:] - b2_i[s:s + 1, :], NEG))
                cols.append(jnp.sum(q_i[lo:] * dec * k_i[s:s + 1, :], axis=1, keepdims=True))
            yield
            att = [jnp.zeros((SUBLANES, sc), F32) for _ in range(sc // SUBLANES)]
            for s in range(sc):
                lo = (s // SUBLANES) * SUBLANES
                for pi in range(lo // SUBLANES, sc // SUBLANES):
                    piece = cols[s][pi * SUBLANES - lo:(pi + 1) * SUBLANES - lo]
                    att[pi] = jnp.where(lane_s == s, piece, att[pi])
            att = jnp.concatenate(att, axis=0) if len(att) > 1 else att[0]
            o_i = _bdot(att, v_i)
            if att_prev is not None:
                o_i = o_i + _bdot(att_prev, vh[0:r0])
            blocks.append(o_i)
        yield
        o = o_inter + (jnp.concatenate(blocks, axis=0) if len(blocks) > 1 else blocks[0])
        gate = zg_ref[gi, :, sl]
        out_ref[gi, :, sl] = _head_rms(o) * gn_ref[:, sl] * (gate * _sigmoid(gate))

    for gi in range(group):
        zf = zf_ref[gi]
        ez = jnp.exp(-jnp.abs(zf))
        big = 1.0 / (1.0 + ez)
        small = ez * big
        pos = zf >= 0.0
        logf = jnp.log(lb + oml * jnp.where(pos, big, small))
        kk_ref[gi] = oml * jnp.where(pos, small, big)

        p0 = logf.astype(BF16)
        r1 = logf - p0.astype(F32)
        p1 = r1.astype(BF16)
        p2 = (r1 - p1.astype(F32)).astype(BF16)
        bcum_ref[gi] = (jnp.dot(tril, p0, preferred_element_type=F32)
                        + jnp.dot(tril, p1, preferred_element_type=F32)
                        + jnp.dot(tril, p2, preferred_element_type=F32))
    _round_robin([unit(gi, h) for gi in range(group) for h in range(HEADS)])


def _hgrn(z_cd, lb_logits, gn_c, s0, b, l, layer):
    c = _chunk_len(l)
    sc = min(c, 16)
    nc = l // c
    grp = _seq_group(b, c)
    depth = lb_logits.shape[0]
    z3 = z_cd.reshape(b, l, z_cd.shape[-1])
    zspec = lambda col: pl.BlockSpec((grp, c, HW), lambda bi, j: (bi, j, col))
    st = pl.BlockSpec((grp, HEADS, DH, DH), lambda bi, j: (bi, 0, 0, 0))
    out, s_new = pl.pallas_call(
        functools.partial(_hgrn_kernel, c=c, sc=sc, layer=layer, group=grp),
        grid=(b // grp, nc),
        in_specs=[zspec(0), zspec(1), zspec(2), zspec(3), _full((depth, HW)), _full((1, HW)), st],
        out_specs=[pl.BlockSpec((grp, c, HW), lambda bi, j: (bi, j, 0)), st],
        out_shape=[jax.ShapeDtypeStruct((b, l, HW), F32),
                   jax.ShapeDtypeStruct((b, HEADS, DH, DH), F32)],
        scratch_shapes=[pltpu.VMEM((grp, c, HW), F32), pltpu.VMEM((grp, c, HW), F32)],
        compiler_params=_params(2),
        name="hgrn2",
    )(z3, z3, z3, z3, lb_logits, gn_c.reshape(1, -1), s0)
    return out.reshape(b * l, HW), s_new


def _s5_kernel(u_ref, are_ref, aim_ref, ldt_ref, bre_ref, bim_ref, cre_ref, cim_ref, d_ref,
               wglu_ref, bglu_ref, x0r_ref, x0i_ref,
               s_ref, xr_ref, xi_ref, ar_sc, ai_sc, bbr_sc, bbi_sc, bur_sc, bui_sc, *, ct):
    j = pl.program_id(1)
    ns = are_ref.shape[-1]
    nu = u_ref.shape[-1]
    hs = ns // 2
    hu = nu // 2
    rows = ct * SUBLANES

    @pl.when(j == 0)
    def _():
        a_re = are_ref[...]
        a_im = aim_ref[...]
        dt = jnp.exp(ldt_ref[...])
        mag = jnp.exp(dt * a_re)
        ar = mag * jnp.cos(dt * a_im)
        ai = mag * jnp.sin(dt * a_im)
        ar_sc[...] = jnp.broadcast_to(ar, (SUBLANES, ns))
        ai_sc[...] = jnp.broadcast_to(ai, (SUBLANES, ns))
        den = a_re * a_re + a_im * a_im
        nr = ar - 1.0
        zr = (nr * a_re + ai * a_im) / den
        zi = (ai * a_re - nr * a_im) / den
        for hg in range(2):
            us = slice(hg * hu, (hg + 1) * hu)
            ss = slice(hg * hs, (hg + 1) * hs)
            bbr_sc[us, :] = (zr[:, ss] * bre_ref[us, :] - zi[:, ss] * bim_ref[us, :]).astype(BF16)
            bbi_sc[us, :] = (zr[:, ss] * bim_ref[us, :] + zi[:, ss] * bre_ref[us, :]).astype(BF16)
        xr_ref[...] = x0r_ref[...]
        xi_ref[...] = x0i_ref[...]

    u = u_ref[...].reshape(rows, nu)
    ub = u.astype(BF16)
    for hg in range(2):
        us = slice(hg * hu, (hg + 1) * hu)
        ss = slice(hg * hs, (hg + 1) * hs)
        bur_sc[:, ss] = jnp.dot(ub[:, us], bbr_sc[us, :], preferred_element_type=F32)
        bui_sc[:, ss] = jnp.dot(ub[:, us], bbi_sc[us, :], preferred_element_type=F32)

    lane_chunk = 4 * LANES
    for lc in range(ns // lane_chunk):
        ls = slice(lc * lane_chunk, (lc + 1) * lane_chunk)
        ar = ar_sc[:, ls]
        ai = ai_sc[:, ls]

        def step(t, carry):
            xr, xi = carry
            r0 = pl.multiple_of(t * SUBLANES, SUBLANES)
            nxr = ar * xr - ai * xi + bur_sc[pl.ds(r0, SUBLANES), ls]
            nxi = ar * xi + ai * xr + bui_sc[pl.ds(r0, SUBLANES), ls]
            bur_sc[pl.ds(r0, SUBLANES), ls] = nxr
            bui_sc[pl.ds(r0, SUBLANES), ls] = nxi
            return nxr, nxi

        xr, xi = lax.fori_loop(0, ct, step, (xr_ref[:, ls], xi_ref[:, ls]), unroll=SUBLANES)
        xr_ref[:, ls] = xr
        xi_ref[:, ls] = xi

    ys = []
    for hg in range(2):
        ss = slice(hg * hs, (hg + 1) * hs)
        ys.append(jnp.dot(bur_sc[:, ss].astype(BF16), cre_ref[ss, :], preferred_element_type=F32)
                  - jnp.dot(bui_sc[:, ss].astype(BF16), cim_ref[ss, :], preferred_element_type=F32))
    y = jnp.concatenate(ys, axis=1) + d_ref[...] * u
    a = 0.5 * y * (1.0 + jnp.tanh(math.sqrt(2.0 / math.pi) * (y + 0.044715 * (y * y * y))))
    s = a * _sigmoid(jnp.dot(a.astype(BF16), wglu_ref[...], preferred_element_type=F32) + bglu_ref[...])
    s_ref[...] = s.reshape(ct, SUBLANES, nu)


def _s5_block_diag(bmat, cmat):
    g, p, hgrp = bmat.shape
    gh = g // 2
    eye = jnp.eye(gh, dtype=F32)
    b4 = bmat.reshape(2, gh, p, hgrp)
    bc = jnp.einsum('agph,gk->aghkp', b4, eye).reshape(2 * gh * hgrp, gh * p)
    c4 = cmat.reshape(2, gh, hgrp, p)
    cc = jnp.einsum('aghp,gk->agpkh', c4, eye).reshape(2 * gh * p, gh * hgrp)
    return bc, cc


def _s5(u_tm, a_re, a_im, log_dt, b_re, b_im, c_re, c_im, d_skip, w_glu, b_glu, x0r, x0i):
    l, b, nu = u_tm.shape
    g, p = a_re.shape
    ns = g * p
    assert b % SUBLANES == 0 and nu == g * S5_GROUP and (g // 2) * S5_GROUP == MXU_DIM
    ct = _chunk_len(l)
    nct = l // ct
    bre_c, cre_c = _s5_block_diag(b_re, c_re)
    bim_c, cim_c = _s5_block_diag(b_im, c_im)
    ldt = jnp.broadcast_to(log_dt[:, None], (g, p)).reshape(1, ns)
    rows = ct * SUBLANES
    xspec = pl.BlockSpec((SUBLANES, ns), lambda bb, j: (bb, 0))
    s, xr, xi = pl.pallas_call(
        functools.partial(_s5_kernel, ct=ct),
        grid=(b // SUBLANES, nct),
        in_specs=[pl.BlockSpec((ct, SUBLANES, nu), lambda bb, j: (j, bb, 0)),
                  _full((1, ns)), _full((1, ns)), _full((1, ns)),
                  _full((nu, ns // 2)), _full((nu, ns // 2)),
                  _full((ns, nu // 2)), _full((ns, nu // 2)),
                  _full((1, nu)), _full((nu, nu)), _full((1, nu)), xspec, xspec],
        out_specs=[pl.BlockSpec((ct, SUBLANES, nu), lambda bb, j: (j, bb, 0)), xspec, xspec],
        out_shape=[jax.ShapeDtypeStruct((l, b, nu), F32),
                   jax.ShapeDtypeStruct((b, ns), F32), jax.ShapeDtypeStruct((b, ns), F32)],
        scratch_shapes=[pltpu.VMEM((SUBLANES, ns), F32), pltpu.VMEM((SUBLANES, ns), F32),
                        pltpu.VMEM((nu, ns // 2), BF16), pltpu.VMEM((nu, ns // 2), BF16),
                        pltpu.VMEM((rows, ns), F32), pltpu.VMEM((rows, ns), F32)],
        compiler_params=_params(2, VMEM_LIMIT),
        name="s5",
    )(u_tm, a_re.reshape(1, ns), a_im.reshape(1, ns), ldt, bre_c, bim_c,
      cre_c.astype(BF16), cim_c.astype(BF16), d_skip.reshape(1, nu), w_glu.astype(BF16),
      b_glu.reshape(1, nu), x0r.reshape(b, ns), x0i.reshape(b, ns))
    return s, xr.reshape(b, g, p), xi.reshape(b, g, p)


def _trunk(groups, prm):
    d = groups[0]['x'].shape[-1]
    tm = 512
    info = []
    for grp in groups:
        b, l, _ = grp['x'].shape
        assert l >= CONV_W - 1 and (b * l) % tm == 0
        info.append(dict(b=b, l=l, t=b * l, direct_tm=b == SUBLANES and l % tm == 0))
    hs = [grp['x'].reshape(-1, d) for grp in groups]
    pps = [grp['p'].reshape(grp['p'].shape[0], -1, grp['p'].shape[-1]) for grp in groups]

    zs = _inproj(hs, prm['norm_mix'][0], prm['w_ab'], (8 * HW, LANES), tm, [None] * len(groups))
    ab_new, post_in = [], []
    for grp, inf, h, pp, (z_main, z_gate) in zip(groups, info, hs, pps, zs):
        b, l = inf['b'], inf['l']
        conv0, c0, n0, m0, ret0 = grp['ab_state']
        conv_new = z_main.reshape(b, l, 8 * HW)[:, l - (CONV_W - 1):, :2 * HW]
        hm, c_new, n_new, m_new = _mlstm(z_main, z_gate, prm['conv_w_ab'][0], prm['conv_b_ab'][0],
                                         prm['b_gate_ab'][0], prm['gn_a'][0], conv0[0], c0[0], n0[0],
                                         m0[0], b, l)
        cos_t, sin_t = _rope_tables(l, grp['pos0'])
        hr, ret_new = _retention(z_main, cos_t, sin_t, ret0[0], b, l)
        ab_new.append((conv_new[None], c_new[None], n_new[None], m_new[None], ret_new[None]))
        post_in.append((h, hm, hr, pp, None))
    hs = _post(post_in, prm['w_out_ab'], prm['norm_ff'][0], prm['w_ff1'], prm['w_ff2'],
               prm['norm_ple'][0], prm['w_ple_gate'], prm['w_ple_proj'], prm['norm_final'],
               layer=0, final=False, tm=tm)

    tml = [(inf['b'], inf['l']) if inf['direct_tm'] else None for inf in info]
    zs = _inproj(hs, prm['norm_mix'][1], prm['w_cd'], (4 * HW, HW), tm, tml)
    cd_new, post_in = [], []
    for grp, inf, h, pp, (z_cd, su) in zip(groups, info, hs, pps, zs):
        b, l, t = inf['b'], inf['l'], inf['t']
        hg0, x0r, x0i = grp['cd_state']
        o, hg_new = _hgrn(z_cd, prm['lb_logits'], prm['gn_c'][0], hg0[0], b, l, layer=1)
        u_tm = (su.reshape(l, b, HW) if inf['direct_tm']
                else jnp.transpose(su.reshape(b, l, HW), (1, 0, 2)))
        s_tm, xr, xi = _s5(u_tm, prm['s5_A_re'][0], prm['s5_A_im'][0], prm['s5_log_dt'][0],
                           prm['s5_B_re'][0], prm['s5_B_im'][0], prm['s5_C_re'][0], prm['s5_C_im'][0],
                           prm['s5_D'][0], prm['w_glu'][0], prm['b_glu'][0], x0r[0], x0i[0])
        s_in = (s_tm.reshape(l, b * HW) if inf['direct_tm']
                else jnp.transpose(s_tm, (1, 0, 2)).reshape(t, HW))
        cd_new.append((hg_new[None], xr[None], xi[None]))
        post_in.append((h, o, s_in, pp, (b, l) if inf['direct_tm'] else None))
    ys = _post(post_in, prm['w_out_cd'], prm['norm_ff'][1], prm['w_ff1'], prm['w_ff2'],
               prm['norm_ple'][1], prm['w_ple_gate'], prm['w_ple_proj'], prm['norm_final'],
               layer=1, final=True, tm=tm)
    ys = [y.reshape(grp['x'].shape) for y, grp in zip(ys, groups)]
    return ys, ab_new, cd_new


def kernel(x_prompt, x_sample, state_mlstm_conv, state_mlstm_C, state_mlstm_n, state_mlstm_m, state_ret, state_hgrn, state_s5_re, state_s5_im, p_prompt, p_sample, norm_mix, norm_ff, norm_ple, norm_final, w_in_ab, b_gate_ab, conv_w_ab, conv_b_ab, gn_a, w_out_ab, w_in_cd, lb_logits, gn_c, s5_A_re, s5_A_im, s5_log_dt, s5_B_re, s5_B_im, s5_C_re, s5_C_im, s5_D, w_glu, b_glu, w_out_cd, w_ff1, w_ff2, w_ple_proj, w_ple_gate):
    assert norm_mix.shape[0] == 2, "two layers: (mLSTM || retention), (HGRN2 || S5)"
    w_ab = w_in_ab[0]
    gate0 = 4 * HW
    w_ab = jnp.concatenate([w_ab[:, :gate0], w_ab[:, gate0 + 2 * HEADS:],
                            w_ab[:, gate0:gate0 + 2 * HEADS],
                            jnp.zeros((w_ab.shape[0], LANES - 2 * HEADS), w_ab.dtype)], axis=1)
    prm = dict(norm_mix=norm_mix, norm_ff=norm_ff, norm_ple=norm_ple, norm_final=norm_final,
               w_ab=w_ab.astype(BF16), b_gate_ab=b_gate_ab, conv_w_ab=conv_w_ab, conv_b_ab=conv_b_ab,
               gn_a=gn_a, w_out_ab=w_out_ab[0].astype(BF16), w_cd=w_in_cd[0].astype(BF16),
               lb_logits=lb_logits, gn_c=gn_c, s5_A_re=s5_A_re, s5_A_im=s5_A_im,
               s5_log_dt=s5_log_dt, s5_B_re=s5_B_re, s5_B_im=s5_B_im, s5_C_re=s5_C_re,
               s5_C_im=s5_C_im, s5_D=s5_D, w_glu=w_glu, b_glu=b_glu,
               w_out_cd=w_out_cd[0].astype(BF16), w_ff1=w_ff1.astype(BF16), w_ff2=w_ff2.astype(BF16),
               w_ple_proj=w_ple_proj.astype(BF16), w_ple_gate=w_ple_gate.astype(BF16))

    bp, lp, _ = x_prompt.shape
    z = lambda *s: jnp.zeros(s, F32)
    zero_ab = (z(1, bp, CONV_W - 1, 2 * HW), z(1, bp, HEADS, DH, DH), z(1, bp, HEADS, DH),
               z(1, bp, HEADS), z(1, bp, HEADS, DH, DH))
    zero_cd = (z(1, bp, HEADS, DH, DH),) + (z(*((1, bp) + s5_A_re.shape[1:])),) * 2
    groups = [dict(x=x_prompt, p=p_prompt, pos0=0, ab_state=zero_ab, cd_state=zero_cd),
              dict(x=x_sample, p=p_sample, pos0=PAST_LEN,
                   ab_state=(state_mlstm_conv, state_mlstm_C, state_mlstm_n, state_mlstm_m, state_ret),
                   cd_state=(state_hgrn, state_s5_re, state_s5_im))]
    (y_p, y_s), (ab_p, ab_s), (cd_p, cd_s) = _trunk(groups, prm)
    return (y_p, y_s,
            ab_p[0], ab_s[0], ab_p[1], ab_s[1], ab_p[2], ab_s[2], ab_p[3], ab_s[3], ab_p[4], ab_s[4],
            cd_p[0], cd_s[0], cd_p[1], cd_s[1], cd_p[2], cd_s[2])
```

```python
import functools
import math

import jax
import jax.numpy as jnp
from jax import lax
from jax.experimental import pallas as pl
from jax.experimental.pallas import tpu as pltpu

F32 = jnp.float32
BF16 = jnp.bfloat16

EPS = 1e-6
NEG = -1e30
LOG2_E = math.log2(math.e)
ROPE_BASE = 10000.0
PAST_LEN = 16384
CHUNK = 64
HEADS = 4
DH = 128
HW = HEADS * DH
CONV_W = 4
S5_GROUP = 16
S5_STATE = 64
SUBLANES = 8
LANES = 128
MXU_DIM = 256
VMEM_LIMIT = 56 * 1024 * 1024


def _params(n_axes, vmem=None):
    return pltpu.CompilerParams(dimension_semantics=("arbitrary",) * n_axes, vmem_limit_bytes=vmem)


def _full(shape):
    return pl.BlockSpec(shape, lambda *_: (0,) * len(shape))


def _bdot(a, b):
    return jnp.dot(a.astype(BF16), b.astype(BF16), preferred_element_type=F32)


def _bdot_nt(a, b):
    return lax.dot_general(a.astype(BF16), b.astype(BF16), (((1,), (1,)), ((), ())),
                           preferred_element_type=F32)


def _bdot_tn(a, b):
    return lax.dot_general(a.astype(BF16), b.astype(BF16), (((0,), (0,)), ((), ())),
                           preferred_element_type=F32)


def _sigmoid(x):
    return 1.0 / (1.0 + jnp.exp(-x))


def _log_sigmoid(x):
    return jnp.minimum(x, 0.0) - jnp.log(1.0 + jnp.exp(-jnp.abs(x)))


def _rms(x, g):
    return x * lax.rsqrt(jnp.mean(x * x, axis=-1, keepdims=True) + EPS) * g


def _head_rms(x):
    return x * lax.rsqrt(jnp.mean(x * x, axis=-1, keepdims=True) + EPS)


def _chunk_len(length):
    return CHUNK if length % CHUNK == 0 else length


def _round_robin(gens):
    gens = list(gens)
    while gens:
        alive = []
        for g in gens:
            try:
                next(g)
                alive.append(g)
            except StopIteration:
                pass
        gens = alive


def _step_ranges(counts):
    starts = [0]
    for n in counts:
        starts.append(starts[-1] + n)
    return starts


def _local(i, start, count):
    return jnp.clip(i - start, 0, count - 1)


def _inproj_kernel(*refs, n_groups, n_out, starts):
    x_refs = refs[:n_groups]
    g_ref, w_ref = refs[n_groups:n_groups + 2]
    out_refs = refs[n_groups + 2:]
    i = pl.program_id(0)

    def run(x_ref, outs):
        hn = _rms(x_ref[...], g_ref[...]).astype(BF16)
        off = 0
        for o_ref in outs:
            n = o_ref.shape[-1]
            for n0 in range(0, n, HW):
                nn = min(HW, n - n0)
                o_ref[:, n0:n0 + nn] = jnp.dot(hn, w_ref[:, off + n0:off + n0 + nn],
                                               preferred_element_type=F32)
            off += n

    for gi in range(n_groups):
        @pl.when((i >= starts[gi]) & (i < starts[gi + 1]))
        def _(gi=gi):
            run(x_refs[gi], out_refs[gi * n_out:(gi + 1) * n_out])


def _inproj(hs, g, w, widths, tm, time_major_last):
    d = hs[0].shape[1]
    n = w.shape[1]
    assert sum(widths) == n
    counts = [h.shape[0] // tm for h in hs]
    starts = _step_ranges(counts)
    in_specs, out_specs, out_shape = [], [], []
    for k, h in enumerate(hs):
        t = h.shape[0]
        assert t % tm == 0
        loc = functools.partial(_local, start=starts[k], count=counts[k])
        in_specs.append(pl.BlockSpec((tm, d), lambda i, loc=loc: (loc(i), 0)))
        for wi, wd in enumerate(widths):
            if wi == len(widths) - 1 and time_major_last[k] is not None:
                b, l = time_major_last[k]
                assert l % tm == 0
                nl = l // tm
                out_shape.append(jax.ShapeDtypeStruct((l, b * wd), F32))
                out_specs.append(pl.BlockSpec((tm, wd), lambda i, loc=loc, nl=nl: (loc(i) % nl, loc(i) // nl)))
            else:
                out_shape.append(jax.ShapeDtypeStruct((t, wd), F32))
                out_specs.append(pl.BlockSpec((tm, wd), lambda i, loc=loc: (loc(i), 0)))
    outs = pl.pallas_call(
        functools.partial(_inproj_kernel, n_groups=len(hs), n_out=len(widths), starts=tuple(starts)),
        grid=(starts[-1],),
        in_specs=in_specs + [_full((1, d)), _full((d, n))],
        out_specs=out_specs,
        out_shape=out_shape,
        compiler_params=_params(1, VMEM_LIMIT),
        name="inproj",
    )(*hs, g.reshape(1, d), w)
    nw = len(widths)
    return [outs[k * nw:(k + 1) * nw] for k in range(len(hs))]


def _post_kernel(*refs, n_groups, starts, final, ff_chunk):
    grp_in = [refs[4 * k:4 * k + 4] for k in range(n_groups)]
    (wo_ref, gff_ref, w1_ref, w2_ref, gple_ref, wg_ref, wp_ref,
     gfin_ref) = refs[4 * n_groups:4 * n_groups + 8]
    out_refs = refs[4 * n_groups + 8:]
    i = pl.program_id(0)

    def run(h_ref, ma_ref, mb_ref, p_ref, o_ref):
        half = ma_ref.shape[-1]
        h = h_ref[...]
        h = h + (jnp.dot(ma_ref[...].astype(BF16), wo_ref[0:half, :], preferred_element_type=F32)
                 + jnp.dot(mb_ref[...].astype(BF16), wo_ref[half:2 * half, :],
                           preferred_element_type=F32))
        hn = _rms(h, gff_ref[...]).astype(BF16)
        d_ff = w1_ref.shape[1]
        acc = jnp.zeros_like(h)
        for f0 in range(0, d_ff, ff_chunk):
            a = jnp.dot(hn, w1_ref[:, f0:f0 + ff_chunk], preferred_element_type=F32)
            a = jnp.square(jnp.maximum(a, 0.0))
            acc = acc + jnp.dot(a.astype(BF16), w2_ref[f0:f0 + ff_chunk, :], preferred_element_type=F32)
        h = h + acc
        gate = _sigmoid(jnp.dot(_rms(h, gple_ref[...]).astype(BF16), wg_ref[...],
                                preferred_element_type=F32))
        h = h + gate * jnp.dot(p_ref[...].astype(BF16), wp_ref[...], preferred_element_type=F32)
        o_ref[...] = _rms(h, gfin_ref[...]) if final else h

    for k in range(n_groups):
        @pl.when((i >= starts[k]) & (i < starts[k + 1]))
        def _(k=k):
            run(*grp_in[k], out_refs[k])


def _post(groups, wo, gff, w1, w2, gple, wg, wp, gfin, *, layer, final, tm):
    d = groups[0][0].shape[1]
    d_ff = w1.shape[-1]
    counts = [grp[0].shape[0] // tm for grp in groups]
    starts = _step_ranges(counts)
    in_specs, out_specs, out_shape, operands = [], [], [], []
    for k, (h, mix_a, mix_b, p, b_time_major) in enumerate(groups):
        t = h.shape[0]
        half = mix_a.shape[-1]
        pd = p.shape[-1]
        assert t % tm == 0
        loc = functools.partial(_local, start=starts[k], count=counts[k])
        row = lambda i, loc=loc: (loc(i), 0)
        mb_spec = pl.BlockSpec((tm, half), row)
        if b_time_major is not None:
            _, l = b_time_major
            assert l % tm == 0
            nl = l // tm
            mb_spec = pl.BlockSpec((tm, half), lambda i, loc=loc, nl=nl: (loc(i) % nl, loc(i) // nl))
        in_specs += [pl.BlockSpec((tm, d), row), pl.BlockSpec((tm, half), row), mb_spec,
                     pl.BlockSpec((None, tm, pd), lambda i, loc=loc: (layer, loc(i), 0))]
        operands += [h, mix_a, mix_b, p]
        out_specs.append(pl.BlockSpec((tm, d), row))
        out_shape.append(jax.ShapeDtypeStruct((t, d), F32))
    pd = groups[0][3].shape[-1]
    lw = lambda r, cdim: pl.BlockSpec((None, r, cdim), lambda i: (layer, 0, 0))
    return pl.pallas_call(
        functools.partial(_post_kernel, n_groups=len(groups), starts=tuple(starts), final=final,
                          ff_chunk=1024),
        grid=(starts[-1],),
        in_specs=in_specs + [_full((d, d)), _full((1, d)), lw(d, d_ff), lw(d_ff, d), _full((1, d)),
                             lw(d, d), lw(pd, d), _full((1, d))],
        out_specs=out_specs,
        out_shape=out_shape,
        compiler_params=_params(1, VMEM_LIMIT),
        name="post",
    )(*operands, wo, gff.reshape(1, d), w1, w2, gple.reshape(1, d), wg, wp, gfin.reshape(1, d))


def _mlstm_kernel(zq_ref, zk_ref, zv_ref, zo_ref, zg_ref, cw_ref, cb_ref, bg_ref, gn_ref,
                  conv0_ref, c0_ref, n0_ref, m0_ref,
                  out_ref, c_ref, n_ref, m_ref, ext_ref, qk_ref, gates_ref, *, c, group):
    j = pl.program_id(1)
    tail = CONV_W - 1

    @pl.when(j == 0)
    def _():
        c_ref[...] = c0_ref[...]
        n_ref[...] = n0_ref[...]
        m_ref[...] = m0_ref[...]
        for gi in range(group):
            ext_ref[gi, 0:SUBLANES, :] = jnp.zeros((SUBLANES, 2 * HW), F32)
            ext_ref[gi, SUBLANES - tail:SUBLANES, :] = conv0_ref[gi]

    ri = lax.broadcasted_iota(jnp.int32, (c, c), 0)
    ci = lax.broadcasted_iota(jnp.int32, (c, c), 1)
    eye = ri == ci
    tril = ri >= ci
    lane = lax.broadcasted_iota(jnp.int32, (c, LANES), 1)
    bg = bg_ref[...]

    def lsum(x):
        return jnp.broadcast_to(jnp.sum(x, axis=1, keepdims=True), (c, DH))

    def unit(gi, h):
        sl = slice(h * DH, (h + 1) * DH)
        gates = gates_ref[gi]
        i_col = lsum(jnp.where(lane == h, gates, 0.0))
        f_col = lsum(jnp.where(lane == HEADS + h, gates, 0.0))
        yield
        b_row = jnp.sum(jnp.where(ri <= ci, f_col[:, :c], 0.0), axis=0, keepdims=True)
        b_col = lsum(jnp.where(eye, b_row, 0.0))
        i_row = jnp.sum(jnp.where(eye, i_col[:, :c], 0.0), axis=0, keepdims=True)
        yield
        m_prev = m_ref[gi, h:h + 1, :]
        dmat = jnp.where(tril, b_col[:, :c] - b_row + i_row, NEG)
        inter = b_col + m_prev
        row_max = jnp.broadcast_to(jnp.max(dmat, axis=1, keepdims=True), (c, DH))
        qh = qk_ref[gi, :, sl]
        kh = qk_ref[gi, :, HW + h * DH:HW + (h + 1) * DH] * (DH ** -0.5)
        vh = zv_ref[gi, :, sl]
        c_h = c_ref[gi, h]
        n_h = n_ref[gi, h:h + 1, :]
        s_raw = _bdot_nt(qh, kh)
        q_c = _bdot(qh, c_h)
        q_n = lsum(qh * n_h)
        yield
        m_t = jnp.maximum(inter, row_max)
        w_intra = jnp.exp(dmat - m_t[:, :c])
        w_inter = jnp.exp(inter - m_t)
        s = s_raw * w_intra
        s_v = _bdot(s, vh)
        s_sum = lsum(s)
        m_new = m_t[c - 1:c, :]
        b_last = b_col[c - 1:c, :]
        w_last = jnp.exp(b_last - b_col + i_col - m_new)
        decay = jnp.exp(b_last + m_prev - m_new)
        kw = w_last * kh
        kw_v = _bdot_tn(kw, vh)
        yield
        num = w_inter * q_c + s_v
        den = w_inter * q_n + s_sum
        hh = num / jnp.maximum(jnp.abs(den), jnp.exp(-m_t))
        c_ref[gi, h] = decay * c_h + kw_v
        n_ref[gi, h:h + 1, :] = decay * n_h + jnp.sum(kw, axis=0, keepdims=True)
        m_ref[gi, h:h + 1, :] = m_new
        hh = _sigmoid(zo_ref[gi, :, sl]) * hh
        out_ref[gi, :, sl] = _head_rms(hh) * gn_ref[:, sl]

    for gi in range(group):
        ext_ref[gi, SUBLANES:SUBLANES + c, 0:HW] = zq_ref[gi]
        ext_ref[gi, SUBLANES:SUBLANES + c, HW:2 * HW] = zk_ref[gi]
        conv = cb_ref[...]
        for jj in range(CONV_W):
            r0 = SUBLANES - tail + jj
            conv = conv + cw_ref[jj:jj + 1, :] * ext_ref[gi, r0:r0 + c, :]
        ext_ref[gi, 0:SUBLANES, :] = ext_ref[gi, c:c + SUBLANES, :]
        qk_ref[gi] = conv * _sigmoid(conv)
        gb = zg_ref[gi] + bg
        gates_ref[gi] = jnp.where(lane < HEADS, gb, _log_sigmoid(gb))
    _round_robin([unit(gi, h) for gi in range(group) for h in range(HEADS)])


def _seq_group(b, c):
    rows = 512
    group = max(1, min(b, rows // c, 2 * SUBLANES))
    assert b % group == 0
    return group


def _mlstm(z_main, z_gate, conv_w, conv_b, b_gate, gn_a, conv0, c0, n0, m0, b, l):
    c = _chunk_len(l)
    nc = l // c
    grp = _seq_group(b, c)
    z3 = z_main.reshape(b, l, z_main.shape[-1])
    zspec = lambda col: pl.BlockSpec((grp, c, HW), lambda bi, j: (bi, j, col))
    st = lambda shape: pl.BlockSpec((grp,) + shape, lambda bi, j: (bi,) + (0,) * len(shape))
    m0b = jnp.broadcast_to(m0[:, :, None], (b, HEADS, DH))
    out, c_new, n_new, m_new = pl.pallas_call(
        functools.partial(_mlstm_kernel, c=c, group=grp),
        grid=(b // grp, nc),
        in_specs=[zspec(0), zspec(1), zspec(2), zspec(3),
                  pl.BlockSpec((grp, c, LANES), lambda bi, j: (bi, j, 0)),
                  _full((CONV_W, 2 * HW)), _full((1, 2 * HW)), _full((1, LANES)), _full((1, HW)),
                  st((CONV_W - 1, 2 * HW)), st((HEADS, DH, DH)), st((HEADS, DH)), st((HEADS, DH))],
        out_specs=[pl.BlockSpec((grp, c, HW), lambda bi, j: (bi, j, 0)),
                   st((HEADS, DH, DH)), st((HEADS, DH)), st((HEADS, DH))],
        out_shape=[jax.ShapeDtypeStruct((b, l, HW), F32),
                   jax.ShapeDtypeStruct((b, HEADS, DH, DH), F32),
                   jax.ShapeDtypeStruct((b, HEADS, DH), F32),
                   jax.ShapeDtypeStruct((b, HEADS, DH), F32)],
        scratch_shapes=[pltpu.VMEM((grp, c + SUBLANES, 2 * HW), F32),
                        pltpu.VMEM((grp, c, 2 * HW), F32), pltpu.VMEM((grp, c, LANES), F32)],
        compiler_params=_params(2),
        name="mlstm",
    )(z3, z3, z3, z3, z_gate.reshape(b, l, LANES), conv_w, conv_b.reshape(1, -1),
      jnp.pad(b_gate, (0, LANES - 2 * HEADS)).reshape(1, LANES), gn_a.reshape(1, -1),
      conv0, c0, n0, m0b)
    return out.reshape(b * l, HW), c_new, n_new, m_new[:, :, 0]


def _rope_table_kernel(inv_ref, sign_ref, cos_ref, sin_ref, *, pos0, rows):
    i = pl.program_id(0)
    pos = (pos0 + i * rows + lax.broadcasted_iota(jnp.int32, (rows, LANES), 0)).astype(F32)
    ang = pos * inv_ref[...]
    cos_ref[...] = jnp.cos(ang)
    sin_ref[...] = jnp.sin(ang) * sign_ref[...]


def _rope_tables(l, pos0):
    half = DH // 2
    inv = ROPE_BASE ** (-jnp.arange(half, dtype=F32) / half)
    inv2 = jnp.concatenate([inv, inv]).reshape(1, DH)
    sign = jnp.concatenate([-jnp.ones((half,), F32), jnp.ones((half,), F32)]).reshape(1, DH)
    rows = min(l, 512)
    assert l % rows == 0
    return pl.pallas_call(
        functools.partial(_rope_table_kernel, pos0=pos0, rows=rows),
        grid=(l // rows,),
        in_specs=[_full((1, DH)), _full((1, DH))],
        out_specs=[pl.BlockSpec((rows, DH), lambda i: (i, 0))] * 2,
        out_shape=[jax.ShapeDtypeStruct((l, DH), F32)] * 2,
        compiler_params=_params(1),
        name="rope_table",
    )(inv2, sign)


def _ret_kernel(zq_ref, zk_ref, zv_ref, zg_ref, cos_ref, sin_ref, s0_ref, out_ref, s_ref,
                *, c, group):
    j = pl.program_id(1)

    @pl.when(j == 0)
    def _():
        s_ref[...] = s0_ref[...]

    cosf = cos_ref[...]
    sinf = sin_ref[...]
    ti = lax.broadcasted_iota(jnp.int32, (c, c), 0)
    si = lax.broadcasted_iota(jnp.int32, (c, c), 1)
    rel = jnp.maximum(ti - si, 0).astype(F32)
    tcol = lax.broadcasted_iota(jnp.int32, (c, 1), 0).astype(F32)

    def rope(x):
        return x * cosf + pltpu.roll(x, DH // 2, axis=1) * sinf

    def unit(gi, h, decay, inter, kdecay, cdecay):
        sl = slice(h * DH, (h + 1) * DH)
        qr = rope(zq_ref[gi, :, sl])
        kr = rope(zk_ref[gi, :, sl]) * (DH ** -0.5)
        yield
        vh = zv_ref[gi, :, sl]
        s_h = s_ref[gi, h]
        qk = _bdot_nt(qr, kr)
        q_s = _bdot(qr, s_h)
        k_v = _bdot_tn(kr * kdecay, vh)
        yield
        o = q_s * inter + _bdot(qk * decay, vh)
        s_ref[gi, h] = cdecay * s_h + k_v
        yield
        gate = zg_ref[gi, :, sl]
        out_ref[gi, :, sl] = _head_rms(o) * (gate * _sigmoid(gate))

    units = []
    for h in range(HEADS):
        lg = math.log1p(-(2.0 ** (-5.0 - h)))
        decay = jnp.where(ti >= si, jnp.exp(rel * lg), 0.0)
        inter = jnp.exp((tcol + 1.0) * lg)
        kdecay = jnp.exp((c - 1.0 - tcol) * lg)
        cdecay = math.exp(c * lg)
        units += [unit(gi, h, decay, inter, kdecay, cdecay) for gi in range(group)]
    _round_robin(units)


def _retention(z_main, cos_t, sin_t, s0, b, l):
    c = _chunk_len(l)
    nc = l // c
    grp = _seq_group(b, c)
    z3 = z_main.reshape(b, l, z_main.shape[-1])
    zspec = lambda col: pl.BlockSpec((grp, c, HW), lambda bi, j: (bi, j, col))
    st = pl.BlockSpec((grp, HEADS, DH, DH), lambda bi, j: (bi, 0, 0, 0))
    tab = pl.BlockSpec((c, DH), lambda bi, j: (j, 0))
    out, s_new = pl.pallas_call(
        functools.partial(_ret_kernel, c=c, group=grp),
        grid=(b // grp, nc),
        in_specs=[zspec(4), zspec(5), zspec(6), zspec(7), tab, tab, st],
        out_specs=[pl.BlockSpec((grp, c, HW), lambda bi, j: (bi, j, 0)), st],
        out_shape=[jax.ShapeDtypeStruct((b, l, HW), F32),
                   jax.ShapeDtypeStruct((b, HEADS, DH, DH), F32)],
        compiler_params=_params(2),
        name="retention",
    )(z3, z3, z3, z3, cos_t, sin_t, s0)
    return out.reshape(b * l, HW), s_new


def _hgrn_kernel(zq_ref, zf_ref, zi_ref, zg_ref, lbl_ref, gn_ref, s0_ref, out_ref, s_ref,
                 kk_ref, bcum_ref, *, c, sc, layer, group):
    j = pl.program_id(1)

    @pl.when(j == 0)
    def _():
        s_ref[...] = s0_ref[...]

    lbl = lbl_ref[...]
    e = jnp.exp(lbl - jnp.max(lbl, axis=0, keepdims=True))
    sm = e / jnp.sum(e, axis=0, keepdims=True)
    cum = sm[0:1, :]
    for r in range(1, layer + 1):
        cum = cum + sm[r:r + 1, :]
    lb = cum - sm[0:1, :]

    oml = 1.0 - lb
    ri = lax.broadcasted_iota(jnp.int32, (c, c), 0)
    ci = lax.broadcasted_iota(jnp.int32, (c, c), 1)
    tril = jnp.where(ri >= ci, 1.0, 0.0).astype(BF16)
    rowi = lax.broadcasted_iota(jnp.int32, (sc, 1), 0)
    lane_s = lax.broadcasted_iota(jnp.int32, (SUBLANES, sc), 1)
    e_r = lax.broadcasted_iota(jnp.int32, (DH, DH), 0)
    e_c = lax.broadcasted_iota(jnp.int32, (DH, DH), 1)
    eye = e_r == e_c

    def unit(gi, h):
        sl = slice(h * DH, (h + 1) * DH)
        bh = bcum_ref[gi, :, sl]
        qh = zq_ref[gi, :, sl] * (DH ** -0.5)
        kh = kk_ref[gi, :, sl]
        vh = zi_ref[gi, :, sl]
        s_h = s_ref[gi, h]
        b_last = bh[c - 1:c, :]
        o_inter = _bdot(qh * jnp.exp(bh), s_h)
        k_v = _bdot_tn(kh * jnp.exp(b_last - bh), vh)
        dec_col = jnp.sum(jnp.where(eye, jnp.exp(b_last), 0.0), axis=1, keepdims=True)
        yield
        s_ref[gi, h] = dec_col * s_h + k_v
        blocks = []
        for blk in range(c // sc):
            r0 = blk * sc
            b_i = bh[r0:r0 + sc]
            q_i = qh[r0:r0 + sc]
            k_i = kh[r0:r0 + sc]
            v_i = vh[r0:r0 + sc]
            att_prev = None
            if blk > 0:
                ref_row = bh[r0 - 1:r0, :]
                a_i = q_i * jnp.exp(b_i - ref_row)
                k_prev = kh[0:r0] * jnp.exp(ref_row - bh[0:r0])
                att_prev = _bdot_nt(a_i, k_prev)
            b2_i = b_i * LOG2_E
            c2_i = b2_i - jnp.log2(k_i)
            cols = []
            for s in range(sc):
                lo = (s // SUBLANES) * SUBLANES
                dec = jnp.exp2(jnp.where(rowi[lo:] >= s, b2_i[lo:] - c2_i[s:s + 1, :], NEG))
                cols.append(jnp.sum(q_i[lo:] * dec, axis=1, keepdims=True))
            yield
            att = [jnp.zeros((SUBLANES, sc), F32) for _ in range(sc // SUBLANES)]
            for s in range(sc):
                lo = (s // SUBLANES) * SUBLANES
                for pi in range(lo // SUBLANES, sc // SUBLANES):
                    piece = cols[s][pi * SUBLANES - lo:(pi + 1) * SUBLANES - lo]
                    att[pi] = jnp.where(lane_s == s, piece, att[pi])
            att = jnp.concatenate(att, axis=0) if len(att) > 1 else att[0]
            o_i = _bdot(att, v_i)
            if att_prev is not None:
                o_i = o_i + _bdot(att_prev, vh[0:r0])
            blocks.append(o_i)
        yield
        o = o_inter + (jnp.concatenate(blocks, axis=0) if len(blocks) > 1 else blocks[0])
        gate = zg_ref[gi, :, sl]
        out_ref[gi, :, sl] = _head_rms(o) * gn_ref[:, sl] * (gate * _sigmoid(gate))

    for gi in range(group):
        zf = zf_ref[gi]
        ez = jnp.exp(-jnp.abs(zf))
        big = 1.0 / (1.0 + ez)
        small = ez * big
        pos = zf >= 0.0
        logf = jnp.log(lb + oml * jnp.where(pos, big, small))
        kk_ref[gi] = oml * jnp.where(pos, small, big)

        p0 = logf.astype(BF16)
        r1 = logf - p0.astype(F32)
        p1 = r1.astype(BF16)
        p2 = (r1 - p1.astype(F32)).astype(BF16)
        bcum_ref[gi] = (jnp.dot(tril, p0, preferred_element_type=F32)
                        + jnp.dot(tril, p1, preferred_element_type=F32)
                        + jnp.dot(tril, p2, preferred_element_type=F32))
    _round_robin([unit(gi, h) for gi in range(group) for h in range(HEADS)])


def _hgrn(z_cd, lb_logits, gn_c, s0, b, l, layer):
    c = _chunk_len(l)
    sc = min(c, SUBLANES)
    nc = l // c
    grp = _seq_group(b, c)
    depth = lb_logits.shape[0]
    z3 = z_cd.reshape(b, l, z_cd.shape[-1])
    zspec = lambda col: pl.BlockSpec((grp, c, HW), lambda bi, j: (bi, j, col))
    st = pl.BlockSpec((grp, HEADS, DH, DH), lambda bi, j: (bi, 0, 0, 0))
    out, s_new = pl.pallas_call(
        functools.partial(_hgrn_kernel, c=c, sc=sc, layer=layer, group=grp),
        grid=(b // grp, nc),
        in_specs=[zspec(0), zspec(1), zspec(2), zspec(3), _full((depth, HW)), _full((1, HW)), st],
        out_specs=[pl.BlockSpec((grp, c, HW), lambda bi, j: (bi, j, 0)), st],
        out_shape=[jax.ShapeDtypeStruct((b, l, HW), F32),
                   jax.ShapeDtypeStruct((b, HEADS, DH, DH), F32)],
        scratch_shapes=[pltpu.VMEM((grp, c, HW), F32), pltpu.VMEM((grp, c, HW), F32)],
        compiler_params=_params(2),
        name="hgrn2",
    )(z3, z3, z3, z3, lb_logits, gn_c.reshape(1, -1), s0)
    return out.reshape(b * l, HW), s_new


def _s5_kernel(u_ref, are_ref, aim_ref, ldt_ref, bre_ref, bim_ref, cre_ref, cim_ref, d_ref,
               wglu_ref, bglu_ref, x0r_ref, x0i_ref,
               s_ref, xr_ref, xi_ref, ar_sc, ai_sc, bbr_sc, bbi_sc, bur_sc, bui_sc, *, ct):
    j = pl.program_id(1)
    ns = are_ref.shape[-1]
    nu = u_ref.shape[-1]
    hs = ns // 2
    hu = nu // 2
    rows = ct * SUBLANES

    @pl.when(j == 0)
    def _():
        a_re = are_ref[...]
        a_im = aim_ref[...]
        dt = jnp.exp(ldt_ref[...])
        mag = jnp.exp(dt * a_re)
        ar = mag * jnp.cos(dt * a_im)
        ai = mag * jnp.sin(dt * a_im)
        ar_sc[...] = jnp.broadcast_to(ar, (SUBLANES, ns))
        ai_sc[...] = jnp.broadcast_to(ai, (SUBLANES, ns))
        den = a_re * a_re + a_im * a_im
        nr = ar - 1.0
        zr = (nr * a_re + ai * a_im) / den
        zi = (ai * a_re - nr * a_im) / den
        for hg in range(2):
            us = slice(hg * hu, (hg + 1) * hu)
            ss = slice(hg * hs, (hg + 1) * hs)
            bbr_sc[us, :] = (zr[:, ss] * bre_ref[us, :] - zi[:, ss] * bim_ref[us, :]).astype(BF16)
            bbi_sc[us, :] = (zr[:, ss] * bim_ref[us, :] + zi[:, ss] * bre_ref[us, :]).astype(BF16)
        xr_ref[...] = x0r_ref[...]
        xi_ref[...] = x0i_ref[...]

    u = u_ref[...].reshape(rows, nu)
    ub = u.astype(BF16)
    for hg in range(2):
        us = slice(hg * hu, (hg + 1) * hu)
        ss = slice(hg * hs, (hg + 1) * hs)
        bur_sc[:, ss] = jnp.dot(ub[:, us], bbr_sc[us, :], preferred_element_type=F32)
        bui_sc[:, ss] = jnp.dot(ub[:, us], bbi_sc[us, :], preferred_element_type=F32)

    lane_chunk = 4 * LANES
    for lc in range(ns // lane_chunk):
        ls = slice(lc * lane_chunk, (lc + 1) * lane_chunk)
        ar = ar_sc[:, ls]
        ai = ai_sc[:, ls]

        def step(t, carry):
            xr, xi = carry
            r0 = pl.multiple_of(t * SUBLANES, SUBLANES)
            nxr = ar * xr - ai * xi + bur_sc[pl.ds(r0, SUBLANES), ls]
            nxi = ar * xi + ai * xr + bui_sc[pl.ds(r0, SUBLANES), ls]
            bur_sc[pl.ds(r0, SUBLANES), ls] = nxr
            bui_sc[pl.ds(r0, SUBLANES), ls] = nxi
            return nxr, nxi

        xr, xi = lax.fori_loop(0, ct, step, (xr_ref[:, ls], xi_ref[:, ls]), unroll=SUBLANES)
        xr_ref[:, ls] = xr
        xi_ref[:, ls] = xi

    ys = []
    for hg in range(2):
        ss = slice(hg * hs, (hg + 1) * hs)
        ys.append(jnp.dot(bur_sc[:, ss].astype(BF16), cre_ref[ss, :], preferred_element_type=F32)
                  - jnp.dot(bui_sc[:, ss].astype(BF16), cim_ref[ss, :], preferred_element_type=F32))
    y = jnp.concatenate(ys, axis=1) + d_ref[...] * u
    a = 0.5 * y * (1.0 + jnp.tanh(math.sqrt(2.0 / math.pi) * (y + 0.044715 * (y * y * y))))
    s = a * _sigmoid(jnp.dot(a.astype(BF16), wglu_ref[...], preferred_element_type=F32) + bglu_ref[...])
    s_ref[...] = s.reshape(ct, SUBLANES, nu)


def _s5_block_diag(bmat, cmat):
    g, p, hgrp = bmat.shape
    gh = g // 2
    eye = jnp.eye(gh, dtype=F32)
    b4 = bmat.reshape(2, gh, p, hgrp)
    bc = jnp.einsum('agph,gk->aghkp', b4, eye).reshape(2 * gh * hgrp, gh * p)
    c4 = cmat.reshape(2, gh, hgrp, p)
    cc = jnp.einsum('aghp,gk->agpkh', c4, eye).reshape(2 * gh * p, gh * hgrp)
    return bc, cc


def _s5_operands(a_re, a_im, log_dt, b_re, b_im, c_re, c_im, d_skip, w_glu, b_glu):
    g, p = a_re.shape
    ns = g * p
    nu = d_skip.shape[-1]
    assert nu == g * S5_GROUP and (g // 2) * S5_GROUP == MXU_DIM
    bre_c, cre_c = _s5_block_diag(b_re, c_re)
    bim_c, cim_c = _s5_block_diag(b_im, c_im)
    ldt = jnp.broadcast_to(log_dt[:, None], (g, p)).reshape(1, ns)
    return (a_re.reshape(1, ns), a_im.reshape(1, ns), ldt, bre_c, bim_c, cre_c.astype(BF16),
            cim_c.astype(BF16), d_skip.reshape(1, nu), w_glu.astype(BF16), b_glu.reshape(1, nu))


def _s5(u_tm, operands, x0r, x0i):
    l, b, nu = u_tm.shape
    g, p = x0r.shape[1:]
    ns = g * p
    assert b % SUBLANES == 0
    ct = _chunk_len(l)
    nct = l // ct
    rows = ct * SUBLANES
    xspec = pl.BlockSpec((SUBLANES, ns), lambda bb, j: (bb, 0))
    s, xr, xi = pl.pallas_call(
        functools.partial(_s5_kernel, ct=ct),
        grid=(b // SUBLANES, nct),
        in_specs=[pl.BlockSpec((ct, SUBLANES, nu), lambda bb, j: (j, bb, 0)),
                  _full((1, ns)), _full((1, ns)), _full((1, ns)),
                  _full((nu, ns // 2)), _full((nu, ns // 2)),
                  _full((ns, nu // 2)), _full((ns, nu // 2)),
                  _full((1, nu)), _full((nu, nu)), _full((1, nu)), xspec, xspec],
        out_specs=[pl.BlockSpec((ct, SUBLANES, nu), lambda bb, j: (j, bb, 0)), xspec, xspec],
        out_shape=[jax.ShapeDtypeStruct((l, b, nu), F32),
                   jax.ShapeDtypeStruct((b, ns), F32), jax.ShapeDtypeStruct((b, ns), F32)],
        scratch_shapes=[pltpu.VMEM((SUBLANES, ns), F32), pltpu.VMEM((SUBLANES, ns), F32),
                        pltpu.VMEM((nu, ns // 2), BF16), pltpu.VMEM((nu, ns // 2), BF16),
                        pltpu.VMEM((rows, ns), F32), pltpu.VMEM((rows, ns), F32)],
        compiler_params=_params(2, VMEM_LIMIT),
        name="s5",
    )(u_tm, *operands, x0r.reshape(b, ns), x0i.reshape(b, ns))
    return s, xr.reshape(b, g, p), xi.reshape(b, g, p)


def _trunk(groups, prm):
    d = groups[0]['x'].shape[-1]
    tm = 512
    info = []
    for grp in groups:
        b, l, _ = grp['x'].shape
        assert l >= CONV_W - 1 and (b * l) % tm == 0
        info.append(dict(b=b, l=l, t=b * l, direct_tm=b == SUBLANES and l % tm == 0))
    hs = [grp['x'].reshape(-1, d) for grp in groups]
    pps = [grp['p'].reshape(grp['p'].shape[0], -1, grp['p'].shape[-1]) for grp in groups]

    zs = _inproj(hs, prm['norm_mix'][0], prm['w_ab'], (8 * HW, LANES), tm, [None] * len(groups))
    ab_new, post_in = [], []
    for grp, inf, h, pp, (z_main, z_gate) in zip(groups, info, hs, pps, zs):
        b, l = inf['b'], inf['l']
        conv0, c0, n0, m0, ret0 = grp['ab_state']
        conv_new = z_main.reshape(b, l, 8 * HW)[:, l - (CONV_W - 1):, :2 * HW]
        hm, c_new, n_new, m_new = _mlstm(z_main, z_gate, prm['conv_w_ab'][0], prm['conv_b_ab'][0],
                                         prm['b_gate_ab'][0], prm['gn_a'][0], conv0[0], c0[0], n0[0],
                                         m0[0], b, l)
        cos_t, sin_t = _rope_tables(l, grp['pos0'])
        hr, ret_new = _retention(z_main, cos_t, sin_t, ret0[0], b, l)
        ab_new.append((conv_new[None], c_new[None], n_new[None], m_new[None], ret_new[None]))
        post_in.append((h, hm, hr, pp, None))
    hs = _post(post_in, prm['w_out_ab'], prm['norm_ff'][0], prm['w_ff1'], prm['w_ff2'],
               prm['norm_ple'][0], prm['w_ple_gate'], prm['w_ple_proj'], prm['norm_final'],
               layer=0, final=False, tm=tm)

    tml = [(inf['b'], inf['l']) if inf['direct_tm'] else None for inf in info]
    zs = _inproj(hs, prm['norm_mix'][1], prm['w_cd'], (4 * HW, HW), tm, tml)
    cd_new, post_in = [], []
    for grp, inf, h, pp, (z_cd, su) in zip(groups, info, hs, pps, zs):
        b, l, t = inf['b'], inf['l'], inf['t']
        hg0, x0r, x0i = grp['cd_state']
        o, hg_new = _hgrn(z_cd, prm['lb_logits'], prm['gn_c'][0], hg0[0], b, l, layer=1)
        u_tm = (su.reshape(l, b, HW) if inf['direct_tm']
                else jnp.transpose(su.reshape(b, l, HW), (1, 0, 2)))
        s_tm, xr, xi = _s5(u_tm, prm['s5_operands'], x0r[0], x0i[0])
        s_in = (s_tm.reshape(l, b * HW) if inf['direct_tm']
                else jnp.transpose(s_tm, (1, 0, 2)).reshape(t, HW))
        cd_new.append((hg_new[None], xr[None], xi[None]))
        post_in.append((h, o, s_in, pp, (b, l) if inf['direct_tm'] else None))
    ys = _post(post_in, prm['w_out_cd'], prm['norm_ff'][1], prm['w_ff1'], prm['w_ff2'],
               prm['norm_ple'][1], prm['w_ple_gate'], prm['w_ple_proj'], prm['norm_final'],
               layer=1, final=True, tm=tm)
    ys = [y.reshape(grp['x'].shape) for y, grp in zip(ys, groups)]
    return ys, ab_new, cd_new


def kernel(x_prompt, x_sample, state_mlstm_conv, state_mlstm_C, state_mlstm_n, state_mlstm_m, state_ret, state_hgrn, state_s5_re, state_s5_im, p_prompt, p_sample, norm_mix, norm_ff, norm_ple, norm_final, w_in_ab, b_gate_ab, conv_w_ab, conv_b_ab, gn_a, w_out_ab, w_in_cd, lb_logits, gn_c, s5_A_re, s5_A_im, s5_log_dt, s5_B_re, s5_B_im, s5_C_re, s5_C_im, s5_D, w_glu, b_glu, w_out_cd, w_ff1, w_ff2, w_ple_proj, w_ple_gate):
    assert norm_mix.shape[0] == 2, "two layers: (mLSTM || retention), (HGRN2 || S5)"
    w_ab = w_in_ab[0]
    gate0 = 4 * HW
    w_ab = jnp.concatenate([w_ab[:, :gate0], w_ab[:, gate0 + 2 * HEADS:],
                            w_ab[:, gate0:gate0 + 2 * HEADS],
                            jnp.zeros((w_ab.shape[0], LANES - 2 * HEADS), w_ab.dtype)], axis=1)
    prm = dict(norm_mix=norm_mix, norm_ff=norm_ff, norm_ple=norm_ple, norm_final=norm_final,
               w_ab=w_ab.astype(BF16), b_gate_ab=b_gate_ab, conv_w_ab=conv_w_ab, conv_b_ab=conv_b_ab,
               gn_a=gn_a, w_out_ab=w_out_ab[0].astype(BF16), w_cd=w_in_cd[0].astype(BF16),
               lb_logits=lb_logits, gn_c=gn_c,
               s5_operands=_s5_operands(s5_A_re[0], s5_A_im[0], s5_log_dt[0], s5_B_re[0], s5_B_im[0],
                                        s5_C_re[0], s5_C_im[0], s5_D[0], w_glu[0], b_glu[0]),
               w_out_cd=w_out_cd[0].astype(BF16), w_ff1=w_ff1.astype(BF16), w_ff2=w_ff2.astype(BF16),
               w_ple_proj=w_ple_proj.astype(BF16), w_ple_gate=w_ple_gate.astype(BF16))

    bp, lp, _ = x_prompt.shape
    z = lambda *s: jnp.zeros(s, F32)
    zero_ab = (z(1, bp, CONV_W - 1, 2 * HW), z(1, bp, HEADS, DH, DH), z(1, bp, HEADS, DH),
               z(1, bp, HEADS), z(1, bp, HEADS, DH, DH))
    zero_cd = (z(1, bp, HEADS, DH, DH),) + (z(*((1, bp) + s5_A_re.shape[1:])),) * 2
    groups = [dict(x=x_prompt, p=p_prompt, pos0=0, ab_state=zero_ab, cd_state=zero_cd),
              dict(x=x_sample, p=p_sample, pos0=PAST_LEN,
                   ab_state=(state_mlstm_conv, state_mlstm_C, state_mlstm_n, state_mlstm_m, state_ret),
                   cd_state=(state_hgrn, state_s5_re, state_s5_im))]
    (y_p, y_s), (ab_p, ab_s), (cd_p, cd_s) = _trunk(groups, prm)
    return (y_p, y_s,
            ab_p[0], ab_s[0], ab_p[1], ab_s[1], ab_p[2], ab_s[2], ab_p[3], ab_s[3], ab_p[4], ab_s[4],
            cd_p[0], cd_s[0], cd_p[1], cd_s[1], cd_p[2], cd_s[2])
```

```python
import functools
import math

import jax
import jax.numpy as jnp
from jax import lax
from jax.experimental import pallas as pl
from jax.experimental.pallas import tpu as pltpu

F32 = jnp.float32
BF16 = jnp.bfloat16

EPS = 1e-6
NEG = -1e30
LOG2_E = math.log2(math.e)
ROPE_BASE = 10000.0
PAST_LEN = 16384
CHUNK = 64
HEADS = 4
DH = 128
HW = HEADS * DH
CONV_W = 4
S5_GROUP = 16
S5_STATE = 64
SUBLANES = 8
LANES = 128
MXU_DIM = 256
VMEM_LIMIT = 56 * 1024 * 1024


def _params(n_axes, vmem=None):
    return pltpu.CompilerParams(dimension_semantics=("arbitrary",) * n_axes, vmem_limit_bytes=vmem)


def _full(shape):
    return pl.BlockSpec(shape, lambda *_: (0,) * len(shape))


def _bdot(a, b):
    return jnp.dot(a.astype(BF16), b.astype(BF16), preferred_element_type=F32)


def _bdot_nt(a, b):
    return lax.dot_general(a.astype(BF16), b.astype(BF16), (((1,), (1,)), ((), ())),
                           preferred_element_type=F32)


def _bdot_tn(a, b):
    return lax.dot_general(a.astype(BF16), b.astype(BF16), (((0,), (0,)), ((), ())),
                           preferred_element_type=F32)


def _sigmoid(x):
    return 1.0 / (1.0 + jnp.exp(-x))


def _log_sigmoid(x):
    return jnp.minimum(x, 0.0) - jnp.log(1.0 + jnp.exp(-jnp.abs(x)))


def _rms(x, g):
    return x * lax.rsqrt(jnp.mean(x * x, axis=-1, keepdims=True) + EPS) * g


def _head_rms(x):
    return x * lax.rsqrt(jnp.mean(x * x, axis=-1, keepdims=True) + EPS)


def _chunk_len(length):
    return CHUNK if length % CHUNK == 0 else length


def _round_robin(gens):
    gens = list(gens)
    while gens:
        alive = []
        for g in gens:
            try:
                next(g)
                alive.append(g)
            except StopIteration:
                pass
        gens = alive


def _step_ranges(counts):
    starts = [0]
    for n in counts:
        starts.append(starts[-1] + n)
    return starts


def _local(i, start, count):
    return jnp.clip(i - start, 0, count - 1)


def _inproj_kernel(*refs, n_groups, n_out, starts):
    x_refs = refs[:n_groups]
    g_ref, w_ref = refs[n_groups:n_groups + 2]
    out_refs = refs[n_groups + 2:]
    i = pl.program_id(0)

    def run(x_ref, outs):
        hn = _rms(x_ref[...], g_ref[...]).astype(BF16)
        off = 0
        for o_ref in outs:
            n = o_ref.shape[-1]
            for n0 in range(0, n, HW):
                nn = min(HW, n - n0)
                o_ref[:, n0:n0 + nn] = jnp.dot(hn, w_ref[:, off + n0:off + n0 + nn],
                                               preferred_element_type=F32)
            off += n

    for gi in range(n_groups):
        @pl.when((i >= starts[gi]) & (i < starts[gi + 1]))
        def _(gi=gi):
            run(x_refs[gi], out_refs[gi * n_out:(gi + 1) * n_out])


def _inproj(hs, g, w, widths, tm, time_major_last):
    d = hs[0].shape[1]
    n = w.shape[1]
    assert sum(widths) == n
    counts = [h.shape[0] // tm for h in hs]
    starts = _step_ranges(counts)
    in_specs, out_specs, out_shape = [], [], []
    for k, h in enumerate(hs):
        t = h.shape[0]
        assert t % tm == 0
        loc = functools.partial(_local, start=starts[k], count=counts[k])
        in_specs.append(pl.BlockSpec((tm, d), lambda i, loc=loc: (loc(i), 0)))
        for wi, wd in enumerate(widths):
            if wi == len(widths) - 1 and time_major_last[k] is not None:
                b, l = time_major_last[k]
                assert l % tm == 0
                nl = l // tm
                out_shape.append(jax.ShapeDtypeStruct((l, b * wd), F32))
                out_specs.append(pl.BlockSpec((tm, wd), lambda i, loc=loc, nl=nl: (loc(i) % nl, loc(i) // nl)))
            else:
                out_shape.append(jax.ShapeDtypeStruct((t, wd), F32))
                out_specs.append(pl.BlockSpec((tm, wd), lambda i, loc=loc: (loc(i), 0)))
    outs = pl.pallas_call(
        functools.partial(_inproj_kernel, n_groups=len(hs), n_out=len(widths), starts=tuple(starts)),
        grid=(starts[-1],),
        in_specs=in_specs + [_full((1, d)), _full((d, n))],
        out_specs=out_specs,
        out_shape=out_shape,
        compiler_params=_params(1, VMEM_LIMIT),
        name="inproj",
    )(*hs, g.reshape(1, d), w)
    nw = len(widths)
    return [outs[k * nw:(k + 1) * nw] for k in range(len(hs))]


def _post_kernel(*refs, n_groups, starts, final, ff_chunk):
    grp_in = [refs[4 * k:4 * k + 4] for k in range(n_groups)]
    (wo_ref, gff_ref, w1_ref, w2_ref, gple_ref, wg_ref, wp_ref,
     gfin_ref) = refs[4 * n_groups:4 * n_groups + 8]
    out_refs = refs[4 * n_groups + 8:]
    i = pl.program_id(0)

    def run(h_ref, ma_ref, mb_ref, p_ref, o_ref):
        half = ma_ref.shape[-1]
        h = h_ref[...]
        h = h + (jnp.dot(ma_ref[...].astype(BF16), wo_ref[0:half, :], preferred_element_type=F32)
                 + jnp.dot(mb_ref[...].astype(BF16), wo_ref[half:2 * half, :],
                           preferred_element_type=F32))
        hn = _rms(h, gff_ref[...]).astype(BF16)
        d_ff = w1_ref.shape[1]
        acc = jnp.zeros_like(h)
        for f0 in range(0, d_ff, ff_chunk):
            a = jnp.dot(hn, w1_ref[:, f0:f0 + ff_chunk], preferred_element_type=F32)
            a = jnp.square(jnp.maximum(a, 0.0))
            acc = acc + jnp.dot(a.astype(BF16), w2_ref[f0:f0 + ff_chunk, :], preferred_element_type=F32)
        h = h + acc
        gate = _sigmoid(jnp.dot(_rms(h, gple_ref[...]).astype(BF16), wg_ref[...],
                                preferred_element_type=F32))
        h = h + gate * jnp.dot(p_ref[...].astype(BF16), wp_ref[...], preferred_element_type=F32)
        o_ref[...] = _rms(h, gfin_ref[...]) if final else h

    for k in range(n_groups):
        @pl.when((i >= starts[k]) & (i < starts[k + 1]))
        def _(k=k):
            run(*grp_in[k], out_refs[k])


def _post(groups, wo, gff, w1, w2, gple, wg, wp, gfin, *, layer, final, tm):
    d = groups[0][0].shape[1]
    d_ff = w1.shape[-1]
    counts = [grp[0].shape[0] // tm for grp in groups]
    starts = _step_ranges(counts)
    in_specs, out_specs, out_shape, operands = [], [], [], []
    for k, (h, mix_a, mix_b, p, b_time_major) in enumerate(groups):
        t = h.shape[0]
        half = mix_a.shape[-1]
        pd = p.shape[-1]
        assert t % tm == 0
        loc = functools.partial(_local, start=starts[k], count=counts[k])
        row = lambda i, loc=loc: (loc(i), 0)
        mb_spec = pl.BlockSpec((tm, half), row)
        if b_time_major is not None:
            _, l = b_time_major
            assert l % tm == 0
            nl = l // tm
            mb_spec = pl.BlockSpec((tm, half), lambda i, loc=loc, nl=nl: (loc(i) % nl, loc(i) // nl))
        in_specs += [pl.BlockSpec((tm, d), row), pl.BlockSpec((tm, half), row), mb_spec,
                     pl.BlockSpec((None, tm, pd), lambda i, loc=loc: (layer, loc(i), 0))]
        operands += [h, mix_a, mix_b, p]
        out_specs.append(pl.BlockSpec((tm, d), row))
        out_shape.append(jax.ShapeDtypeStruct((t, d), F32))
    pd = groups[0][3].shape[-1]
    lw = lambda r, cdim: pl.BlockSpec((None, r, cdim), lambda i: (layer, 0, 0))
    return pl.pallas_call(
        functools.partial(_post_kernel, n_groups=len(groups), starts=tuple(starts), final=final,
                          ff_chunk=1024),
        grid=(starts[-1],),
        in_specs=in_specs + [_full((d, d)), _full((1, d)), lw(d, d_ff), lw(d_ff, d), _full((1, d)),
                             lw(d, d), lw(pd, d), _full((1, d))],
        out_specs=out_specs,
        out_shape=out_shape,
        compiler_params=_params(1, VMEM_LIMIT),
        name="post",
    )(*operands, wo, gff.reshape(1, d), w1, w2, gple.reshape(1, d), wg, wp, gfin.reshape(1, d))


def _mlstm_kernel(zq_ref, zk_ref, zv_ref, zo_ref, zg_ref, cw_ref, cb_ref, bg_ref, gn_ref,
                  conv0_ref, c0_ref, n0_ref, m0_ref,
                  out_ref, conv_ref, c_ref, n_ref, m_ref, ext_ref, qk_ref, gates_ref, *, c, group):
    j = pl.program_id(1)
    tail = CONV_W - 1

    @pl.when(j == 0)
    def _():
        c_ref[...] = c0_ref[...]
        n_ref[...] = n0_ref[...]
        m_ref[...] = m0_ref[...]
        for gi in range(group):
            ext_ref[gi, 0:SUBLANES, :] = jnp.zeros((SUBLANES, 2 * HW), F32)
            ext_ref[gi, SUBLANES - tail:SUBLANES, :] = conv0_ref[gi]

    ri = lax.broadcasted_iota(jnp.int32, (c, c), 0)
    ci = lax.broadcasted_iota(jnp.int32, (c, c), 1)
    eye = ri == ci
    tril = ri >= ci
    lane = lax.broadcasted_iota(jnp.int32, (c, LANES), 1)
    bg = bg_ref[...]

    def lsum(x):
        return jnp.broadcast_to(jnp.sum(x, axis=1, keepdims=True), (c, DH))

    def unit(gi, h):
        sl = slice(h * DH, (h + 1) * DH)
        gates = gates_ref[gi]
        i_col = lsum(jnp.where(lane == h, gates, 0.0))
        f_col = lsum(jnp.where(lane == HEADS + h, gates, 0.0))
        yield
        b_row = jnp.sum(jnp.where(ri <= ci, f_col[:, :c], 0.0), axis=0, keepdims=True)
        b_col = lsum(jnp.where(eye, b_row, 0.0))
        i_row = jnp.sum(jnp.where(eye, i_col[:, :c], 0.0), axis=0, keepdims=True)
        yield
        m_prev = m_ref[gi, h:h + 1, :]
        dmat = jnp.where(tril, b_col[:, :c] - b_row + i_row, NEG)
        inter = b_col + m_prev
        row_max = jnp.broadcast_to(jnp.max(dmat, axis=1, keepdims=True), (c, DH))
        qh = qk_ref[gi, :, sl]
        kh = qk_ref[gi, :, HW + h * DH:HW + (h + 1) * DH] * (DH ** -0.5)
        vh = zv_ref[gi, :, sl]
        c_h = c_ref[gi, h]
        n_h = n_ref[gi, h:h + 1, :]
        s_raw = _bdot_nt(qh, kh)
        q_c = _bdot(qh, c_h)
        q_n = lsum(qh * n_h)
        yield
        m_t = jnp.maximum(inter, row_max)
        w_intra = jnp.exp(dmat - m_t[:, :c])
        w_inter = jnp.exp(inter - m_t)
        s = s_raw * w_intra
        s_v = _bdot(s, vh)
        s_sum = lsum(s)
        m_new = m_t[c - 1:c, :]
        b_last = b_col[c - 1:c, :]
        w_last = jnp.exp(b_last - b_col + i_col - m_new)
        decay = jnp.exp(b_last + m_prev - m_new)
        kw = w_last * kh
        kw_v = _bdot_tn(kw, vh)
        yield
        num = w_inter * q_c + s_v
        den = w_inter * q_n + s_sum
        hh = num / jnp.maximum(jnp.abs(den), jnp.exp(-m_t))
        c_ref[gi, h] = decay * c_h + kw_v
        n_ref[gi, h:h + 1, :] = decay * n_h + jnp.sum(kw, axis=0, keepdims=True)
        m_ref[gi, h:h + 1, :] = m_new
        hh = _sigmoid(zo_ref[gi, :, sl]) * hh
        out_ref[gi, :, sl] = _head_rms(hh) * gn_ref[:, sl]

    for gi in range(group):
        ext_ref[gi, SUBLANES:SUBLANES + c, 0:HW] = zq_ref[gi]
        ext_ref[gi, SUBLANES:SUBLANES + c, HW:2 * HW] = zk_ref[gi]
        conv = cb_ref[...]
        for jj in range(CONV_W):
            r0 = SUBLANES - tail + jj
            conv = conv + cw_ref[jj:jj + 1, :] * ext_ref[gi, r0:r0 + c, :]
        ext_ref[gi, 0:SUBLANES, :] = ext_ref[gi, c:c + SUBLANES, :]
        conv_ref[gi] = ext_ref[gi, SUBLANES - tail:SUBLANES, :]
        qk_ref[gi] = conv * _sigmoid(conv)
        gb = zg_ref[gi] + bg
        gates_ref[gi] = jnp.where(lane < HEADS, gb, _log_sigmoid(gb))
    _round_robin([unit(gi, h) for gi in range(group) for h in range(HEADS)])


def _seq_group(b, c):
    rows = 512
    group = max(1, min(b, rows // c, 2 * SUBLANES))
    assert b % group == 0
    return group


def _mlstm(z_main, z_gate, conv_w, conv_b, b_gate, gn_a, conv0, c0, n0, m0, b, l):
    c = _chunk_len(l)
    nc = l // c
    grp = _seq_group(b, c)
    z3 = z_main.reshape(b, l, z_main.shape[-1])
    zspec = lambda col: pl.BlockSpec((grp, c, HW), lambda bi, j: (bi, j, col))
    st = lambda shape: pl.BlockSpec((grp,) + shape, lambda bi, j: (bi,) + (0,) * len(shape))
    m0b = jnp.broadcast_to(m0[:, :, None], (b, HEADS, DH))
    out, conv_new, c_new, n_new, m_new = pl.pallas_call(
        functools.partial(_mlstm_kernel, c=c, group=grp),
        grid=(b // grp, nc),
        in_specs=[zspec(0), zspec(1), zspec(2), zspec(3),
                  pl.BlockSpec((grp, c, LANES), lambda bi, j: (bi, j, 0)),
                  _full((CONV_W, 2 * HW)), _full((1, 2 * HW)), _full((1, LANES)), _full((1, HW)),
                  st((CONV_W - 1, 2 * HW)), st((HEADS, DH, DH)), st((HEADS, DH)), st((HEADS, DH))],
        out_specs=[pl.BlockSpec((grp, c, HW), lambda bi, j: (bi, j, 0)),
                   st((CONV_W - 1, 2 * HW)), st((HEADS, DH, DH)), st((HEADS, DH)), st((HEADS, DH))],
        out_shape=[jax.ShapeDtypeStruct((b, l, HW), F32),
                   jax.ShapeDtypeStruct((b, CONV_W - 1, 2 * HW), F32),
                   jax.ShapeDtypeStruct((b, HEADS, DH, DH), F32),
                   jax.ShapeDtypeStruct((b, HEADS, DH), F32),
                   jax.ShapeDtypeStruct((b, HEADS, DH), F32)],
        scratch_shapes=[pltpu.VMEM((grp, c + SUBLANES, 2 * HW), F32),
                        pltpu.VMEM((grp, c, 2 * HW), F32), pltpu.VMEM((grp, c, LANES), F32)],
        compiler_params=_params(2),
        name="mlstm",
    )(z3, z3, z3, z3, z_gate.reshape(b, l, LANES), conv_w, conv_b.reshape(1, -1),
      jnp.pad(b_gate, (0, LANES - 2 * HEADS)).reshape(1, LANES), gn_a.reshape(1, -1),
      conv0, c0, n0, m0b)
    return out.reshape(b * l, HW), conv_new, c_new, n_new, m_new[:, :, 0]


def _rope_table_kernel(inv_ref, sign_ref, cos_ref, sin_ref, *, pos0, rows):
    i = pl.program_id(0)
    pos = (pos0 + i * rows + lax.broadcasted_iota(jnp.int32, (rows, LANES), 0)).astype(F32)
    ang = pos * inv_ref[...]
    cos_ref[...] = jnp.cos(ang)
    sin_ref[...] = jnp.sin(ang) * sign_ref[...]


def _rope_tables(l, pos0):
    half = DH // 2
    inv = ROPE_BASE ** (-jnp.arange(half, dtype=F32) / half)
    inv2 = jnp.concatenate([inv, inv]).reshape(1, DH)
    sign = jnp.concatenate([-jnp.ones((half,), F32), jnp.ones((half,), F32)]).reshape(1, DH)
    rows = min(l, 512)
    assert l % rows == 0
    return pl.pallas_call(
        functools.partial(_rope_table_kernel, pos0=pos0, rows=rows),
        grid=(l // rows,),
        in_specs=[_full((1, DH)), _full((1, DH))],
        out_specs=[pl.BlockSpec((rows, DH), lambda i: (i, 0))] * 2,
        out_shape=[jax.ShapeDtypeStruct((l, DH), F32)] * 2,
        compiler_params=_params(1),
        name="rope_table",
    )(inv2, sign)


def _ret_kernel(zq_ref, zk_ref, zv_ref, zg_ref, cos_ref, sin_ref, s0_ref, out_ref, s_ref,
                *, c, group):
    j = pl.program_id(1)

    @pl.when(j == 0)
    def _():
        s_ref[...] = s0_ref[...]

    cosf = cos_ref[...]
    sinf = sin_ref[...]
    ti = lax.broadcasted_iota(jnp.int32, (c, c), 0)
    si = lax.broadcasted_iota(jnp.int32, (c, c), 1)
    rel = jnp.maximum(ti - si, 0).astype(F32)
    tcol = lax.broadcasted_iota(jnp.int32, (c, 1), 0).astype(F32)

    def rope(x):
        return x * cosf + pltpu.roll(x, DH // 2, axis=1) * sinf

    def unit(gi, h, decay, inter, kdecay, cdecay):
        sl = slice(h * DH, (h + 1) * DH)
        qr = rope(zq_ref[gi, :, sl])
        kr = rope(zk_ref[gi, :, sl]) * (DH ** -0.5)
        yield
        vh = zv_ref[gi, :, sl]
        s_h = s_ref[gi, h]
        qk = _bdot_nt(qr, kr)
        q_s = _bdot(qr, s_h)
        k_v = _bdot_tn(kr * kdecay, vh)
        yield
        o = q_s * inter + _bdot(qk * decay, vh)
        s_ref[gi, h] = cdecay * s_h + k_v
        yield
        gate = zg_ref[gi, :, sl]
        out_ref[gi, :, sl] = _head_rms(o) * (gate * _sigmoid(gate))

    units = []
    for h in range(HEADS):
        lg = math.log1p(-(2.0 ** (-5.0 - h)))
        decay = jnp.where(ti >= si, jnp.exp(rel * lg), 0.0)
        inter = jnp.exp((tcol + 1.0) * lg)
        kdecay = jnp.exp((c - 1.0 - tcol) * lg)
        cdecay = math.exp(c * lg)
        units += [unit(gi, h, decay, inter, kdecay, cdecay) for gi in range(group)]
    _round_robin(units)


def _retention(z_main, cos_t, sin_t, s0, b, l):
    c = _chunk_len(l)
    nc = l // c
    grp = _seq_group(b, c)
    z3 = z_main.reshape(b, l, z_main.shape[-1])
    zspec = lambda col: pl.BlockSpec((grp, c, HW), lambda bi, j: (bi, j, col))
    st = pl.BlockSpec((grp, HEADS, DH, DH), lambda bi, j: (bi, 0, 0, 0))
    tab = pl.BlockSpec((c, DH), lambda bi, j: (j, 0))
    out, s_new = pl.pallas_call(
        functools.partial(_ret_kernel, c=c, group=grp),
        grid=(b // grp, nc),
        in_specs=[zspec(4), zspec(5), zspec(6), zspec(7), tab, tab, st],
        out_specs=[pl.BlockSpec((grp, c, HW), lambda bi, j: (bi, j, 0)), st],
        out_shape=[jax.ShapeDtypeStruct((b, l, HW), F32),
                   jax.ShapeDtypeStruct((b, HEADS, DH, DH), F32)],
        compiler_params=_params(2),
        name="retention",
    )(z3, z3, z3, z3, cos_t, sin_t, s0)
    return out.reshape(b * l, HW), s_new


def _hgrn_kernel(zq_ref, zf_ref, zi_ref, zg_ref, lbl_ref, gn_ref, s0_ref, out_ref, s_ref,
                 kk_ref, bcum_ref, *, c, sc, layer, group):
    j = pl.program_id(1)

    @pl.when(j == 0)
    def _():
        s_ref[...] = s0_ref[...]

    lbl = lbl_ref[...]
    e = jnp.exp(lbl - jnp.max(lbl, axis=0, keepdims=True))
    sm = e / jnp.sum(e, axis=0, keepdims=True)
    cum = sm[0:1, :]
    for r in range(1, layer + 1):
        cum = cum + sm[r:r + 1, :]
    lb = cum - sm[0:1, :]

    oml = 1.0 - lb
    ri = lax.broadcasted_iota(jnp.int32, (c, c), 0)
    ci = lax.broadcasted_iota(jnp.int32, (c, c), 1)
    tril = jnp.where(ri >= ci, 1.0, 0.0).astype(BF16)
    rowi = lax.broadcasted_iota(jnp.int32, (sc, 1), 0)
    lane_s = lax.broadcasted_iota(jnp.int32, (SUBLANES, sc), 1)
    e_r = lax.broadcasted_iota(jnp.int32, (DH, DH), 0)
    e_c = lax.broadcasted_iota(jnp.int32, (DH, DH), 1)
    eye = e_r == e_c

    def unit(gi, h):
        sl = slice(h * DH, (h + 1) * DH)
        bh = bcum_ref[gi, :, sl]
        qh = zq_ref[gi, :, sl] * (DH ** -0.5)
        kh = kk_ref[gi, :, sl]
        vh = zi_ref[gi, :, sl]
        s_h = s_ref[gi, h]
        b_last = bh[c - 1:c, :]
        o_inter = _bdot(qh * jnp.exp(bh), s_h)
        k_v = _bdot_tn(kh * jnp.exp(b_last - bh), vh)
        dec_col = jnp.sum(jnp.where(eye, jnp.exp(b_last), 0.0), axis=1, keepdims=True)
        yield
        s_ref[gi, h] = dec_col * s_h + k_v
        blocks = []
        for blk in range(c // sc):
            r0 = blk * sc
            b_i = bh[r0:r0 + sc]
            q_i = qh[r0:r0 + sc]
            k_i = kh[r0:r0 + sc]
            v_i = vh[r0:r0 + sc]
            att_prev = None
            if blk > 0:
                ref_row = bh[r0 - 1:r0, :]
                a_i = q_i * jnp.exp(b_i - ref_row)
                k_prev = kh[0:r0] * jnp.exp(ref_row - bh[0:r0])
                att_prev = _bdot_nt(a_i, k_prev)
            b2_i = b_i * LOG2_E
            c2_i = b2_i - jnp.log2(k_i)
            cols = []
            for s in range(sc):
                lo = (s // SUBLANES) * SUBLANES
                dec = jnp.exp2(jnp.where(rowi[lo:] >= s, b2_i[lo:] - c2_i[s:s + 1, :], NEG))
                cols.append(jnp.sum(q_i[lo:] * dec, axis=1, keepdims=True))
            yield
            att = [jnp.zeros((SUBLANES, sc), F32) for _ in range(sc // SUBLANES)]
            for s in range(sc):
                lo = (s // SUBLANES) * SUBLANES
                for pi in range(lo // SUBLANES, sc // SUBLANES):
                    piece = cols[s][pi * SUBLANES - lo:(pi + 1) * SUBLANES - lo]
                    att[pi] = jnp.where(lane_s == s, piece, att[pi])
            att = jnp.concatenate(att, axis=0) if len(att) > 1 else att[0]
            o_i = _bdot(att, v_i)
            if att_prev is not None:
                o_i = o_i + _bdot(att_prev, vh[0:r0])
            blocks.append(o_i)
        yield
        o = o_inter + (jnp.concatenate(blocks, axis=0) if len(blocks) > 1 else blocks[0])
        gate = zg_ref[gi, :, sl]
        out_ref[gi, :, sl] = _head_rms(o) * gn_ref[:, sl] * (gate * _sigmoid(gate))

    for gi in range(group):
        zf = zf_ref[gi]
        ez = jnp.exp(-jnp.abs(zf))
        big = 1.0 / (1.0 + ez)
        small = ez * big
        pos = zf >= 0.0
        logf = jnp.log(lb + oml * jnp.where(pos, big, small))
        kk_ref[gi] = oml * jnp.where(pos, small, big)

        p0 = logf.astype(BF16)
        r1 = logf - p0.astype(F32)
        p1 = r1.astype(BF16)
        p2 = (r1 - p1.astype(F32)).astype(BF16)
        bcum_ref[gi] = (jnp.dot(tril, p0, preferred_element_type=F32)
                        + jnp.dot(tril, p1, preferred_element_type=F32)
                        + jnp.dot(tril, p2, preferred_element_type=F32))
    _round_robin([unit(gi, h) for gi in range(group) for h in range(HEADS)])


def _hgrn(z_cd, lb_logits, gn_c, s0, b, l, layer):
    c = _chunk_len(l)
    sc = min(c, SUBLANES)
    nc = l // c
    grp = _seq_group(b, c)
    depth = lb_logits.shape[0]
    z3 = z_cd.reshape(b, l, z_cd.shape[-1])
    zspec = lambda col: pl.BlockSpec((grp, c, HW), lambda bi, j: (bi, j, col))
    st = pl.BlockSpec((grp, HEADS, DH, DH), lambda bi, j: (bi, 0, 0, 0))
    out, s_new = pl.pallas_call(
        functools.partial(_hgrn_kernel, c=c, sc=sc, layer=layer, group=grp),
        grid=(b // grp, nc),
        in_specs=[zspec(0), zspec(1), zspec(2), zspec(3), _full((depth, HW)), _full((1, HW)), st],
        out_specs=[pl.BlockSpec((grp, c, HW), lambda bi, j: (bi, j, 0)), st],
        out_shape=[jax.ShapeDtypeStruct((b, l, HW), F32),
                   jax.ShapeDtypeStruct((b, HEADS, DH, DH), F32)],
        scratch_shapes=[pltpu.VMEM((grp, c, HW), F32), pltpu.VMEM((grp, c, HW), F32)],
        compiler_params=_params(2),
        name="hgrn2",
    )(z3, z3, z3, z3, lb_logits, gn_c.reshape(1, -1), s0)
    return out.reshape(b * l, HW), s_new


def _s5_kernel(u_ref, are_ref, aim_ref, ldt_ref, bre_ref, bim_ref, cre_ref, cim_ref, d_ref,
               wglu_ref, bglu_ref, x0r_ref, x0i_ref,
               s_ref, xr_ref, xi_ref, ar_sc, ai_sc, bbr_sc, bbi_sc, bur_sc, bui_sc, *, ct):
    j = pl.program_id(1)
    ns = are_ref.shape[-1]
    nu = u_ref.shape[-1]
    hs = ns // 2
    hu = nu // 2
    rows = ct * SUBLANES

    @pl.when(j == 0)
    def _():
        a_re = are_ref[...]
        a_im = aim_ref[...]
        dt = jnp.exp(ldt_ref[...])
        mag = jnp.exp(dt * a_re)
        ar = mag * jnp.cos(dt * a_im)
        ai = mag * jnp.sin(dt * a_im)
        ar_sc[...] = jnp.broadcast_to(ar, (SUBLANES, ns))
        ai_sc[...] = jnp.broadcast_to(ai, (SUBLANES, ns))
        den = a_re * a_re + a_im * a_im
        nr = ar - 1.0
        zr = (nr * a_re + ai * a_im) / den
        zi = (ai * a_re - nr * a_im) / den
        for hg in range(2):
            us = slice(hg * hu, (hg + 1) * hu)
            ss = slice(hg * hs, (hg + 1) * hs)
            bbr_sc[us, :] = (zr[:, ss] * bre_ref[us, :] - zi[:, ss] * bim_ref[us, :]).astype(BF16)
            bbi_sc[us, :] = (zr[:, ss] * bim_ref[us, :] + zi[:, ss] * bre_ref[us, :]).astype(BF16)
        xr_ref[...] = x0r_ref[...]
        xi_ref[...] = x0i_ref[...]

    u = u_ref[...].reshape(rows, nu)
    ub = u.astype(BF16)
    for hg in range(2):
        us = slice(hg * hu, (hg + 1) * hu)
        ss = slice(hg * hs, (hg + 1) * hs)
        bur_sc[:, ss] = jnp.dot(ub[:, us], bbr_sc[us, :], preferred_element_type=F32)
        bui_sc[:, ss] = jnp.dot(ub[:, us], bbi_sc[us, :], preferred_element_type=F32)

    lane_chunk = 4 * LANES
    for lc in range(ns // lane_chunk):
        ls = slice(lc * lane_chunk, (lc + 1) * lane_chunk)
        ar = ar_sc[:, ls]
        ai = ai_sc[:, ls]

        def step(t, carry):
            xr, xi = carry
            r0 = pl.multiple_of(t * SUBLANES, SUBLANES)
            nxr = ar * xr - ai * xi + bur_sc[pl.ds(r0, SUBLANES), ls]
            nxi = ar * xi + ai * xr + bui_sc[pl.ds(r0, SUBLANES), ls]
            bur_sc[pl.ds(r0, SUBLANES), ls] = nxr
            bui_sc[pl.ds(r0, SUBLANES), ls] = nxi
            return nxr, nxi

        xr, xi = lax.fori_loop(0, ct, step, (xr_ref[:, ls], xi_ref[:, ls]), unroll=SUBLANES)
        xr_ref[:, ls] = xr
        xi_ref[:, ls] = xi

    ys = []
    for hg in range(2):
        ss = slice(hg * hs, (hg + 1) * hs)
        ys.append(jnp.dot(bur_sc[:, ss].astype(BF16), cre_ref[ss, :], preferred_element_type=F32)
                  - jnp.dot(bui_sc[:, ss].astype(BF16), cim_ref[ss, :], preferred_element_type=F32))
    y = jnp.concatenate(ys, axis=1) + d_ref[...] * u
    a = 0.5 * y * (1.0 + jnp.tanh(math.sqrt(2.0 / math.pi) * (y + 0.044715 * (y * y * y))))
    s = a * _sigmoid(jnp.dot(a.astype(BF16), wglu_ref[...], preferred_element_type=F32) + bglu_ref[...])
    s_ref[...] = s.reshape(ct, SUBLANES, nu)


def _s5_block_diag(bmat, cmat):
    g, p, hgrp = bmat.shape
    gh = g // 2
    eye = jnp.eye(gh, dtype=F32)
    b4 = bmat.reshape(2, gh, p, hgrp)
    bc = jnp.einsum('agph,gk->aghkp', b4, eye).reshape(2 * gh * hgrp, gh * p)
    c4 = cmat.reshape(2, gh, hgrp, p)
    cc = jnp.einsum('aghp,gk->agpkh', c4, eye).reshape(2 * gh * p, gh * hgrp)
    return bc, cc


def _s5_operands(a_re, a_im, log_dt, b_re, b_im, c_re, c_im, d_skip, w_glu, b_glu):
    g, p = a_re.shape
    ns = g * p
    nu = d_skip.shape[-1]
    assert nu == g * S5_GROUP and (g // 2) * S5_GROUP == MXU_DIM
    bre_c, cre_c = _s5_block_diag(b_re, c_re)
    bim_c, cim_c = _s5_block_diag(b_im, c_im)
    ldt = jnp.broadcast_to(log_dt[:, None], (g, p)).reshape(1, ns)
    return (a_re.reshape(1, ns), a_im.reshape(1, ns), ldt, bre_c, bim_c, cre_c.astype(BF16),
            cim_c.astype(BF16), d_skip.reshape(1, nu), w_glu.astype(BF16), b_glu.reshape(1, nu))


def _s5(u_tm, operands, x0r, x0i):
    l, b, nu = u_tm.shape
    g, p = x0r.shape[1:]
    ns = g * p
    assert b % SUBLANES == 0
    ct = _chunk_len(l)
    nct = l // ct
    rows = ct * SUBLANES
    xspec = pl.BlockSpec((SUBLANES, ns), lambda bb, j: (bb, 0))
    s, xr, xi = pl.pallas_call(
        functools.partial(_s5_kernel, ct=ct),
        grid=(b // SUBLANES, nct),
        in_specs=[pl.BlockSpec((ct, SUBLANES, nu), lambda bb, j: (j, bb, 0)),
                  _full((1, ns)), _full((1, ns)), _full((1, ns)),
                  _full((nu, ns // 2)), _full((nu, ns // 2)),
                  _full((ns, nu // 2)), _full((ns, nu // 2)),
                  _full((1, nu)), _full((nu, nu)), _full((1, nu)), xspec, xspec],
        out_specs=[pl.BlockSpec((ct, SUBLANES, nu), lambda bb, j: (j, bb, 0)), xspec, xspec],
        out_shape=[jax.ShapeDtypeStruct((l, b, nu), F32),
                   jax.ShapeDtypeStruct((b, ns), F32), jax.ShapeDtypeStruct((b, ns), F32)],
        scratch_shapes=[pltpu.VMEM((SUBLANES, ns), F32), pltpu.VMEM((SUBLANES, ns), F32),
                        pltpu.VMEM((nu, ns // 2), BF16), pltpu.VMEM((nu, ns // 2), BF16),
                        pltpu.VMEM((rows, ns), F32), pltpu.VMEM((rows, ns), F32)],
        compiler_params=_params(2, VMEM_LIMIT),
        name="s5",
    )(u_tm, *operands, x0r.reshape(b, ns), x0i.reshape(b, ns))
    return s, xr.reshape(b, g, p), xi.reshape(b, g, p)


def _trunk(groups, prm):
    d = groups[0]['x'].shape[-1]
    tm = 512
    info = []
    for grp in groups:
        b, l, _ = grp['x'].shape
        assert l >= CONV_W - 1 and (b * l) % tm == 0
        info.append(dict(b=b, l=l, t=b * l, direct_tm=b == SUBLANES and l % tm == 0))
    hs = [grp['x'].reshape(-1, d) for grp in groups]
    pps = [grp['p'].reshape(grp['p'].shape[0], -1, grp['p'].shape[-1]) for grp in groups]

    zs = _inproj(hs, prm['norm_mix'][0], prm['w_ab'], (8 * HW, LANES), tm, [None] * len(groups))
    ab_new, post_in = [], []
    for grp, inf, h, pp, (z_main, z_gate) in zip(groups, info, hs, pps, zs):
        b, l = inf['b'], inf['l']
        conv0, c0, n0, m0, ret0 = grp['ab_state']
        hm, conv_new, c_new, n_new, m_new = _mlstm(z_main, z_gate, prm['conv_w_ab'][0], prm['conv_b_ab'][0],
                                         prm['b_gate_ab'][0], prm['gn_a'][0], conv0[0], c0[0], n0[0],
                                         m0[0], b, l)
        cos_t, sin_t = _rope_tables(l, grp['pos0'])
        hr, ret_new = _retention(z_main, cos_t, sin_t, ret0[0], b, l)
        ab_new.append((conv_new[None], c_new[None], n_new[None], m_new[None], ret_new[None]))
        post_in.append((h, hm, hr, pp, None))
    hs = _post(post_in, prm['w_out_ab'], prm['norm_ff'][0], prm['w_ff1'], prm['w_ff2'],
               prm['norm_ple'][0], prm['w_ple_gate'], prm['w_ple_proj'], prm['norm_final'],
               layer=0, final=False, tm=tm)

    tml = [(inf['b'], inf['l']) if inf['direct_tm'] else None for inf in info]
    zs = _inproj(hs, prm['norm_mix'][1], prm['w_cd'], (4 * HW, HW), tm, tml)
    cd_new, post_in = [], []
    for grp, inf, h, pp, (z_cd, su) in zip(groups, info, hs, pps, zs):
        b, l, t = inf['b'], inf['l'], inf['t']
        hg0, x0r, x0i = grp['cd_state']
        o, hg_new = _hgrn(z_cd, prm['lb_logits'], prm['gn_c'][0], hg0[0], b, l, layer=1)
        u_tm = (su.reshape(l, b, HW) if inf['direct_tm']
                else jnp.transpose(su.reshape(b, l, HW), (1, 0, 2)))
        s_tm, xr, xi = _s5(u_tm, prm['s5_operands'], x0r[0], x0i[0])
        s_in = (s_tm.reshape(l, b * HW) if inf['direct_tm']
                else jnp.transpose(s_tm, (1, 0, 2)).reshape(t, HW))
        cd_new.append((hg_new[None], xr[None], xi[None]))
        post_in.append((h, o, s_in, pp, (b, l) if inf['direct_tm'] else None))
    ys = _post(post_in, prm['w_out_cd'], prm['norm_ff'][1], prm['w_ff1'], prm['w_ff2'],
               prm['norm_ple'][1], prm['w_ple_gate'], prm['w_ple_proj'], prm['norm_final'],
               layer=1, final=True, tm=tm)
    ys = [y.reshape(grp['x'].shape) for y, grp in zip(ys, groups)]
    return ys, ab_new, cd_new


def kernel(x_prompt, x_sample, state_mlstm_conv, state_mlstm_C, state_mlstm_n, state_mlstm_m, state_ret, state_hgrn, state_s5_re, state_s5_im, p_prompt, p_sample, norm_mix, norm_ff, norm_ple, norm_final, w_in_ab, b_gate_ab, conv_w_ab, conv_b_ab, gn_a, w_out_ab, w_in_cd, lb_logits, gn_c, s5_A_re, s5_A_im, s5_log_dt, s5_B_re, s5_B_im, s5_C_re, s5_C_im, s5_D, w_glu, b_glu, w_out_cd, w_ff1, w_ff2, w_ple_proj, w_ple_gate):
    assert norm_mix.shape[0] == 2, "two layers: (mLSTM || retention), (HGRN2 || S5)"
    w_ab = w_in_ab[0]
    gate0 = 4 * HW
    w_ab = jnp.concatenate([w_ab[:, :gate0].astype(BF16), w_ab[:, gate0 + 2 * HEADS:].astype(BF16),
                            w_ab[:, gate0:gate0 + 2 * HEADS].astype(BF16),
                            jnp.zeros((w_ab.shape[0], LANES - 2 * HEADS), BF16)], axis=1)
    prm = dict(norm_mix=norm_mix, norm_ff=norm_ff, norm_ple=norm_ple, norm_final=norm_final,
               w_ab=w_ab, b_gate_ab=b_gate_ab, conv_w_ab=conv_w_ab, conv_b_ab=conv_b_ab,
               gn_a=gn_a, w_out_ab=w_out_ab[0].astype(BF16), w_cd=w_in_cd[0].astype(BF16),
               lb_logits=lb_logits, gn_c=gn_c,
               s5_operands=_s5_operands(s5_A_re[0], s5_A_im[0], s5_log_dt[0], s5_B_re[0], s5_B_im[0],
                                        s5_C_re[0], s5_C_im[0], s5_D[0], w_glu[0], b_glu[0]),
               w_out_cd=w_out_cd[0].astype(BF16), w_ff1=w_ff1.astype(BF16), w_ff2=w_ff2.astype(BF16),
               w_ple_proj=w_ple_proj.astype(BF16), w_ple_gate=w_ple_gate.astype(BF16))

    bp, lp, _ = x_prompt.shape
    z = lambda *s: jnp.zeros(s, F32)
    zero_ab = (z(1, bp, CONV_W - 1, 2 * HW), z(1, bp, HEADS, DH, DH), z(1, bp, HEADS, DH),
               z(1, bp, HEADS), z(1, bp, HEADS, DH, DH))
    zero_cd = (z(1, bp, HEADS, DH, DH),) + (z(*((1, bp) + s5_A_re.shape[1:])),) * 2
    groups = [dict(x=x_prompt, p=p_prompt, pos0=0, ab_state=zero_ab, cd_state=zero_cd),
              dict(x=x_sample, p=p_sample, pos0=PAST_LEN,
                   ab_state=(state_mlstm_conv, state_mlstm_C, state_mlstm_n, state_mlstm_m, state_ret),
                   cd_state=(state_hgrn, state_s5_re, state_s5_im))]
    (y_p, y_s), (ab_p, ab_s), (cd_p, cd_s) = _trunk(groups, prm)
    return (y_p, y_s,
            ab_p[0], ab_s[0], ab_p[1], ab_s[1], ab_p[2], ab_s[2], ab_p[3], ab_s[3], ab_p[4], ab_s[4],
            cd_p[0], cd_s[0], cd_p[1], cd_s[1], cd_p[2], cd_s[2])
```

```python
import functools
import math

import jax
import jax.numpy as jnp
from jax import lax
from jax.experimental import pallas as pl
from jax.experimental.pallas import tpu as pltpu

F32 = jnp.float32
BF16 = jnp.bfloat16

EPS = 1e-6
NEG = -1e30
LOG2_E = math.log2(math.e)
ROPE_BASE = 10000.0
PAST_LEN = 16384
CHUNK = 64
HEADS = 4
DH = 128
HW = HEADS * DH
CONV_W = 4
S5_GROUP = 16
S5_STATE = 64
SUBLANES = 8
LANES = 128
MXU_DIM = 256
VMEM_LIMIT = 56 * 1024 * 1024


def _params(n_axes, vmem=None):
    return pltpu.CompilerParams(dimension_semantics=("arbitrary",) * n_axes, vmem_limit_bytes=vmem)


def _full(shape):
    return pl.BlockSpec(shape, lambda *_: (0,) * len(shape))


def _bdot(a, b):
    return jnp.dot(a.astype(BF16), b.astype(BF16), preferred_element_type=F32)


def _bdot_nt(a, b):
    return lax.dot_general(a.astype(BF16), b.astype(BF16), (((1,), (1,)), ((), ())),
                           preferred_element_type=F32)


def _bdot_tn(a, b):
    return lax.dot_general(a.astype(BF16), b.astype(BF16), (((0,), (0,)), ((), ())),
                           preferred_element_type=F32)


def _sigmoid(x):
    return 1.0 / (1.0 + jnp.exp(-x))


def _log_sigmoid(x):
    return jnp.minimum(x, 0.0) - jnp.log(1.0 + jnp.exp(-jnp.abs(x)))


def _rms(x, g):
    return x * lax.rsqrt(jnp.mean(x * x, axis=-1, keepdims=True) + EPS) * g


def _head_rms(x):
    return x * lax.rsqrt(jnp.mean(x * x, axis=-1, keepdims=True) + EPS)


def _chunk_len(length):
    return CHUNK if length % CHUNK == 0 else length


def _round_robin(gens):
    gens = list(gens)
    while gens:
        alive = []
        for g in gens:
            try:
                next(g)
                alive.append(g)
            except StopIteration:
                pass
        gens = alive


def _step_ranges(counts):
    starts = [0]
    for n in counts:
        starts.append(starts[-1] + n)
    return starts


def _local(i, start, count):
    return jnp.clip(i - start, 0, count - 1)


def _inproj_kernel(*refs, n_groups, n_out, starts):
    x_refs = refs[:n_groups]
    g_ref, w_ref = refs[n_groups:n_groups + 2]
    out_refs = refs[n_groups + 2:]
    i = pl.program_id(0)

    def run(x_ref, outs):
        hn = _rms(x_ref[...], g_ref[...]).astype(BF16)
        off = 0
        for o_ref in outs:
            n = o_ref.shape[-1]
            for n0 in range(0, n, HW):
                nn = min(HW, n - n0)
                o_ref[:, n0:n0 + nn] = jnp.dot(hn, w_ref[:, off + n0:off + n0 + nn],
                                               preferred_element_type=F32)
            off += n

    for gi in range(n_groups):
        @pl.when((i >= starts[gi]) & (i < starts[gi + 1]))
        def _(gi=gi):
            run(x_refs[gi], out_refs[gi * n_out:(gi + 1) * n_out])


def _inproj(hs, g, w, widths, tm, time_major_last):
    d = hs[0].shape[1]
    n = w.shape[1]
    assert sum(widths) == n
    counts = [h.shape[0] // tm for h in hs]
    starts = _step_ranges(counts)
    in_specs, out_specs, out_shape = [], [], []
    for k, h in enumerate(hs):
        t = h.shape[0]
        assert t % tm == 0
        loc = functools.partial(_local, start=starts[k], count=counts[k])
        in_specs.append(pl.BlockSpec((tm, d), lambda i, loc=loc: (loc(i), 0)))
        for wi, wd in enumerate(widths):
            if wi == len(widths) - 1 and time_major_last[k] is not None:
                b, l = time_major_last[k]
                assert l % tm == 0
                nl = l // tm
                out_shape.append(jax.ShapeDtypeStruct((l, b * wd), F32))
                out_specs.append(pl.BlockSpec((tm, wd), lambda i, loc=loc, nl=nl: (loc(i) % nl, loc(i) // nl)))
            else:
                out_shape.append(jax.ShapeDtypeStruct((t, wd), F32))
                out_specs.append(pl.BlockSpec((tm, wd), lambda i, loc=loc: (loc(i), 0)))
    outs = pl.pallas_call(
        functools.partial(_inproj_kernel, n_groups=len(hs), n_out=len(widths), starts=tuple(starts)),
        grid=(starts[-1],),
        in_specs=in_specs + [_full((1, d)), _full((d, n))],
        out_specs=out_specs,
        out_shape=out_shape,
        compiler_params=_params(1, VMEM_LIMIT),
        name="inproj",
    )(*hs, g.reshape(1, d), w)
    nw = len(widths)
    return [outs[k * nw:(k + 1) * nw] for k in range(len(hs))]


def _post_kernel(*refs, n_groups, starts, final, ff_chunk):
    grp_in = [refs[4 * k:4 * k + 4] for k in range(n_groups)]
    (wo_ref, gff_ref, w1_ref, w2_ref, gple_ref, wg_ref, wp_ref,
     gfin_ref) = refs[4 * n_groups:4 * n_groups + 8]
    out_refs = refs[4 * n_groups + 8:]
    i = pl.program_id(0)

    def run(h_ref, ma_ref, mb_ref, p_ref, o_ref):
        half = ma_ref.shape[-1]
        h = h_ref[...]
        h = h + (jnp.dot(ma_ref[...].astype(BF16), wo_ref[0:half, :], preferred_element_type=F32)
                 + jnp.dot(mb_ref[...].astype(BF16), wo_ref[half:2 * half, :],
                           preferred_element_type=F32))
        hn = _rms(h, gff_ref[...]).astype(BF16)
        d_ff = w1_ref.shape[1]
        acc = jnp.zeros_like(h)
        for f0 in range(0, d_ff, ff_chunk):
            a = jnp.dot(hn, w1_ref[:, f0:f0 + ff_chunk], preferred_element_type=F32)
            a = jnp.square(jnp.maximum(a, 0.0))
            acc = acc + jnp.dot(a.astype(BF16), w2_ref[f0:f0 + ff_chunk, :], preferred_element_type=F32)
        h = h + acc
        gate = _sigmoid(jnp.dot(_rms(h, gple_ref[...]).astype(BF16), wg_ref[...],
                                preferred_element_type=F32))
        h = h + gate * jnp.dot(p_ref[...].astype(BF16), wp_ref[...], preferred_element_type=F32)
        o_ref[...] = _rms(h, gfin_ref[...]) if final else h

    for k in range(n_groups):
        @pl.when((i >= starts[k]) & (i < starts[k + 1]))
        def _(k=k):
            run(*grp_in[k], out_refs[k])


def _post(groups, wo, gff, w1, w2, gple, wg, wp, gfin, *, layer, final, tm):
    d = groups[0][0].shape[1]
    d_ff = w1.shape[-1]
    counts = [grp[0].shape[0] // tm for grp in groups]
    starts = _step_ranges(counts)
    in_specs, out_specs, out_shape, operands = [], [], [], []
    for k, (h, mix_a, mix_b, p, b_time_major) in enumerate(groups):
        t = h.shape[0]
        half = mix_a.shape[-1]
        pd = p.shape[-1]
        assert t % tm == 0
        loc = functools.partial(_local, start=starts[k], count=counts[k])
        row = lambda i, loc=loc: (loc(i), 0)
        mb_spec = pl.BlockSpec((tm, half), row)
        if b_time_major is not None:
            _, l = b_time_major
            assert l % tm == 0
            nl = l // tm
            mb_spec = pl.BlockSpec((tm, half), lambda i, loc=loc, nl=nl: (loc(i) % nl, loc(i) // nl))
        in_specs += [pl.BlockSpec((tm, d), row), pl.BlockSpec((tm, half), row), mb_spec,
                     pl.BlockSpec((None, tm, pd), lambda i, loc=loc: (layer, loc(i), 0))]
        operands += [h, mix_a, mix_b, p]
        out_specs.append(pl.BlockSpec((tm, d), row))
        out_shape.append(jax.ShapeDtypeStruct((t, d), F32))
    pd = groups[0][3].shape[-1]
    lw = lambda r, cdim: pl.BlockSpec((None, r, cdim), lambda i: (layer, 0, 0))
    return pl.pallas_call(
        functools.partial(_post_kernel, n_groups=len(groups), starts=tuple(starts), final=final,
                          ff_chunk=1024),
        grid=(starts[-1],),
        in_specs=in_specs + [_full((d, d)), _full((1, d)), lw(d, d_ff), lw(d_ff, d), _full((1, d)),
                             lw(d, d), lw(pd, d), _full((1, d))],
        out_specs=out_specs,
        out_shape=out_shape,
        compiler_params=_params(1, VMEM_LIMIT),
        name="post",
    )(*operands, wo, gff.reshape(1, d), w1, w2, gple.reshape(1, d), wg, wp, gfin.reshape(1, d))


def _mlstm_kernel(zq_ref, zk_ref, zv_ref, zo_ref, zg_ref, cw_ref, cb_ref, bg_ref, gn_ref,
                  conv0_ref, c0_ref, n0_ref, m0_ref,
                  out_ref, conv_ref, c_ref, n_ref, m_ref, ext_ref, u_ref, qk_ref, gates_ref,
                  *, c, group):
    j = pl.program_id(1)
    tail = CONV_W - 1

    @pl.when(j == 0)
    def _():
        c_ref[...] = c0_ref[...]
        n_ref[...] = n0_ref[...]
        m_ref[...] = m0_ref[...]
        for gi in range(group):
            ext_ref[gi, 0:SUBLANES, :] = jnp.zeros((SUBLANES, 2 * HW), F32)
            ext_ref[gi, SUBLANES - tail:SUBLANES, :] = conv0_ref[gi]

    ri = lax.broadcasted_iota(jnp.int32, (c, c), 0)
    ci = lax.broadcasted_iota(jnp.int32, (c, c), 1)
    eye = ri == ci
    tril = ri >= ci
    lane = lax.broadcasted_iota(jnp.int32, (c, LANES), 1)
    bg = bg_ref[...]

    def lsum(x):
        return jnp.broadcast_to(jnp.sum(x, axis=1, keepdims=True), (c, DH))

    def unit(gi, h):
        sl = slice(h * DH, (h + 1) * DH)
        gates = gates_ref[gi]
        i_col = lsum(jnp.where(lane == h, gates, 0.0))
        f_col = lsum(jnp.where(lane == HEADS + h, gates, 0.0))
        yield
        b_row = jnp.sum(jnp.where(ri <= ci, f_col[:, :c], 0.0), axis=0, keepdims=True)
        b_col = lsum(jnp.where(eye, b_row, 0.0))
        i_row = jnp.sum(jnp.where(eye, i_col[:, :c], 0.0), axis=0, keepdims=True)
        yield
        m_prev = m_ref[gi, h:h + 1, :]
        dmat = jnp.where(tril, b_col[:, :c] - b_row + i_row, NEG)
        inter = b_col + m_prev
        row_max = jnp.broadcast_to(jnp.max(dmat, axis=1, keepdims=True), (c, DH))
        qh = qk_ref[gi, :, sl]
        kh = qk_ref[gi, :, HW + h * DH:HW + (h + 1) * DH] * (DH ** -0.5)
        vh = zv_ref[gi, :, sl]
        c_h = c_ref[gi, h]
        n_h = n_ref[gi, h:h + 1, :]
        s_raw = _bdot_nt(qh, kh)
        q_c = _bdot(qh, c_h)
        q_n = lsum(qh * n_h)
        yield
        m_t = jnp.maximum(inter, row_max)
        w_intra = jnp.exp(dmat - m_t[:, :c])
        w_inter = jnp.exp(inter - m_t)
        s = s_raw * w_intra
        s_v = _bdot(s, vh)
        s_sum = lsum(s)
        m_new = m_t[c - 1:c, :]
        b_last = b_col[c - 1:c, :]
        w_last = jnp.exp(b_last - b_col + i_col - m_new)
        decay = jnp.exp(b_last + m_prev - m_new)
        kw = w_last * kh
        kw_v = _bdot_tn(kw, vh)
        yield
        num = w_inter * q_c + s_v
        den = w_inter * q_n + s_sum
        hh = num / jnp.maximum(jnp.abs(den), jnp.exp(-m_t))
        c_ref[gi, h] = decay * c_h + kw_v
        n_ref[gi, h:h + 1, :] = decay * n_h + jnp.sum(kw, axis=0, keepdims=True)
        m_ref[gi, h:h + 1, :] = m_new
        hh = _sigmoid(zo_ref[gi, :, sl]) * hh
        out_ref[gi, :, sl] = _head_rms(hh) * gn_ref[:, sl]

    for gi in range(group):
        ext_ref[gi, SUBLANES:SUBLANES + c, 0:HW] = zq_ref[gi]
        ext_ref[gi, SUBLANES:SUBLANES + c, HW:2 * HW] = zk_ref[gi]
        assert CONV_W == 4
        x0 = ext_ref[gi, SUBLANES:SUBLANES + c, :]
        x2 = ext_ref[gi, SUBLANES - 2:SUBLANES - 2 + c, :]
        u_ref[gi, SUBLANES:SUBLANES + c, :] = cw_ref[2:3, :] * x0 + cw_ref[0:1, :] * x2
        u_ref[gi, SUBLANES - 1:SUBLANES, :] = (cw_ref[2:3, :] * ext_ref[gi, SUBLANES - 1:SUBLANES, :]
                                               + cw_ref[0:1, :] * ext_ref[gi, SUBLANES - 3:SUBLANES - 2, :])
        conv = (cb_ref[...] + cw_ref[3:4, :] * x0 + cw_ref[1:2, :] * x2
                + u_ref[gi, SUBLANES - 1:SUBLANES - 1 + c, :])
        ext_ref[gi, 0:SUBLANES, :] = ext_ref[gi, c:c + SUBLANES, :]
        conv_ref[gi] = ext_ref[gi, SUBLANES - tail:SUBLANES, :]
        qk_ref[gi] = conv * _sigmoid(conv)
        gb = zg_ref[gi] + bg
        gates_ref[gi] = jnp.where(lane < HEADS, gb, _log_sigmoid(gb))
    _round_robin([unit(gi, h) for gi in range(group) for h in range(HEADS)])


def _seq_group(b, c):
    rows = 512
    group = max(1, min(b, rows // c, 2 * SUBLANES))
    assert b % group == 0
    return group


def _mlstm(z_main, z_gate, conv_w, conv_b, b_gate, gn_a, conv0, c0, n0, m0, b, l):
    c = _chunk_len(l)
    nc = l // c
    grp = _seq_group(b, c)
    z3 = z_main.reshape(b, l, z_main.shape[-1])
    zspec = lambda col: pl.BlockSpec((grp, c, HW), lambda bi, j: (bi, j, col))
    st = lambda shape: pl.BlockSpec((grp,) + shape, lambda bi, j: (bi,) + (0,) * len(shape))
    m0b = jnp.broadcast_to(m0[:, :, None], (b, HEADS, DH))
    out, conv_new, c_new, n_new, m_new = pl.pallas_call(
        functools.partial(_mlstm_kernel, c=c, group=grp),
        grid=(b // grp, nc),
        in_specs=[zspec(0), zspec(1), zspec(2), zspec(3),
                  pl.BlockSpec((grp, c, LANES), lambda bi, j: (bi, j, 0)),
                  _full((CONV_W, 2 * HW)), _full((1, 2 * HW)), _full((1, LANES)), _full((1, HW)),
                  st((CONV_W - 1, 2 * HW)), st((HEADS, DH, DH)), st((HEADS, DH)), st((HEADS, DH))],
        out_specs=[pl.BlockSpec((grp, c, HW), lambda bi, j: (bi, j, 0)),
                   st((CONV_W - 1, 2 * HW)), st((HEADS, DH, DH)), st((HEADS, DH)), st((HEADS, DH))],
        out_shape=[jax.ShapeDtypeStruct((b, l, HW), F32),
                   jax.ShapeDtypeStruct((b, CONV_W - 1, 2 * HW), F32),
                   jax.ShapeDtypeStruct((b, HEADS, DH, DH), F32),
                   jax.ShapeDtypeStruct((b, HEADS, DH), F32),
                   jax.ShapeDtypeStruct((b, HEADS, DH), F32)],
        scratch_shapes=[pltpu.VMEM((grp, c + SUBLANES, 2 * HW), F32),
                        pltpu.VMEM((grp, c + SUBLANES, 2 * HW), F32),
                        pltpu.VMEM((grp, c, 2 * HW), F32), pltpu.VMEM((grp, c, LANES), F32)],
        compiler_params=_params(2),
        name="mlstm",
    )(z3, z3, z3, z3, z_gate.reshape(b, l, LANES), conv_w, conv_b.reshape(1, -1),
      jnp.pad(b_gate, (0, LANES - 2 * HEADS)).reshape(1, LANES), gn_a.reshape(1, -1),
      conv0, c0, n0, m0b)
    return out.reshape(b * l, HW), conv_new, c_new, n_new, m_new[:, :, 0]


def _rope_table_kernel(inv_ref, sign_ref, cos_ref, sin_ref, *, pos0, rows):
    i = pl.program_id(0)
    pos = (pos0 + i * rows + lax.broadcasted_iota(jnp.int32, (rows, LANES), 0)).astype(F32)
    ang = pos * inv_ref[...]
    cos_ref[...] = jnp.cos(ang)
    sin_ref[...] = jnp.sin(ang) * sign_ref[...]


def _rope_tables(l, pos0):
    half = DH // 2
    inv = ROPE_BASE ** (-jnp.arange(half, dtype=F32) / half)
    inv2 = jnp.concatenate([inv, inv]).reshape(1, DH)
    sign = jnp.concatenate([-jnp.ones((half,), F32), jnp.ones((half,), F32)]).reshape(1, DH)
    rows = min(l, 512)
    assert l % rows == 0
    return pl.pallas_call(
        functools.partial(_rope_table_kernel, pos0=pos0, rows=rows),
        grid=(l // rows,),
        in_specs=[_full((1, DH)), _full((1, DH))],
        out_specs=[pl.BlockSpec((rows, DH), lambda i: (i, 0))] * 2,
        out_shape=[jax.ShapeDtypeStruct((l, DH), F32)] * 2,
        compiler_params=_params(1),
        name="rope_table",
    )(inv2, sign)


def _ret_kernel(zq_ref, zk_ref, zv_ref, zg_ref, cos_ref, sin_ref, s0_ref, out_ref, s_ref,
                *, c, group):
    j = pl.program_id(1)

    @pl.when(j == 0)
    def _():
        s_ref[...] = s0_ref[...]

    cosf = cos_ref[...]
    sinf = sin_ref[...]
    ti = lax.broadcasted_iota(jnp.int32, (c, c), 0)
    si = lax.broadcasted_iota(jnp.int32, (c, c), 1)
    rel = jnp.maximum(ti - si, 0).astype(F32)
    tcol = lax.broadcasted_iota(jnp.int32, (c, 1), 0).astype(F32)

    def rope(x):
        return x * cosf + pltpu.roll(x, DH // 2, axis=1) * sinf

    def unit(gi, h, decay, inter, kdecay, cdecay):
        sl = slice(h * DH, (h + 1) * DH)
        qr = rope(zq_ref[gi, :, sl])
        kr = rope(zk_ref[gi, :, sl]) * (DH ** -0.5)
        yield
        vh = zv_ref[gi, :, sl]
        s_h = s_ref[gi, h]
        qk = _bdot_nt(qr, kr)
        q_s = _bdot(qr, s_h)
        k_v = _bdot_tn(kr * kdecay, vh)
        yield
        o = q_s * inter + _bdot(qk * decay, vh)
        s_ref[gi, h] = cdecay * s_h + k_v
        yield
        gate = zg_ref[gi, :, sl]
        out_ref[gi, :, sl] = _head_rms(o) * (gate * _sigmoid(gate))

    units = []
    for h in range(HEADS):
        lg = math.log1p(-(2.0 ** (-5.0 - h)))
        decay = jnp.where(ti >= si, jnp.exp(rel * lg), 0.0)
        inter = jnp.exp((tcol + 1.0) * lg)
        kdecay = jnp.exp((c - 1.0 - tcol) * lg)
        cdecay = math.exp(c * lg)
        units += [unit(gi, h, decay, inter, kdecay, cdecay) for gi in range(group)]
    _round_robin(units)


def _retention(z_main, cos_t, sin_t, s0, b, l):
    c = _chunk_len(l)
    nc = l // c
    grp = _seq_group(b, c)
    z3 = z_main.reshape(b, l, z_main.shape[-1])
    zspec = lambda col: pl.BlockSpec((grp, c, HW), lambda bi, j: (bi, j, col))
    st = pl.BlockSpec((grp, HEADS, DH, DH), lambda bi, j: (bi, 0, 0, 0))
    tab = pl.BlockSpec((c, DH), lambda bi, j: (j, 0))
    out, s_new = pl.pallas_call(
        functools.partial(_ret_kernel, c=c, group=grp),
        grid=(b // grp, nc),
        in_specs=[zspec(4), zspec(5), zspec(6), zspec(7), tab, tab, st],
        out_specs=[pl.BlockSpec((grp, c, HW), lambda bi, j: (bi, j, 0)), st],
        out_shape=[jax.ShapeDtypeStruct((b, l, HW), F32),
                   jax.ShapeDtypeStruct((b, HEADS, DH, DH), F32)],
        compiler_params=_params(2),
        name="retention",
    )(z3, z3, z3, z3, cos_t, sin_t, s0)
    return out.reshape(b * l, HW), s_new


def _hgrn_kernel(zq_ref, zf_ref, zi_ref, zg_ref, lbl_ref, gn_ref, s0_ref, out_ref, s_ref,
                 kk_ref, bcum_ref, *, c, sc, layer, group):
    j = pl.program_id(1)

    @pl.when(j == 0)
    def _():
        s_ref[...] = s0_ref[...]

    lbl = lbl_ref[...]
    e = jnp.exp(lbl - jnp.max(lbl, axis=0, keepdims=True))
    sm = e / jnp.sum(e, axis=0, keepdims=True)
    cum = sm[0:1, :]
    for r in range(1, layer + 1):
        cum = cum + sm[r:r + 1, :]
    lb = cum - sm[0:1, :]

    oml = 1.0 - lb
    ri = lax.broadcasted_iota(jnp.int32, (c, c), 0)
    ci = lax.broadcasted_iota(jnp.int32, (c, c), 1)
    tril = jnp.where(ri >= ci, 1.0, 0.0).astype(BF16)
    row_s = lax.broadcasted_iota(jnp.int32, (sc, sc), 0)
    lane_s = lax.broadcasted_iota(jnp.int32, (sc, sc), 1)
    causal_col = jnp.where(row_s >= lane_s, lane_s, -1)
    keep = [causal_col == s for s in range(sc)]
    e_r = lax.broadcasted_iota(jnp.int32, (DH, DH), 0)
    e_c = lax.broadcasted_iota(jnp.int32, (DH, DH), 1)
    eye = e_r == e_c

    def unit(gi, h):
        sl = slice(h * DH, (h + 1) * DH)
        bh = bcum_ref[gi, :, sl]
        qh = zq_ref[gi, :, sl] * (DH ** -0.5)
        kh = kk_ref[gi, :, sl]
        vh = zi_ref[gi, :, sl]
        s_h = s_ref[gi, h]
        b_last = bh[c - 1:c, :]
        o_inter = _bdot(qh * jnp.exp(bh), s_h)
        k_v = _bdot_tn(kh * jnp.exp(b_last - bh), vh)
        dec_col = jnp.sum(jnp.where(eye, jnp.exp(b_last), 0.0), axis=1, keepdims=True)
        yield
        s_ref[gi, h] = dec_col * s_h + k_v
        blocks = []
        for blk in range(c // sc):
            r0 = blk * sc
            b_i = bh[r0:r0 + sc]
            q_i = qh[r0:r0 + sc]
            k_i = kh[r0:r0 + sc]
            v_i = vh[r0:r0 + sc]
            att_prev = None
            if blk > 0:
                ref_row = bh[r0 - 1:r0, :]
                a_i = q_i * jnp.exp(b_i - ref_row)
                k_prev = kh[0:r0] * jnp.exp(ref_row - bh[0:r0])
                att_prev = _bdot_nt(a_i, k_prev)
            b2_i = b_i * LOG2_E
            c2_i = b2_i - jnp.log2(k_i)
            cols = [jnp.sum(q_i * jnp.exp2(b2_i - c2_i[s:s + 1, :]), axis=1, keepdims=True)
                    for s in range(sc)]
            yield
            att = jnp.zeros((sc, sc), F32)
            for s in range(sc):
                att = jnp.where(keep[s], cols[s], att)
            o_i = _bdot(att, v_i)
            if att_prev is not None:
                o_i = o_i + _bdot(att_prev, vh[0:r0])
            blocks.append(o_i)
        yield
        o = o_inter + (jnp.concatenate(blocks, axis=0) if len(blocks) > 1 else blocks[0])
        gate = zg_ref[gi, :, sl]
        out_ref[gi, :, sl] = _head_rms(o) * gn_ref[:, sl] * (gate * _sigmoid(gate))

    for gi in range(group):
        zf = zf_ref[gi]
        ez = jnp.exp(-jnp.abs(zf))
        big = 1.0 / (1.0 + ez)
        small = ez * big
        pos = zf >= 0.0
        logf = jnp.log(lb + oml * jnp.where(pos, big, small))
        kk_ref[gi] = oml * jnp.where(pos, small, big)

        p0 = logf.astype(BF16)
        r1 = logf - p0.astype(F32)
        p1 = r1.astype(BF16)
        p2 = (r1 - p1.astype(F32)).astype(BF16)
        bcum_ref[gi] = (jnp.dot(tril, p0, preferred_element_type=F32)
                        + jnp.dot(tril, p1, preferred_element_type=F32)
                        + jnp.dot(tril, p2, preferred_element_type=F32))
    _round_robin([unit(gi, h) for gi in range(group) for h in range(HEADS)])


def _hgrn(z_cd, lb_logits, gn_c, s0, b, l, layer):
    c = _chunk_len(l)
    sc = min(c, SUBLANES)
    nc = l // c
    grp = _seq_group(b, c)
    depth = lb_logits.shape[0]
    z3 = z_cd.reshape(b, l, z_cd.shape[-1])
    zspec = lambda col: pl.BlockSpec((grp, c, HW), lambda bi, j: (bi, j, col))
    st = pl.BlockSpec((grp, HEADS, DH, DH), lambda bi, j: (bi, 0, 0, 0))
    out, s_new = pl.pallas_call(
        functools.partial(_hgrn_kernel, c=c, sc=sc, layer=layer, group=grp),
        grid=(b // grp, nc),
        in_specs=[zspec(0), zspec(1), zspec(2), zspec(3), _full((depth, HW)), _full((1, HW)), st],
        out_specs=[pl.BlockSpec((grp, c, HW), lambda bi, j: (bi, j, 0)), st],
        out_shape=[jax.ShapeDtypeStruct((b, l, HW), F32),
                   jax.ShapeDtypeStruct((b, HEADS, DH, DH), F32)],
        scratch_shapes=[pltpu.VMEM((grp, c, HW), F32), pltpu.VMEM((grp, c, HW), F32)],
        compiler_params=_params(2),
        name="hgrn2",
    )(z3, z3, z3, z3, lb_logits, gn_c.reshape(1, -1), s0)
    return out.reshape(b * l, HW), s_new


def _s5_kernel(u_ref, are_ref, aim_ref, ldt_ref, bre_ref, bim_ref, cre_ref, cim_ref, d_ref,
               wglu_ref, bglu_ref, x0r_ref, x0i_ref,
               s_ref, xr_ref, xi_ref, ar_sc, ai_sc, bbr_sc, bbi_sc, bur_sc, bui_sc, *, ct):
    j = pl.program_id(1)
    ns = are_ref.shape[-1]
    nu = u_ref.shape[-1]
    hs = ns // 2
    hu = nu // 2
    rows = ct * SUBLANES

    @pl.when(j == 0)
    def _():
        a_re = are_ref[...]
        a_im = aim_ref[...]
        dt = jnp.exp(ldt_ref[...])
        mag = jnp.exp(dt * a_re)
        ar = mag * jnp.cos(dt * a_im)
        ai = mag * jnp.sin(dt * a_im)
        ar_sc[...] = jnp.broadcast_to(ar, (SUBLANES, ns))
        ai_sc[...] = jnp.broadcast_to(ai, (SUBLANES, ns))
        den = a_re * a_re + a_im * a_im
        nr = ar - 1.0
        zr = (nr * a_re + ai * a_im) / den
        zi = (ai * a_re - nr * a_im) / den
        for hg in range(2):
            us = slice(hg * hu, (hg + 1) * hu)
            ss = slice(hg * hs, (hg + 1) * hs)
            bbr_sc[us, :] = (zr[:, ss] * bre_ref[us, :] - zi[:, ss] * bim_ref[us, :]).astype(BF16)
            bbi_sc[us, :] = (zr[:, ss] * bim_ref[us, :] + zi[:, ss] * bre_ref[us, :]).astype(BF16)
        xr_ref[...] = x0r_ref[...]
        xi_ref[...] = x0i_ref[...]

    u = u_ref[...].reshape(rows, nu)
    ub = u.astype(BF16)
    for hg in range(2):
        us = slice(hg * hu, (hg + 1) * hu)
        ss = slice(hg * hs, (hg + 1) * hs)
        bur_sc[:, ss] = jnp.dot(ub[:, us], bbr_sc[us, :], preferred_element_type=F32)
        bui_sc[:, ss] = jnp.dot(ub[:, us], bbi_sc[us, :], preferred_element_type=F32)

    lane_chunk = 4 * LANES
    for lc in range(ns // lane_chunk):
        ls = slice(lc * lane_chunk, (lc + 1) * lane_chunk)
        ar = ar_sc[:, ls]
        ai = ai_sc[:, ls]

        def step(t, carry):
            xr, xi = carry
            r0 = pl.multiple_of(t * SUBLANES, SUBLANES)
            nxr = ar * xr - ai * xi + bur_sc[pl.ds(r0, SUBLANES), ls]
            nxi = ar * xi + ai * xr + bui_sc[pl.ds(r0, SUBLANES), ls]
            bur_sc[pl.ds(r0, SUBLANES), ls] = nxr
            bui_sc[pl.ds(r0, SUBLANES), ls] = nxi
            return nxr, nxi

        xr, xi = lax.fori_loop(0, ct, step, (xr_ref[:, ls], xi_ref[:, ls]), unroll=SUBLANES)
        xr_ref[:, ls] = xr
        xi_ref[:, ls] = xi

    ys = []
    for hg in range(2):
        ss = slice(hg * hs, (hg + 1) * hs)
        ys.append(jnp.dot(bur_sc[:, ss].astype(BF16), cre_ref[ss, :], preferred_element_type=F32)
                  - jnp.dot(bui_sc[:, ss].astype(BF16), cim_ref[ss, :], preferred_element_type=F32))
    y = jnp.concatenate(ys, axis=1) + d_ref[...] * u
    a = 0.5 * y * (1.0 + jnp.tanh(math.sqrt(2.0 / math.pi) * (y + 0.044715 * (y * y * y))))
    s = a * _sigmoid(jnp.dot(a.astype(BF16), wglu_ref[...], preferred_element_type=F32) + bglu_ref[...])
    s_ref[...] = s.reshape(ct, SUBLANES, nu)


def _s5_block_diag(bmat, cmat):
    g, p, hgrp = bmat.shape
    gh = g // 2
    eye = jnp.eye(gh, dtype=F32)
    b4 = bmat.reshape(2, gh, p, hgrp)
    bc = jnp.einsum('agph,gk->aghkp', b4, eye).reshape(2 * gh * hgrp, gh * p)
    c4 = cmat.reshape(2, gh, hgrp, p)
    cc = jnp.einsum('aghp,gk->agpkh', c4, eye).reshape(2 * gh * p, gh * hgrp)
    return bc, cc


def _s5_operands(a_re, a_im, log_dt, b_re, b_im, c_re, c_im, d_skip, w_glu, b_glu):
    g, p = a_re.shape
    ns = g * p
    nu = d_skip.shape[-1]
    assert nu == g * S5_GROUP and (g // 2) * S5_GROUP == MXU_DIM
    bre_c, cre_c = _s5_block_diag(b_re, c_re)
    bim_c, cim_c = _s5_block_diag(b_im, c_im)
    ldt = jnp.broadcast_to(log_dt[:, None], (g, p)).reshape(1, ns)
    return (a_re.reshape(1, ns), a_im.reshape(1, ns), ldt, bre_c, bim_c, cre_c.astype(BF16),
            cim_c.astype(BF16), d_skip.reshape(1, nu), w_glu.astype(BF16), b_glu.reshape(1, nu))


def _s5(u_tm, operands, x0r, x0i):
    l, b, nu = u_tm.shape
    g, p = x0r.shape[1:]
    ns = g * p
    assert b % SUBLANES == 0
    ct = _chunk_len(l)
    nct = l // ct
    rows = ct * SUBLANES
    xspec = pl.BlockSpec((SUBLANES, ns), lambda bb, j: (bb, 0))
    s, xr, xi = pl.pallas_call(
        functools.partial(_s5_kernel, ct=ct),
        grid=(b // SUBLANES, nct),
        in_specs=[pl.BlockSpec((ct, SUBLANES, nu), lambda bb, j: (j, bb, 0)),
                  _full((1, ns)), _full((1, ns)), _full((1, ns)),
                  _full((nu, ns // 2)), _full((nu, ns // 2)),
                  _full((ns, nu // 2)), _full((ns, nu // 2)),
                  _full((1, nu)), _full((nu, nu)), _full((1, nu)), xspec, xspec],
        out_specs=[pl.BlockSpec((ct, SUBLANES, nu), lambda bb, j: (j, bb, 0)), xspec, xspec],
        out_shape=[jax.ShapeDtypeStruct((l, b, nu), F32),
                   jax.ShapeDtypeStruct((b, ns), F32), jax.ShapeDtypeStruct((b, ns), F32)],
        scratch_shapes=[pltpu.VMEM((SUBLANES, ns), F32), pltpu.VMEM((SUBLANES, ns), F32),
                        pltpu.VMEM((nu, ns // 2), BF16), pltpu.VMEM((nu, ns // 2), BF16),
                        pltpu.VMEM((rows, ns), F32), pltpu.VMEM((rows, ns), F32)],
        compiler_params=_params(2, VMEM_LIMIT),
        name="s5",
    )(u_tm, *operands, x0r.reshape(b, ns), x0i.reshape(b, ns))
    return s, xr.reshape(b, g, p), xi.reshape(b, g, p)


def _trunk(groups, prm):
    d = groups[0]['x'].shape[-1]
    tm = 512
    info = []
    for grp in groups:
        b, l, _ = grp['x'].shape
        assert l >= CONV_W - 1 and (b * l) % tm == 0
        info.append(dict(b=b, l=l, t=b * l, direct_tm=b == SUBLANES and l % tm == 0))
    hs = [grp['x'].reshape(-1, d) for grp in groups]
    pps = [grp['p'].reshape(grp['p'].shape[0], -1, grp['p'].shape[-1]) for grp in groups]

    zs = _inproj(hs, prm['norm_mix'][0], prm['w_ab'], (8 * HW, LANES), tm, [None] * len(groups))
    ab_new, post_in = [], []
    for grp, inf, h, pp, (z_main, z_gate) in zip(groups, info, hs, pps, zs):
        b, l = inf['b'], inf['l']
        conv0, c0, n0, m0, ret0 = grp['ab_state']
        hm, conv_new, c_new, n_new, m_new = _mlstm(z_main, z_gate, prm['conv_w_ab'][0], prm['conv_b_ab'][0],
                                         prm['b_gate_ab'][0], prm['gn_a'][0], conv0[0], c0[0], n0[0],
                                         m0[0], b, l)
        cos_t, sin_t = _rope_tables(l, grp['pos0'])
        hr, ret_new = _retention(z_main, cos_t, sin_t, ret0[0], b, l)
        ab_new.append((conv_new[None], c_new[None], n_new[None], m_new[None], ret_new[None]))
        post_in.append((h, hm, hr, pp, None))
    hs = _post(post_in, prm['w_out_ab'], prm['norm_ff'][0], prm['w_ff1'], prm['w_ff2'],
               prm['norm_ple'][0], prm['w_ple_gate'], prm['w_ple_proj'], prm['norm_final'],
               layer=0, final=False, tm=tm)

    tml = [(inf['b'], inf['l']) if inf['direct_tm'] else None for inf in info]
    zs = _inproj(hs, prm['norm_mix'][1], prm['w_cd'], (4 * HW, HW), tm, tml)
    cd_new, post_in = [], []
    for grp, inf, h, pp, (z_cd, su) in zip(groups, info, hs, pps, zs):
        b, l, t = inf['b'], inf['l'], inf['t']
        hg0, x0r, x0i = grp['cd_state']
        o, hg_new = _hgrn(z_cd, prm['lb_logits'], prm['gn_c'][0], hg0[0], b, l, layer=1)
        u_tm = (su.reshape(l, b, HW) if inf['direct_tm']
                else jnp.transpose(su.reshape(b, l, HW), (1, 0, 2)))
        s_tm, xr, xi = _s5(u_tm, prm['s5_operands'], x0r[0], x0i[0])
        s_in = (s_tm.reshape(l, b * HW) if inf['direct_tm']
                else jnp.transpose(s_tm, (1, 0, 2)).reshape(t, HW))
        cd_new.append((hg_new[None], xr[None], xi[None]))
        post_in.append((h, o, s_in, pp, (b, l) if inf['direct_tm'] else None))
    ys = _post(post_in, prm['w_out_cd'], prm['norm_ff'][1], prm['w_ff1'], prm['w_ff2'],
               prm['norm_ple'][1], prm['w_ple_gate'], prm['w_ple_proj'], prm['norm_final'],
               layer=1, final=True, tm=tm)
    ys = [y.reshape(grp['x'].shape) for y, grp in zip(ys, groups)]
    return ys, ab_new, cd_new


def kernel(x_prompt, x_sample, state_mlstm_conv, state_mlstm_C, state_mlstm_n, state_mlstm_m, state_ret, state_hgrn, state_s5_re, state_s5_im, p_prompt, p_sample, norm_mix, norm_ff, norm_ple, norm_final, w_in_ab, b_gate_ab, conv_w_ab, conv_b_ab, gn_a, w_out_ab, w_in_cd, lb_logits, gn_c, s5_A_re, s5_A_im, s5_log_dt, s5_B_re, s5_B_im, s5_C_re, s5_C_im, s5_D, w_glu, b_glu, w_out_cd, w_ff1, w_ff2, w_ple_proj, w_ple_gate):
    assert norm_mix.shape[0] == 2, "two layers: (mLSTM || retention), (HGRN2 || S5)"
    w_ab = w_in_ab[0]
    gate0 = 4 * HW
    w_ab = jnp.concatenate([w_ab[:, :gate0].astype(BF16), w_ab[:, gate0 + 2 * HEADS:].astype(BF16),
                            w_ab[:, gate0:gate0 + 2 * HEADS].astype(BF16),
                            jnp.zeros((w_ab.shape[0], LANES - 2 * HEADS), BF16)], axis=1)
    prm = dict(norm_mix=norm_mix, norm_ff=norm_ff, norm_ple=norm_ple, norm_final=norm_final,
               w_ab=w_ab, b_gate_ab=b_gate_ab, conv_w_ab=conv_w_ab, conv_b_ab=conv_b_ab,
               gn_a=gn_a, w_out_ab=w_out_ab[0].astype(BF16), w_cd=w_in_cd[0].astype(BF16),
               lb_logits=lb_logits, gn_c=gn_c,
               s5_operands=_s5_operands(s5_A_re[0], s5_A_im[0], s5_log_dt[0], s5_B_re[0], s5_B_im[0],
                                        s5_C_re[0], s5_C_im[0], s5_D[0], w_glu[0], b_glu[0]),
               w_out_cd=w_out_cd[0].astype(BF16), w_ff1=w_ff1.astype(BF16), w_ff2=w_ff2.astype(BF16),
               w_ple_proj=w_ple_proj.astype(BF16), w_ple_gate=w_ple_gate.astype(BF16))

    bp, lp, _ = x_prompt.shape
    z = lambda *s: jnp.zeros(s, F32)
    zero_ab = (z(1, bp, CONV_W - 1, 2 * HW), z(1, bp, HEADS, DH, DH), z(1, bp, HEADS, DH),
               z(1, bp, HEADS), z(1, bp, HEADS, DH, DH))
    zero_cd = (z(1, bp, HEADS, DH, DH),) + (z(*((1, bp) + s5_A_re.shape[1:])),) * 2
    groups = [dict(x=x_prompt, p=p_prompt, pos0=0, ab_state=zero_ab, cd_state=zero_cd),
              dict(x=x_sample, p=p_sample, pos0=PAST_LEN,
                   ab_state=(state_mlstm_conv, state_mlstm_C, state_mlstm_n, state_mlstm_m, state_ret),
                   cd_state=(state_hgrn, state_s5_re, state_s5_im))]
    (y_p, y_s), (ab_p, ab_s), (cd_p, cd_s) = _trunk(groups, prm)
    return (y_p, y_s,
            ab_p[0], ab_s[0], ab_p[1], ab_s[1], ab_p[2], ab_s[2], ab_p[3], ab_s[3], ab_p[4], ab_s[4],
            cd_p[0], cd_s[0], cd_p[1], cd_s[1], cd_p[2], cd_s[2])
```

```python
import functools
import math

import jax
import jax.numpy as jnp
from jax import lax
from jax.experimental import pallas as pl
from jax.experimental.pallas import tpu as pltpu

F32 = jnp.float32
BF16 = jnp.bfloat16

EPS = 1e-6
NEG = -1e30
LOG2_E = math.log2(math.e)
ROPE_BASE = 10000.0
PAST_LEN = 16384
CHUNK = 64
HEADS = 4
DH = 128
HW = HEADS * DH
CONV_W = 4
S5_GROUP = 16
S5_STATE = 64
SUBLANES = 8
LANES = 128
MXU_DIM = 256
VMEM_LIMIT = 56 * 1024 * 1024


def _params(n_axes, vmem=None):
    return pltpu.CompilerParams(dimension_semantics=("arbitrary",) * n_axes, vmem_limit_bytes=vmem)


def _full(shape):
    return pl.BlockSpec(shape, lambda *_: (0,) * len(shape))


def _bdot(a, b):
    return jnp.dot(a.astype(BF16), b.astype(BF16), preferred_element_type=F32)


def _bdot_nt(a, b):
    return lax.dot_general(a.astype(BF16), b.astype(BF16), (((1,), (1,)), ((), ())),
                           preferred_element_type=F32)


def _bdot_tn(a, b):
    return lax.dot_general(a.astype(BF16), b.astype(BF16), (((0,), (0,)), ((), ())),
                           preferred_element_type=F32)


def _sigmoid(x):
    return 1.0 / (1.0 + jnp.exp(-x))


def _log_sigmoid(x):
    return jnp.minimum(x, 0.0) - jnp.log(1.0 + jnp.exp(-jnp.abs(x)))


def _rms(x, g):
    return x * lax.rsqrt(jnp.mean(x * x, axis=-1, keepdims=True) + EPS) * g


def _head_rms(x):
    return x * lax.rsqrt(jnp.mean(x * x, axis=-1, keepdims=True) + EPS)


def _chunk_len(length):
    return CHUNK if length % CHUNK == 0 else length


def _round_robin(gens):
    gens = list(gens)
    while gens:
        alive = []
        for g in gens:
            try:
                next(g)
                alive.append(g)
            except StopIteration:
                pass
        gens = alive


def _step_ranges(counts):
    starts = [0]
    for n in counts:
        starts.append(starts[-1] + n)
    return starts


def _local(i, start, count):
    return jnp.clip(i - start, 0, count - 1)


def _inproj_kernel(*refs, n_groups, n_out, starts):
    x_refs = refs[:n_groups]
    g_ref, w_ref = refs[n_groups:n_groups + 2]
    out_refs = refs[n_groups + 2:]
    i = pl.program_id(0)

    def run(x_ref, outs):
        hn = _rms(x_ref[...], g_ref[...]).astype(BF16)
        off = 0
        for o_ref in outs:
            n = o_ref.shape[-1]
            for n0 in range(0, n, HW):
                nn = min(HW, n - n0)
                o_ref[:, n0:n0 + nn] = jnp.dot(hn, w_ref[:, off + n0:off + n0 + nn],
                                               preferred_element_type=F32)
            off += n

    for gi in range(n_groups):
        @pl.when((i >= starts[gi]) & (i < starts[gi + 1]))
        def _(gi=gi):
            run(x_refs[gi], out_refs[gi * n_out:(gi + 1) * n_out])


def _inproj(hs, g, w, widths, tm, time_major_last):
    d = hs[0].shape[1]
    n = w.shape[1]
    assert sum(widths) == n
    counts = [h.shape[0] // tm for h in hs]
    starts = _step_ranges(counts)
    in_specs, out_specs, out_shape = [], [], []
    for k, h in enumerate(hs):
        t = h.shape[0]
        assert t % tm == 0
        loc = functools.partial(_local, start=starts[k], count=counts[k])
        in_specs.append(pl.BlockSpec((tm, d), lambda i, loc=loc: (loc(i), 0)))
        for wi, wd in enumerate(widths):
            if wi == len(widths) - 1 and time_major_last[k] is not None:
                b, l = time_major_last[k]
                assert l % tm == 0
                nl = l // tm
                out_shape.append(jax.ShapeDtypeStruct((l, b * wd), F32))
                out_specs.append(pl.BlockSpec((tm, wd), lambda i, loc=loc, nl=nl: (loc(i) % nl, loc(i) // nl)))
            else:
                out_shape.append(jax.ShapeDtypeStruct((t, wd), F32))
                out_specs.append(pl.BlockSpec((tm, wd), lambda i, loc=loc: (loc(i), 0)))
    outs = pl.pallas_call(
        functools.partial(_inproj_kernel, n_groups=len(hs), n_out=len(widths), starts=tuple(starts)),
        grid=(starts[-1],),
        in_specs=in_specs + [_full((1, d)), _full((d, n))],
        out_specs=out_specs,
        out_shape=out_shape,
        compiler_params=_params(1, VMEM_LIMIT),
        name="inproj",
    )(*hs, g.reshape(1, d), w)
    nw = len(widths)
    return [outs[k * nw:(k + 1) * nw] for k in range(len(hs))]


def _post_kernel(*refs, n_groups, starts, final, ff_chunk):
    grp_in = [refs[4 * k:4 * k + 4] for k in range(n_groups)]
    (wo_ref, gff_ref, w1_ref, w2_ref, gple_ref, wg_ref, wp_ref,
     gfin_ref) = refs[4 * n_groups:4 * n_groups + 8]
    out_refs = refs[4 * n_groups + 8:]
    i = pl.program_id(0)

    def run(h_ref, ma_ref, mb_ref, p_ref, o_ref):
        half = ma_ref.shape[-1]
        h = h_ref[...]
        h = h + (jnp.dot(ma_ref[...].astype(BF16), wo_ref[0:half, :], preferred_element_type=F32)
                 + jnp.dot(mb_ref[...].astype(BF16), wo_ref[half:2 * half, :],
                           preferred_element_type=F32))
        hn = _rms(h, gff_ref[...]).astype(BF16)
        d_ff = w1_ref.shape[1]
        acc = jnp.zeros_like(h)
        for f0 in range(0, d_ff, ff_chunk):
            a = jnp.dot(hn, w1_ref[:, f0:f0 + ff_chunk], preferred_element_type=F32)
            a = jnp.square(jnp.maximum(a, 0.0))
            acc = acc + jnp.dot(a.astype(BF16), w2_ref[f0:f0 + ff_chunk, :], preferred_element_type=F32)
        h = h + acc
        gate = _sigmoid(jnp.dot(_rms(h, gple_ref[...]).astype(BF16), wg_ref[...],
                                preferred_element_type=F32))
        h = h + gate * jnp.dot(p_ref[...].astype(BF16), wp_ref[...], preferred_element_type=F32)
        o_ref[...] = _rms(h, gfin_ref[...]) if final else h

    for k in range(n_groups):
        @pl.when((i >= starts[k]) & (i < starts[k + 1]))
        def _(k=k):
            run(*grp_in[k], out_refs[k])


def _post(groups, wo, gff, w1, w2, gple, wg, wp, gfin, *, layer, final, tm):
    d = groups[0][0].shape[1]
    d_ff = w1.shape[-1]
    counts = [grp[0].shape[0] // tm for grp in groups]
    starts = _step_ranges(counts)
    in_specs, out_specs, out_shape, operands = [], [], [], []
    for k, (h, mix_a, mix_b, p, b_time_major) in enumerate(groups):
        t = h.shape[0]
        half = mix_a.shape[-1]
        pd = p.shape[-1]
        assert t % tm == 0
        loc = functools.partial(_local, start=starts[k], count=counts[k])
        row = lambda i, loc=loc: (loc(i), 0)
        mb_spec = pl.BlockSpec((tm, half), row)
        if b_time_major is not None:
            _, l = b_time_major
            assert l % tm == 0
            nl = l // tm
            mb_spec = pl.BlockSpec((tm, half), lambda i, loc=loc, nl=nl: (loc(i) % nl, loc(i) // nl))
        in_specs += [pl.BlockSpec((tm, d), row), pl.BlockSpec((tm, half), row), mb_spec,
                     pl.BlockSpec((None, tm, pd), lambda i, loc=loc: (layer, loc(i), 0))]
        operands += [h, mix_a, mix_b, p]
        out_specs.append(pl.BlockSpec((tm, d), row))
        out_shape.append(jax.ShapeDtypeStruct((t, d), F32))
    pd = groups[0][3].shape[-1]
    lw = lambda r, cdim: pl.BlockSpec((None, r, cdim), lambda i: (layer, 0, 0))
    return pl.pallas_call(
        functools.partial(_post_kernel, n_groups=len(groups), starts=tuple(starts), final=final,
                          ff_chunk=1024),
        grid=(starts[-1],),
        in_specs=in_specs + [_full((d, d)), _full((1, d)), lw(d, d_ff), lw(d_ff, d), _full((1, d)),
                             lw(d, d), lw(pd, d), _full((1, d))],
        out_specs=out_specs,
        out_shape=out_shape,
        compiler_params=_params(1, VMEM_LIMIT),
        name="post",
    )(*operands, wo, gff.reshape(1, d), w1, w2, gple.reshape(1, d), wg, wp, gfin.reshape(1, d))


def _mlstm_kernel(zq_ref, zk_ref, zv_ref, zo_ref, zg_ref, cw_ref, cb_ref, bg_ref, gn_ref,
                  conv0_ref, c0_ref, n0_ref, m0_ref,
                  out_ref, conv_ref, c_ref, n_ref, m_ref, ext_ref, u_ref, qk_ref, gates_ref,
                  *, c, group):
    j = pl.program_id(1)
    tail = CONV_W - 1

    @pl.when(j == 0)
    def _():
        c_ref[...] = c0_ref[...]
        n_ref[...] = n0_ref[...]
        m_ref[...] = m0_ref[...]
        for gi in range(group):
            ext_ref[gi, 0:SUBLANES, :] = jnp.zeros((SUBLANES, 2 * HW), F32)
            ext_ref[gi, SUBLANES - tail:SUBLANES, :] = conv0_ref[gi]

    ri = lax.broadcasted_iota(jnp.int32, (c, c), 0)
    ci = lax.broadcasted_iota(jnp.int32, (c, c), 1)
    eye = ri == ci
    tril = ri >= ci
    lane = lax.broadcasted_iota(jnp.int32, (c, LANES), 1)
    bg = bg_ref[...]

    def lsum(x):
        return jnp.broadcast_to(jnp.sum(x, axis=1, keepdims=True), (c, DH))

    def unit(gi, h):
        sl = slice(h * DH, (h + 1) * DH)
        gates = gates_ref[gi]
        i_col = lsum(jnp.where(lane == h, gates, 0.0))
        f_col = lsum(jnp.where(lane == HEADS + h, gates, 0.0))
        yield
        b_row = jnp.sum(jnp.where(ri <= ci, f_col[:, :c], 0.0), axis=0, keepdims=True)
        b_col = lsum(jnp.where(eye, b_row, 0.0))
        i_row = jnp.sum(jnp.where(eye, i_col[:, :c], 0.0), axis=0, keepdims=True)
        yield
        m_prev = m_ref[gi, h:h + 1, :]
        dmat = jnp.where(tril, b_col[:, :c] - b_row + i_row, NEG)
        inter = b_col + m_prev
        row_max = jnp.broadcast_to(jnp.max(dmat, axis=1, keepdims=True), (c, DH))
        qh = qk_ref[gi, :, sl]
        kh = qk_ref[gi, :, HW + h * DH:HW + (h + 1) * DH] * (DH ** -0.5)
        vh = zv_ref[gi, :, sl]
        c_h = c_ref[gi, h]
        n_h = n_ref[gi, h:h + 1, :]
        s_raw = _bdot_nt(qh, kh)
        q_c = _bdot(qh, c_h)
        q_n = lsum(qh * n_h)
        yield
        m_t = jnp.maximum(inter, row_max)
        w_intra = jnp.exp(dmat - m_t[:, :c])
        w_inter = jnp.exp(inter - m_t)
        s = s_raw * w_intra
        s_v = _bdot(s, vh)
        s_sum = lsum(s)
        m_new = m_t[c - 1:c, :]
        b_last = b_col[c - 1:c, :]
        w_last = jnp.exp(b_last - b_col + i_col - m_new)
        decay = jnp.exp(b_last + m_prev - m_new)
        kw = w_last * kh
        kw_v = _bdot_tn(kw, vh)
        yield
        num = w_inter * q_c + s_v
        den = w_inter * q_n + s_sum
        hh = num / jnp.maximum(jnp.abs(den), jnp.exp(-m_t))
        c_ref[gi, h] = decay * c_h + kw_v
        n_ref[gi, h:h + 1, :] = decay * n_h + jnp.sum(kw, axis=0, keepdims=True)
        m_ref[gi, h:h + 1, :] = m_new
        hh = _sigmoid(zo_ref[gi, :, sl]) * hh
        out_ref[gi, :, sl] = _head_rms(hh) * gn_ref[:, sl]

    for gi in range(group):
        ext_ref[gi, SUBLANES:SUBLANES + c, 0:HW] = zq_ref[gi]
        ext_ref[gi, SUBLANES:SUBLANES + c, HW:2 * HW] = zk_ref[gi]
        assert CONV_W == 4
        x0 = ext_ref[gi, SUBLANES:SUBLANES + c, :]
        x2 = ext_ref[gi, SUBLANES - 2:SUBLANES - 2 + c, :]
        u_ref[gi, SUBLANES:SUBLANES + c, :] = cw_ref[2:3, :] * x0 + cw_ref[0:1, :] * x2
        u_ref[gi, SUBLANES - 1:SUBLANES, :] = (cw_ref[2:3, :] * ext_ref[gi, SUBLANES - 1:SUBLANES, :]
                                               + cw_ref[0:1, :] * ext_ref[gi, SUBLANES - 3:SUBLANES - 2, :])
        conv = (cb_ref[...] + cw_ref[3:4, :] * x0 + cw_ref[1:2, :] * x2
                + u_ref[gi, SUBLANES - 1:SUBLANES - 1 + c, :])
        ext_ref[gi, 0:SUBLANES, :] = ext_ref[gi, c:c + SUBLANES, :]
        conv_ref[gi] = ext_ref[gi, SUBLANES - tail:SUBLANES, :]
        qk_ref[gi] = conv * _sigmoid(conv)
        gb = zg_ref[gi] + bg
        gates_ref[gi] = jnp.where(lane < HEADS, gb, _log_sigmoid(gb))
    _round_robin([unit(gi, h) for gi in range(group) for h in range(HEADS)])


def _seq_group(b, c):
    rows = 512
    group = max(1, min(b, rows // c, 2 * SUBLANES))
    assert b % group == 0
    return group


def _mlstm(z_main, z_gate, conv_w, conv_b, b_gate, gn_a, conv0, c0, n0, m0, b, l):
    c = _chunk_len(l)
    nc = l // c
    grp = _seq_group(b, c)
    z3 = z_main.reshape(b, l, z_main.shape[-1])
    zspec = lambda col: pl.BlockSpec((grp, c, HW), lambda bi, j: (bi, j, col))
    st = lambda shape: pl.BlockSpec((grp,) + shape, lambda bi, j: (bi,) + (0,) * len(shape))
    m0b = jnp.broadcast_to(m0[:, :, None], (b, HEADS, DH))
    out, conv_new, c_new, n_new, m_new = pl.pallas_call(
        functools.partial(_mlstm_kernel, c=c, group=grp),
        grid=(b // grp, nc),
        in_specs=[zspec(0), zspec(1), zspec(2), zspec(3),
                  pl.BlockSpec((grp, c, LANES), lambda bi, j: (bi, j, 0)),
                  _full((CONV_W, 2 * HW)), _full((1, 2 * HW)), _full((1, LANES)), _full((1, HW)),
                  st((CONV_W - 1, 2 * HW)), st((HEADS, DH, DH)), st((HEADS, DH)), st((HEADS, DH))],
        out_specs=[pl.BlockSpec((grp, c, HW), lambda bi, j: (bi, j, 0)),
                   st((CONV_W - 1, 2 * HW)), st((HEADS, DH, DH)), st((HEADS, DH)), st((HEADS, DH))],
        out_shape=[jax.ShapeDtypeStruct((b, l, HW), F32),
                   jax.ShapeDtypeStruct((b, CONV_W - 1, 2 * HW), F32),
                   jax.ShapeDtypeStruct((b, HEADS, DH, DH), F32),
                   jax.ShapeDtypeStruct((b, HEADS, DH), F32),
                   jax.ShapeDtypeStruct((b, HEADS, DH), F32)],
        scratch_shapes=[pltpu.VMEM((grp, c + SUBLANES, 2 * HW), F32),
                        pltpu.VMEM((grp, c + SUBLANES, 2 * HW), F32),
                        pltpu.VMEM((grp, c, 2 * HW), F32), pltpu.VMEM((grp, c, LANES), F32)],
        compiler_params=_params(2),
        name="mlstm",
    )(z3, z3, z3, z3, z_gate.reshape(b, l, LANES), conv_w, conv_b.reshape(1, -1),
      jnp.pad(b_gate, (0, LANES - 2 * HEADS)).reshape(1, LANES), gn_a.reshape(1, -1),
      conv0, c0, n0, m0b)
    return out.reshape(b * l, HW), conv_new, c_new, n_new, m_new[:, :, 0]


def _rope_table_kernel(inv_ref, sign_ref, cos_ref, sin_ref, *, pos0, rows):
    i = pl.program_id(0)
    pos = (pos0 + i * rows + lax.broadcasted_iota(jnp.int32, (rows, LANES), 0)).astype(F32)
    ang = pos * inv_ref[...]
    cos_ref[...] = jnp.cos(ang)
    sin_ref[...] = jnp.sin(ang) * sign_ref[...]


def _rope_tables(l, pos0):
    half = DH // 2
    inv = ROPE_BASE ** (-jnp.arange(half, dtype=F32) / half)
    inv2 = jnp.concatenate([inv, inv]).reshape(1, DH)
    sign = jnp.concatenate([-jnp.ones((half,), F32), jnp.ones((half,), F32)]).reshape(1, DH)
    rows = min(l, 512)
    assert l % rows == 0
    return pl.pallas_call(
        functools.partial(_rope_table_kernel, pos0=pos0, rows=rows),
        grid=(l // rows,),
        in_specs=[_full((1, DH)), _full((1, DH))],
        out_specs=[pl.BlockSpec((rows, DH), lambda i: (i, 0))] * 2,
        out_shape=[jax.ShapeDtypeStruct((l, DH), F32)] * 2,
        compiler_params=_params(1),
        name="rope_table",
    )(inv2, sign)


def _ret_kernel(zq_ref, zk_ref, zv_ref, zg_ref, cos_ref, sin_ref, s0_ref, out_ref, s_ref,
                *, c, group):
    j = pl.program_id(1)

    @pl.when(j == 0)
    def _():
        s_ref[...] = s0_ref[...]

    cosf = cos_ref[...]
    sinf = sin_ref[...]
    ti = lax.broadcasted_iota(jnp.int32, (c, c), 0)
    si = lax.broadcasted_iota(jnp.int32, (c, c), 1)
    rel = jnp.maximum(ti - si, 0).astype(F32)
    tcol = lax.broadcasted_iota(jnp.int32, (c, 1), 0).astype(F32)

    def rope(x):
        return x * cosf + pltpu.roll(x, DH // 2, axis=1) * sinf

    def unit(gi, h, decay, inter, kdecay, cdecay):
        sl = slice(h * DH, (h + 1) * DH)
        qr = rope(zq_ref[gi, :, sl])
        kr = rope(zk_ref[gi, :, sl]) * (DH ** -0.5)
        yield
        vh = zv_ref[gi, :, sl]
        s_h = s_ref[gi, h]
        qk = _bdot_nt(qr, kr)
        q_s = _bdot(qr, s_h)
        k_v = _bdot_tn(kr * kdecay, vh)
        yield
        o = q_s * inter + _bdot(qk * decay, vh)
        s_ref[gi, h] = cdecay * s_h + k_v
        yield
        gate = zg_ref[gi, :, sl]
        out_ref[gi, :, sl] = _head_rms(o) * (gate * _sigmoid(gate))

    units = []
    for h in range(HEADS):
        lg = math.log1p(-(2.0 ** (-5.0 - h)))
        decay = jnp.where(ti >= si, jnp.exp(rel * lg), 0.0)
        inter = jnp.exp((tcol + 1.0) * lg)
        kdecay = jnp.exp((c - 1.0 - tcol) * lg)
        cdecay = math.exp(c * lg)
        units += [unit(gi, h, decay, inter, kdecay, cdecay) for gi in range(group)]
    _round_robin(units)


def _retention(z_main, cos_t, sin_t, s0, b, l):
    c = _chunk_len(l)
    nc = l // c
    grp = _seq_group(b, c)
    z3 = z_main.reshape(b, l, z_main.shape[-1])
    zspec = lambda col: pl.BlockSpec((grp, c, HW), lambda bi, j: (bi, j, col))
    st = pl.BlockSpec((grp, HEADS, DH, DH), lambda bi, j: (bi, 0, 0, 0))
    tab = pl.BlockSpec((c, DH), lambda bi, j: (j, 0))
    out, s_new = pl.pallas_call(
        functools.partial(_ret_kernel, c=c, group=grp),
        grid=(b // grp, nc),
        in_specs=[zspec(4), zspec(5), zspec(6), zspec(7), tab, tab, st],
        out_specs=[pl.BlockSpec((grp, c, HW), lambda bi, j: (bi, j, 0)), st],
        out_shape=[jax.ShapeDtypeStruct((b, l, HW), F32),
                   jax.ShapeDtypeStruct((b, HEADS, DH, DH), F32)],
        compiler_params=_params(2),
        name="retention",
    )(z3, z3, z3, z3, cos_t, sin_t, s0)
    return out.reshape(b * l, HW), s_new


def _hgrn_kernel(zq_ref, zf_ref, zi_ref, zg_ref, lbl_ref, gn_ref, s0_ref, out_ref, s_ref,
                 kk_ref, bcum_ref, *, c, sc, layer, group):
    j = pl.program_id(1)

    @pl.when(j == 0)
    def _():
        s_ref[...] = s0_ref[...]

    lbl = lbl_ref[...]
    e = jnp.exp(lbl - jnp.max(lbl, axis=0, keepdims=True))
    sm = e / jnp.sum(e, axis=0, keepdims=True)
    cum = sm[0:1, :]
    for r in range(1, layer + 1):
        cum = cum + sm[r:r + 1, :]
    lb = cum - sm[0:1, :]

    oml = 1.0 - lb
    ri = lax.broadcasted_iota(jnp.int32, (c, c), 0)
    ci = lax.broadcasted_iota(jnp.int32, (c, c), 1)
    tril = jnp.where(ri >= ci, 1.0, 0.0).astype(BF16)
    row_s = lax.broadcasted_iota(jnp.int32, (sc, sc), 0)
    lane_s = lax.broadcasted_iota(jnp.int32, (sc, sc), 1)
    causal_col = jnp.where(row_s >= lane_s, lane_s, -1)
    keep = [causal_col == s for s in range(sc)]
    e_r = lax.broadcasted_iota(jnp.int32, (DH, DH), 0)
    e_c = lax.broadcasted_iota(jnp.int32, (DH, DH), 1)
    eye = e_r == e_c

    def unit(gi, h):
        sl = slice(h * DH, (h + 1) * DH)
        bh = bcum_ref[gi, :, sl]
        qh = zq_ref[gi, :, sl] * (DH ** -0.5)
        kh = kk_ref[gi, :, sl]
        vh = zi_ref[gi, :, sl]
        s_h = s_ref[gi, h]
        b_last = bh[c - 1:c, :]
        o_inter = _bdot(qh * jnp.exp(bh), s_h)
        k_v = _bdot_tn(kh * jnp.exp(b_last - bh), vh)
        dec_col = jnp.sum(jnp.where(eye, jnp.exp(b_last), 0.0), axis=1, keepdims=True)
        yield
        s_ref[gi, h] = dec_col * s_h + k_v
        blocks = []
        for blk in range(c // sc):
            r0 = blk * sc
            b_i = bh[r0:r0 + sc]
            q_i = qh[r0:r0 + sc]
            k_i = kh[r0:r0 + sc]
            v_i = vh[r0:r0 + sc]
            att_prev = None
            if blk > 0:
                ref_row = bh[r0 - 1:r0, :]
                a_i = q_i * jnp.exp(b_i - ref_row)
                k_prev = kh[0:r0] * jnp.exp(ref_row - bh[0:r0])
                att_prev = _bdot_nt(a_i, k_prev)
            b2_i = b_i * LOG2_E
            c2_i = b2_i - jnp.log2(k_i)
            cols = [jnp.sum(q_i * jnp.exp2(b2_i - c2_i[s:s + 1, :]), axis=1, keepdims=True)
                    for s in range(sc)]
            yield
            att = jnp.zeros((sc, sc), F32)
            for s in range(sc):
                att = jnp.where(keep[s], cols[s], att)
            o_i = _bdot(att, v_i)
            if att_prev is not None:
                o_i = o_i + _bdot(att_prev, vh[0:r0])
            blocks.append(o_i)
        yield
        o = o_inter + (jnp.concatenate(blocks, axis=0) if len(blocks) > 1 else blocks[0])
        gate = zg_ref[gi, :, sl]
        out_ref[gi, :, sl] = _head_rms(o) * gn_ref[:, sl] * (gate * _sigmoid(gate))

    for gi in range(group):
        zf = zf_ref[gi]
        ez = jnp.exp(-jnp.abs(zf))
        big = 1.0 / (1.0 + ez)
        small = ez * big
        pos = zf >= 0.0
        logf = jnp.log(lb + oml * jnp.where(pos, big, small))
        kk_ref[gi] = oml * jnp.where(pos, small, big)

        p0 = logf.astype(BF16)
        r1 = logf - p0.astype(F32)
        p1 = r1.astype(BF16)
        p2 = (r1 - p1.astype(F32)).astype(BF16)
        bcum_ref[gi] = (jnp.dot(tril, p0, preferred_element_type=F32)
                        + jnp.dot(tril, p1, preferred_element_type=F32)
                        + jnp.dot(tril, p2, preferred_element_type=F32))
    _round_robin([unit(gi, h) for gi in range(group) for h in range(HEADS)])


def _hgrn(z_cd, lb_logits, gn_c, s0, b, l, layer):
    c = _chunk_len(l)
    sc = min(c, SUBLANES)
    nc = l // c
    grp = _seq_group(b, c)
    depth = lb_logits.shape[0]
    z3 = z_cd.reshape(b, l, z_cd.shape[-1])
    zspec = lambda col: pl.BlockSpec((grp, c, HW), lambda bi, j: (bi, j, col))
    st = pl.BlockSpec((grp, HEADS, DH, DH), lambda bi, j: (bi, 0, 0, 0))
    out, s_new = pl.pallas_call(
        functools.partial(_hgrn_kernel, c=c, sc=sc, layer=layer, group=grp),
        grid=(b // grp, nc),
        in_specs=[zspec(0), zspec(1), zspec(2), zspec(3), _full((depth, HW)), _full((1, HW)), st],
        out_specs=[pl.BlockSpec((grp, c, HW), lambda bi, j: (bi, j, 0)), st],
        out_shape=[jax.ShapeDtypeStruct((b, l, HW), F32),
                   jax.ShapeDtypeStruct((b, HEADS, DH, DH), F32)],
        scratch_shapes=[pltpu.VMEM((grp, c, HW), F32), pltpu.VMEM((grp, c, HW), F32)],
        compiler_params=_params(2),
        name="hgrn2",
    )(z3, z3, z3, z3, lb_logits, gn_c.reshape(1, -1), s0)
    return out.reshape(b * l, HW), s_new


def _s5_kernel(u_ref, are_ref, aim_ref, ldt_ref, bre_ref, bim_ref, cre_ref, cim_ref, d_ref,
               wglu_ref, bglu_ref, x0r_ref, x0i_ref,
               s_ref, xr_ref, xi_ref, ar_sc, ai_sc, bbr_sc, bbi_sc, bur_sc, bui_sc, *, ct):
    j = pl.program_id(1)
    ns = are_ref.shape[-1]
    nu = u_ref.shape[-1]
    hs = ns // 2
    hu = nu // 2
    rows = ct * SUBLANES

    @pl.when(j == 0)
    def _():
        a_re = are_ref[...]
        a_im = aim_ref[...]
        dt = jnp.exp(ldt_ref[...])
        mag = jnp.exp(dt * a_re)
        ar = mag * jnp.cos(dt * a_im)
        ai = mag * jnp.sin(dt * a_im)
        ar_sc[...] = jnp.broadcast_to(ar, (SUBLANES, ns))
        ai_sc[...] = jnp.broadcast_to(ai, (SUBLANES, ns))
        den = a_re * a_re + a_im * a_im
        nr = ar - 1.0
        zr = (nr * a_re + ai * a_im) / den
        zi = (ai * a_re - nr * a_im) / den
        for hg in range(2):
            us = slice(hg * hu, (hg + 1) * hu)
            ss = slice(hg * hs, (hg + 1) * hs)
            bbr_sc[us, :] = (zr[:, ss] * bre_ref[us, :] - zi[:, ss] * bim_ref[us, :]).astype(BF16)
            bbi_sc[us, :] = (zr[:, ss] * bim_ref[us, :] + zi[:, ss] * bre_ref[us, :]).astype(BF16)
        xr_ref[...] = x0r_ref[...]
        xi_ref[...] = x0i_ref[...]

    u = u_ref[...].reshape(rows, nu)
    ub = u.astype(BF16)
    for hg in range(2):
        us = slice(hg * hu, (hg + 1) * hu)
        ss = slice(hg * hs, (hg + 1) * hs)
        bur_sc[:, ss] = jnp.dot(ub[:, us], bbr_sc[us, :], preferred_element_type=F32)
        bui_sc[:, ss] = jnp.dot(ub[:, us], bbi_sc[us, :], preferred_element_type=F32)

    lane_chunk = 8 * LANES
    for lc in range(ns // lane_chunk):
        ls = slice(lc * lane_chunk, (lc + 1) * lane_chunk)
        ar = ar_sc[:, ls]
        ai = ai_sc[:, ls]

        def step(t, carry):
            xr, xi = carry
            r0 = pl.multiple_of(t * SUBLANES, SUBLANES)
            nxr = ar * xr - ai * xi + bur_sc[pl.ds(r0, SUBLANES), ls]
            nxi = ar * xi + ai * xr + bui_sc[pl.ds(r0, SUBLANES), ls]
            bur_sc[pl.ds(r0, SUBLANES), ls] = nxr
            bui_sc[pl.ds(r0, SUBLANES), ls] = nxi
            return nxr, nxi

        xr, xi = lax.fori_loop(0, ct, step, (xr_ref[:, ls], xi_ref[:, ls]), unroll=SUBLANES)
        xr_ref[:, ls] = xr
        xi_ref[:, ls] = xi

    ys = []
    for hg in range(2):
        ss = slice(hg * hs, (hg + 1) * hs)
        ys.append(jnp.dot(bur_sc[:, ss].astype(BF16), cre_ref[ss, :], preferred_element_type=F32)
                  - jnp.dot(bui_sc[:, ss].astype(BF16), cim_ref[ss, :], preferred_element_type=F32))
    y = jnp.concatenate(ys, axis=1) + d_ref[...] * u
    a = 0.5 * y * (1.0 + jnp.tanh(math.sqrt(2.0 / math.pi) * (y + 0.044715 * (y * y * y))))
    s = a * _sigmoid(jnp.dot(a.astype(BF16), wglu_ref[...], preferred_element_type=F32) + bglu_ref[...])
    s_ref[...] = s.reshape(ct, SUBLANES, nu)


def _s5_block_diag(bmat, cmat):
    g, p, hgrp = bmat.shape
    gh = g // 2
    eye = jnp.eye(gh, dtype=F32)
    b4 = bmat.reshape(2, gh, p, hgrp)
    bc = jnp.einsum('agph,gk->aghkp', b4, eye).reshape(2 * gh * hgrp, gh * p)
    c4 = cmat.reshape(2, gh, hgrp, p)
    cc = jnp.einsum('aghp,gk->agpkh', c4, eye).reshape(2 * gh * p, gh * hgrp)
    return bc, cc


def _s5_operands(a_re, a_im, log_dt, b_re, b_im, c_re, c_im, d_skip, w_glu, b_glu):
    g, p = a_re.shape
    ns = g * p
    nu = d_skip.shape[-1]
    assert nu == g * S5_GROUP and (g // 2) * S5_GROUP == MXU_DIM
    bre_c, cre_c = _s5_block_diag(b_re, c_re)
    bim_c, cim_c = _s5_block_diag(b_im, c_im)
    ldt = jnp.broadcast_to(log_dt[:, None], (g, p)).reshape(1, ns)
    return (a_re.reshape(1, ns), a_im.reshape(1, ns), ldt, bre_c, bim_c, cre_c.astype(BF16),
            cim_c.astype(BF16), d_skip.reshape(1, nu), w_glu.astype(BF16), b_glu.reshape(1, nu))


def _s5(u_tm, operands, x0r, x0i):
    l, b, nu = u_tm.shape
    g, p = x0r.shape[1:]
    ns = g * p
    assert b % SUBLANES == 0
    ct = _chunk_len(l)
    nct = l // ct
    rows = ct * SUBLANES
    xspec = pl.BlockSpec((SUBLANES, ns), lambda bb, j: (bb, 0))
    s, xr, xi = pl.pallas_call(
        functools.partial(_s5_kernel, ct=ct),
        grid=(b // SUBLANES, nct),
        in_specs=[pl.BlockSpec((ct, SUBLANES, nu), lambda bb, j: (j, bb, 0)),
                  _full((1, ns)), _full((1, ns)), _full((1, ns)),
                  _full((nu, ns // 2)), _full((nu, ns // 2)),
                  _full((ns, nu // 2)), _full((ns, nu // 2)),
                  _full((1, nu)), _full((nu, nu)), _full((1, nu)), xspec, xspec],
        out_specs=[pl.BlockSpec((ct, SUBLANES, nu), lambda bb, j: (j, bb, 0)), xspec, xspec],
        out_shape=[jax.ShapeDtypeStruct((l, b, nu), F32),
                   jax.ShapeDtypeStruct((b, ns), F32), jax.ShapeDtypeStruct((b, ns), F32)],
        scratch_shapes=[pltpu.VMEM((SUBLANES, ns), F32), pltpu.VMEM((SUBLANES, ns), F32),
                        pltpu.VMEM((nu, ns // 2), BF16), pltpu.VMEM((nu, ns // 2), BF16),
                        pltpu.VMEM((rows, ns), F32), pltpu.VMEM((rows, ns), F32)],
        compiler_params=_params(2, VMEM_LIMIT),
        name="s5",
    )(u_tm, *operands, x0r.reshape(b, ns), x0i.reshape(b, ns))
    return s, xr.reshape(b, g, p), xi.reshape(b, g, p)


def _trunk(groups, prm):
    d = groups[0]['x'].shape[-1]
    tm = 512
    info = []
    for grp in groups:
        b, l, _ = grp['x'].shape
        assert l >= CONV_W - 1 and (b * l) % tm == 0
        info.append(dict(b=b, l=l, t=b * l, direct_tm=b == SUBLANES and l % tm == 0))
    hs = [grp['x'].reshape(-1, d) for grp in groups]
    pps = [grp['p'].reshape(grp['p'].shape[0], -1, grp['p'].shape[-1]) for grp in groups]

    zs = _inproj(hs, prm['norm_mix'][0], prm['w_ab'], (8 * HW, LANES), tm, [None] * len(groups))
    ab_new, post_in = [], []
    for grp, inf, h, pp, (z_main, z_gate) in zip(groups, info, hs, pps, zs):
        b, l = inf['b'], inf['l']
        conv0, c0, n0, m0, ret0 = grp['ab_state']
        hm, conv_new, c_new, n_new, m_new = _mlstm(z_main, z_gate, prm['conv_w_ab'][0], prm['conv_b_ab'][0],
                                         prm['b_gate_ab'][0], prm['gn_a'][0], conv0[0], c0[0], n0[0],
                                         m0[0], b, l)
        cos_t, sin_t = _rope_tables(l, grp['pos0'])
        hr, ret_new = _retention(z_main, cos_t, sin_t, ret0[0], b, l)
        ab_new.append((conv_new[None], c_new[None], n_new[None], m_new[None], ret_new[None]))
        post_in.append((h, hm, hr, pp, None))
    hs = _post(post_in, prm['w_out_ab'], prm['norm_ff'][0], prm['w_ff1'], prm['w_ff2'],
               prm['norm_ple'][0], prm['w_ple_gate'], prm['w_ple_proj'], prm['norm_final'],
               layer=0, final=False, tm=tm)

    tml = [(inf['b'], inf['l']) if inf['direct_tm'] else None for inf in info]
    zs = _inproj(hs, prm['norm_mix'][1], prm['w_cd'], (4 * HW, HW), tm, tml)
    cd_new, post_in = [], []
    for grp, inf, h, pp, (z_cd, su) in zip(groups, info, hs, pps, zs):
        b, l, t = inf['b'], inf['l'], inf['t']
        hg0, x0r, x0i = grp['cd_state']
        o, hg_new = _hgrn(z_cd, prm['lb_logits'], prm['gn_c'][0], hg0[0], b, l, layer=1)
        u_tm = (su.reshape(l, b, HW) if inf['direct_tm']
                else jnp.transpose(su.reshape(b, l, HW), (1, 0, 2)))
        s_tm, xr, xi = _s5(u_tm, prm['s5_operands'], x0r[0], x0i[0])
        s_in = (s_tm.reshape(l, b * HW) if inf['direct_tm']
                else jnp.transpose(s_tm, (1, 0, 2)).reshape(t, HW))
        cd_new.append((hg_new[None], xr[None], xi[None]))
        post_in.append((h, o, s_in, pp, (b, l) if inf['direct_tm'] else None))
    ys = _post(post_in, prm['w_out_cd'], prm['norm_ff'][1], prm['w_ff1'], prm['w_ff2'],
               prm['norm_ple'][1], prm['w_ple_gate'], prm['w_ple_proj'], prm['norm_final'],
               layer=1, final=True, tm=tm)
    ys = [y.reshape(grp['x'].shape) for y, grp in zip(ys, groups)]
    return ys, ab_new, cd_new


def kernel(x_prompt, x_sample, state_mlstm_conv, state_mlstm_C, state_mlstm_n, state_mlstm_m, state_ret, state_hgrn, state_s5_re, state_s5_im, p_prompt, p_sample, norm_mix, norm_ff, norm_ple, norm_final, w_in_ab, b_gate_ab, conv_w_ab, conv_b_ab, gn_a, w_out_ab, w_in_cd, lb_logits, gn_c, s5_A_re, s5_A_im, s5_log_dt, s5_B_re, s5_B_im, s5_C_re, s5_C_im, s5_D, w_glu, b_glu, w_out_cd, w_ff1, w_ff2, w_ple_proj, w_ple_gate):
    assert norm_mix.shape[0] == 2, "two layers: (mLSTM || retention), (HGRN2 || S5)"
    w_ab = w_in_ab[0]
    gate0 = 4 * HW
    w_ab = jnp.concatenate([w_ab[:, :gate0].astype(BF16), w_ab[:, gate0 + 2 * HEADS:].astype(BF16),
                            w_ab[:, gate0:gate0 + 2 * HEADS].astype(BF16),
                            jnp.zeros((w_ab.shape[0], LANES - 2 * HEADS), BF16)], axis=1)
    prm = dict(norm_mix=norm_mix, norm_ff=norm_ff, norm_ple=norm_ple, norm_final=norm_final,
               w_ab=w_ab, b_gate_ab=b_gate_ab, conv_w_ab=conv_w_ab, conv_b_ab=conv_b_ab,
               gn_a=gn_a, w_out_ab=w_out_ab[0].astype(BF16), w_cd=w_in_cd[0].astype(BF16),
               lb_logits=lb_logits, gn_c=gn_c,
               s5_operands=_s5_operands(s5_A_re[0], s5_A_im[0], s5_log_dt[0], s5_B_re[0], s5_B_im[0],
                                        s5_C_re[0], s5_C_im[0], s5_D[0], w_glu[0], b_glu[0]),
               w_out_cd=w_out_cd[0].astype(BF16), w_ff1=w_ff1.astype(BF16), w_ff2=w_ff2.astype(BF16),
               w_ple_proj=w_ple_proj.astype(BF16), w_ple_gate=w_ple_gate.astype(BF16))

    bp, lp, _ = x_prompt.shape
    z = lambda *s: jnp.zeros(s, F32)
    zero_ab = (z(1, bp, CONV_W - 1, 2 * HW), z(1, bp, HEADS, DH, DH), z(1, bp, HEADS, DH),
               z(1, bp, HEADS), z(1, bp, HEADS, DH, DH))
    zero_cd = (z(1, bp, HEADS, DH, DH),) + (z(*((1, bp) + s5_A_re.shape[1:])),) * 2
    groups = [dict(x=x_prompt, p=p_prompt, pos0=0, ab_state=zero_ab, cd_state=zero_cd),
              dict(x=x_sample, p=p_sample, pos0=PAST_LEN,
                   ab_state=(state_mlstm_conv, state_mlstm_C, state_mlstm_n, state_mlstm_m, state_ret),
                   cd_state=(state_hgrn, state_s5_re, state_s5_im))]
    (y_p, y_s), (ab_p, ab_s), (cd_p, cd_s) = _trunk(groups, prm)
    return (y_p, y_s,
            ab_p[0], ab_s[0], ab_p[1], ab_s[1], ab_p[2], ab_s[2], ab_p[3], ab_s[3], ab_p[4], ab_s[4],
            cd_p[0], cd_s[0], cd_p[1], cd_s[1], cd_p[2], cd_s[2])
```

```python
import functools
import math

import jax
import jax.numpy as jnp
from jax import lax
from jax.experimental import pallas as pl
from jax.experimental.pallas import tpu as pltpu

F32 = jnp.float32
BF16 = jnp.bfloat16

EPS = 1e-6
NEG = -1e30
LOG2_E = math.log2(math.e)
ROPE_BASE = 10000.0
PAST_LEN = 16384
CHUNK = 64
HEADS = 4
DH = 128
HW = HEADS * DH
CONV_W = 4
S5_GROUP = 16
S5_STATE = 64
SUBLANES = 8
LANES = 128
MXU_DIM = 256
VMEM_LIMIT = 56 * 1024 * 1024


def _params(n_axes, vmem=None):
    return pltpu.CompilerParams(dimension_semantics=("arbitrary",) * n_axes, vmem_limit_bytes=vmem)


def _full(shape):
    return pl.BlockSpec(shape, lambda *_: (0,) * len(shape))


def _bdot(a, b):
    return jnp.dot(a.astype(BF16), b.astype(BF16), preferred_element_type=F32)


def _bdot_nt(a, b):
    return lax.dot_general(a.astype(BF16), b.astype(BF16), (((1,), (1,)), ((), ())),
                           preferred_element_type=F32)


def _bdot_tn(a, b):
    return lax.dot_general(a.astype(BF16), b.astype(BF16), (((0,), (0,)), ((), ())),
                           preferred_element_type=F32)


def _sigmoid(x):
    return 1.0 / (1.0 + jnp.exp(-x))


def _log_sigmoid(x):
    return jnp.minimum(x, 0.0) - jnp.log(1.0 + jnp.exp(-jnp.abs(x)))


def _rms(x, g):
    return x * lax.rsqrt(jnp.mean(x * x, axis=-1, keepdims=True) + EPS) * g


def _head_rms(x):
    return x * lax.rsqrt(jnp.mean(x * x, axis=-1, keepdims=True) + EPS)


def _chunk_len(length):
    return CHUNK if length % CHUNK == 0 else length


def _round_robin(gens):
    gens = list(gens)
    while gens:
        alive = []
        for g in gens:
            try:
                next(g)
                alive.append(g)
            except StopIteration:
                pass
        gens = alive


def _step_ranges(counts):
    starts = [0]
    for n in counts:
        starts.append(starts[-1] + n)
    return starts


def _local(i, start, count):
    return jnp.clip(i - start, 0, count - 1)


def _inproj_kernel(*refs, n_groups, n_w, n_out, starts):
    x_refs = refs[:n_groups]
    g_ref = refs[n_groups]
    w_refs = refs[n_groups + 1:n_groups + 1 + n_w]
    out_refs = refs[n_groups + 1 + n_w:]
    i = pl.program_id(0)
    w_starts = _step_ranges([w.shape[1] for w in w_refs])

    def w_cols(c0, nn):
        for w_ref, s0, s1 in zip(w_refs, w_starts[:-1], w_starts[1:]):
            if s0 <= c0 and c0 + nn <= s1:
                return w_ref[:, c0 - s0:c0 - s0 + nn]
        raise ValueError("output column chunk straddles two weight parts")

    def run(x_ref, outs):
        hn = _rms(x_ref[...], g_ref[...]).astype(BF16)
        off = 0
        for o_ref in outs:
            n = o_ref.shape[-1]
            for n0 in range(0, n, HW):
                nn = min(HW, n - n0)
                o_ref[:, n0:n0 + nn] = jnp.dot(hn, w_cols(off + n0, nn), preferred_element_type=F32)
            off += n

    for gi in range(n_groups):
        @pl.when((i >= starts[gi]) & (i < starts[gi + 1]))
        def _(gi=gi):
            run(x_refs[gi], out_refs[gi * n_out:(gi + 1) * n_out])


def _inproj(hs, g, ws, widths, tm, time_major_last):
    d = hs[0].shape[1]
    assert sum(widths) == sum(w.shape[1] for w in ws)
    counts = [h.shape[0] // tm for h in hs]
    starts = _step_ranges(counts)
    in_specs, out_specs, out_shape = [], [], []
    for k, h in enumerate(hs):
        t = h.shape[0]
        assert t % tm == 0
        loc = functools.partial(_local, start=starts[k], count=counts[k])
        in_specs.append(pl.BlockSpec((tm, d), lambda i, loc=loc: (loc(i), 0)))
        for wi, wd in enumerate(widths):
            if wi == len(widths) - 1 and time_major_last[k] is not None:
                b, l = time_major_last[k]
                assert l % tm == 0
                nl = l // tm
                out_shape.append(jax.ShapeDtypeStruct((l, b * wd), F32))
                out_specs.append(pl.BlockSpec((tm, wd), lambda i, loc=loc, nl=nl: (loc(i) % nl, loc(i) // nl)))
            else:
                out_shape.append(jax.ShapeDtypeStruct((t, wd), F32))
                out_specs.append(pl.BlockSpec((tm, wd), lambda i, loc=loc: (loc(i), 0)))
    outs = pl.pallas_call(
        functools.partial(_inproj_kernel, n_groups=len(hs), n_w=len(ws), n_out=len(widths),
                          starts=tuple(starts)),
        grid=(starts[-1],),
        in_specs=in_specs + [_full((1, d))] + [_full(w.shape) for w in ws],
        out_specs=out_specs,
        out_shape=out_shape,
        compiler_params=_params(1, VMEM_LIMIT),
        name="inproj",
    )(*hs, g.reshape(1, d), *ws)
    nw = len(widths)
    return [outs[k * nw:(k + 1) * nw] for k in range(len(hs))]


def _post_kernel(*refs, n_groups, starts, final, ff_chunk):
    grp_in = [refs[4 * k:4 * k + 4] for k in range(n_groups)]
    (wo_ref, gff_ref, w1_ref, w2_ref, gple_ref, wg_ref, wp_ref,
     gfin_ref) = refs[4 * n_groups:4 * n_groups + 8]
    out_refs = refs[4 * n_groups + 8:]
    i = pl.program_id(0)

    def run(h_ref, ma_ref, mb_ref, p_ref, o_ref):
        half = ma_ref.shape[-1]
        h = h_ref[...]
        h = h + (jnp.dot(ma_ref[...].astype(BF16), wo_ref[0:half, :], preferred_element_type=F32)
                 + jnp.dot(mb_ref[...].astype(BF16), wo_ref[half:2 * half, :],
                           preferred_element_type=F32))
        hn = _rms(h, gff_ref[...]).astype(BF16)
        d_ff = w1_ref.shape[1]
        acc = jnp.zeros_like(h)
        for f0 in range(0, d_ff, ff_chunk):
            a = jnp.dot(hn, w1_ref[:, f0:f0 + ff_chunk], preferred_element_type=F32)
            a = jnp.square(jnp.maximum(a, 0.0))
            acc = acc + jnp.dot(a.astype(BF16), w2_ref[f0:f0 + ff_chunk, :], preferred_element_type=F32)
        h = h + acc
        gate = _sigmoid(jnp.dot(_rms(h, gple_ref[...]).astype(BF16), wg_ref[...],
                                preferred_element_type=F32))
        h = h + gate * jnp.dot(p_ref[...].astype(BF16), wp_ref[...], preferred_element_type=F32)
        o_ref[...] = _rms(h, gfin_ref[...]) if final else h

    for k in range(n_groups):
        @pl.when((i >= starts[k]) & (i < starts[k + 1]))
        def _(k=k):
            run(*grp_in[k], out_refs[k])


def _post(groups, wo, gff, w1, w2, gple, wg, wp, gfin, *, layer, final, tm):
    d = groups[0][0].shape[1]
    d_ff = w1.shape[-1]
    counts = [grp[0].shape[0] // tm for grp in groups]
    starts = _step_ranges(counts)
    in_specs, out_specs, out_shape, operands = [], [], [], []
    for k, (h, mix_a, mix_b, p, b_time_major) in enumerate(groups):
        t = h.shape[0]
        half = mix_a.shape[-1]
        pd = p.shape[-1]
        assert t % tm == 0
        loc = functools.partial(_local, start=starts[k], count=counts[k])
        row = lambda i, loc=loc: (loc(i), 0)
        mb_spec = pl.BlockSpec((tm, half), row)
        if b_time_major is not None:
            _, l = b_time_major
            assert l % tm == 0
            nl = l // tm
            mb_spec = pl.BlockSpec((tm, half), lambda i, loc=loc, nl=nl: (loc(i) % nl, loc(i) // nl))
        in_specs += [pl.BlockSpec((tm, d), row), pl.BlockSpec((tm, half), row), mb_spec,
                     pl.BlockSpec((None, tm, pd), lambda i, loc=loc: (layer, loc(i), 0))]
        operands += [h, mix_a, mix_b, p]
        out_specs.append(pl.BlockSpec((tm, d), row))
        out_shape.append(jax.ShapeDtypeStruct((t, d), F32))
    pd = groups[0][3].shape[-1]
    lw = lambda r, cdim: pl.BlockSpec((None, r, cdim), lambda i: (layer, 0, 0))
    return pl.pallas_call(
        functools.partial(_post_kernel, n_groups=len(groups), starts=tuple(starts), final=final,
                          ff_chunk=1024),
        grid=(starts[-1],),
        in_specs=in_specs + [_full((d, d)), _full((1, d)), lw(d, d_ff), lw(d_ff, d), _full((1, d)),
                             lw(d, d), lw(pd, d), _full((1, d))],
        out_specs=out_specs,
        out_shape=out_shape,
        compiler_params=_params(1, VMEM_LIMIT),
        name="post",
    )(*operands, wo, gff.reshape(1, d), w1, w2, gple.reshape(1, d), wg, wp, gfin.reshape(1, d))


def _mlstm_kernel(zq_ref, zk_ref, zv_ref, zo_ref, zg_ref, cw_ref, cb_ref, bg_ref, gn_ref,
                  conv0_ref, c0_ref, n0_ref, m0_ref,
                  out_ref, conv_ref, c_ref, n_ref, m_ref, ext_ref, u_ref, qk_ref, gates_ref,
                  *, c, group):
    j = pl.program_id(1)
    tail = CONV_W - 1

    @pl.when(j == 0)
    def _():
        c_ref[...] = c0_ref[...]
        n_ref[...] = n0_ref[...]
        m_ref[...] = m0_ref[...]
        for gi in range(group):
            ext_ref[gi, 0:SUBLANES, :] = jnp.zeros((SUBLANES, 2 * HW), F32)
            ext_ref[gi, SUBLANES - tail:SUBLANES, :] = conv0_ref[gi]

    ri = lax.broadcasted_iota(jnp.int32, (c, c), 0)
    ci = lax.broadcasted_iota(jnp.int32, (c, c), 1)
    eye = ri == ci
    tril = ri >= ci
    lane = lax.broadcasted_iota(jnp.int32, (c, LANES), 1)
    bg = bg_ref[...]

    def lsum(x):
        return jnp.broadcast_to(jnp.sum(x, axis=1, keepdims=True), (c, DH))

    def unit(gi, h):
        sl = slice(h * DH, (h + 1) * DH)
        gates = gates_ref[gi]
        i_col = lsum(jnp.where(lane == h, gates, 0.0))
        f_col = lsum(jnp.where(lane == HEADS + h, gates, 0.0))
        yield
        b_row = jnp.sum(jnp.where(ri <= ci, f_col[:, :c], 0.0), axis=0, keepdims=True)
        b_col = lsum(jnp.where(eye, b_row, 0.0))
        i_row = jnp.sum(jnp.where(eye, i_col[:, :c], 0.0), axis=0, keepdims=True)
        yield
        m_prev = m_ref[gi, h:h + 1, :]
        dmat = jnp.where(tril, b_col[:, :c] - b_row + i_row, NEG)
        inter = b_col + m_prev
        row_max = jnp.broadcast_to(jnp.max(dmat, axis=1, keepdims=True), (c, DH))
        qh = qk_ref[gi, :, sl]
        kh = qk_ref[gi, :, HW + h * DH:HW + (h + 1) * DH] * (DH ** -0.5)
        vh = zv_ref[gi, :, sl]
        c_h = c_ref[gi, h]
        n_h = n_ref[gi, h:h + 1, :]
        s_raw = _bdot_nt(qh, kh)
        q_c = _bdot(qh, c_h)
        q_n = lsum(qh * n_h)
        yield
        m_t = jnp.maximum(inter, row_max)
        w_intra = jnp.exp(dmat - m_t[:, :c])
        w_inter = jnp.exp(inter - m_t)
        s = s_raw * w_intra
        s_v = _bdot(s, vh)
        s_sum = lsum(s)
        m_new = m_t[c - 1:c, :]
        b_last = b_col[c - 1:c, :]
        w_last = jnp.exp(b_last - b_col + i_col - m_new)
        decay = jnp.exp(b_last + m_prev - m_new)
        kw = w_last * kh
        kw_v = _bdot_tn(kw, vh)
        yield
        num = w_inter * q_c + s_v
        den = w_inter * q_n + s_sum
        hh = num / jnp.maximum(jnp.abs(den), jnp.exp(-m_t))
        c_ref[gi, h] = decay * c_h + kw_v
        n_ref[gi, h:h + 1, :] = decay * n_h + jnp.sum(kw, axis=0, keepdims=True)
        m_ref[gi, h:h + 1, :] = m_new
        hh = _sigmoid(zo_ref[gi, :, sl]) * hh
        out_ref[gi, :, sl] = _head_rms(hh) * gn_ref[:, sl]

    for gi in range(group):
        ext_ref[gi, SUBLANES:SUBLANES + c, 0:HW] = zq_ref[gi]
        ext_ref[gi, SUBLANES:SUBLANES + c, HW:2 * HW] = zk_ref[gi]
        assert CONV_W == 4
        x0 = ext_ref[gi, SUBLANES:SUBLANES + c, :]
        x2 = ext_ref[gi, SUBLANES - 2:SUBLANES - 2 + c, :]
        u_ref[gi, SUBLANES:SUBLANES + c, :] = cw_ref[2:3, :] * x0 + cw_ref[0:1, :] * x2
        u_ref[gi, SUBLANES - 1:SUBLANES, :] = (cw_ref[2:3, :] * ext_ref[gi, SUBLANES - 1:SUBLANES, :]
                                               + cw_ref[0:1, :] * ext_ref[gi, SUBLANES - 3:SUBLANES - 2, :])
        conv = (cb_ref[...] + cw_ref[3:4, :] * x0 + cw_ref[1:2, :] * x2
                + u_ref[gi, SUBLANES - 1:SUBLANES - 1 + c, :])
        ext_ref[gi, 0:SUBLANES, :] = ext_ref[gi, c:c + SUBLANES, :]
        conv_ref[gi] = ext_ref[gi, SUBLANES - tail:SUBLANES, :]
        qk_ref[gi] = conv * _sigmoid(conv)
        gb = zg_ref[gi] + bg
        gates_ref[gi] = jnp.where(lane < HEADS, gb, _log_sigmoid(gb))
    _round_robin([unit(gi, h) for gi in range(group) for h in range(HEADS)])


def _seq_group(b, c):
    rows = 512
    group = max(1, min(b, rows // c, 2 * SUBLANES))
    assert b % group == 0
    return group


def _mlstm(z_main, z_gate, conv_w, conv_b, b_gate, gn_a, conv0, c0, n0, m0, b, l):
    c = _chunk_len(l)
    nc = l // c
    grp = _seq_group(b, c)
    z3 = z_main.reshape(b, l, z_main.shape[-1])
    zspec = lambda col: pl.BlockSpec((grp, c, HW), lambda bi, j: (bi, j, col))
    st = lambda shape: pl.BlockSpec((grp,) + shape, lambda bi, j: (bi,) + (0,) * len(shape))
    m0b = jnp.broadcast_to(m0[:, :, None], (b, HEADS, DH))
    out, conv_new, c_new, n_new, m_new = pl.pallas_call(
        functools.partial(_mlstm_kernel, c=c, group=grp),
        grid=(b // grp, nc),
        in_specs=[zspec(0), zspec(1), zspec(2), zspec(3),
                  pl.BlockSpec((grp, c, LANES), lambda bi, j: (bi, j, 0)),
                  _full((CONV_W, 2 * HW)), _full((1, 2 * HW)), _full((1, LANES)), _full((1, HW)),
                  st((CONV_W - 1, 2 * HW)), st((HEADS, DH, DH)), st((HEADS, DH)), st((HEADS, DH))],
        out_specs=[pl.BlockSpec((grp, c, HW), lambda bi, j: (bi, j, 0)),
                   st((CONV_W - 1, 2 * HW)), st((HEADS, DH, DH)), st((HEADS, DH)), st((HEADS, DH))],
        out_shape=[jax.ShapeDtypeStruct((b, l, HW), F32),
                   jax.ShapeDtypeStruct((b, CONV_W - 1, 2 * HW), F32),
                   jax.ShapeDtypeStruct((b, HEADS, DH, DH), F32),
                   jax.ShapeDtypeStruct((b, HEADS, DH), F32),
                   jax.ShapeDtypeStruct((b, HEADS, DH), F32)],
        scratch_shapes=[pltpu.VMEM((grp, c + SUBLANES, 2 * HW), F32),
                        pltpu.VMEM((grp, c + SUBLANES, 2 * HW), F32),
                        pltpu.VMEM((grp, c, 2 * HW), F32), pltpu.VMEM((grp, c, LANES), F32)],
        compiler_params=_params(2),
        name="mlstm",
    )(z3, z3, z3, z3, z_gate.reshape(b, l, LANES), conv_w, conv_b.reshape(1, -1),
      jnp.pad(b_gate, (0, LANES - 2 * HEADS)).reshape(1, LANES), gn_a.reshape(1, -1),
      conv0, c0, n0, m0b)
    return out.reshape(b * l, HW), conv_new, c_new, n_new, m_new[:, :, 0]


def _rope_table_kernel(inv_ref, sign_ref, cos_ref, sin_ref, *, pos0, rows):
    i = pl.program_id(0)
    pos = (pos0 + i * rows + lax.broadcasted_iota(jnp.int32, (rows, LANES), 0)).astype(F32)
    ang = pos * inv_ref[...]
    cos_ref[...] = jnp.cos(ang)
    sin_ref[...] = jnp.sin(ang) * sign_ref[...]


def _rope_tables(l, pos0):
    half = DH // 2
    inv = ROPE_BASE ** (-jnp.arange(half, dtype=F32) / half)
    inv2 = jnp.concatenate([inv, inv]).reshape(1, DH)
    sign = jnp.concatenate([-jnp.ones((half,), F32), jnp.ones((half,), F32)]).reshape(1, DH)
    rows = min(l, 512)
    assert l % rows == 0
    return pl.pallas_call(
        functools.partial(_rope_table_kernel, pos0=pos0, rows=rows),
        grid=(l // rows,),
        in_specs=[_full((1, DH)), _full((1, DH))],
        out_specs=[pl.BlockSpec((rows, DH), lambda i: (i, 0))] * 2,
        out_shape=[jax.ShapeDtypeStruct((l, DH), F32)] * 2,
        compiler_params=_params(1),
        name="rope_table",
    )(inv2, sign)


def _ret_kernel(zq_ref, zk_ref, zv_ref, zg_ref, cos_ref, sin_ref, s0_ref, out_ref, s_ref,
                *, c, group):
    j = pl.program_id(1)

    @pl.when(j == 0)
    def _():
        s_ref[...] = s0_ref[...]

    cosf = cos_ref[...]
    sinf = sin_ref[...]
    ti = lax.broadcasted_iota(jnp.int32, (c, c), 0)
    si = lax.broadcasted_iota(jnp.int32, (c, c), 1)
    rel = jnp.maximum(ti - si, 0).astype(F32)
    tcol = lax.broadcasted_iota(jnp.int32, (c, 1), 0).astype(F32)

    def rope(x):
        return x * cosf + pltpu.roll(x, DH // 2, axis=1) * sinf

    def unit(gi, h, decay, inter, kdecay, cdecay):
        sl = slice(h * DH, (h + 1) * DH)
        qr = rope(zq_ref[gi, :, sl])
        kr = rope(zk_ref[gi, :, sl]) * (DH ** -0.5)
        yield
        vh = zv_ref[gi, :, sl]
        s_h = s_ref[gi, h]
        qk = _bdot_nt(qr, kr)
        q_s = _bdot(qr, s_h)
        k_v = _bdot_tn(kr * kdecay, vh)
        yield
        o = q_s * inter + _bdot(qk * decay, vh)
        s_ref[gi, h] = cdecay * s_h + k_v
        yield
        gate = zg_ref[gi, :, sl]
        out_ref[gi, :, sl] = _head_rms(o) * (gate * _sigmoid(gate))

    units = []
    for h in range(HEADS):
        lg = math.log1p(-(2.0 ** (-5.0 - h)))
        decay = jnp.where(ti >= si, jnp.exp(rel * lg), 0.0)
        inter = jnp.exp((tcol + 1.0) * lg)
        kdecay = jnp.exp((c - 1.0 - tcol) * lg)
        cdecay = math.exp(c * lg)
        units += [unit(gi, h, decay, inter, kdecay, cdecay) for gi in range(group)]
    _round_robin(units)


def _retention(z_main, cos_t, sin_t, s0, b, l):
    c = _chunk_len(l)
    nc = l // c
    grp = _seq_group(b, c)
    z3 = z_main.reshape(b, l, z_main.shape[-1])
    zspec = lambda col: pl.BlockSpec((grp, c, HW), lambda bi, j: (bi, j, col))
    st = pl.BlockSpec((grp, HEADS, DH, DH), lambda bi, j: (bi, 0, 0, 0))
    tab = pl.BlockSpec((c, DH), lambda bi, j: (j, 0))
    out, s_new = pl.pallas_call(
        functools.partial(_ret_kernel, c=c, group=grp),
        grid=(b // grp, nc),
        in_specs=[zspec(4), zspec(5), zspec(6), zspec(7), tab, tab, st],
        out_specs=[pl.BlockSpec((grp, c, HW), lambda bi, j: (bi, j, 0)), st],
        out_shape=[jax.ShapeDtypeStruct((b, l, HW), F32),
                   jax.ShapeDtypeStruct((b, HEADS, DH, DH), F32)],
        compiler_params=_params(2),
        name="retention",
    )(z3, z3, z3, z3, cos_t, sin_t, s0)
    return out.reshape(b * l, HW), s_new


def _hgrn_kernel(zq_ref, zf_ref, zi_ref, zg_ref, lbl_ref, gn_ref, s0_ref, out_ref, s_ref,
                 kk_ref, bcum_ref, *, c, sc, layer, group):
    j = pl.program_id(1)

    @pl.when(j == 0)
    def _():
        s_ref[...] = s0_ref[...]

    lbl = lbl_ref[...]
    e = jnp.exp(lbl - jnp.max(lbl, axis=0, keepdims=True))
    sm = e / jnp.sum(e, axis=0, keepdims=True)
    cum = sm[0:1, :]
    for r in range(1, layer + 1):
        cum = cum + sm[r:r + 1, :]
    lb = cum - sm[0:1, :]

    oml = 1.0 - lb
    ri = lax.broadcasted_iota(jnp.int32, (c, c), 0)
    ci = lax.broadcasted_iota(jnp.int32, (c, c), 1)
    tril = jnp.where(ri >= ci, 1.0, 0.0).astype(BF16)
    row_s = lax.broadcasted_iota(jnp.int32, (sc, sc), 0)
    lane_s = lax.broadcasted_iota(jnp.int32, (sc, sc), 1)
    causal_col = jnp.where(row_s >= lane_s, lane_s, -1)
    keep = [causal_col == s for s in range(sc)]
    e_r = lax.broadcasted_iota(jnp.int32, (DH, DH), 0)
    e_c = lax.broadcasted_iota(jnp.int32, (DH, DH), 1)
    eye = e_r == e_c

    def unit(gi, h):
        sl = slice(h * DH, (h + 1) * DH)
        bh = bcum_ref[gi, :, sl]
        qh = zq_ref[gi, :, sl] * (DH ** -0.5)
        kh = kk_ref[gi, :, sl]
        vh = zi_ref[gi, :, sl]
        s_h = s_ref[gi, h]
        b_last = bh[c - 1:c, :]
        o_inter = _bdot(qh * jnp.exp(bh), s_h)
        k_v = _bdot_tn(kh * jnp.exp(b_last - bh), vh)
        dec_col = jnp.sum(jnp.where(eye, jnp.exp(b_last), 0.0), axis=1, keepdims=True)
        yield
        s_ref[gi, h] = dec_col * s_h + k_v
        blocks = []
        for blk in range(c // sc):
            r0 = blk * sc
            b_i = bh[r0:r0 + sc]
            q_i = qh[r0:r0 + sc]
            k_i = kh[r0:r0 + sc]
            v_i = vh[r0:r0 + sc]
            att_prev = None
            if blk > 0:
                ref_row = bh[r0 - 1:r0, :]
                a_i = q_i * jnp.exp(b_i - ref_row)
                k_prev = kh[0:r0] * jnp.exp(ref_row - bh[0:r0])
                att_prev = _bdot_nt(a_i, k_prev)
            b2_i = b_i * LOG2_E
            c2_i = b2_i - jnp.log2(k_i)
            cols = [jnp.sum(q_i * jnp.exp2(b2_i - c2_i[s:s + 1, :]), axis=1, keepdims=True)
                    for s in range(sc)]
            yield
            att = jnp.zeros((sc, sc), F32)
            for s in range(sc):
                att = jnp.where(keep[s], cols[s], att)
            o_i = _bdot(att, v_i)
            if att_prev is not None:
                o_i = o_i + _bdot(att_prev, vh[0:r0])
            blocks.append(o_i)
        yield
        o = o_inter + (jnp.concatenate(blocks, axis=0) if len(blocks) > 1 else blocks[0])
        gate = zg_ref[gi, :, sl]
        out_ref[gi, :, sl] = _head_rms(o) * gn_ref[:, sl] * (gate * _sigmoid(gate))

    for gi in range(group):
        zf = zf_ref[gi]
        ez = jnp.exp(-jnp.abs(zf))
        big = 1.0 / (1.0 + ez)
        small = ez * big
        pos = zf >= 0.0
        logf = jnp.log(lb + oml * jnp.where(pos, big, small))
        kk_ref[gi] = oml * jnp.where(pos, small, big)

        p0 = logf.astype(BF16)
        r1 = logf - p0.astype(F32)
        p1 = r1.astype(BF16)
        p2 = (r1 - p1.astype(F32)).astype(BF16)
        bcum_ref[gi] = (jnp.dot(tril, p0, preferred_element_type=F32)
                        + jnp.dot(tril, p1, preferred_element_type=F32)
                        + jnp.dot(tril, p2, preferred_element_type=F32))
    _round_robin([unit(gi, h) for gi in range(group) for h in range(HEADS)])


def _hgrn(z_cd, lb_logits, gn_c, s0, b, l, layer):
    c = _chunk_len(l)
    sc = min(c, SUBLANES)
    nc = l // c
    grp = _seq_group(b, c)
    depth = lb_logits.shape[0]
    z3 = z_cd.reshape(b, l, z_cd.shape[-1])
    zspec = lambda col: pl.BlockSpec((grp, c, HW), lambda bi, j: (bi, j, col))
    st = pl.BlockSpec((grp, HEADS, DH, DH), lambda bi, j: (bi, 0, 0, 0))
    out, s_new = pl.pallas_call(
        functools.partial(_hgrn_kernel, c=c, sc=sc, layer=layer, group=grp),
        grid=(b // grp, nc),
        in_specs=[zspec(0), zspec(1), zspec(2), zspec(3), _full((depth, HW)), _full((1, HW)), st],
        out_specs=[pl.BlockSpec((grp, c, HW), lambda bi, j: (bi, j, 0)), st],
        out_shape=[jax.ShapeDtypeStruct((b, l, HW), F32),
                   jax.ShapeDtypeStruct((b, HEADS, DH, DH), F32)],
        scratch_shapes=[pltpu.VMEM((grp, c, HW), F32), pltpu.VMEM((grp, c, HW), F32)],
        compiler_params=_params(2),
        name="hgrn2",
    )(z3, z3, z3, z3, lb_logits, gn_c.reshape(1, -1), s0)
    return out.reshape(b * l, HW), s_new


def _s5_kernel(u_ref, are_ref, aim_ref, ldt_ref, bre_ref, bim_ref, cre_ref, cim_ref, d_ref,
               wglu_ref, bglu_ref, x0r_ref, x0i_ref,
               s_ref, xr_ref, xi_ref, ar_sc, ai_sc, bbr_sc, bbi_sc, bur_sc, bui_sc, *, ct):
    j = pl.program_id(1)
    ns = are_ref.shape[-1]
    nu = u_ref.shape[-1]
    hs = ns // 2
    hu = nu // 2
    rows = ct * SUBLANES

    @pl.when(j == 0)
    def _():
        a_re = are_ref[...]
        a_im = aim_ref[...]
        dt = jnp.exp(ldt_ref[...])
        mag = jnp.exp(dt * a_re)
        ar = mag * jnp.cos(dt * a_im)
        ai = mag * jnp.sin(dt * a_im)
        ar_sc[...] = jnp.broadcast_to(ar, (SUBLANES, ns))
        ai_sc[...] = jnp.broadcast_to(ai, (SUBLANES, ns))
        den = a_re * a_re + a_im * a_im
        nr = ar - 1.0
        zr = (nr * a_re + ai * a_im) / den
        zi = (ai * a_re - nr * a_im) / den
        for hg in range(2):
            us = slice(hg * hu, (hg + 1) * hu)
            ss = slice(hg * hs, (hg + 1) * hs)
            bbr_sc[us, :] = (zr[:, ss] * bre_ref[us, :] - zi[:, ss] * bim_ref[us, :]).astype(BF16)
            bbi_sc[us, :] = (zr[:, ss] * bim_ref[us, :] + zi[:, ss] * bre_ref[us, :]).astype(BF16)
        xr_ref[...] = x0r_ref[...]
        xi_ref[...] = x0i_ref[...]

    u = u_ref[...].reshape(rows, nu)
    ub = u.astype(BF16)
    for hg in range(2):
        us = slice(hg * hu, (hg + 1) * hu)
        ss = slice(hg * hs, (hg + 1) * hs)
        bur_sc[:, ss] = jnp.dot(ub[:, us], bbr_sc[us, :], preferred_element_type=F32)
        bui_sc[:, ss] = jnp.dot(ub[:, us], bbi_sc[us, :], preferred_element_type=F32)

    lane_chunk = 8 * LANES
    for lc in range(ns // lane_chunk):
        ls = slice(lc * lane_chunk, (lc + 1) * lane_chunk)
        ar = ar_sc[:, ls]
        ai = ai_sc[:, ls]

        def step(t, carry):
            xr, xi = carry
            r0 = pl.multiple_of(t * SUBLANES, SUBLANES)
            nxr = ar * xr - ai * xi + bur_sc[pl.ds(r0, SUBLANES), ls]
            nxi = ar * xi + ai * xr + bui_sc[pl.ds(r0, SUBLANES), ls]
            bur_sc[pl.ds(r0, SUBLANES), ls] = nxr
            bui_sc[pl.ds(r0, SUBLANES), ls] = nxi
            return nxr, nxi

        xr, xi = lax.fori_loop(0, ct, step, (xr_ref[:, ls], xi_ref[:, ls]), unroll=SUBLANES)
        xr_ref[:, ls] = xr
        xi_ref[:, ls] = xi

    ys = []
    for hg in range(2):
        ss = slice(hg * hs, (hg + 1) * hs)
        ys.append(jnp.dot(bur_sc[:, ss].astype(BF16), cre_ref[ss, :], preferred_element_type=F32)
                  - jnp.dot(bui_sc[:, ss].astype(BF16), cim_ref[ss, :], preferred_element_type=F32))
    y = jnp.concatenate(ys, axis=1) + d_ref[...] * u
    a = 0.5 * y * (1.0 + jnp.tanh(math.sqrt(2.0 / math.pi) * (y + 0.044715 * (y * y * y))))
    s = a * _sigmoid(jnp.dot(a.astype(BF16), wglu_ref[...], preferred_element_type=F32) + bglu_ref[...])
    s_ref[...] = s.reshape(ct, SUBLANES, nu)


def _s5_block_diag(bmat, cmat):
    g, p, hgrp = bmat.shape
    gh = g // 2
    eye = jnp.eye(gh, dtype=F32)
    b4 = bmat.reshape(2, gh, p, hgrp)
    bc = jnp.einsum('agph,gk->aghkp', b4, eye).reshape(2 * gh * hgrp, gh * p)
    c4 = cmat.reshape(2, gh, hgrp, p)
    cc = jnp.einsum('aghp,gk->agpkh', c4, eye).reshape(2 * gh * p, gh * hgrp)
    return bc, cc


def _s5_operands(a_re, a_im, log_dt, b_re, b_im, c_re, c_im, d_skip, w_glu, b_glu):
    g, p = a_re.shape
    ns = g * p
    nu = d_skip.shape[-1]
    assert nu == g * S5_GROUP and (g // 2) * S5_GROUP == MXU_DIM
    bre_c, cre_c = _s5_block_diag(b_re, c_re)
    bim_c, cim_c = _s5_block_diag(b_im, c_im)
    ldt = jnp.broadcast_to(log_dt[:, None], (g, p)).reshape(1, ns)
    return (a_re.reshape(1, ns), a_im.reshape(1, ns), ldt, bre_c, bim_c, cre_c.astype(BF16),
            cim_c.astype(BF16), d_skip.reshape(1, nu), w_glu.astype(BF16), b_glu.reshape(1, nu))


def _s5(u_tm, operands, x0r, x0i):
    l, b, nu = u_tm.shape
    g, p = x0r.shape[1:]
    ns = g * p
    assert b % SUBLANES == 0
    ct = _chunk_len(l)
    nct = l // ct
    rows = ct * SUBLANES
    xspec = pl.BlockSpec((SUBLANES, ns), lambda bb, j: (bb, 0))
    s, xr, xi = pl.pallas_call(
        functools.partial(_s5_kernel, ct=ct),
        grid=(b // SUBLANES, nct),
        in_specs=[pl.BlockSpec((ct, SUBLANES, nu), lambda bb, j: (j, bb, 0)),
                  _full((1, ns)), _full((1, ns)), _full((1, ns)),
                  _full((nu, ns // 2)), _full((nu, ns // 2)),
                  _full((ns, nu // 2)), _full((ns, nu // 2)),
                  _full((1, nu)), _full((nu, nu)), _full((1, nu)), xspec, xspec],
        out_specs=[pl.BlockSpec((ct, SUBLANES, nu), lambda bb, j: (j, bb, 0)), xspec, xspec],
        out_shape=[jax.ShapeDtypeStruct((l, b, nu), F32),
                   jax.ShapeDtypeStruct((b, ns), F32), jax.ShapeDtypeStruct((b, ns), F32)],
        scratch_shapes=[pltpu.VMEM((SUBLANES, ns), F32), pltpu.VMEM((SUBLANES, ns), F32),
                        pltpu.VMEM((nu, ns // 2), BF16), pltpu.VMEM((nu, ns // 2), BF16),
                        pltpu.VMEM((rows, ns), F32), pltpu.VMEM((rows, ns), F32)],
        compiler_params=_params(2, VMEM_LIMIT),
        name="s5",
    )(u_tm, *operands, x0r.reshape(b, ns), x0i.reshape(b, ns))
    return s, xr.reshape(b, g, p), xi.reshape(b, g, p)


def _trunk(groups, prm):
    d = groups[0]['x'].shape[-1]
    tm = 512
    info = []
    for grp in groups:
        b, l, _ = grp['x'].shape
        assert l >= CONV_W - 1 and (b * l) % tm == 0
        info.append(dict(b=b, l=l, t=b * l, direct_tm=b == SUBLANES and l % tm == 0))
    hs = [grp['x'].reshape(-1, d) for grp in groups]
    pps = [grp['p'].reshape(grp['p'].shape[0], -1, grp['p'].shape[-1]) for grp in groups]

    zs = _inproj(hs, prm['norm_mix'][0], prm['w_ab'], (8 * HW, LANES), tm, [None] * len(groups))
    ab_new, post_in = [], []
    for grp, inf, h, pp, (z_main, z_gate) in zip(groups, info, hs, pps, zs):
        b, l = inf['b'], inf['l']
        conv0, c0, n0, m0, ret0 = grp['ab_state']
        hm, conv_new, c_new, n_new, m_new = _mlstm(z_main, z_gate, prm['conv_w_ab'][0], prm['conv_b_ab'][0],
                                         prm['b_gate_ab'][0], prm['gn_a'][0], conv0[0], c0[0], n0[0],
                                         m0[0], b, l)
        cos_t, sin_t = _rope_tables(l, grp['pos0'])
        hr, ret_new = _retention(z_main, cos_t, sin_t, ret0[0], b, l)
        ab_new.append((conv_new[None], c_new[None], n_new[None], m_new[None], ret_new[None]))
        post_in.append((h, hm, hr, pp, None))
    hs = _post(post_in, prm['w_out_ab'], prm['norm_ff'][0], prm['w_ff1'], prm['w_ff2'],
               prm['norm_ple'][0], prm['w_ple_gate'], prm['w_ple_proj'], prm['norm_final'],
               layer=0, final=False, tm=tm)

    tml = [(inf['b'], inf['l']) if inf['direct_tm'] else None for inf in info]
    zs = _inproj(hs, prm['norm_mix'][1], prm['w_cd'], (4 * HW, HW), tm, tml)
    cd_new, post_in = [], []
    for grp, inf, h, pp, (z_cd, su) in zip(groups, info, hs, pps, zs):
        b, l, t = inf['b'], inf['l'], inf['t']
        hg0, x0r, x0i = grp['cd_state']
        o, hg_new = _hgrn(z_cd, prm['lb_logits'], prm['gn_c'][0], hg0[0], b, l, layer=1)
        u_tm = (su.reshape(l, b, HW) if inf['direct_tm']
                else jnp.transpose(su.reshape(b, l, HW), (1, 0, 2)))
        s_tm, xr, xi = _s5(u_tm, prm['s5_operands'], x0r[0], x0i[0])
        s_in = (s_tm.reshape(l, b * HW) if inf['direct_tm']
                else jnp.transpose(s_tm, (1, 0, 2)).reshape(t, HW))
        cd_new.append((hg_new[None], xr[None], xi[None]))
        post_in.append((h, o, s_in, pp, (b, l) if inf['direct_tm'] else None))
    ys = _post(post_in, prm['w_out_cd'], prm['norm_ff'][1], prm['w_ff1'], prm['w_ff2'],
               prm['norm_ple'][1], prm['w_ple_gate'], prm['w_ple_proj'], prm['norm_final'],
               layer=1, final=True, tm=tm)
    ys = [y.reshape(grp['x'].shape) for y, grp in zip(ys, groups)]
    return ys, ab_new, cd_new


def kernel(x_prompt, x_sample, state_mlstm_conv, state_mlstm_C, state_mlstm_n, state_mlstm_m, state_ret, state_hgrn, state_s5_re, state_s5_im, p_prompt, p_sample, norm_mix, norm_ff, norm_ple, norm_final, w_in_ab, b_gate_ab, conv_w_ab, conv_b_ab, gn_a, w_out_ab, w_in_cd, lb_logits, gn_c, s5_A_re, s5_A_im, s5_log_dt, s5_B_re, s5_B_im, s5_C_re, s5_C_im, s5_D, w_glu, b_glu, w_out_cd, w_ff1, w_ff2, w_ple_proj, w_ple_gate):
    assert norm_mix.shape[0] == 2, "two layers: (mLSTM || retention), (HGRN2 || S5)"
    w_ab = w_in_ab[0]
    gate0 = 4 * HW
    w_ab = (w_ab[:, :gate0].astype(BF16), w_ab[:, gate0 + 2 * HEADS:].astype(BF16),
            jnp.pad(w_ab[:, gate0:gate0 + 2 * HEADS], ((0, 0), (0, LANES - 2 * HEADS))).astype(BF16))
    prm = dict(norm_mix=norm_mix, norm_ff=norm_ff, norm_ple=norm_ple, norm_final=norm_final,
               w_ab=w_ab, b_gate_ab=b_gate_ab, conv_w_ab=conv_w_ab, conv_b_ab=conv_b_ab,
               gn_a=gn_a, w_out_ab=w_out_ab[0].astype(BF16), w_cd=(w_in_cd[0].astype(BF16),),
               lb_logits=lb_logits, gn_c=gn_c,
               s5_operands=_s5_operands(s5_A_re[0], s5_A_im[0], s5_log_dt[0], s5_B_re[0], s5_B_im[0],
                                        s5_C_re[0], s5_C_im[0], s5_D[0], w_glu[0], b_glu[0]),
               w_out_cd=w_out_cd[0].astype(BF16), w_ff1=w_ff1.astype(BF16), w_ff2=w_ff2.astype(BF16),
               w_ple_proj=w_ple_proj.astype(BF16), w_ple_gate=w_ple_gate.astype(BF16))

    bp, lp, _ = x_prompt.shape
    z = lambda *s: jnp.zeros(s, F32)
    zero_ab = (z(1, bp, CONV_W - 1, 2 * HW), z(1, bp, HEADS, DH, DH), z(1, bp, HEADS, DH),
               z(1, bp, HEADS), z(1, bp, HEADS, DH, DH))
    zero_cd = (z(1, bp, HEADS, DH, DH),) + (z(*((1, bp) + s5_A_re.shape[1:])),) * 2
    groups = [dict(x=x_prompt, p=p_prompt, pos0=0, ab_state=zero_ab, cd_state=zero_cd),
              dict(x=x_sample, p=p_sample, pos0=PAST_LEN,
                   ab_state=(state_mlstm_conv, state_mlstm_C, state_mlstm_n, state_mlstm_m, state_ret),
                   cd_state=(state_hgrn, state_s5_re, state_s5_im))]
    (y_p, y_s), (ab_p, ab_s), (cd_p, cd_s) = _trunk(groups, prm)
    return (y_p, y_s,
            ab_p[0], ab_s[0], ab_p[1], ab_s[1], ab_p[2], ab_s[2], ab_p[3], ab_s[3], ab_p[4], ab_s[4],
            cd_p[0], cd_s[0], cd_p[1], cd_s[1], cd_p[2], cd_s[2])
```

```python
import functools
import math

import jax
import jax.numpy as jnp
from jax import lax
from jax.experimental import pallas as pl
from jax.experimental.pallas import tpu as pltpu

F32 = jnp.float32
BF16 = jnp.bfloat16

EPS = 1e-6
NEG = -1e30
LOG2_E = math.log2(math.e)
ROPE_BASE = 10000.0
PAST_LEN = 16384
CHUNK = 64
HEADS = 4
DH = 128
HW = HEADS * DH
CONV_W = 4
S5_GROUP = 16
S5_STATE = 64
SUBLANES = 8
LANES = 128
MXU_DIM = 256
VMEM_LIMIT = 56 * 1024 * 1024


def _params(n_axes, vmem=None):
    return pltpu.CompilerParams(dimension_semantics=("arbitrary",) * n_axes, vmem_limit_bytes=vmem)


def _full(shape):
    return pl.BlockSpec(shape, lambda *_: (0,) * len(shape))


def _bdot(a, b):
    return jnp.dot(a.astype(BF16), b.astype(BF16), preferred_element_type=F32)


def _bdot_nt(a, b):
    return lax.dot_general(a.astype(BF16), b.astype(BF16), (((1,), (1,)), ((), ())),
                           preferred_element_type=F32)


def _bdot_tn(a, b):
    return lax.dot_general(a.astype(BF16), b.astype(BF16), (((0,), (0,)), ((), ())),
                           preferred_element_type=F32)


def _sigmoid(x):
    return 1.0 / (1.0 + jnp.exp(-x))


def _log_sigmoid(x):
    return jnp.minimum(x, 0.0) - jnp.log(1.0 + jnp.exp(-jnp.abs(x)))


def _rms(x, g):
    return x * lax.rsqrt(jnp.mean(x * x, axis=-1, keepdims=True) + EPS) * g


def _head_rms(x):
    return x * lax.rsqrt(jnp.mean(x * x, axis=-1, keepdims=True) + EPS)


def _chunk_len(length):
    return CHUNK if length % CHUNK == 0 else length


def _round_robin(gens):
    gens = list(gens)
    while gens:
        alive = []
        for g in gens:
            try:
                next(g)
                alive.append(g)
            except StopIteration:
                pass
        gens = alive


def _step_ranges(counts):
    starts = [0]
    for n in counts:
        starts.append(starts[-1] + n)
    return starts


def _local(i, start, count):
    return jnp.clip(i - start, 0, count - 1)


def _inproj_kernel(*refs, n_groups, n_w, n_out, starts):
    x_refs = refs[:n_groups]
    g_ref = refs[n_groups]
    w_refs = refs[n_groups + 1:n_groups + 1 + n_w]
    out_refs = refs[n_groups + 1 + n_w:]
    i = pl.program_id(0)
    w_starts = _step_ranges([w.shape[1] for w in w_refs])

    def w_cols(c0, nn):
        for w_ref, s0, s1 in zip(w_refs, w_starts[:-1], w_starts[1:]):
            if s0 <= c0 and c0 + nn <= s1:
                return w_ref[:, c0 - s0:c0 - s0 + nn]
        raise ValueError("output column chunk straddles two weight parts")

    def run(x_ref, outs):
        hn = _rms(x_ref[...], g_ref[...]).astype(BF16)
        off = 0
        for o_ref in outs:
            n = o_ref.shape[-1]
            for n0 in range(0, n, HW):
                nn = min(HW, n - n0)
                o_ref[:, n0:n0 + nn] = jnp.dot(hn, w_cols(off + n0, nn), preferred_element_type=F32)
            off += n

    for gi in range(n_groups):
        @pl.when((i >= starts[gi]) & (i < starts[gi + 1]))
        def _(gi=gi):
            run(x_refs[gi], out_refs[gi * n_out:(gi + 1) * n_out])


def _inproj(hs, g, ws, widths, tm, time_major_last):
    d = hs[0].shape[1]
    assert sum(widths) == sum(w.shape[1] for w in ws)
    counts = [h.shape[0] // tm for h in hs]
    starts = _step_ranges(counts)
    in_specs, out_specs, out_shape = [], [], []
    for k, h in enumerate(hs):
        t = h.shape[0]
        assert t % tm == 0
        loc = functools.partial(_local, start=starts[k], count=counts[k])
        in_specs.append(pl.BlockSpec((tm, d), lambda i, loc=loc: (loc(i), 0)))
        for wi, wd in enumerate(widths):
            if wi == len(widths) - 1 and time_major_last[k] is not None:
                b, l = time_major_last[k]
                assert l % tm == 0
                nl = l // tm
                out_shape.append(jax.ShapeDtypeStruct((l, b * wd), F32))
                out_specs.append(pl.BlockSpec((tm, wd), lambda i, loc=loc, nl=nl: (loc(i) % nl, loc(i) // nl)))
            else:
                out_shape.append(jax.ShapeDtypeStruct((t, wd), F32))
                out_specs.append(pl.BlockSpec((tm, wd), lambda i, loc=loc: (loc(i), 0)))
    outs = pl.pallas_call(
        functools.partial(_inproj_kernel, n_groups=len(hs), n_w=len(ws), n_out=len(widths),
                          starts=tuple(starts)),
        grid=(starts[-1],),
        in_specs=in_specs + [_full((1, d))] + [_full(w.shape) for w in ws],
        out_specs=out_specs,
        out_shape=out_shape,
        compiler_params=_params(1, VMEM_LIMIT),
        name="inproj",
    )(*hs, g.reshape(1, d), *ws)
    nw = len(widths)
    return [outs[k * nw:(k + 1) * nw] for k in range(len(hs))]


def _post_kernel(*refs, n_groups, starts, final, ff_chunk):
    grp_in = [refs[4 * k:4 * k + 4] for k in range(n_groups)]
    (wo_ref, gff_ref, w1_ref, w2_ref, gple_ref, wg_ref, wp_ref,
     gfin_ref) = refs[4 * n_groups:4 * n_groups + 8]
    out_refs = refs[4 * n_groups + 8:]
    i = pl.program_id(0)

    def run(h_ref, ma_ref, mb_ref, p_ref, o_ref):
        half = ma_ref.shape[-1]
        h = h_ref[...]
        h = h + (jnp.dot(ma_ref[...].astype(BF16), wo_ref[0:half, :], preferred_element_type=F32)
                 + jnp.dot(mb_ref[...].astype(BF16), wo_ref[half:2 * half, :],
                           preferred_element_type=F32))
        hn = _rms(h, gff_ref[...]).astype(BF16)
        d_ff = w1_ref.shape[1]
        acc = jnp.zeros_like(h)
        for f0 in range(0, d_ff, ff_chunk):
            a = jnp.dot(hn, w1_ref[:, f0:f0 + ff_chunk], preferred_element_type=F32)
            a = jnp.square(jnp.maximum(a, 0.0))
            acc = acc + jnp.dot(a.astype(BF16), w2_ref[f0:f0 + ff_chunk, :], preferred_element_type=F32)
        h = h + acc
        gate = _sigmoid(jnp.dot(_rms(h, gple_ref[...]).astype(BF16), wg_ref[...],
                                preferred_element_type=F32))
        h = h + gate * jnp.dot(p_ref[...].astype(BF16), wp_ref[...], preferred_element_type=F32)
        o_ref[...] = _rms(h, gfin_ref[...]) if final else h

    for k in range(n_groups):
        @pl.when((i >= starts[k]) & (i < starts[k + 1]))
        def _(k=k):
            run(*grp_in[k], out_refs[k])


def _post(groups, wo, gff, w1, w2, gple, wg, wp, gfin, *, layer, final, tm):
    d = groups[0][0].shape[1]
    d_ff = w1.shape[-1]
    counts = [grp[0].shape[0] // tm for grp in groups]
    starts = _step_ranges(counts)
    in_specs, out_specs, out_shape, operands = [], [], [], []
    for k, (h, mix_a, mix_b, p, b_time_major) in enumerate(groups):
        t = h.shape[0]
        half = mix_a.shape[-1]
        pd = p.shape[-1]
        assert t % tm == 0
        loc = functools.partial(_local, start=starts[k], count=counts[k])
        row = lambda i, loc=loc: (loc(i), 0)
        mb_spec = pl.BlockSpec((tm, half), row)
        if b_time_major is not None:
            _, l = b_time_major
            assert l % tm == 0
            nl = l // tm
            mb_spec = pl.BlockSpec((tm, half), lambda i, loc=loc, nl=nl: (loc(i) % nl, loc(i) // nl))
        in_specs += [pl.BlockSpec((tm, d), row), pl.BlockSpec((tm, half), row), mb_spec,
                     pl.BlockSpec((None, tm, pd), lambda i, loc=loc: (layer, loc(i), 0))]
        operands += [h, mix_a, mix_b, p]
        out_specs.append(pl.BlockSpec((tm, d), row))
        out_shape.append(jax.ShapeDtypeStruct((t, d), F32))
    pd = groups[0][3].shape[-1]
    lw = lambda r, cdim: pl.BlockSpec((None, r, cdim), lambda i: (layer, 0, 0))
    return pl.pallas_call(
        functools.partial(_post_kernel, n_groups=len(groups), starts=tuple(starts), final=final,
                          ff_chunk=1024),
        grid=(starts[-1],),
        in_specs=in_specs + [_full((d, d)), _full((1, d)), lw(d, d_ff), lw(d_ff, d), _full((1, d)),
                             lw(d, d), lw(pd, d), _full((1, d))],
        out_specs=out_specs,
        out_shape=out_shape,
        compiler_params=_params(1, VMEM_LIMIT),
        name="post",
    )(*operands, wo, gff.reshape(1, d), w1, w2, gple.reshape(1, d), wg, wp, gfin.reshape(1, d))


def _mlstm_kernel(zq_ref, zk_ref, zv_ref, zo_ref, zg_ref, cw_ref, cb_ref, bg_ref, gn_ref,
                  conv0_ref, c0_ref, n0_ref, m0_ref,
                  out_ref, conv_ref, c_ref, n_ref, m_ref, ext_ref, u_ref, qk_ref, gates_ref,
                  *, c, group):
    j = pl.program_id(1)
    tail = CONV_W - 1

    @pl.when(j == 0)
    def _():
        c_ref[...] = c0_ref[...]
        n_ref[...] = n0_ref[...]
        m_ref[...] = m0_ref[...]
        for gi in range(group):
            ext_ref[gi, 0:SUBLANES, :] = jnp.zeros((SUBLANES, 2 * HW), F32)
            ext_ref[gi, SUBLANES - tail:SUBLANES, :] = conv0_ref[gi]

    ri = lax.broadcasted_iota(jnp.int32, (c, c), 0)
    ci = lax.broadcasted_iota(jnp.int32, (c, c), 1)
    eye = ri == ci
    tril = ri >= ci
    lane = lax.broadcasted_iota(jnp.int32, (c, LANES), 1)
    bg = bg_ref[...]

    def lsum(x):
        return jnp.broadcast_to(jnp.sum(x, axis=1, keepdims=True), (c, DH))

    def unit(gi, h):
        sl = slice(h * DH, (h + 1) * DH)
        gates = gates_ref[gi]
        i_col = lsum(jnp.where(lane == h, gates, 0.0))
        f_col = lsum(jnp.where(lane == HEADS + h, gates, 0.0))
        yield
        b_row = jnp.sum(jnp.where(ri <= ci, f_col[:, :c], 0.0), axis=0, keepdims=True)
        b_col = lsum(jnp.where(eye, b_row, 0.0))
        i_row = jnp.sum(jnp.where(eye, i_col[:, :c], 0.0), axis=0, keepdims=True)
        yield
        m_prev = m_ref[gi, h:h + 1, :]
        dmat = jnp.where(tril, b_col[:, :c] - b_row + i_row, NEG)
        inter = b_col + m_prev
        row_max = jnp.broadcast_to(jnp.max(dmat, axis=1, keepdims=True), (c, DH))
        qh = qk_ref[gi, :, sl]
        kh = qk_ref[gi, :, HW + h * DH:HW + (h + 1) * DH] * (DH ** -0.5)
        vh = zv_ref[gi, :, sl]
        c_h = c_ref[gi, h]
        n_h = n_ref[gi, h:h + 1, :]
        s_raw = _bdot_nt(qh, kh)
        q_c = _bdot(qh, c_h)
        q_n = lsum(qh * n_h)
        yield
        m_t = jnp.maximum(inter, row_max)
        w_intra = jnp.exp(dmat - m_t[:, :c])
        w_inter = jnp.exp(inter - m_t)
        s = s_raw * w_intra
        s_v = _bdot(s, vh)
        s_sum = lsum(s)
        m_new = m_t[c - 1:c, :]
        b_last = b_col[c - 1:c, :]
        w_last = jnp.exp(b_last - b_col + i_col - m_new)
        decay = jnp.exp(b_last + m_prev - m_new)
        kw = w_last * kh
        kw_v = _bdot_tn(kw, vh)
        yield
        num = w_inter * q_c + s_v
        den = w_inter * q_n + s_sum
        hh = num / jnp.maximum(jnp.abs(den), jnp.exp(-m_t))
        c_ref[gi, h] = decay * c_h + kw_v
        n_ref[gi, h:h + 1, :] = decay * n_h + jnp.sum(kw, axis=0, keepdims=True)
        m_ref[gi, h:h + 1, :] = m_new
        hh = _sigmoid(zo_ref[gi, :, sl]) * hh
        out_ref[gi, :, sl] = _head_rms(hh) * gn_ref[:, sl]

    for gi in range(group):
        ext_ref[gi, SUBLANES:SUBLANES + c, 0:HW] = zq_ref[gi]
        ext_ref[gi, SUBLANES:SUBLANES + c, HW:2 * HW] = zk_ref[gi]
        assert CONV_W == 4
        x0 = ext_ref[gi, SUBLANES:SUBLANES + c, :]
        x2 = ext_ref[gi, SUBLANES - 2:SUBLANES - 2 + c, :]
        u_ref[gi, SUBLANES:SUBLANES + c, :] = cw_ref[2:3, :] * x0 + cw_ref[0:1, :] * x2
        u_ref[gi, SUBLANES - 1:SUBLANES, :] = (cw_ref[2:3, :] * ext_ref[gi, SUBLANES - 1:SUBLANES, :]
                                               + cw_ref[0:1, :] * ext_ref[gi, SUBLANES - 3:SUBLANES - 2, :])
        conv = (cb_ref[...] + cw_ref[3:4, :] * x0 + cw_ref[1:2, :] * x2
                + u_ref[gi, SUBLANES - 1:SUBLANES - 1 + c, :])
        ext_ref[gi, 0:SUBLANES, :] = ext_ref[gi, c:c + SUBLANES, :]
        conv_ref[gi] = ext_ref[gi, SUBLANES - tail:SUBLANES, :]
        qk_ref[gi] = conv * _sigmoid(conv)
        gb = zg_ref[gi] + bg
        gates_ref[gi] = jnp.where(lane < HEADS, gb, _log_sigmoid(gb))
    _round_robin([unit(gi, h) for gi in range(group) for h in range(HEADS)])


def _seq_group(b, c):
    rows = 512
    group = max(1, min(b, rows // c, 2 * SUBLANES))
    assert b % group == 0
    return group


def _mlstm(z_main, z_gate, conv_w, conv_b, b_gate, gn_a, conv0, c0, n0, m0, b, l):
    c = _chunk_len(l)
    nc = l // c
    grp = _seq_group(b, c)
    z3 = z_main.reshape(b, l, z_main.shape[-1])
    zspec = lambda col: pl.BlockSpec((grp, c, HW), lambda bi, j: (bi, j, col))
    st = lambda shape: pl.BlockSpec((grp,) + shape, lambda bi, j: (bi,) + (0,) * len(shape))
    m0b = jnp.broadcast_to(m0[:, :, None], (b, HEADS, DH))
    out, conv_new, c_new, n_new, m_new = pl.pallas_call(
        functools.partial(_mlstm_kernel, c=c, group=grp),
        grid=(b // grp, nc),
        in_specs=[zspec(0), zspec(1), zspec(2), zspec(3),
                  pl.BlockSpec((grp, c, LANES), lambda bi, j: (bi, j, 0)),
                  _full((CONV_W, 2 * HW)), _full((1, 2 * HW)), _full((1, LANES)), _full((1, HW)),
                  st((CONV_W - 1, 2 * HW)), st((HEADS, DH, DH)), st((HEADS, DH)), st((HEADS, DH))],
        out_specs=[pl.BlockSpec((grp, c, HW), lambda bi, j: (bi, j, 0)),
                   st((CONV_W - 1, 2 * HW)), st((HEADS, DH, DH)), st((HEADS, DH)), st((HEADS, DH))],
        out_shape=[jax.ShapeDtypeStruct((b, l, HW), F32),
                   jax.ShapeDtypeStruct((b, CONV_W - 1, 2 * HW), F32),
                   jax.ShapeDtypeStruct((b, HEADS, DH, DH), F32),
                   jax.ShapeDtypeStruct((b, HEADS, DH), F32),
                   jax.ShapeDtypeStruct((b, HEADS, DH), F32)],
        scratch_shapes=[pltpu.VMEM((grp, c + SUBLANES, 2 * HW), F32),
                        pltpu.VMEM((grp, c + SUBLANES, 2 * HW), F32),
                        pltpu.VMEM((grp, c, 2 * HW), F32), pltpu.VMEM((grp, c, LANES), F32)],
        compiler_params=_params(2),
        name="mlstm",
    )(z3, z3, z3, z3, z_gate.reshape(b, l, LANES), conv_w, conv_b.reshape(1, -1),
      jnp.pad(b_gate, (0, LANES - 2 * HEADS)).reshape(1, LANES), gn_a.reshape(1, -1),
      conv0, c0, n0, m0b)
    return out.reshape(b * l, HW), conv_new, c_new, n_new, m_new[:, :, 0]


def _rope_table_kernel(inv_ref, sign_ref, cos_ref, sin_ref, *, pos0, rows):
    i = pl.program_id(0)
    pos = (pos0 + i * rows + lax.broadcasted_iota(jnp.int32, (rows, LANES), 0)).astype(F32)
    ang = pos * inv_ref[...]
    cos_ref[...] = jnp.cos(ang)
    sin_ref[...] = jnp.sin(ang) * sign_ref[...]


def _rope_tables(l, pos0):
    half = DH // 2
    inv = ROPE_BASE ** (-jnp.arange(half, dtype=F32) / half)
    inv2 = jnp.concatenate([inv, inv]).reshape(1, DH)
    sign = jnp.concatenate([-jnp.ones((half,), F32), jnp.ones((half,), F32)]).reshape(1, DH)
    rows = min(l, 512)
    assert l % rows == 0
    return pl.pallas_call(
        functools.partial(_rope_table_kernel, pos0=pos0, rows=rows),
        grid=(l // rows,),
        in_specs=[_full((1, DH)), _full((1, DH))],
        out_specs=[pl.BlockSpec((rows, DH), lambda i: (i, 0))] * 2,
        out_shape=[jax.ShapeDtypeStruct((l, DH), F32)] * 2,
        compiler_params=_params(1),
        name="rope_table",
    )(inv2, sign)


def _ret_kernel(zq_ref, zk_ref, zv_ref, zg_ref, cos_ref, sin_ref, s0_ref, out_ref, s_ref,
                *, c, group):
    j = pl.program_id(1)

    @pl.when(j == 0)
    def _():
        s_ref[...] = s0_ref[...]

    cosf = cos_ref[...]
    sinf = sin_ref[...]
    ti = lax.broadcasted_iota(jnp.int32, (c, c), 0)
    si = lax.broadcasted_iota(jnp.int32, (c, c), 1)
    rel = jnp.maximum(ti - si, 0).astype(F32)
    tcol = lax.broadcasted_iota(jnp.int32, (c, 1), 0).astype(F32)

    def rope(x):
        return x * cosf + pltpu.roll(x, DH // 2, axis=1) * sinf

    def unit(gi, h, decay, inter, kdecay, cdecay):
        sl = slice(h * DH, (h + 1) * DH)
        qr = rope(zq_ref[gi, :, sl])
        kr = rope(zk_ref[gi, :, sl]) * (DH ** -0.5)
        yield
        vh = zv_ref[gi, :, sl]
        s_h = s_ref[gi, h]
        qk = _bdot_nt(qr, kr)
        q_s = _bdot(qr, s_h)
        k_v = _bdot_tn(kr * kdecay, vh)
        yield
        o = q_s * inter + _bdot(qk * decay, vh)
        s_ref[gi, h] = cdecay * s_h + k_v
        yield
        gate = zg_ref[gi, :, sl]
        out_ref[gi, :, sl] = _head_rms(o) * (gate * _sigmoid(gate))

    units = []
    for h in range(HEADS):
        lg = math.log1p(-(2.0 ** (-5.0 - h)))
        decay = jnp.where(ti >= si, jnp.exp(rel * lg), 0.0)
        inter = jnp.exp((tcol + 1.0) * lg)
        kdecay = jnp.exp((c - 1.0 - tcol) * lg)
        cdecay = math.exp(c * lg)
        units += [unit(gi, h, decay, inter, kdecay, cdecay) for gi in range(group)]
    _round_robin(units)


def _retention(z_main, cos_t, sin_t, s0, b, l):
    c = _chunk_len(l)
    nc = l // c
    grp = _seq_group(b, c)
    z3 = z_main.reshape(b, l, z_main.shape[-1])
    zspec = lambda col: pl.BlockSpec((grp, c, HW), lambda bi, j: (bi, j, col))
    st = pl.BlockSpec((grp, HEADS, DH, DH), lambda bi, j: (bi, 0, 0, 0))
    tab = pl.BlockSpec((c, DH), lambda bi, j: (j, 0))
    out, s_new = pl.pallas_call(
        functools.partial(_ret_kernel, c=c, group=grp),
        grid=(b // grp, nc),
        in_specs=[zspec(4), zspec(5), zspec(6), zspec(7), tab, tab, st],
        out_specs=[pl.BlockSpec((grp, c, HW), lambda bi, j: (bi, j, 0)), st],
        out_shape=[jax.ShapeDtypeStruct((b, l, HW), F32),
                   jax.ShapeDtypeStruct((b, HEADS, DH, DH), F32)],
        compiler_params=_params(2),
        name="retention",
    )(z3, z3, z3, z3, cos_t, sin_t, s0)
    return out.reshape(b * l, HW), s_new


def _hgrn_kernel(zq_ref, zf_ref, zi_ref, zg_ref, lbl_ref, gn_ref, s0_ref, out_ref, s_ref,
                 kk_ref, bcum_ref, *, c, sc, layer, group):
    j = pl.program_id(1)

    @pl.when(j == 0)
    def _():
        s_ref[...] = s0_ref[...]

    lbl = lbl_ref[...]
    e = jnp.exp(lbl - jnp.max(lbl, axis=0, keepdims=True))
    sm = e / jnp.sum(e, axis=0, keepdims=True)
    cum = sm[0:1, :]
    for r in range(1, layer + 1):
        cum = cum + sm[r:r + 1, :]
    lb = cum - sm[0:1, :]

    oml = 1.0 - lb
    ri = lax.broadcasted_iota(jnp.int32, (c, c), 0)
    ci = lax.broadcasted_iota(jnp.int32, (c, c), 1)
    tril = jnp.where(ri >= ci, 1.0, 0.0).astype(BF16)
    row_s = lax.broadcasted_iota(jnp.int32, (sc, sc), 0)
    lane_s = lax.broadcasted_iota(jnp.int32, (sc, sc), 1)
    causal_col = jnp.where(row_s >= lane_s, lane_s, -1)
    keep = [causal_col == s for s in range(sc)]
    e_r = lax.broadcasted_iota(jnp.int32, (DH, DH), 0)
    e_c = lax.broadcasted_iota(jnp.int32, (DH, DH), 1)
    eye = e_r == e_c

    def unit(gi, h):
        sl = slice(h * DH, (h + 1) * DH)
        bh = bcum_ref[gi, :, sl]
        qh = zq_ref[gi, :, sl] * (DH ** -0.5)
        kh = kk_ref[gi, :, sl]
        vh = zi_ref[gi, :, sl]
        s_h = s_ref[gi, h]
        b_last = bh[c - 1:c, :]
        o_inter = _bdot(qh * jnp.exp(bh), s_h)
        k_v = _bdot_tn(kh * jnp.exp(b_last - bh), vh)
        dec_col = jnp.sum(jnp.where(eye, jnp.exp(b_last), 0.0), axis=1, keepdims=True)
        yield
        s_ref[gi, h] = dec_col * s_h + k_v
        blocks = []
        for blk in range(c // sc):
            r0 = blk * sc
            b_i = bh[r0:r0 + sc]
            q_i = qh[r0:r0 + sc]
            k_i = kh[r0:r0 + sc]
            v_i = vh[r0:r0 + sc]
            att_prev = None
            if blk > 0:
                ref_row = bh[r0 - 1:r0, :]
                a_i = q_i * jnp.exp(b_i - ref_row)
                k_prev = kh[0:r0] * jnp.exp(ref_row - bh[0:r0])
                att_prev = _bdot_nt(a_i, k_prev)
            b2_i = b_i * LOG2_E
            c2_i = b2_i - jnp.log2(k_i)
            cols = [jnp.sum(q_i * jnp.exp2(b2_i - c2_i[s:s + 1, :]), axis=1, keepdims=True)
                    for s in range(sc)]
            yield
            att = jnp.zeros((sc, sc), F32)
            for s in range(sc):
                att = jnp.where(keep[s], cols[s], att)
            o_i = _bdot(att, v_i)
            if att_prev is not None:
                o_i = o_i + _bdot(att_prev, vh[0:r0])
            blocks.append(o_i)
        yield
        o = o_inter + (jnp.concatenate(blocks, axis=0) if len(blocks) > 1 else blocks[0])
        gate = zg_ref[gi, :, sl]
        out_ref[gi, :, sl] = _head_rms(o) * gn_ref[:, sl] * (gate * _sigmoid(gate))

    for gi in range(group):
        zf = zf_ref[gi]
        ez = jnp.exp(-jnp.abs(zf))
        big = 1.0 / (1.0 + ez)
        small = ez * big
        pos = zf >= 0.0
        logf = jnp.log(lb + oml * jnp.where(pos, big, small))
        kk_ref[gi] = oml * jnp.where(pos, small, big)

        p0 = logf.astype(BF16)
        r1 = logf - p0.astype(F32)
        p1 = r1.astype(BF16)
        p2 = (r1 - p1.astype(F32)).astype(BF16)
        bcum_ref[gi] = (jnp.dot(tril, p0, preferred_element_type=F32)
                        + jnp.dot(tril, p1, preferred_element_type=F32)
                        + jnp.dot(tril, p2, preferred_element_type=F32))
    _round_robin([unit(gi, h) for gi in range(group) for h in range(HEADS)])


def _hgrn(z_cd, lb_logits, gn_c, s0, b, l, layer):
    c = _chunk_len(l)
    sc = min(c, SUBLANES)
    nc = l // c
    grp = _seq_group(b, c)
    depth = lb_logits.shape[0]
    z3 = z_cd.reshape(b, l, z_cd.shape[-1])
    zspec = lambda col: pl.BlockSpec((grp, c, HW), lambda bi, j: (bi, j, col))
    st = pl.BlockSpec((grp, HEADS, DH, DH), lambda bi, j: (bi, 0, 0, 0))
    out, s_new = pl.pallas_call(
        functools.partial(_hgrn_kernel, c=c, sc=sc, layer=layer, group=grp),
        grid=(b // grp, nc),
        in_specs=[zspec(0), zspec(1), zspec(2), zspec(3), _full((depth, HW)), _full((1, HW)), st],
        out_specs=[pl.BlockSpec((grp, c, HW), lambda bi, j: (bi, j, 0)), st],
        out_shape=[jax.ShapeDtypeStruct((b, l, HW), F32),
                   jax.ShapeDtypeStruct((b, HEADS, DH, DH), F32)],
        scratch_shapes=[pltpu.VMEM((grp, c, HW), F32), pltpu.VMEM((grp, c, HW), F32)],
        compiler_params=_params(2),
        name="hgrn2",
    )(z3, z3, z3, z3, lb_logits, gn_c.reshape(1, -1), s0)
    return out.reshape(b * l, HW), s_new


def _s5_kernel(u_ref, are_ref, aim_ref, ldt_ref, bre_ref, bim_ref, cre_ref, cim_ref, d_ref,
               wglu_ref, bglu_ref, x0r_ref, x0i_ref,
               s_ref, xr_ref, xi_ref, ar_sc, ai_sc, bbr_sc, bbi_sc, bur_sc, bui_sc, *, ct):
    j = pl.program_id(1)
    ns = are_ref.shape[-1]
    nu = u_ref.shape[-1]
    hs = ns // 2
    hu = nu // 2
    rows = ct * SUBLANES

    @pl.when((pl.program_id(0) == 0) & (j == 0))
    def _():
        a_re = are_ref[...]
        a_im = aim_ref[...]
        dt = jnp.exp(ldt_ref[...])
        mag = jnp.exp(dt * a_re)
        ar = mag * jnp.cos(dt * a_im)
        ai = mag * jnp.sin(dt * a_im)
        ar_sc[...] = jnp.broadcast_to(ar, (SUBLANES, ns))
        ai_sc[...] = jnp.broadcast_to(ai, (SUBLANES, ns))
        den = a_re * a_re + a_im * a_im
        nr = ar - 1.0
        zr = (nr * a_re + ai * a_im) / den
        zi = (ai * a_re - nr * a_im) / den
        for hg in range(2):
            us = slice(hg * hu, (hg + 1) * hu)
            ss = slice(hg * hs, (hg + 1) * hs)
            bbr_sc[us, :] = (zr[:, ss] * bre_ref[us, :] - zi[:, ss] * bim_ref[us, :]).astype(BF16)
            bbi_sc[us, :] = (zr[:, ss] * bim_ref[us, :] + zi[:, ss] * bre_ref[us, :]).astype(BF16)

    @pl.when(j == 0)
    def _():
        xr_ref[...] = x0r_ref[...]
        xi_ref[...] = x0i_ref[...]

    u = u_ref[...].reshape(rows, nu)
    ub = u.astype(BF16)
    for hg in range(2):
        us = slice(hg * hu, (hg + 1) * hu)
        ss = slice(hg * hs, (hg + 1) * hs)
        bur_sc[:, ss] = jnp.dot(ub[:, us], bbr_sc[us, :], preferred_element_type=F32)
        bui_sc[:, ss] = jnp.dot(ub[:, us], bbi_sc[us, :], preferred_element_type=F32)

    lane_chunk = 8 * LANES
    for lc in range(ns // lane_chunk):
        ls = slice(lc * lane_chunk, (lc + 1) * lane_chunk)
        ar = ar_sc[:, ls]
        ai = ai_sc[:, ls]

        def step(t, carry):
            xr, xi = carry
            r0 = pl.multiple_of(t * SUBLANES, SUBLANES)
            nxr = ar * xr - ai * xi + bur_sc[pl.ds(r0, SUBLANES), ls]
            nxi = ar * xi + ai * xr + bui_sc[pl.ds(r0, SUBLANES), ls]
            bur_sc[pl.ds(r0, SUBLANES), ls] = nxr
            bui_sc[pl.ds(r0, SUBLANES), ls] = nxi
            return nxr, nxi

        xr, xi = lax.fori_loop(0, ct, step, (xr_ref[:, ls], xi_ref[:, ls]), unroll=SUBLANES)
        xr_ref[:, ls] = xr
        xi_ref[:, ls] = xi

    ys = []
    for hg in range(2):
        ss = slice(hg * hs, (hg + 1) * hs)
        ys.append(jnp.dot(bur_sc[:, ss].astype(BF16), cre_ref[ss, :], preferred_element_type=F32)
                  - jnp.dot(bui_sc[:, ss].astype(BF16), cim_ref[ss, :], preferred_element_type=F32))
    y = jnp.concatenate(ys, axis=1) + d_ref[...] * u
    a = 0.5 * y * (1.0 + jnp.tanh(math.sqrt(2.0 / math.pi) * (y + 0.044715 * (y * y * y))))
    s = a * _sigmoid(jnp.dot(a.astype(BF16), wglu_ref[...], preferred_element_type=F32) + bglu_ref[...])
    s_ref[...] = s.reshape(ct, SUBLANES, nu)


def _s5_block_diag(bmat, cmat):
    g, p, hgrp = bmat.shape
    gh = g // 2
    eye = jnp.eye(gh, dtype=F32)
    b4 = bmat.reshape(2, gh, p, hgrp)
    bc = jnp.einsum('agph,gk->aghkp', b4, eye).reshape(2 * gh * hgrp, gh * p)
    c4 = cmat.reshape(2, gh, hgrp, p)
    cc = jnp.einsum('aghp,gk->agpkh', c4, eye).reshape(2 * gh * p, gh * hgrp)
    return bc, cc


def _s5_operands(a_re, a_im, log_dt, b_re, b_im, c_re, c_im, d_skip, w_glu, b_glu):
    g, p = a_re.shape
    ns = g * p
    nu = d_skip.shape[-1]
    assert nu == g * S5_GROUP and (g // 2) * S5_GROUP == MXU_DIM
    bre_c, cre_c = _s5_block_diag(b_re, c_re)
    bim_c, cim_c = _s5_block_diag(b_im, c_im)
    ldt = jnp.broadcast_to(log_dt[:, None], (g, p)).reshape(1, ns)
    return (a_re.reshape(1, ns), a_im.reshape(1, ns), ldt, bre_c, bim_c, cre_c.astype(BF16),
            cim_c.astype(BF16), d_skip.reshape(1, nu), w_glu.astype(BF16), b_glu.reshape(1, nu))


def _s5(u_tm, operands, x0r, x0i):
    l, b, nu = u_tm.shape
    g, p = x0r.shape[1:]
    ns = g * p
    assert b % SUBLANES == 0
    ct = _chunk_len(l)
    nct = l // ct
    rows = ct * SUBLANES
    xspec = pl.BlockSpec((SUBLANES, ns), lambda bb, j: (bb, 0))
    s, xr, xi = pl.pallas_call(
        functools.partial(_s5_kernel, ct=ct),
        grid=(b // SUBLANES, nct),
        in_specs=[pl.BlockSpec((ct, SUBLANES, nu), lambda bb, j: (j, bb, 0)),
                  _full((1, ns)), _full((1, ns)), _full((1, ns)),
                  _full((nu, ns // 2)), _full((nu, ns // 2)),
                  _full((ns, nu // 2)), _full((ns, nu // 2)),
                  _full((1, nu)), _full((nu, nu)), _full((1, nu)), xspec, xspec],
        out_specs=[pl.BlockSpec((ct, SUBLANES, nu), lambda bb, j: (j, bb, 0)), xspec, xspec],
        out_shape=[jax.ShapeDtypeStruct((l, b, nu), F32),
                   jax.ShapeDtypeStruct((b, ns), F32), jax.ShapeDtypeStruct((b, ns), F32)],
        scratch_shapes=[pltpu.VMEM((SUBLANES, ns), F32), pltpu.VMEM((SUBLANES, ns), F32),
                        pltpu.VMEM((nu, ns // 2), BF16), pltpu.VMEM((nu, ns // 2), BF16),
                        pltpu.VMEM((rows, ns), F32), pltpu.VMEM((rows, ns), F32)],
        compiler_params=_params(2, VMEM_LIMIT),
        name="s5",
    )(u_tm, *operands, x0r.reshape(b, ns), x0i.reshape(b, ns))
    return s, xr.reshape(b, g, p), xi.reshape(b, g, p)


def _trunk(groups, prm):
    d = groups[0]['x'].shape[-1]
    tm = 512
    info = []
    for grp in groups:
        b, l, _ = grp['x'].shape
        assert l >= CONV_W - 1 and (b * l) % tm == 0
        info.append(dict(b=b, l=l, t=b * l, direct_tm=b == SUBLANES and l % tm == 0))
    hs = [grp['x'].reshape(-1, d) for grp in groups]
    pps = [grp['p'].reshape(grp['p'].shape[0], -1, grp['p'].shape[-1]) for grp in groups]

    zs = _inproj(hs, prm['norm_mix'][0], prm['w_ab'], (8 * HW, LANES), tm, [None] * len(groups))
    ab_new, post_in = [], []
    for grp, inf, h, pp, (z_main, z_gate) in zip(groups, info, hs, pps, zs):
        b, l = inf['b'], inf['l']
        conv0, c0, n0, m0, ret0 = grp['ab_state']
        hm, conv_new, c_new, n_new, m_new = _mlstm(z_main, z_gate, prm['conv_w_ab'][0], prm['conv_b_ab'][0],
                                         prm['b_gate_ab'][0], prm['gn_a'][0], conv0[0], c0[0], n0[0],
                                         m0[0], b, l)
        cos_t, sin_t = _rope_tables(l, grp['pos0'])
        hr, ret_new = _retention(z_main, cos_t, sin_t, ret0[0], b, l)
        ab_new.append((conv_new[None], c_new[None], n_new[None], m_new[None], ret_new[None]))
        post_in.append((h, hm, hr, pp, None))
    hs = _post(post_in, prm['w_out_ab'], prm['norm_ff'][0], prm['w_ff1'], prm['w_ff2'],
               prm['norm_ple'][0], prm['w_ple_gate'], prm['w_ple_proj'], prm['norm_final'],
               layer=0, final=False, tm=tm)

    tml = [(inf['b'], inf['l']) if inf['direct_tm'] else None for inf in info]
    zs = _inproj(hs, prm['norm_mix'][1], prm['w_cd'], (4 * HW, HW), tm, tml)
    cd_new, post_in = [], []
    for grp, inf, h, pp, (z_cd, su) in zip(groups, info, hs, pps, zs):
        b, l, t = inf['b'], inf['l'], inf['t']
        hg0, x0r, x0i = grp['cd_state']
        o, hg_new = _hgrn(z_cd, prm['lb_logits'], prm['gn_c'][0], hg0[0], b, l, layer=1)
        u_tm = (su.reshape(l, b, HW) if inf['direct_tm']
                else jnp.transpose(su.reshape(b, l, HW), (1, 0, 2)))
        s_tm, xr, xi = _s5(u_tm, prm['s5_operands'], x0r[0], x0i[0])
        s_in = (s_tm.reshape(l, b * HW) if inf['direct_tm']
                else jnp.transpose(s_tm, (1, 0, 2)).reshape(t, HW))
        cd_new.append((hg_new[None], xr[None], xi[None]))
        post_in.append((h, o, s_in, pp, (b, l) if inf['direct_tm'] else None))
    ys = _post(post_in, prm['w_out_cd'], prm['norm_ff'][1], prm['w_ff1'], prm['w_ff2'],
               prm['norm_ple'][1], prm['w_ple_gate'], prm['w_ple_proj'], prm['norm_final'],
               layer=1, final=True, tm=tm)
    ys = [y.reshape(grp['x'].shape) for y, grp in zip(ys, groups)]
    return ys, ab_new, cd_new


def kernel(x_prompt, x_sample, state_mlstm_conv, state_mlstm_C, state_mlstm_n, state_mlstm_m, state_ret, state_hgrn, state_s5_re, state_s5_im, p_prompt, p_sample, norm_mix, norm_ff, norm_ple, norm_final, w_in_ab, b_gate_ab, conv_w_ab, conv_b_ab, gn_a, w_out_ab, w_in_cd, lb_logits, gn_c, s5_A_re, s5_A_im, s5_log_dt, s5_B_re, s5_B_im, s5_C_re, s5_C_im, s5_D, w_glu, b_glu, w_out_cd, w_ff1, w_ff2, w_ple_proj, w_ple_gate):
    assert norm_mix.shape[0] == 2, "two layers: (mLSTM || retention), (HGRN2 || S5)"
    w_ab = w_in_ab[0]
    gate0 = 4 * HW
    w_ab = (w_ab[:, :gate0].astype(BF16), w_ab[:, gate0 + 2 * HEADS:].astype(BF16),
            jnp.pad(w_ab[:, gate0:gate0 + 2 * HEADS], ((0, 0), (0, LANES - 2 * HEADS))).astype(BF16))
    prm = dict(norm_mix=norm_mix, norm_ff=norm_ff, norm_ple=norm_ple, norm_final=norm_final,
               w_ab=w_ab, b_gate_ab=b_gate_ab, conv_w_ab=conv_w_ab, conv_b_ab=conv_b_ab,
               gn_a=gn_a, w_out_ab=w_out_ab[0].astype(BF16), w_cd=(w_in_cd[0].astype(BF16),),
               lb_logits=lb_logits, gn_c=gn_c,
               s5_operands=_s5_operands(s5_A_re[0], s5_A_im[0], s5_log_dt[0], s5_B_re[0], s5_B_im[0],
                                        s5_C_re[0], s5_C_im[0], s5_D[0], w_glu[0], b_glu[0]),
               w_out_cd=w_out_cd[0].astype(BF16), w_ff1=w_ff1.astype(BF16), w_ff2=w_ff2.astype(BF16),
               w_ple_proj=w_ple_proj.astype(BF16), w_ple_gate=w_ple_gate.astype(BF16))

    bp, lp, _ = x_prompt.shape
    z = lambda *s: jnp.zeros(s, F32)
    zero_ab = (z(1, bp, CONV_W - 1, 2 * HW), z(1, bp, HEADS, DH, DH), z(1, bp, HEADS, DH),
               z(1, bp, HEADS), z(1, bp, HEADS, DH, DH))
    zero_cd = (z(1, bp, HEADS, DH, DH),) + (z(*((1, bp) + s5_A_re.shape[1:])),) * 2
    groups = [dict(x=x_prompt, p=p_prompt, pos0=0, ab_state=zero_ab, cd_state=zero_cd),
              dict(x=x_sample, p=p_sample, pos0=PAST_LEN,
                   ab_state=(state_mlstm_conv, state_mlstm_C, state_mlstm_n, state_mlstm_m, state_ret),
                   cd_state=(state_hgrn, state_s5_re, state_s5_im))]
    (y_p, y_s), (ab_p, ab_s), (cd_p, cd_s) = _trunk(groups, prm)
    return (y_p, y_s,
            ab_p[0], ab_s[0], ab_p[1], ab_s[1], ab_p[2], ab_s[2], ab_p[3], ab_s[3], ab_p[4], ab_s[4],
            cd_p[0], cd_s[0], cd_p[1], cd_s[1], cd_p[2], cd_s[2])
```

```python
import functools
import math

import jax
import jax.numpy as jnp
from jax import lax
from jax.experimental import pallas as pl
from jax.experimental.pallas import tpu as pltpu

F32 = jnp.float32
BF16 = jnp.bfloat16

EPS = 1e-6
NEG = -1e30
LOG2_E = math.log2(math.e)
ROPE_BASE = 10000.0
PAST_LEN = 16384
CHUNK = 64
HEADS = 4
DH = 128
HW = HEADS * DH
CONV_W = 4
S5_GROUP = 16
SUBLANES = 8
LANES = 128
MXU_DIM = 256
VMEM_LIMIT = 56 * 1024 * 1024


def _params(n_axes, vmem=None):
    return pltpu.CompilerParams(dimension_semantics=("arbitrary",) * n_axes, vmem_limit_bytes=vmem)


def _full(shape):
    return pl.BlockSpec(shape, lambda *_: (0,) * len(shape))


def _bdot(a, b):
    return jnp.dot(a.astype(BF16), b.astype(BF16), preferred_element_type=F32)


def _bdot_nt(a, b):
    return lax.dot_general(a.astype(BF16), b.astype(BF16), (((1,), (1,)), ((), ())),
                           preferred_element_type=F32)


def _bdot_tn(a, b):
    return lax.dot_general(a.astype(BF16), b.astype(BF16), (((0,), (0,)), ((), ())),
                           preferred_element_type=F32)


def _sigmoid(x):
    return 1.0 / (1.0 + jnp.exp(-x))


def _log_sigmoid(x):
    return jnp.minimum(x, 0.0) - jnp.log(1.0 + jnp.exp(-jnp.abs(x)))


def _rms(x, g):
    return x * lax.rsqrt(jnp.mean(x * x, axis=-1, keepdims=True) + EPS) * g


def _head_rms(x):
    return x * lax.rsqrt(jnp.mean(x * x, axis=-1, keepdims=True) + EPS)


def _chunk_len(length):
    return CHUNK if length % CHUNK == 0 else length


def _round_robin(gens):
    gens = list(gens)
    while gens:
        alive = []
        for g in gens:
            try:
                next(g)
                alive.append(g)
            except StopIteration:
                pass
        gens = alive


def _step_ranges(counts):
    starts = [0]
    for n in counts:
        starts.append(starts[-1] + n)
    return starts


def _local(i, start, count):
    return jnp.clip(i - start, 0, count - 1)


def _inproj_kernel(*refs, n_groups, n_w, n_out, starts):
    x_refs = refs[:n_groups]
    g_ref = refs[n_groups]
    w_refs = refs[n_groups + 1:n_groups + 1 + n_w]
    out_refs = refs[n_groups + 1 + n_w:]
    i = pl.program_id(0)
    w_starts = _step_ranges([w.shape[1] for w in w_refs])

    def w_cols(c0, nn):
        for w_ref, s0, s1 in zip(w_refs, w_starts[:-1], w_starts[1:]):
            if s0 <= c0 and c0 + nn <= s1:
                return w_ref[:, c0 - s0:c0 - s0 + nn]
        raise ValueError("output column chunk straddles two weight parts")

    def run(x_ref, outs):
        hn = _rms(x_ref[...], g_ref[...]).astype(BF16)
        off = 0
        for o_ref in outs:
            n = o_ref.shape[-1]
            for n0 in range(0, n, HW):
                nn = min(HW, n - n0)
                o_ref[:, n0:n0 + nn] = jnp.dot(hn, w_cols(off + n0, nn), preferred_element_type=F32)
            off += n

    for gi in range(n_groups):
        @pl.when((i >= starts[gi]) & (i < starts[gi + 1]))
        def _(gi=gi):
            run(x_refs[gi], out_refs[gi * n_out:(gi + 1) * n_out])


def _inproj(hs, g, ws, widths, tm, time_major_last):
    d = hs[0].shape[1]
    assert sum(widths) == sum(w.shape[1] for w in ws)
    counts = [h.shape[0] // tm for h in hs]
    starts = _step_ranges(counts)
    in_specs, out_specs, out_shape = [], [], []
    for k, h in enumerate(hs):
        t = h.shape[0]
        assert t % tm == 0
        loc = functools.partial(_local, start=starts[k], count=counts[k])
        in_specs.append(pl.BlockSpec((tm, d), lambda i, loc=loc: (loc(i), 0)))
        for wi, wd in enumerate(widths):
            if wi == len(widths) - 1 and time_major_last[k] is not None:
                b, l = time_major_last[k]
                assert l % tm == 0
                nl = l // tm
                out_shape.append(jax.ShapeDtypeStruct((l, b * wd), F32))
                out_specs.append(pl.BlockSpec((tm, wd), lambda i, loc=loc, nl=nl: (loc(i) % nl, loc(i) // nl)))
            else:
                out_shape.append(jax.ShapeDtypeStruct((t, wd), F32))
                out_specs.append(pl.BlockSpec((tm, wd), lambda i, loc=loc: (loc(i), 0)))
    outs = pl.pallas_call(
        functools.partial(_inproj_kernel, n_groups=len(hs), n_w=len(ws), n_out=len(widths),
                          starts=tuple(starts)),
        grid=(starts[-1],),
        in_specs=in_specs + [_full((1, d))] + [_full(w.shape) for w in ws],
        out_specs=out_specs,
        out_shape=out_shape,
        compiler_params=_params(1, VMEM_LIMIT),
        name="inproj",
    )(*hs, g.reshape(1, d), *ws)
    nw = len(widths)
    return [outs[k * nw:(k + 1) * nw] for k in range(len(hs))]


def _post_kernel(*refs, n_groups, starts, final, ff_chunk):
    grp_in = [refs[4 * k:4 * k + 4] for k in range(n_groups)]
    (wo_ref, gff_ref, w1_ref, w2_ref, gple_ref, wg_ref, wp_ref,
     gfin_ref) = refs[4 * n_groups:4 * n_groups + 8]
    out_refs = refs[4 * n_groups + 8:]
    i = pl.program_id(0)

    def run(h_ref, ma_ref, mb_ref, p_ref, o_ref):
        half = ma_ref.shape[-1]
        h = h_ref[...]
        h = h + (jnp.dot(ma_ref[...].astype(BF16), wo_ref[0:half, :], preferred_element_type=F32)
                 + jnp.dot(mb_ref[...].astype(BF16), wo_ref[half:2 * half, :],
                           preferred_element_type=F32))
        hn = _rms(h, gff_ref[...]).astype(BF16)
        d_ff = w1_ref.shape[1]
        acc = jnp.zeros_like(h)
        for f0 in range(0, d_ff, ff_chunk):
            a = jnp.dot(hn, w1_ref[:, f0:f0 + ff_chunk], preferred_element_type=F32)
            a = jnp.square(jnp.maximum(a, 0.0))
            acc = acc + jnp.dot(a.astype(BF16), w2_ref[f0:f0 + ff_chunk, :], preferred_element_type=F32)
        h = h + acc
        gate = _sigmoid(jnp.dot(_rms(h, gple_ref[...]).astype(BF16), wg_ref[...],
                                preferred_element_type=F32))
        h = h + gate * jnp.dot(p_ref[...].astype(BF16), wp_ref[...], preferred_element_type=F32)
        o_ref[...] = _rms(h, gfin_ref[...]) if final else h

    for k in range(n_groups):
        @pl.when((i >= starts[k]) & (i < starts[k + 1]))
        def _(k=k):
            run(*grp_in[k], out_refs[k])


def _post(groups, wo, gff, w1, w2, gple, wg, wp, gfin, *, layer, final, tm):
    d = groups[0][0].shape[1]
    d_ff = w1.shape[-1]
    counts = [grp[0].shape[0] // tm for grp in groups]
    starts = _step_ranges(counts)
    in_specs, out_specs, out_shape, operands = [], [], [], []
    for k, (h, mix_a, mix_b, p, b_time_major) in enumerate(groups):
        t = h.shape[0]
        half = mix_a.shape[-1]
        pd = p.shape[-1]
        assert t % tm == 0
        loc = functools.partial(_local, start=starts[k], count=counts[k])
        row = lambda i, loc=loc: (loc(i), 0)
        mb_spec = pl.BlockSpec((tm, half), row)
        if b_time_major is not None:
            _, l = b_time_major
            assert l % tm == 0
            nl = l // tm
            mb_spec = pl.BlockSpec((tm, half), lambda i, loc=loc, nl=nl: (loc(i) % nl, loc(i) // nl))
        in_specs += [pl.BlockSpec((tm, d), row), pl.BlockSpec((tm, half), row), mb_spec,
                     pl.BlockSpec((None, tm, pd), lambda i, loc=loc: (layer, loc(i), 0))]
        operands += [h, mix_a, mix_b, p]
        out_specs.append(pl.BlockSpec((tm, d), row))
        out_shape.append(jax.ShapeDtypeStruct((t, d), F32))
    pd = groups[0][3].shape[-1]
    lw = lambda r, cdim: pl.BlockSpec((None, r, cdim), lambda i: (layer, 0, 0))
    return pl.pallas_call(
        functools.partial(_post_kernel, n_groups=len(groups), starts=tuple(starts), final=final,
                          ff_chunk=1024),
        grid=(starts[-1],),
        in_specs=in_specs + [_full((d, d)), _full((1, d)), lw(d, d_ff), lw(d_ff, d), _full((1, d)),
                             lw(d, d), lw(pd, d), _full((1, d))],
        out_specs=out_specs,
        out_shape=out_shape,
        compiler_params=_params(1, VMEM_LIMIT),
        name="post",
    )(*operands, wo, gff.reshape(1, d), w1, w2, gple.reshape(1, d), wg, wp, gfin.reshape(1, d))


def _mlstm_kernel(zq_ref, zk_ref, zv_ref, zo_ref, zg_ref, cw_ref, cb_ref, bg_ref, gn_ref,
                  conv0_ref, c0_ref, n0_ref, m0_ref,
                  out_ref, conv_ref, c_ref, n_ref, m_ref, ext_ref, u_ref, qk_ref, gates_ref,
                  *, c, group):
    j = pl.program_id(1)
    tail = CONV_W - 1

    @pl.when(j == 0)
    def _():
        c_ref[...] = c0_ref[...]
        n_ref[...] = n0_ref[...]
        m_ref[...] = m0_ref[...]
        for gi in range(group):
            ext_ref[gi, 0:SUBLANES, :] = jnp.zeros((SUBLANES, 2 * HW), F32)
            ext_ref[gi, SUBLANES - tail:SUBLANES, :] = conv0_ref[gi]

    ri = lax.broadcasted_iota(jnp.int32, (c, c), 0)
    ci = lax.broadcasted_iota(jnp.int32, (c, c), 1)
    eye = ri == ci
    tril = ri >= ci
    lane = lax.broadcasted_iota(jnp.int32, (c, LANES), 1)
    bg = bg_ref[...]

    def lsum(x):
        return jnp.broadcast_to(jnp.sum(x, axis=1, keepdims=True), (c, DH))

    def unit(gi, h):
        sl = slice(h * DH, (h + 1) * DH)
        gates = gates_ref[gi]
        i_col = lsum(jnp.where(lane == h, gates, 0.0))
        f_col = lsum(jnp.where(lane == HEADS + h, gates, 0.0))
        yield
        b_row = jnp.sum(jnp.where(ri <= ci, f_col[:, :c], 0.0), axis=0, keepdims=True)
        b_col = lsum(jnp.where(eye, b_row, 0.0))
        i_row = jnp.sum(jnp.where(eye, i_col[:, :c], 0.0), axis=0, keepdims=True)
        yield
        m_prev = m_ref[gi, h:h + 1, :]
        dmat = jnp.where(tril, b_col[:, :c] - b_row + i_row, NEG)
        inter = b_col + m_prev
        row_max = jnp.broadcast_to(jnp.max(dmat, axis=1, keepdims=True), (c, DH))
        qh = qk_ref[gi, :, sl]
        kh = qk_ref[gi, :, HW + h * DH:HW + (h + 1) * DH] * (DH ** -0.5)
        vh = zv_ref[gi, :, sl]
        c_h = c_ref[gi, h]
        n_h = n_ref[gi, h:h + 1, :]
        s_raw = _bdot_nt(qh, kh)
        q_c = _bdot(qh, c_h)
        q_n = lsum(qh * n_h)
        yield
        m_t = jnp.maximum(inter, row_max)
        w_intra = jnp.exp(dmat - m_t[:, :c])
        w_inter = jnp.exp(inter - m_t)
        s = s_raw * w_intra
        s_v = _bdot(s, vh)
        s_sum = lsum(s)
        m_new = m_t[c - 1:c, :]
        b_last = b_col[c - 1:c, :]
        w_last = jnp.exp(b_last - b_col + i_col - m_new)
        decay = jnp.exp(b_last + m_prev - m_new)
        kw = w_last * kh
        kw_v = _bdot_tn(kw, vh)
        yield
        num = w_inter * q_c + s_v
        den = w_inter * q_n + s_sum
        hh = num / jnp.maximum(jnp.abs(den), jnp.exp(-m_t))
        c_ref[gi, h] = decay * c_h + kw_v
        n_ref[gi, h:h + 1, :] = decay * n_h + jnp.sum(kw, axis=0, keepdims=True)
        m_ref[gi, h:h + 1, :] = m_new
        hh = _sigmoid(zo_ref[gi, :, sl]) * hh
        out_ref[gi, :, sl] = _head_rms(hh) * gn_ref[:, sl]

    for gi in range(group):
        ext_ref[gi, SUBLANES:SUBLANES + c, 0:HW] = zq_ref[gi]
        ext_ref[gi, SUBLANES:SUBLANES + c, HW:2 * HW] = zk_ref[gi]
        assert CONV_W == 4
        x0 = ext_ref[gi, SUBLANES:SUBLANES + c, :]
        x2 = ext_ref[gi, SUBLANES - 2:SUBLANES - 2 + c, :]
        u_ref[gi, SUBLANES:SUBLANES + c, :] = cw_ref[2:3, :] * x0 + cw_ref[0:1, :] * x2
        u_ref[gi, SUBLANES - 1:SUBLANES, :] = (cw_ref[2:3, :] * ext_ref[gi, SUBLANES - 1:SUBLANES, :]
                                               + cw_ref[0:1, :] * ext_ref[gi, SUBLANES - 3:SUBLANES - 2, :])
        conv = (cb_ref[...] + cw_ref[3:4, :] * x0 + cw_ref[1:2, :] * x2
                + u_ref[gi, SUBLANES - 1:SUBLANES - 1 + c, :])
        ext_ref[gi, 0:SUBLANES, :] = ext_ref[gi, c:c + SUBLANES, :]
        conv_ref[gi] = ext_ref[gi, SUBLANES - tail:SUBLANES, :]
        qk_ref[gi] = conv * _sigmoid(conv)
        gb = zg_ref[gi] + bg
        gates_ref[gi] = jnp.where(lane < HEADS, gb, _log_sigmoid(gb))
    _round_robin([unit(gi, h) for gi in range(group) for h in range(HEADS)])


def _seq_group(b, c):
    rows = 512
    group = max(1, min(b, rows // c, 2 * SUBLANES))
    assert b % group == 0
    return group


def _mlstm(z_main, z_gate, conv_w, conv_b, b_gate, gn_a, conv0, c0, n0, m0, b, l):
    c = _chunk_len(l)
    nc = l // c
    grp = _seq_group(b, c)
    z3 = z_main.reshape(b, l, z_main.shape[-1])
    zspec = lambda col: pl.BlockSpec((grp, c, HW), lambda bi, j: (bi, j, col))
    st = lambda shape: pl.BlockSpec((grp,) + shape, lambda bi, j: (bi,) + (0,) * len(shape))
    m0b = jnp.broadcast_to(m0[:, :, None], (b, HEADS, DH))
    out, conv_new, c_new, n_new, m_new = pl.pallas_call(
        functools.partial(_mlstm_kernel, c=c, group=grp),
        grid=(b // grp, nc),
        in_specs=[zspec(0), zspec(1), zspec(2), zspec(3),
                  pl.BlockSpec((grp, c, LANES), lambda bi, j: (bi, j, 0)),
                  _full((CONV_W, 2 * HW)), _full((1, 2 * HW)), _full((1, LANES)), _full((1, HW)),
                  st((CONV_W - 1, 2 * HW)), st((HEADS, DH, DH)), st((HEADS, DH)), st((HEADS, DH))],
        out_specs=[pl.BlockSpec((grp, c, HW), lambda bi, j: (bi, j, 0)),
                   st((CONV_W - 1, 2 * HW)), st((HEADS, DH, DH)), st((HEADS, DH)), st((HEADS, DH))],
        out_shape=[jax.ShapeDtypeStruct((b, l, HW), F32),
                   jax.ShapeDtypeStruct((b, CONV_W - 1, 2 * HW), F32),
                   jax.ShapeDtypeStruct((b, HEADS, DH, DH), F32),
                   jax.ShapeDtypeStruct((b, HEADS, DH), F32),
                   jax.ShapeDtypeStruct((b, HEADS, DH), F32)],
        scratch_shapes=[pltpu.VMEM((grp, c + SUBLANES, 2 * HW), F32),
                        pltpu.VMEM((grp, c + SUBLANES, 2 * HW), F32),
                        pltpu.VMEM((grp, c, 2 * HW), F32), pltpu.VMEM((grp, c, LANES), F32)],
        compiler_params=_params(2),
        name="mlstm",
    )(z3, z3, z3, z3, z_gate.reshape(b, l, LANES), conv_w, conv_b.reshape(1, -1),
      jnp.pad(b_gate, (0, LANES - 2 * HEADS)).reshape(1, LANES), gn_a.reshape(1, -1),
      conv0, c0, n0, m0b)
    return out.reshape(b * l, HW), conv_new, c_new, n_new, m_new[:, :, 0]


def _rope_table_kernel(inv_ref, sign_ref, cos_ref, sin_ref, *, pos0, rows):
    i = pl.program_id(0)
    pos = (pos0 + i * rows + lax.broadcasted_iota(jnp.int32, (rows, LANES), 0)).astype(F32)
    ang = pos * inv_ref[...]
    cos_ref[...] = jnp.cos(ang)
    sin_ref[...] = jnp.sin(ang) * sign_ref[...]


def _rope_tables(l, pos0):
    half = DH // 2
    inv = ROPE_BASE ** (-jnp.arange(half, dtype=F32) / half)
    inv2 = jnp.concatenate([inv, inv]).reshape(1, DH)
    sign = jnp.concatenate([-jnp.ones((half,), F32), jnp.ones((half,), F32)]).reshape(1, DH)
    rows = min(l, 512)
    assert l % rows == 0
    return pl.pallas_call(
        functools.partial(_rope_table_kernel, pos0=pos0, rows=rows),
        grid=(l // rows,),
        in_specs=[_full((1, DH)), _full((1, DH))],
        out_specs=[pl.BlockSpec((rows, DH), lambda i: (i, 0))] * 2,
        out_shape=[jax.ShapeDtypeStruct((l, DH), F32)] * 2,
        compiler_params=_params(1),
        name="rope_table",
    )(inv2, sign)


def _ret_kernel(zq_ref, zk_ref, zv_ref, zg_ref, cos_ref, sin_ref, s0_ref, out_ref, s_ref,
                *, c, group):
    j = pl.program_id(1)

    @pl.when(j == 0)
    def _():
        s_ref[...] = s0_ref[...]

    cosf = cos_ref[...]
    sinf = sin_ref[...]
    ti = lax.broadcasted_iota(jnp.int32, (c, c), 0)
    si = lax.broadcasted_iota(jnp.int32, (c, c), 1)
    rel = jnp.maximum(ti - si, 0).astype(F32)
    tcol = lax.broadcasted_iota(jnp.int32, (c, 1), 0).astype(F32)

    def rope(x):
        return x * cosf + pltpu.roll(x, DH // 2, axis=1) * sinf

    def unit(gi, h, decay, inter, kdecay, cdecay):
        sl = slice(h * DH, (h + 1) * DH)
        qr = rope(zq_ref[gi, :, sl])
        yield
        kr = rope(zk_ref[gi, :, sl]) * (DH ** -0.5)
        yield
        vh = zv_ref[gi, :, sl]
        s_h = s_ref[gi, h]
        qk = _bdot_nt(qr, kr)
        yield
        q_s = _bdot(qr, s_h)
        yield
        k_v = _bdot_tn(kr * kdecay, vh)
        yield
        o = q_s * inter + _bdot(qk * decay, vh)
        s_ref[gi, h] = cdecay * s_h + k_v
        yield
        gate = zg_ref[gi, :, sl]
        out_ref[gi, :, sl] = _head_rms(o) * (gate * _sigmoid(gate))

    units = []
    for h in range(HEADS):
        lg = math.log1p(-(2.0 ** (-5.0 - h)))
        decay = jnp.where(ti >= si, jnp.exp(rel * lg), 0.0)
        inter = jnp.exp((tcol + 1.0) * lg)
        kdecay = jnp.exp((c - 1.0 - tcol) * lg)
        cdecay = math.exp(c * lg)
        units += [unit(gi, h, decay, inter, kdecay, cdecay) for gi in range(group)]
    _round_robin(units)


def _retention(z_main, cos_t, sin_t, s0, b, l):
    c = _chunk_len(l)
    nc = l // c
    grp = _seq_group(b, c)
    z3 = z_main.reshape(b, l, z_main.shape[-1])
    zspec = lambda col: pl.BlockSpec((grp, c, HW), lambda bi, j: (bi, j, col))
    st = pl.BlockSpec((grp, HEADS, DH, DH), lambda bi, j: (bi, 0, 0, 0))
    tab = pl.BlockSpec((c, DH), lambda bi, j: (j, 0))
    out, s_new = pl.pallas_call(
        functools.partial(_ret_kernel, c=c, group=grp),
        grid=(b // grp, nc),
        in_specs=[zspec(4), zspec(5), zspec(6), zspec(7), tab, tab, st],
        out_specs=[pl.BlockSpec((grp, c, HW), lambda bi, j: (bi, j, 0)), st],
        out_shape=[jax.ShapeDtypeStruct((b, l, HW), F32),
                   jax.ShapeDtypeStruct((b, HEADS, DH, DH), F32)],
        compiler_params=_params(2),
        name="retention",
    )(z3, z3, z3, z3, cos_t, sin_t, s0)
    return out.reshape(b * l, HW), s_new


def _hgrn_kernel(zq_ref, zf_ref, zi_ref, zg_ref, lbl_ref, gn_ref, s0_ref, out_ref, s_ref,
                 kk_ref, bcum_ref, *, c, sc, layer, group):
    j = pl.program_id(1)

    @pl.when(j == 0)
    def _():
        s_ref[...] = s0_ref[...]

    lbl = lbl_ref[...]
    e = jnp.exp(lbl - jnp.max(lbl, axis=0, keepdims=True))
    sm = e / jnp.sum(e, axis=0, keepdims=True)
    cum = sm[0:1, :]
    for r in range(1, layer + 1):
        cum = cum + sm[r:r + 1, :]
    lb = cum - sm[0:1, :]

    oml = 1.0 - lb
    ri = lax.broadcasted_iota(jnp.int32, (c, c), 0)
    ci = lax.broadcasted_iota(jnp.int32, (c, c), 1)
    tril = jnp.where(ri >= ci, 1.0, 0.0).astype(BF16)
    row_s = lax.broadcasted_iota(jnp.int32, (sc, sc), 0)
    lane_s = lax.broadcasted_iota(jnp.int32, (sc, sc), 1)
    causal_col = jnp.where(row_s >= lane_s, lane_s, -1)
    keep = [causal_col == s for s in range(sc)]
    e_r = lax.broadcasted_iota(jnp.int32, (DH, DH), 0)
    e_c = lax.broadcasted_iota(jnp.int32, (DH, DH), 1)
    eye = e_r == e_c

    def unit(gi, h):
        sl = slice(h * DH, (h + 1) * DH)
        bh = bcum_ref[gi, :, sl]
        qh = zq_ref[gi, :, sl] * (DH ** -0.5)
        kh = kk_ref[gi, :, sl]
        vh = zi_ref[gi, :, sl]
        s_h = s_ref[gi, h]
        b_last = bh[c - 1:c, :]
        o_inter = _bdot(qh * jnp.exp(bh), s_h)
        yield
        k_v = _bdot_tn(kh * jnp.exp(b_last - bh), vh)
        yield
        dec_col = jnp.sum(jnp.where(eye, jnp.exp(b_last), 0.0), axis=1, keepdims=True)
        yield
        s_ref[gi, h] = dec_col * s_h + k_v
        blocks = []
        for blk in range(c // sc):
            r0 = blk * sc
            b_i = bh[r0:r0 + sc]
            q_i = qh[r0:r0 + sc]
            k_i = kh[r0:r0 + sc]
            v_i = vh[r0:r0 + sc]
            att_prev = None
            if blk > 0:
                ref_row = bh[r0 - 1:r0, :]
                a_i = q_i * jnp.exp(b_i - ref_row)
                k_prev = kh[0:r0] * jnp.exp(ref_row - bh[0:r0])
                att_prev = _bdot_nt(a_i, k_prev)
                yield
            b2_i = b_i * LOG2_E
            c2_i = b2_i - jnp.log2(k_i)
            cols = [jnp.sum(q_i * jnp.exp2(b2_i - c2_i[s:s + 1, :]), axis=1, keepdims=True)
                    for s in range(sc)]
            yield
            att = jnp.zeros((sc, sc), F32)
            for s in range(sc):
                att = jnp.where(keep[s], cols[s], att)
            o_i = _bdot(att, v_i)
            if att_prev is not None:
                o_i = o_i + _bdot(att_prev, vh[0:r0])
            blocks.append(o_i)
        yield
        o = o_inter + (jnp.concatenate(blocks, axis=0) if len(blocks) > 1 else blocks[0])
        gate = zg_ref[gi, :, sl]
        out_ref[gi, :, sl] = _head_rms(o) * gn_ref[:, sl] * (gate * _sigmoid(gate))

    for gi in range(group):
        zf = zf_ref[gi]
        ez = jnp.exp(-jnp.abs(zf))
        big = 1.0 / (1.0 + ez)
        small = ez * big
        pos = zf >= 0.0
        logf = jnp.log(lb + oml * jnp.where(pos, big, small))
        kk_ref[gi] = oml * jnp.where(pos, small, big)

        p0 = logf.astype(BF16)
        r1 = logf - p0.astype(F32)
        p1 = r1.astype(BF16)
        p2 = (r1 - p1.astype(F32)).astype(BF16)
        bcum_ref[gi] = (jnp.dot(tril, p0, preferred_element_type=F32)
                        + jnp.dot(tril, p1, preferred_element_type=F32)
                        + jnp.dot(tril, p2, preferred_element_type=F32))
    _round_robin([unit(gi, h) for gi in range(group) for h in range(HEADS)])


def _hgrn(z_cd, lb_logits, gn_c, s0, b, l, layer):
    c = _chunk_len(l)
    sc = min(c, SUBLANES)
    nc = l // c
    grp = _seq_group(b, c)
    depth = lb_logits.shape[0]
    z3 = z_cd.reshape(b, l, z_cd.shape[-1])
    zspec = lambda col: pl.BlockSpec((grp, c, HW), lambda bi, j: (bi, j, col))
    st = pl.BlockSpec((grp, HEADS, DH, DH), lambda bi, j: (bi, 0, 0, 0))
    out, s_new = pl.pallas_call(
        functools.partial(_hgrn_kernel, c=c, sc=sc, layer=layer, group=grp),
        grid=(b // grp, nc),
        in_specs=[zspec(0), zspec(1), zspec(2), zspec(3), _full((depth, HW)), _full((1, HW)), st],
        out_specs=[pl.BlockSpec((grp, c, HW), lambda bi, j: (bi, j, 0)), st],
        out_shape=[jax.ShapeDtypeStruct((b, l, HW), F32),
                   jax.ShapeDtypeStruct((b, HEADS, DH, DH), F32)],
        scratch_shapes=[pltpu.VMEM((grp, c, HW), F32), pltpu.VMEM((grp, c, HW), F32)],
        compiler_params=_params(2),
        name="hgrn2",
    )(z3, z3, z3, z3, lb_logits, gn_c.reshape(1, -1), s0)
    return out.reshape(b * l, HW), s_new


def _s5_kernel(u_ref, are_ref, aim_ref, ldt_ref, bre_ref, bim_ref, cre_ref, cim_ref, d_ref,
               wglu_ref, bglu_ref, x0r_ref, x0i_ref,
               s_ref, xr_ref, xi_ref, ar_sc, ai_sc, bbr_sc, bbi_sc, bur_sc, bui_sc, *, ct):
    j = pl.program_id(1)
    ns = are_ref.shape[-1]
    nu = u_ref.shape[-1]
    hs = ns // 2
    hu = nu // 2
    rows = ct * SUBLANES

    @pl.when((pl.program_id(0) == 0) & (j == 0))
    def _():
        a_re = are_ref[...]
        a_im = aim_ref[...]
        dt = jnp.exp(ldt_ref[...])
        mag = jnp.exp(dt * a_re)
        ar = mag * jnp.cos(dt * a_im)
        ai = mag * jnp.sin(dt * a_im)
        ar_sc[...] = jnp.broadcast_to(ar, (SUBLANES, ns))
        ai_sc[...] = jnp.broadcast_to(ai, (SUBLANES, ns))
        den = a_re * a_re + a_im * a_im
        nr = ar - 1.0
        zr = (nr * a_re + ai * a_im) / den
        zi = (ai * a_re - nr * a_im) / den
        for hg in range(2):
            us = slice(hg * hu, (hg + 1) * hu)
            ss = slice(hg * hs, (hg + 1) * hs)
            bbr_sc[us, :] = (zr[:, ss] * bre_ref[us, :] - zi[:, ss] * bim_ref[us, :]).astype(BF16)
            bbi_sc[us, :] = (zr[:, ss] * bim_ref[us, :] + zi[:, ss] * bre_ref[us, :]).astype(BF16)

    @pl.when(j == 0)
    def _():
        xr_ref[...] = x0r_ref[...]
        xi_ref[...] = x0i_ref[...]

    u = u_ref[...].reshape(rows, nu)
    ub = u.astype(BF16)
    for hg in range(2):
        us = slice(hg * hu, (hg + 1) * hu)
        ss = slice(hg * hs, (hg + 1) * hs)
        bur_sc[:, ss] = jnp.dot(ub[:, us], bbr_sc[us, :], preferred_element_type=F32)
        bui_sc[:, ss] = jnp.dot(ub[:, us], bbi_sc[us, :], preferred_element_type=F32)

    lane_chunk = 8 * LANES
    for lc in range(ns // lane_chunk):
        ls = slice(lc * lane_chunk, (lc + 1) * lane_chunk)
        ar = ar_sc[:, ls]
        ai = ai_sc[:, ls]

        def step(t, carry):
            xr, xi = carry
            r0 = pl.multiple_of(t * SUBLANES, SUBLANES)
            nxr = ar * xr - ai * xi + bur_sc[pl.ds(r0, SUBLANES), ls]
            nxi = ar * xi + ai * xr + bui_sc[pl.ds(r0, SUBLANES), ls]
            bur_sc[pl.ds(r0, SUBLANES), ls] = nxr
            bui_sc[pl.ds(r0, SUBLANES), ls] = nxi
            return nxr, nxi

        xr, xi = lax.fori_loop(0, ct, step, (xr_ref[:, ls], xi_ref[:, ls]), unroll=SUBLANES)
        xr_ref[:, ls] = xr
        xi_ref[:, ls] = xi

    ys = []
    for hg in range(2):
        ss = slice(hg * hs, (hg + 1) * hs)
        ys.append(jnp.dot(bur_sc[:, ss].astype(BF16), cre_ref[ss, :], preferred_element_type=F32)
                  - jnp.dot(bui_sc[:, ss].astype(BF16), cim_ref[ss, :], preferred_element_type=F32))
    y = jnp.concatenate(ys, axis=1) + d_ref[...] * u
    a = 0.5 * y * (1.0 + jnp.tanh(math.sqrt(2.0 / math.pi) * (y + 0.044715 * (y * y * y))))
    s = a * _sigmoid(jnp.dot(a.astype(BF16), wglu_ref[...], preferred_element_type=F32) + bglu_ref[...])
    s_ref[...] = s.reshape(ct, SUBLANES, nu)


def _s5_block_diag(bmat, cmat):
    g, p, hgrp = bmat.shape
    gh = g // 2
    eye = jnp.eye(gh, dtype=F32)
    b4 = bmat.reshape(2, gh, p, hgrp)
    bc = jnp.einsum('agph,gk->aghkp', b4, eye).reshape(2 * gh * hgrp, gh * p)
    c4 = cmat.reshape(2, gh, hgrp, p)
    cc = jnp.einsum('aghp,gk->agpkh', c4, eye).reshape(2 * gh * p, gh * hgrp)
    return bc, cc


def _s5_operands(a_re, a_im, log_dt, b_re, b_im, c_re, c_im, d_skip, w_glu, b_glu):
    g, p = a_re.shape
    ns = g * p
    nu = d_skip.shape[-1]
    assert nu == g * S5_GROUP and (g // 2) * S5_GROUP == MXU_DIM
    bre_c, cre_c = _s5_block_diag(b_re, c_re)
    bim_c, cim_c = _s5_block_diag(b_im, c_im)
    ldt = jnp.broadcast_to(log_dt[:, None], (g, p)).reshape(1, ns)
    return (a_re.reshape(1, ns), a_im.reshape(1, ns), ldt, bre_c, bim_c, cre_c.astype(BF16),
            cim_c.astype(BF16), d_skip.reshape(1, nu), w_glu.astype(BF16), b_glu.reshape(1, nu))


def _s5(u_tm, operands, x0r, x0i):
    l, b, nu = u_tm.shape
    g, p = x0r.shape[1:]
    ns = g * p
    assert b % SUBLANES == 0
    ct = _chunk_len(l)
    nct = l // ct
    rows = ct * SUBLANES
    xspec = pl.BlockSpec((SUBLANES, ns), lambda bb, j: (bb, 0))
    s, xr, xi = pl.pallas_call(
        functools.partial(_s5_kernel, ct=ct),
        grid=(b // SUBLANES, nct),
        in_specs=[pl.BlockSpec((ct, SUBLANES, nu), lambda bb, j: (j, bb, 0)),
                  _full((1, ns)), _full((1, ns)), _full((1, ns)),
                  _full((nu, ns // 2)), _full((nu, ns // 2)),
                  _full((ns, nu // 2)), _full((ns, nu // 2)),
                  _full((1, nu)), _full((nu, nu)), _full((1, nu)), xspec, xspec],
        out_specs=[pl.BlockSpec((ct, SUBLANES, nu), lambda bb, j: (j, bb, 0)), xspec, xspec],
        out_shape=[jax.ShapeDtypeStruct((l, b, nu), F32),
                   jax.ShapeDtypeStruct((b, ns), F32), jax.ShapeDtypeStruct((b, ns), F32)],
        scratch_shapes=[pltpu.VMEM((SUBLANES, ns), F32), pltpu.VMEM((SUBLANES, ns), F32),
                        pltpu.VMEM((nu, ns // 2), BF16), pltpu.VMEM((nu, ns // 2), BF16),
                        pltpu.VMEM((rows, ns), F32), pltpu.VMEM((rows, ns), F32)],
        compiler_params=_params(2, VMEM_LIMIT),
        name="s5",
    )(u_tm, *operands, x0r.reshape(b, ns), x0i.reshape(b, ns))
    return s, xr.reshape(b, g, p), xi.reshape(b, g, p)


def _trunk(groups, prm):
    d = groups[0]['x'].shape[-1]
    tm = 512
    info = []
    for grp in groups:
        b, l, _ = grp['x'].shape
        assert l >= CONV_W - 1 and (b * l) % tm == 0
        info.append(dict(b=b, l=l, t=b * l, direct_tm=b == SUBLANES and l % tm == 0))
    hs = [grp['x'].reshape(-1, d) for grp in groups]
    pps = [grp['p'].reshape(grp['p'].shape[0], -1, grp['p'].shape[-1]) for grp in groups]

    zs = _inproj(hs, prm['norm_mix'][0], prm['w_ab'], (8 * HW, LANES), tm, [None] * len(groups))
    ab_new, post_in = [], []
    for grp, inf, h, pp, (z_main, z_gate) in zip(groups, info, hs, pps, zs):
        b, l = inf['b'], inf['l']
        conv0, c0, n0, m0, ret0 = grp['ab_state']
        hm, conv_new, c_new, n_new, m_new = _mlstm(z_main, z_gate, prm['conv_w_ab'][0], prm['conv_b_ab'][0],
                                         prm['b_gate_ab'][0], prm['gn_a'][0], conv0[0], c0[0], n0[0],
                                         m0[0], b, l)
        cos_t, sin_t = _rope_tables(l, grp['pos0'])
        hr, ret_new = _retention(z_main, cos_t, sin_t, ret0[0], b, l)
        ab_new.append((conv_new[None], c_new[None], n_new[None], m_new[None], ret_new[None]))
        post_in.append((h, hm, hr, pp, None))
    hs = _post(post_in, prm['w_out_ab'], prm['norm_ff'][0], prm['w_ff1'], prm['w_ff2'],
               prm['norm_ple'][0], prm['w_ple_gate'], prm['w_ple_proj'], prm['norm_final'],
               layer=0, final=False, tm=tm)

    tml = [(inf['b'], inf['l']) if inf['direct_tm'] else None for inf in info]
    zs = _inproj(hs, prm['norm_mix'][1], prm['w_cd'], (4 * HW, HW), tm, tml)
    cd_new, post_in = [], []
    for grp, inf, h, pp, (z_cd, su) in zip(groups, info, hs, pps, zs):
        b, l, t = inf['b'], inf['l'], inf['t']
        hg0, x0r, x0i = grp['cd_state']
        o, hg_new = _hgrn(z_cd, prm['lb_logits'], prm['gn_c'][0], hg0[0], b, l, layer=1)
        u_tm = (su.reshape(l, b, HW) if inf['direct_tm']
                else jnp.transpose(su.reshape(b, l, HW), (1, 0, 2)))
        s_tm, xr, xi = _s5(u_tm, prm['s5_operands'], x0r[0], x0i[0])
        s_in = (s_tm.reshape(l, b * HW) if inf['direct_tm']
                else jnp.transpose(s_tm, (1, 0, 2)).reshape(t, HW))
        cd_new.append((hg_new[None], xr[None], xi[None]))
        post_in.append((h, o, s_in, pp, (b, l) if inf['direct_tm'] else None))
    ys = _post(post_in, prm['w_out_cd'], prm['norm_ff'][1], prm['w_ff1'], prm['w_ff2'],
               prm['norm_ple'][1], prm['w_ple_gate'], prm['w_ple_proj'], prm['norm_final'],
               layer=1, final=True, tm=tm)
    ys = [y.reshape(grp['x'].shape) for y, grp in zip(ys, groups)]
    return ys, ab_new, cd_new


def kernel(x_prompt, x_sample, state_mlstm_conv, state_mlstm_C, state_mlstm_n, state_mlstm_m, state_ret, state_hgrn, state_s5_re, state_s5_im, p_prompt, p_sample, norm_mix, norm_ff, norm_ple, norm_final, w_in_ab, b_gate_ab, conv_w_ab, conv_b_ab, gn_a, w_out_ab, w_in_cd, lb_logits, gn_c, s5_A_re, s5_A_im, s5_log_dt, s5_B_re, s5_B_im, s5_C_re, s5_C_im, s5_D, w_glu, b_glu, w_out_cd, w_ff1, w_ff2, w_ple_proj, w_ple_gate):
    assert norm_mix.shape[0] == 2, "two layers: (mLSTM || retention), (HGRN2 || S5)"
    w_ab = w_in_ab[0]
    gate0 = 4 * HW
    w_ab = (w_ab[:, :gate0].astype(BF16), w_ab[:, gate0 + 2 * HEADS:].astype(BF16),
            jnp.pad(w_ab[:, gate0:gate0 + 2 * HEADS], ((0, 0), (0, LANES - 2 * HEADS))).astype(BF16))
    prm = dict(norm_mix=norm_mix, norm_ff=norm_ff, norm_ple=norm_ple, norm_final=norm_final,
               w_ab=w_ab, b_gate_ab=b_gate_ab, conv_w_ab=conv_w_ab, conv_b_ab=conv_b_ab,
               gn_a=gn_a, w_out_ab=w_out_ab[0].astype(BF16), w_cd=(w_in_cd[0].astype(BF16),),
               lb_logits=lb_logits, gn_c=gn_c,
               s5_operands=_s5_operands(s5_A_re[0], s5_A_im[0], s5_log_dt[0], s5_B_re[0], s5_B_im[0],
                                        s5_C_re[0], s5_C_im[0], s5_D[0], w_glu[0], b_glu[0]),
               w_out_cd=w_out_cd[0].astype(BF16), w_ff1=w_ff1.astype(BF16), w_ff2=w_ff2.astype(BF16),
               w_ple_proj=w_ple_proj.astype(BF16), w_ple_gate=w_ple_gate.astype(BF16))

    bp = x_prompt.shape[0]
    z = lambda *s: jnp.zeros(s, F32)
    zero_ab = (z(1, bp, CONV_W - 1, 2 * HW), z(1, bp, HEADS, DH, DH), z(1, bp, HEADS, DH),
               z(1, bp, HEADS), z(1, bp, HEADS, DH, DH))
    zero_cd = (z(1, bp, HEADS, DH, DH),) + (z(*((1, bp) + s5_A_re.shape[1:])),) * 2
    groups = [dict(x=x_prompt, p=p_prompt, pos0=0, ab_state=zero_ab, cd_state=zero_cd),
              dict(x=x_sample, p=p_sample, pos0=PAST_LEN,
                   ab_state=(state_mlstm_conv, state_mlstm_C, state_mlstm_n, state_mlstm_m, state_ret),
                   cd_state=(state_hgrn, state_s5_re, state_s5_im))]
    (y_p, y_s), (ab_p, ab_s), (cd_p, cd_s) = _trunk(groups, prm)
    return (y_p, y_s,
            ab_p[0], ab_s[0], ab_p[1], ab_s[1], ab_p[2], ab_s[2], ab_p[3], ab_s[3], ab_p[4], ab_s[4],
            cd_p[0], cd_s[0], cd_p[1], cd_s[1], cd_p[2], cd_s[2])
```

```python
import functools
import math

import jax
import jax.numpy as jnp
from jax import lax
from jax.experimental import pallas as pl
from jax.experimental.pallas import tpu as pltpu

F32 = jnp.float32
BF16 = jnp.bfloat16

EPS = 1e-6
NEG = -1e30
LOG2_E = math.log2(math.e)
ROPE_BASE = 10000.0
PAST_LEN = 16384
CHUNK = 64
HEADS = 4
DH = 128
HW = HEADS * DH
CONV_W = 4
S5_GROUP = 16
SUBLANES = 8
LANES = 128
MXU_DIM = 256
VMEM_LIMIT = 56 * 1024 * 1024


def _params(n_axes, vmem=None):
    return pltpu.CompilerParams(dimension_semantics=("arbitrary",) * n_axes, vmem_limit_bytes=vmem)


def _full(shape):
    return pl.BlockSpec(shape, lambda *_: (0,) * len(shape))


def _bdot(a, b):
    return jnp.dot(a.astype(BF16), b.astype(BF16), preferred_element_type=F32)


def _bdot_nt(a, b):
    return lax.dot_general(a.astype(BF16), b.astype(BF16), (((1,), (1,)), ((), ())),
                           preferred_element_type=F32)


def _bdot_tn(a, b):
    return lax.dot_general(a.astype(BF16), b.astype(BF16), (((0,), (0,)), ((), ())),
                           preferred_element_type=F32)


def _sigmoid(x):
    return 1.0 / (1.0 + jnp.exp(-x))


def _log_sigmoid(x):
    return jnp.minimum(x, 0.0) - jnp.log(1.0 + jnp.exp(-jnp.abs(x)))


def _rms(x, g):
    return x * lax.rsqrt(jnp.mean(x * x, axis=-1, keepdims=True) + EPS) * g


def _head_rms(x):
    return x * lax.rsqrt(jnp.mean(x * x, axis=-1, keepdims=True) + EPS)


def _chunk_len(length):
    return CHUNK if length % CHUNK == 0 else length


def _round_robin(gens):
    gens = list(gens)
    while gens:
        alive = []
        for g in gens:
            try:
                next(g)
                alive.append(g)
            except StopIteration:
                pass
        gens = alive


def _step_ranges(counts):
    starts = [0]
    for n in counts:
        starts.append(starts[-1] + n)
    return starts


def _local(i, start, count):
    return jnp.clip(i - start, 0, count - 1)


def _inproj_kernel(*refs, n_groups, n_w, n_out, starts):
    x_refs = refs[:n_groups]
    g_ref = refs[n_groups]
    w_refs = refs[n_groups + 1:n_groups + 1 + n_w]
    out_refs = refs[n_groups + 1 + n_w:]
    i = pl.program_id(0)
    w_starts = _step_ranges([w.shape[1] for w in w_refs])

    def w_cols(c0, nn):
        for w_ref, s0, s1 in zip(w_refs, w_starts[:-1], w_starts[1:]):
            if s0 <= c0 and c0 + nn <= s1:
                return w_ref[:, c0 - s0:c0 - s0 + nn]
        raise ValueError("output column chunk straddles two weight parts")

    def run(x_ref, outs):
        hn = _rms(x_ref[...], g_ref[...]).astype(BF16)
        off = 0
        for o_ref in outs:
            n = o_ref.shape[-1]
            for n0 in range(0, n, HW):
                nn = min(HW, n - n0)
                o_ref[:, n0:n0 + nn] = jnp.dot(hn, w_cols(off + n0, nn), preferred_element_type=F32)
            off += n

    for gi in range(n_groups):
        @pl.when((i >= starts[gi]) & (i < starts[gi + 1]))
        def _(gi=gi):
            run(x_refs[gi], out_refs[gi * n_out:(gi + 1) * n_out])


def _inproj(hs, g, ws, widths, tm, time_major_last):
    d = hs[0].shape[1]
    assert sum(widths) == sum(w.shape[1] for w in ws)
    counts = [h.shape[0] // tm for h in hs]
    starts = _step_ranges(counts)
    in_specs, out_specs, out_shape = [], [], []
    for k, h in enumerate(hs):
        t = h.shape[0]
        assert t % tm == 0
        loc = functools.partial(_local, start=starts[k], count=counts[k])
        in_specs.append(pl.BlockSpec((tm, d), lambda i, loc=loc: (loc(i), 0)))
        for wi, wd in enumerate(widths):
            if wi == len(widths) - 1 and time_major_last[k] is not None:
                b, l = time_major_last[k]
                assert l % tm == 0
                nl = l // tm
                out_shape.append(jax.ShapeDtypeStruct((l, b * wd), F32))
                out_specs.append(pl.BlockSpec((tm, wd), lambda i, loc=loc, nl=nl: (loc(i) % nl, loc(i) // nl)))
            else:
                out_shape.append(jax.ShapeDtypeStruct((t, wd), F32))
                out_specs.append(pl.BlockSpec((tm, wd), lambda i, loc=loc: (loc(i), 0)))
    outs = pl.pallas_call(
        functools.partial(_inproj_kernel, n_groups=len(hs), n_w=len(ws), n_out=len(widths),
                          starts=tuple(starts)),
        grid=(starts[-1],),
        in_specs=in_specs + [_full((1, d))] + [_full(w.shape) for w in ws],
        out_specs=out_specs,
        out_shape=out_shape,
        compiler_params=_params(1, VMEM_LIMIT),
        name="inproj",
    )(*hs, g.reshape(1, d), *ws)
    nw = len(widths)
    return [outs[k * nw:(k + 1) * nw] for k in range(len(hs))]


def _post_kernel(*refs, n_groups, starts, final, ff_chunk):
    grp_in = [refs[4 * k:4 * k + 4] for k in range(n_groups)]
    (wo_ref, gff_ref, w1_ref, w2_ref, gple_ref, wg_ref, wp_ref,
     gfin_ref) = refs[4 * n_groups:4 * n_groups + 8]
    out_refs = refs[4 * n_groups + 8:]
    i = pl.program_id(0)

    def run(h_ref, ma_ref, mb_ref, p_ref, o_ref):
        half = ma_ref.shape[-1]
        h = h_ref[...]
        h = h + (jnp.dot(ma_ref[...].astype(BF16), wo_ref[0:half, :], preferred_element_type=F32)
                 + jnp.dot(mb_ref[...].astype(BF16), wo_ref[half:2 * half, :],
                           preferred_element_type=F32))
        hn = _rms(h, gff_ref[...]).astype(BF16)
        d_ff = w1_ref.shape[1]
        acc = jnp.zeros_like(h)
        for f0 in range(0, d_ff, ff_chunk):
            a = jnp.dot(hn, w1_ref[:, f0:f0 + ff_chunk], preferred_element_type=F32)
            a = jnp.square(jnp.maximum(a, 0.0))
            acc = acc + jnp.dot(a.astype(BF16), w2_ref[f0:f0 + ff_chunk, :], preferred_element_type=F32)
        h = h + acc
        gate = _sigmoid(jnp.dot(_rms(h, gple_ref[...]).astype(BF16), wg_ref[...],
                                preferred_element_type=F32))
        h = h + gate * jnp.dot(p_ref[...].astype(BF16), wp_ref[...], preferred_element_type=F32)
        o_ref[...] = _rms(h, gfin_ref[...]) if final else h

    for k in range(n_groups):
        @pl.when((i >= starts[k]) & (i < starts[k + 1]))
        def _(k=k):
            run(*grp_in[k], out_refs[k])


def _post(groups, wo, gff, w1, w2, gple, wg, wp, gfin, *, layer, final, tm):
    d = groups[0][0].shape[1]
    d_ff = w1.shape[-1]
    counts = [grp[0].shape[0] // tm for grp in groups]
    starts = _step_ranges(counts)
    in_specs, out_specs, out_shape, operands = [], [], [], []
    for k, (h, mix_a, mix_b, p, b_time_major) in enumerate(groups):
        t = h.shape[0]
        half = mix_a.shape[-1]
        pd = p.shape[-1]
        assert t % tm == 0
        loc = functools.partial(_local, start=starts[k], count=counts[k])
        row = lambda i, loc=loc: (loc(i), 0)
        mb_spec = pl.BlockSpec((tm, half), row)
        if b_time_major is not None:
            _, l = b_time_major
            assert l % tm == 0
            nl = l // tm
            mb_spec = pl.BlockSpec((tm, half), lambda i, loc=loc, nl=nl: (loc(i) % nl, loc(i) // nl))
        in_specs += [pl.BlockSpec((tm, d), row), pl.BlockSpec((tm, half), row), mb_spec,
                     pl.BlockSpec((None, tm, pd), lambda i, loc=loc: (layer, loc(i), 0))]
        operands += [h, mix_a, mix_b, p]
        out_specs.append(pl.BlockSpec((tm, d), row))
        out_shape.append(jax.ShapeDtypeStruct((t, d), F32))
    pd = groups[0][3].shape[-1]
    lw = lambda r, cdim: pl.BlockSpec((None, r, cdim), lambda i: (layer, 0, 0))
    return pl.pallas_call(
        functools.partial(_post_kernel, n_groups=len(groups), starts=tuple(starts), final=final,
                          ff_chunk=1024),
        grid=(starts[-1],),
        in_specs=in_specs + [_full((d, d)), _full((1, d)), lw(d, d_ff), lw(d_ff, d), _full((1, d)),
                             lw(d, d), lw(pd, d), _full((1, d))],
        out_specs=out_specs,
        out_shape=out_shape,
        compiler_params=_params(1, VMEM_LIMIT),
        name="post",
    )(*operands, wo, gff.reshape(1, d), w1, w2, gple.reshape(1, d), wg, wp, gfin.reshape(1, d))


def _mlstm_kernel(zq_ref, zk_ref, zv_ref, zo_ref, zg_ref, cw_ref, cb_ref, bg_ref, gn_ref,
                  conv0_ref, c0_ref, n0_ref, m0_ref,
                  out_ref, conv_ref, c_ref, n_ref, m_ref, ext_ref, u_ref, qk_ref, gates_ref,
                  *, c, group):
    j = pl.program_id(1)
    tail = CONV_W - 1

    @pl.when(j == 0)
    def _():
        c_ref[...] = c0_ref[...]
        n_ref[...] = n0_ref[...]
        m_ref[...] = m0_ref[...]
        for gi in range(group):
            ext_ref[gi, 0:SUBLANES, :] = jnp.zeros((SUBLANES, 2 * HW), F32)
            ext_ref[gi, SUBLANES - tail:SUBLANES, :] = conv0_ref[gi]

    ri = lax.broadcasted_iota(jnp.int32, (c, c), 0)
    ci = lax.broadcasted_iota(jnp.int32, (c, c), 1)
    eye = ri == ci
    tril = ri >= ci
    lane = lax.broadcasted_iota(jnp.int32, (c, LANES), 1)
    bg = bg_ref[...]

    def lsum(x):
        return jnp.broadcast_to(jnp.sum(x, axis=1, keepdims=True), (c, DH))

    def unit(gi, h):
        sl = slice(h * DH, (h + 1) * DH)
        gates = gates_ref[gi]
        i_col = lsum(jnp.where(lane == h, gates, 0.0))
        f_col = lsum(jnp.where(lane == HEADS + h, gates, 0.0))
        yield
        b_row = jnp.sum(jnp.where(ri <= ci, f_col[:, :c], 0.0), axis=0, keepdims=True)
        b_col = lsum(jnp.where(eye, b_row, 0.0))
        i_row = jnp.sum(jnp.where(eye, i_col[:, :c], 0.0), axis=0, keepdims=True)
        yield
        m_prev = m_ref[gi, h:h + 1, :]
        dmat = jnp.where(tril, b_col[:, :c] - b_row + i_row, NEG)
        inter = b_col + m_prev
        row_max = jnp.broadcast_to(jnp.max(dmat, axis=1, keepdims=True), (c, DH))
        qh = qk_ref[gi, :, sl]
        kh = qk_ref[gi, :, HW + h * DH:HW + (h + 1) * DH] * (DH ** -0.5)
        vh = zv_ref[gi, :, sl]
        c_h = c_ref[gi, h]
        n_h = n_ref[gi, h:h + 1, :]
        s_raw = _bdot_nt(qh, kh)
        q_c = _bdot(qh, c_h)
        q_n = lsum(qh * n_h)
        yield
        m_t = jnp.maximum(inter, row_max)
        w_intra = jnp.exp(dmat - m_t[:, :c])
        w_inter = jnp.exp(inter - m_t)
        s = s_raw * w_intra
        s_v = _bdot(s, vh)
        s_sum = lsum(s)
        m_new = m_t[c - 1:c, :]
        b_last = b_col[c - 1:c, :]
        w_last = jnp.exp(b_last - b_col + i_col - m_new)
        decay = jnp.exp(b_last + m_prev - m_new)
        kw = w_last * kh
        kw_v = _bdot_tn(kw, vh)
        yield
        num = w_inter * q_c + s_v
        den = w_inter * q_n + s_sum
        hh = num / jnp.maximum(jnp.abs(den), jnp.exp(-m_t))
        c_ref[gi, h] = decay * c_h + kw_v
        n_ref[gi, h:h + 1, :] = decay * n_h + jnp.sum(kw, axis=0, keepdims=True)
        m_ref[gi, h:h + 1, :] = m_new
        hh = _sigmoid(zo_ref[gi, :, sl]) * hh
        out_ref[gi, :, sl] = _head_rms(hh) * gn_ref[:, sl]

    for gi in range(group):
        ext_ref[gi, SUBLANES:SUBLANES + c, 0:HW] = zq_ref[gi]
        ext_ref[gi, SUBLANES:SUBLANES + c, HW:2 * HW] = zk_ref[gi]
        assert CONV_W == 4
        x0 = ext_ref[gi, SUBLANES:SUBLANES + c, :]
        x2 = ext_ref[gi, SUBLANES - 2:SUBLANES - 2 + c, :]
        u_ref[gi, SUBLANES:SUBLANES + c, :] = cw_ref[2:3, :] * x0 + cw_ref[0:1, :] * x2
        u_ref[gi, SUBLANES - 1:SUBLANES, :] = (cw_ref[2:3, :] * ext_ref[gi, SUBLANES - 1:SUBLANES, :]
                                               + cw_ref[0:1, :] * ext_ref[gi, SUBLANES - 3:SUBLANES - 2, :])
        conv = (cb_ref[...] + cw_ref[3:4, :] * x0 + cw_ref[1:2, :] * x2
                + u_ref[gi, SUBLANES - 1:SUBLANES - 1 + c, :])
        ext_ref[gi, 0:SUBLANES, :] = ext_ref[gi, c:c + SUBLANES, :]
        conv_ref[gi] = ext_ref[gi, SUBLANES - tail:SUBLANES, :]
        qk_ref[gi] = conv * _sigmoid(conv)
        gb = zg_ref[gi] + bg
        gates_ref[gi] = jnp.where(lane < HEADS, gb, _log_sigmoid(gb))
    _round_robin([unit(gi, h) for gi in range(group) for h in range(HEADS)])


def _seq_group(b, c):
    rows = 512
    group = max(1, min(b, rows // c, 2 * SUBLANES))
    assert b % group == 0
    return group


def _mlstm(z_main, z_gate, conv_w, conv_b, b_gate, gn_a, conv0, c0, n0, m0, b, l):
    c = _chunk_len(l)
    nc = l // c
    grp = _seq_group(b, c)
    z3 = z_main.reshape(b, l, z_main.shape[-1])
    zspec = lambda col: pl.BlockSpec((grp, c, HW), lambda bi, j: (bi, j, col))
    st = lambda shape: pl.BlockSpec((grp,) + shape, lambda bi, j: (bi,) + (0,) * len(shape))
    m0b = jnp.broadcast_to(m0[:, :, None], (b, HEADS, DH))
    out, conv_new, c_new, n_new, m_new = pl.pallas_call(
        functools.partial(_mlstm_kernel, c=c, group=grp),
        grid=(b // grp, nc),
        in_specs=[zspec(0), zspec(1), zspec(2), zspec(3),
                  pl.BlockSpec((grp, c, LANES), lambda bi, j: (bi, j, 0)),
                  _full((CONV_W, 2 * HW)), _full((1, 2 * HW)), _full((1, LANES)), _full((1, HW)),
                  st((CONV_W - 1, 2 * HW)), st((HEADS, DH, DH)), st((HEADS, DH)), st((HEADS, DH))],
        out_specs=[pl.BlockSpec((grp, c, HW), lambda bi, j: (bi, j, 0)),
                   st((CONV_W - 1, 2 * HW)), st((HEADS, DH, DH)), st((HEADS, DH)), st((HEADS, DH))],
        out_shape=[jax.ShapeDtypeStruct((b, l, HW), F32),
                   jax.ShapeDtypeStruct((b, CONV_W - 1, 2 * HW), F32),
                   jax.ShapeDtypeStruct((b, HEADS, DH, DH), F32),
                   jax.ShapeDtypeStruct((b, HEADS, DH), F32),
                   jax.ShapeDtypeStruct((b, HEADS, DH), F32)],
        scratch_shapes=[pltpu.VMEM((grp, c + SUBLANES, 2 * HW), F32),
                        pltpu.VMEM((grp, c + SUBLANES, 2 * HW), F32),
                        pltpu.VMEM((grp, c, 2 * HW), F32), pltpu.VMEM((grp, c, LANES), F32)],
        compiler_params=_params(2),
        name="mlstm",
    )(z3, z3, z3, z3, z_gate.reshape(b, l, LANES), conv_w, conv_b.reshape(1, -1),
      jnp.pad(b_gate, (0, LANES - 2 * HEADS)).reshape(1, LANES), gn_a.reshape(1, -1),
      conv0, c0, n0, m0b)
    return out.reshape(b * l, HW), conv_new, c_new, n_new, m_new[:, :, 0]


def _rope_table_kernel(inv_ref, sign_ref, cos_ref, sin_ref, *, pos0, rows):
    i = pl.program_id(0)
    pos = (pos0 + i * rows + lax.broadcasted_iota(jnp.int32, (rows, LANES), 0)).astype(F32)
    ang = pos * inv_ref[...]
    cos_ref[...] = jnp.cos(ang)
    sin_ref[...] = jnp.sin(ang) * sign_ref[...]


def _rope_tables(l, pos0):
    half = DH // 2
    inv = ROPE_BASE ** (-jnp.arange(half, dtype=F32) / half)
    inv2 = jnp.concatenate([inv, inv]).reshape(1, DH)
    sign = jnp.concatenate([-jnp.ones((half,), F32), jnp.ones((half,), F32)]).reshape(1, DH)
    rows = min(l, 512)
    assert l % rows == 0
    return pl.pallas_call(
        functools.partial(_rope_table_kernel, pos0=pos0, rows=rows),
        grid=(l // rows,),
        in_specs=[_full((1, DH)), _full((1, DH))],
        out_specs=[pl.BlockSpec((rows, DH), lambda i: (i, 0))] * 2,
        out_shape=[jax.ShapeDtypeStruct((l, DH), F32)] * 2,
        compiler_params=_params(1),
        name="rope_table",
    )(inv2, sign)


def _ret_kernel(zq_ref, zk_ref, zv_ref, zg_ref, cos_ref, sin_ref, s0_ref, out_ref, s_ref,
                *, c, group):
    j = pl.program_id(1)

    @pl.when(j == 0)
    def _():
        s_ref[...] = s0_ref[...]

    cosf = cos_ref[...]
    sinf = sin_ref[...]
    ti = lax.broadcasted_iota(jnp.int32, (c, c), 0)
    si = lax.broadcasted_iota(jnp.int32, (c, c), 1)
    rel = jnp.maximum(ti - si, 0).astype(F32)
    tcol = lax.broadcasted_iota(jnp.int32, (c, 1), 0).astype(F32)

    def rope(x):
        return x * cosf + pltpu.roll(x, DH // 2, axis=1) * sinf

    def unit(gi, h, decay, inter, kdecay, cdecay):
        sl = slice(h * DH, (h + 1) * DH)
        qr = rope(zq_ref[gi, :, sl])
        yield
        kr = rope(zk_ref[gi, :, sl]) * (DH ** -0.5)
        yield
        vh = zv_ref[gi, :, sl]
        s_h = s_ref[gi, h]
        qk = _bdot_nt(qr, kr)
        yield
        q_s = _bdot(qr, s_h)
        yield
        k_v = _bdot_tn(kr * kdecay, vh)
        yield
        o = q_s * inter + _bdot(qk * decay, vh)
        s_ref[gi, h] = cdecay * s_h + k_v
        yield
        gate = zg_ref[gi, :, sl]
        out_ref[gi, :, sl] = _head_rms(o) * (gate * _sigmoid(gate))

    units = []
    for h in range(HEADS):
        lg = math.log1p(-(2.0 ** (-5.0 - h)))
        decay = jnp.where(ti >= si, jnp.exp(rel * lg), 0.0)
        inter = jnp.exp((tcol + 1.0) * lg)
        kdecay = jnp.exp((c - 1.0 - tcol) * lg)
        cdecay = math.exp(c * lg)
        units += [unit(gi, h, decay, inter, kdecay, cdecay) for gi in range(group)]
    _round_robin(units)


def _retention(z_main, cos_t, sin_t, s0, b, l):
    c = _chunk_len(l)
    nc = l // c
    grp = _seq_group(b, c)
    z3 = z_main.reshape(b, l, z_main.shape[-1])
    zspec = lambda col: pl.BlockSpec((grp, c, HW), lambda bi, j: (bi, j, col))
    st = pl.BlockSpec((grp, HEADS, DH, DH), lambda bi, j: (bi, 0, 0, 0))
    tab = pl.BlockSpec((c, DH), lambda bi, j: (j, 0))
    out, s_new = pl.pallas_call(
        functools.partial(_ret_kernel, c=c, group=grp),
        grid=(b // grp, nc),
        in_specs=[zspec(4), zspec(5), zspec(6), zspec(7), tab, tab, st],
        out_specs=[pl.BlockSpec((grp, c, HW), lambda bi, j: (bi, j, 0)), st],
        out_shape=[jax.ShapeDtypeStruct((b, l, HW), F32),
                   jax.ShapeDtypeStruct((b, HEADS, DH, DH), F32)],
        compiler_params=_params(2),
        name="retention",
    )(z3, z3, z3, z3, cos_t, sin_t, s0)
    return out.reshape(b * l, HW), s_new


def _hgrn_kernel(zq_ref, zf_ref, zi_ref, zg_ref, lbl_ref, gn_ref, s0_ref, out_ref, s_ref,
                 kk_ref, bcum_ref, *, c, sc, layer, group):
    j = pl.program_id(1)

    @pl.when(j == 0)
    def _():
        s_ref[...] = s0_ref[...]

    lbl = lbl_ref[...]
    e = jnp.exp(lbl - jnp.max(lbl, axis=0, keepdims=True))
    sm = e / jnp.sum(e, axis=0, keepdims=True)
    cum = sm[0:1, :]
    for r in range(1, layer + 1):
        cum = cum + sm[r:r + 1, :]
    lb = cum - sm[0:1, :]

    oml = 1.0 - lb
    ri = lax.broadcasted_iota(jnp.int32, (c, c), 0)
    ci = lax.broadcasted_iota(jnp.int32, (c, c), 1)
    tril = jnp.where(ri >= ci, 1.0, 0.0).astype(BF16)
    row_s = lax.broadcasted_iota(jnp.int32, (sc, sc), 0)
    lane_s = lax.broadcasted_iota(jnp.int32, (sc, sc), 1)
    causal_col = jnp.where(row_s >= lane_s, lane_s, -1)
    keep = [causal_col == s for s in range(sc)]
    e_r = lax.broadcasted_iota(jnp.int32, (DH, DH), 0)
    e_c = lax.broadcasted_iota(jnp.int32, (DH, DH), 1)
    eye = e_r == e_c

    def unit(gi, h):
        sl = slice(h * DH, (h + 1) * DH)
        bh = bcum_ref[gi, :, sl]
        qh = zq_ref[gi, :, sl] * (DH ** -0.5)
        kh = kk_ref[gi, :, sl]
        vh = zi_ref[gi, :, sl]
        s_h = s_ref[gi, h]
        b_last = bh[c - 1:c, :]
        o_inter = _bdot(qh * jnp.exp(bh), s_h)
        yield
        k_v = _bdot_tn(kh * jnp.exp(b_last - bh), vh)
        yield
        dec_col = jnp.sum(jnp.where(eye, jnp.exp(b_last), 0.0), axis=1, keepdims=True)
        yield
        s_ref[gi, h] = dec_col * s_h + k_v
        blocks = []
        for blk in range(c // sc):
            r0 = blk * sc
            b_i = bh[r0:r0 + sc]
            q_i = qh[r0:r0 + sc]
            k_i = kh[r0:r0 + sc]
            v_i = vh[r0:r0 + sc]
            att_prev = None
            if blk > 0:
                ref_row = bh[r0 - 1:r0, :]
                a_i = q_i * jnp.exp(b_i - ref_row)
                k_prev = kh[0:r0] * jnp.exp(ref_row - bh[0:r0])
                att_prev = _bdot_nt(a_i, k_prev)
                yield
            b2_i = b_i * LOG2_E
            c2_i = b2_i - jnp.log2(k_i)
            cols = [jnp.sum(q_i * jnp.exp2(b2_i - c2_i[s:s + 1, :]), axis=1, keepdims=True)
                    for s in range(sc)]
            yield
            att = jnp.zeros((sc, sc), F32)
            for s in range(sc):
                att = jnp.where(keep[s], cols[s], att)
            o_i = _bdot(att, v_i)
            if att_prev is not None:
                o_i = o_i + _bdot(att_prev, vh[0:r0])
            blocks.append(o_i)
        yield
        o = o_inter + (jnp.concatenate(blocks, axis=0) if len(blocks) > 1 else blocks[0])
        gate = zg_ref[gi, :, sl]
        out_ref[gi, :, sl] = _head_rms(o) * gn_ref[:, sl] * (gate * _sigmoid(gate))

    for gi in range(group):
        zf = zf_ref[gi]
        ez = jnp.exp(-jnp.abs(zf))
        big = 1.0 / (1.0 + ez)
        small = ez * big
        pos = zf >= 0.0
        logf = jnp.log(lb + oml * jnp.where(pos, big, small))
        kk_ref[gi] = oml * jnp.where(pos, small, big)

        p0 = logf.astype(BF16)
        r1 = logf - p0.astype(F32)
        p1 = r1.astype(BF16)
        p2 = (r1 - p1.astype(F32)).astype(BF16)
        bcum_ref[gi] = (jnp.dot(tril, p0, preferred_element_type=F32)
                        + jnp.dot(tril, p1, preferred_element_type=F32)
                        + jnp.dot(tril, p2, preferred_element_type=F32))
    _round_robin([unit(gi, h) for gi in range(group) for h in range(HEADS)])


def _hgrn(z_cd, lb_logits, gn_c, s0, b, l, layer):
    c = _chunk_len(l)
    sc = min(c, SUBLANES)
    nc = l // c
    grp = _seq_group(b, c)
    depth = lb_logits.shape[0]
    z3 = z_cd.reshape(b, l, z_cd.shape[-1])
    zspec = lambda col: pl.BlockSpec((grp, c, HW), lambda bi, j: (bi, j, col))
    st = pl.BlockSpec((grp, HEADS, DH, DH), lambda bi, j: (bi, 0, 0, 0))
    out, s_new = pl.pallas_call(
        functools.partial(_hgrn_kernel, c=c, sc=sc, layer=layer, group=grp),
        grid=(b // grp, nc),
        in_specs=[zspec(0), zspec(1), zspec(2), zspec(3), _full((depth, HW)), _full((1, HW)), st],
        out_specs=[pl.BlockSpec((grp, c, HW), lambda bi, j: (bi, j, 0)), st],
        out_shape=[jax.ShapeDtypeStruct((b, l, HW), F32),
                   jax.ShapeDtypeStruct((b, HEADS, DH, DH), F32)],
        scratch_shapes=[pltpu.VMEM((grp, c, HW), F32), pltpu.VMEM((grp, c, HW), F32)],
        compiler_params=_params(2),
        name="hgrn2",
    )(z3, z3, z3, z3, lb_logits, gn_c.reshape(1, -1), s0)
    return out.reshape(b * l, HW), s_new


def _s5_kernel(u_ref, are_ref, aim_ref, ldt_ref, bre_ref, bim_ref, cre_ref, cim_ref, d_ref,
               wglu_ref, bglu_ref, x0r_ref, x0i_ref,
               s_ref, xr_ref, xi_ref, ar_sc, ai_sc, bbr_sc, bbi_sc, bur_sc, bui_sc, *, ct):
    j = pl.program_id(1)
    ns = are_ref.shape[-1]
    nu = u_ref.shape[-1]
    hs = ns // 2
    hu = nu // 2
    bt = u_ref.shape[1]
    rows = ct * bt

    @pl.when((pl.program_id(0) == 0) & (j == 0))
    def _():
        a_re = are_ref[...]
        a_im = aim_ref[...]
        dt = jnp.exp(ldt_ref[...])
        mag = jnp.exp(dt * a_re)
        ar = mag * jnp.cos(dt * a_im)
        ai = mag * jnp.sin(dt * a_im)
        ar_sc[...] = jnp.broadcast_to(ar, (SUBLANES, ns))
        ai_sc[...] = jnp.broadcast_to(ai, (SUBLANES, ns))
        den = a_re * a_re + a_im * a_im
        nr = ar - 1.0
        zr = (nr * a_re + ai * a_im) / den
        zi = (ai * a_re - nr * a_im) / den
        for hg in range(2):
            us = slice(hg * hu, (hg + 1) * hu)
            ss = slice(hg * hs, (hg + 1) * hs)
            bbr_sc[us, :] = (zr[:, ss] * bre_ref[us, :] - zi[:, ss] * bim_ref[us, :]).astype(BF16)
            bbi_sc[us, :] = (zr[:, ss] * bim_ref[us, :] + zi[:, ss] * bre_ref[us, :]).astype(BF16)

    @pl.when(j == 0)
    def _():
        xr_ref[...] = x0r_ref[...]
        xi_ref[...] = x0i_ref[...]

    u = u_ref[...].reshape(rows, nu)
    ub = u.astype(BF16)
    for hg in range(2):
        us = slice(hg * hu, (hg + 1) * hu)
        ss = slice(hg * hs, (hg + 1) * hs)
        bur_sc[:, ss] = jnp.dot(ub[:, us], bbr_sc[us, :], preferred_element_type=F32)
        bui_sc[:, ss] = jnp.dot(ub[:, us], bbi_sc[us, :], preferred_element_type=F32)

    lane_chunk = 8 * LANES
    for lc in range(ns // lane_chunk):
        ls = slice(lc * lane_chunk, (lc + 1) * lane_chunk)
        ar = ar_sc[:, ls]
        ai = ai_sc[:, ls]
        for sg in range(bt // SUBLANES):
            srows = slice(sg * SUBLANES, (sg + 1) * SUBLANES)

            def step(t, carry, sg=sg, ls=ls, ar=ar, ai=ai):
                xr, xi = carry
                r0 = pl.multiple_of(t * bt + sg * SUBLANES, SUBLANES)
                nxr = ar * xr - ai * xi + bur_sc[pl.ds(r0, SUBLANES), ls]
                nxi = ar * xi + ai * xr + bui_sc[pl.ds(r0, SUBLANES), ls]
                bur_sc[pl.ds(r0, SUBLANES), ls] = nxr
                bui_sc[pl.ds(r0, SUBLANES), ls] = nxi
                return nxr, nxi

            xr, xi = lax.fori_loop(0, ct, step, (xr_ref[srows, ls], xi_ref[srows, ls]), unroll=SUBLANES)
            xr_ref[srows, ls] = xr
            xi_ref[srows, ls] = xi

    ys = []
    for hg in range(2):
        ss = slice(hg * hs, (hg + 1) * hs)
        ys.append(jnp.dot(bur_sc[:, ss].astype(BF16), cre_ref[ss, :], preferred_element_type=F32)
                  - jnp.dot(bui_sc[:, ss].astype(BF16), cim_ref[ss, :], preferred_element_type=F32))
    y = jnp.concatenate(ys, axis=1) + d_ref[...] * u
    a = 0.5 * y * (1.0 + jnp.tanh(math.sqrt(2.0 / math.pi) * (y + 0.044715 * (y * y * y))))
    s = a * _sigmoid(jnp.dot(a.astype(BF16), wglu_ref[...], preferred_element_type=F32) + bglu_ref[...])
    s_ref[...] = s.reshape(ct, bt, nu)


def _s5_block_diag(bmat, cmat):
    g, p, hgrp = bmat.shape
    gh = g // 2
    eye = jnp.eye(gh, dtype=F32)
    b4 = bmat.reshape(2, gh, p, hgrp)
    bc = jnp.einsum('agph,gk->aghkp', b4, eye).reshape(2 * gh * hgrp, gh * p)
    c4 = cmat.reshape(2, gh, hgrp, p)
    cc = jnp.einsum('aghp,gk->agpkh', c4, eye).reshape(2 * gh * p, gh * hgrp)
    return bc, cc


def _s5_operands(a_re, a_im, log_dt, b_re, b_im, c_re, c_im, d_skip, w_glu, b_glu):
    g, p = a_re.shape
    ns = g * p
    nu = d_skip.shape[-1]
    assert nu == g * S5_GROUP and (g // 2) * S5_GROUP == MXU_DIM
    bre_c, cre_c = _s5_block_diag(b_re, c_re)
    bim_c, cim_c = _s5_block_diag(b_im, c_im)
    ldt = jnp.broadcast_to(log_dt[:, None], (g, p)).reshape(1, ns)
    return (a_re.reshape(1, ns), a_im.reshape(1, ns), ldt, bre_c, bim_c, cre_c.astype(BF16),
            cim_c.astype(BF16), d_skip.reshape(1, nu), w_glu.astype(BF16), b_glu.reshape(1, nu))


def _s5(u_tm, operands, x0r, x0i):
    l, b, nu = u_tm.shape
    g, p = x0r.shape[1:]
    ns = g * p
    assert b % SUBLANES == 0
    ct = _chunk_len(l)
    nct = l // ct
    bt = min(b, max(SUBLANES, 512 // ct))
    assert b % bt == 0 and bt % SUBLANES == 0
    rows = ct * bt
    xspec = pl.BlockSpec((bt, ns), lambda bb, j: (bb, 0))
    s, xr, xi = pl.pallas_call(
        functools.partial(_s5_kernel, ct=ct),
        grid=(b // bt, nct),
        in_specs=[pl.BlockSpec((ct, bt, nu), lambda bb, j: (j, bb, 0)),
                  _full((1, ns)), _full((1, ns)), _full((1, ns)),
                  _full((nu, ns // 2)), _full((nu, ns // 2)),
                  _full((ns, nu // 2)), _full((ns, nu // 2)),
                  _full((1, nu)), _full((nu, nu)), _full((1, nu)), xspec, xspec],
        out_specs=[pl.BlockSpec((ct, bt, nu), lambda bb, j: (j, bb, 0)), xspec, xspec],
        out_shape=[jax.ShapeDtypeStruct((l, b, nu), F32),
                   jax.ShapeDtypeStruct((b, ns), F32), jax.ShapeDtypeStruct((b, ns), F32)],
        scratch_shapes=[pltpu.VMEM((SUBLANES, ns), F32), pltpu.VMEM((SUBLANES, ns), F32),
                        pltpu.VMEM((nu, ns // 2), BF16), pltpu.VMEM((nu, ns // 2), BF16),
                        pltpu.VMEM((rows, ns), F32), pltpu.VMEM((rows, ns), F32)],
        compiler_params=_params(2, VMEM_LIMIT),
        name="s5",
    )(u_tm, *operands, x0r.reshape(b, ns), x0i.reshape(b, ns))
    return s, xr.reshape(b, g, p), xi.reshape(b, g, p)


def _trunk(groups, prm):
    d = groups[0]['x'].shape[-1]
    tm = 512
    info = []
    for grp in groups:
        b, l, _ = grp['x'].shape
        assert l >= CONV_W - 1 and (b * l) % tm == 0
        info.append(dict(b=b, l=l, t=b * l, direct_tm=b == SUBLANES and l % tm == 0))
    hs = [grp['x'].reshape(-1, d) for grp in groups]
    pps = [grp['p'].reshape(grp['p'].shape[0], -1, grp['p'].shape[-1]) for grp in groups]

    zs = _inproj(hs, prm['norm_mix'][0], prm['w_ab'], (8 * HW, LANES), tm, [None] * len(groups))
    ab_new, post_in = [], []
    for grp, inf, h, pp, (z_main, z_gate) in zip(groups, info, hs, pps, zs):
        b, l = inf['b'], inf['l']
        conv0, c0, n0, m0, ret0 = grp['ab_state']
        hm, conv_new, c_new, n_new, m_new = _mlstm(z_main, z_gate, prm['conv_w_ab'][0], prm['conv_b_ab'][0],
                                         prm['b_gate_ab'][0], prm['gn_a'][0], conv0[0], c0[0], n0[0],
                                         m0[0], b, l)
        cos_t, sin_t = _rope_tables(l, grp['pos0'])
        hr, ret_new = _retention(z_main, cos_t, sin_t, ret0[0], b, l)
        ab_new.append((conv_new[None], c_new[None], n_new[None], m_new[None], ret_new[None]))
        post_in.append((h, hm, hr, pp, None))
    hs = _post(post_in, prm['w_out_ab'], prm['norm_ff'][0], prm['w_ff1'], prm['w_ff2'],
               prm['norm_ple'][0], prm['w_ple_gate'], prm['w_ple_proj'], prm['norm_final'],
               layer=0, final=False, tm=tm)

    tml = [(inf['b'], inf['l']) if inf['direct_tm'] else None for inf in info]
    zs = _inproj(hs, prm['norm_mix'][1], prm['w_cd'], (4 * HW, HW), tm, tml)
    cd_new, post_in = [], []
    for grp, inf, h, pp, (z_cd, su) in zip(groups, info, hs, pps, zs):
        b, l, t = inf['b'], inf['l'], inf['t']
        hg0, x0r, x0i = grp['cd_state']
        o, hg_new = _hgrn(z_cd, prm['lb_logits'], prm['gn_c'][0], hg0[0], b, l, layer=1)
        u_tm = (su.reshape(l, b, HW) if inf['direct_tm']
                else jnp.transpose(su.reshape(b, l, HW), (1, 0, 2)))
        s_tm, xr, xi = _s5(u_tm, prm['s5_operands'], x0r[0], x0i[0])
        s_in = (s_tm.reshape(l, b * HW) if inf['direct_tm']
                else jnp.transpose(s_tm, (1, 0, 2)).reshape(t, HW))
        cd_new.append((hg_new[None], xr[None], xi[None]))
        post_in.append((h, o, s_in, pp, (b, l) if inf['direct_tm'] else None))
    ys = _post(post_in, prm['w_out_cd'], prm['norm_ff'][1], prm['w_ff1'], prm['w_ff2'],
               prm['norm_ple'][1], prm['w_ple_gate'], prm['w_ple_proj'], prm['norm_final'],
               layer=1, final=True, tm=tm)
    ys = [y.reshape(grp['x'].shape) for y, grp in zip(ys, groups)]
    return ys, ab_new, cd_new


def kernel(x_prompt, x_sample, state_mlstm_conv, state_mlstm_C, state_mlstm_n, state_mlstm_m, state_ret, state_hgrn, state_s5_re, state_s5_im, p_prompt, p_sample, norm_mix, norm_ff, norm_ple, norm_final, w_in_ab, b_gate_ab, conv_w_ab, conv_b_ab, gn_a, w_out_ab, w_in_cd, lb_logits, gn_c, s5_A_re, s5_A_im, s5_log_dt, s5_B_re, s5_B_im, s5_C_re, s5_C_im, s5_D, w_glu, b_glu, w_out_cd, w_ff1, w_ff2, w_ple_proj, w_ple_gate):
    assert norm_mix.shape[0] == 2, "two layers: (mLSTM || retention), (HGRN2 || S5)"
    w_ab = w_in_ab[0]
    gate0 = 4 * HW
    w_ab = (w_ab[:, :gate0].astype(BF16), w_ab[:, gate0 + 2 * HEADS:].astype(BF16),
            jnp.pad(w_ab[:, gate0:gate0 + 2 * HEADS], ((0, 0), (0, LANES - 2 * HEADS))).astype(BF16))
    prm = dict(norm_mix=norm_mix, norm_ff=norm_ff, norm_ple=norm_ple, norm_final=norm_final,
               w_ab=w_ab, b_gate_ab=b_gate_ab, conv_w_ab=conv_w_ab, conv_b_ab=conv_b_ab,
               gn_a=gn_a, w_out_ab=w_out_ab[0].astype(BF16), w_cd=(w_in_cd[0].astype(BF16),),
               lb_logits=lb_logits, gn_c=gn_c,
               s5_operands=_s5_operands(s5_A_re[0], s5_A_im[0], s5_log_dt[0], s5_B_re[0], s5_B_im[0],
                                        s5_C_re[0], s5_C_im[0], s5_D[0], w_glu[0], b_glu[0]),
               w_out_cd=w_out_cd[0].astype(BF16), w_ff1=w_ff1.astype(BF16), w_ff2=w_ff2.astype(BF16),
               w_ple_proj=w_ple_proj.astype(BF16), w_ple_gate=w_ple_gate.astype(BF16))

    bp = x_prompt.shape[0]
    z = lambda *s: jnp.zeros(s, F32)
    zero_ab = (z(1, bp, CONV_W - 1, 2 * HW), z(1, bp, HEADS, DH, DH), z(1, bp, HEADS, DH),
               z(1, bp, HEADS), z(1, bp, HEADS, DH, DH))
    zero_cd = (z(1, bp, HEADS, DH, DH),) + (z(*((1, bp) + s5_A_re.shape[1:])),) * 2
    groups = [dict(x=x_prompt, p=p_prompt, pos0=0, ab_state=zero_ab, cd_state=zero_cd),
              dict(x=x_sample, p=p_sample, pos0=PAST_LEN,
                   ab_state=(state_mlstm_conv, state_mlstm_C, state_mlstm_n, state_mlstm_m, state_ret),
                   cd_state=(state_hgrn, state_s5_re, state_s5_im))]
    (y_p, y_s), (ab_p, ab_s), (cd_p, cd_s) = _trunk(groups, prm)
    return (y_p, y_s,
            ab_p[0], ab_s[0], ab_p[1], ab_s[1], ab_p[2], ab_s[2], ab_p[3], ab_s[3], ab_p[4], ab_s[4],
            cd_p[0], cd_s[0], cd_p[1], cd_s[1], cd_p[2], cd_s[2])
```

```python
import functools
import math

import jax
import jax.numpy as jnp
from jax import lax
from jax.experimental import pallas as pl
from jax.experimental.pallas import tpu as pltpu

F32 = jnp.float32
BF16 = jnp.bfloat16

EPS = 1e-6
NEG = -1e30
LOG2_E = math.log2(math.e)
ROPE_BASE = 10000.0
PAST_LEN = 16384
CHUNK = 64
HEADS = 4
DH = 128
HW = HEADS * DH
CONV_W = 4
S5_GROUP = 16
SUBLANES = 8
LANES = 128
MXU_DIM = 256
VMEM_LIMIT = 56 * 1024 * 1024


def _params(n_axes, vmem=None):
    return pltpu.CompilerParams(dimension_semantics=("arbitrary",) * n_axes, vmem_limit_bytes=vmem)


def _full(shape):
    return pl.BlockSpec(shape, lambda *_: (0,) * len(shape))


def _bdot(a, b):
    return jnp.dot(a.astype(BF16), b.astype(BF16), preferred_element_type=F32)


def _bdot_nt(a, b):
    return lax.dot_general(a.astype(BF16), b.astype(BF16), (((1,), (1,)), ((), ())),
                           preferred_element_type=F32)


def _bdot_tn(a, b):
    return lax.dot_general(a.astype(BF16), b.astype(BF16), (((0,), (0,)), ((), ())),
                           preferred_element_type=F32)


def _sigmoid(x):
    return 1.0 / (1.0 + jnp.exp(-x))


def _log_sigmoid(x):
    return jnp.minimum(x, 0.0) - jnp.log(1.0 + jnp.exp(-jnp.abs(x)))


def _rms(x, g):
    return x * lax.rsqrt(jnp.mean(x * x, axis=-1, keepdims=True) + EPS) * g


def _head_rms(x):
    return x * lax.rsqrt(jnp.mean(x * x, axis=-1, keepdims=True) + EPS)


def _chunk_len(length):
    return CHUNK if length % CHUNK == 0 else length


def _round_robin(gens):
    gens = list(gens)
    while gens:
        alive = []
        for g in gens:
            try:
                next(g)
                alive.append(g)
            except StopIteration:
                pass
        gens = alive


def _step_ranges(counts):
    starts = [0]
    for n in counts:
        starts.append(starts[-1] + n)
    return starts


def _local(i, start, count):
    return jnp.clip(i - start, 0, count - 1)


def _inproj_kernel(*refs, n_groups, n_w, n_out, starts):
    x_refs = refs[:n_groups]
    g_ref = refs[n_groups]
    w_refs = refs[n_groups + 1:n_groups + 1 + n_w]
    out_refs = refs[n_groups + 1 + n_w:]
    i = pl.program_id(0)
    w_starts = _step_ranges([w.shape[1] for w in w_refs])

    def w_cols(c0, nn):
        for w_ref, s0, s1 in zip(w_refs, w_starts[:-1], w_starts[1:]):
            if s0 <= c0 and c0 + nn <= s1:
                return w_ref[:, c0 - s0:c0 - s0 + nn]
        raise ValueError("output column chunk straddles two weight parts")

    def run(x_ref, outs):
        hn = _rms(x_ref[...], g_ref[...]).astype(BF16)
        off = 0
        for o_ref in outs:
            n = o_ref.shape[-1]
            for n0 in range(0, n, HW):
                nn = min(HW, n - n0)
                o_ref[:, n0:n0 + nn] = jnp.dot(hn, w_cols(off + n0, nn), preferred_element_type=F32)
            off += n

    for gi in range(n_groups):
        @pl.when((i >= starts[gi]) & (i < starts[gi + 1]))
        def _(gi=gi):
            run(x_refs[gi], out_refs[gi * n_out:(gi + 1) * n_out])


def _inproj(hs, g, ws, widths, tm, time_major_last):
    d = hs[0].shape[1]
    assert sum(widths) == sum(w.shape[1] for w in ws)
    counts = [h.shape[0] // tm for h in hs]
    starts = _step_ranges(counts)
    in_specs, out_specs, out_shape = [], [], []
    for k, h in enumerate(hs):
        t = h.shape[0]
        assert t % tm == 0
        loc = functools.partial(_local, start=starts[k], count=counts[k])
        in_specs.append(pl.BlockSpec((tm, d), lambda i, loc=loc: (loc(i), 0)))
        for wi, wd in enumerate(widths):
            if wi == len(widths) - 1 and time_major_last[k] is not None:
                b, l = time_major_last[k]
                assert l % tm == 0
                nl = l // tm
                out_shape.append(jax.ShapeDtypeStruct((l, b * wd), F32))
                out_specs.append(pl.BlockSpec((tm, wd), lambda i, loc=loc, nl=nl: (loc(i) % nl, loc(i) // nl)))
            else:
                out_shape.append(jax.ShapeDtypeStruct((t, wd), F32))
                out_specs.append(pl.BlockSpec((tm, wd), lambda i, loc=loc: (loc(i), 0)))
    outs = pl.pallas_call(
        functools.partial(_inproj_kernel, n_groups=len(hs), n_w=len(ws), n_out=len(widths),
                          starts=tuple(starts)),
        grid=(starts[-1],),
        in_specs=in_specs + [_full((1, d))] + [_full(w.shape) for w in ws],
        out_specs=out_specs,
        out_shape=out_shape,
        compiler_params=_params(1, VMEM_LIMIT),
        name="inproj",
    )(*hs, g.reshape(1, d), *ws)
    nw = len(widths)
    return [outs[k * nw:(k + 1) * nw] for k in range(len(hs))]


def _post_kernel(*refs, n_groups, starts, final, ff_chunk):
    grp_in = [refs[4 * k:4 * k + 4] for k in range(n_groups)]
    (wo_ref, gff_ref, w1_ref, w2_ref, gple_ref, wg_ref, wp_ref,
     gfin_ref) = refs[4 * n_groups:4 * n_groups + 8]
    out_refs = refs[4 * n_groups + 8:]
    i = pl.program_id(0)

    def run(h_ref, ma_ref, mb_ref, p_ref, o_ref):
        half = ma_ref.shape[-1]
        h = h_ref[...]
        h = h + (jnp.dot(ma_ref[...].astype(BF16), wo_ref[0:half, :], preferred_element_type=F32)
                 + jnp.dot(mb_ref[...].astype(BF16), wo_ref[half:2 * half, :],
                           preferred_element_type=F32))
        hn = _rms(h, gff_ref[...]).astype(BF16)
        d_ff = w1_ref.shape[1]
        acc = jnp.zeros_like(h)
        for f0 in range(0, d_ff, ff_chunk):
            a = jnp.dot(hn, w1_ref[:, f0:f0 + ff_chunk], preferred_element_type=F32)
            a = jnp.square(jnp.maximum(a, 0.0))
            acc = acc + jnp.dot(a.astype(BF16), w2_ref[f0:f0 + ff_chunk, :], preferred_element_type=F32)
        h = h + acc
        gate = _sigmoid(jnp.dot(_rms(h, gple_ref[...]).astype(BF16), wg_ref[...],
                                preferred_element_type=F32))
        h = h + gate * jnp.dot(p_ref[...].astype(BF16), wp_ref[...], preferred_element_type=F32)
        o_ref[...] = _rms(h, gfin_ref[...]) if final else h

    for k in range(n_groups):
        @pl.when((i >= starts[k]) & (i < starts[k + 1]))
        def _(k=k):
            run(*grp_in[k], out_refs[k])


def _post(groups, wo, gff, w1, w2, gple, wg, wp, gfin, *, layer, final, tm):
    d = groups[0][0].shape[1]
    d_ff = w1.shape[-1]
    counts = [grp[0].shape[0] // tm for grp in groups]
    starts = _step_ranges(counts)
    in_specs, out_specs, out_shape, operands = [], [], [], []
    for k, (h, mix_a, mix_b, p, b_time_major) in enumerate(groups):
        t = h.shape[0]
        half = mix_a.shape[-1]
        pd = p.shape[-1]
        assert t % tm == 0
        loc = functools.partial(_local, start=starts[k], count=counts[k])
        row = lambda i, loc=loc: (loc(i), 0)
        mb_spec = pl.BlockSpec((tm, half), row)
        if b_time_major is not None:
            _, l = b_time_major
            assert l % tm == 0
            nl = l // tm
            mb_spec = pl.BlockSpec((tm, half), lambda i, loc=loc, nl=nl: (loc(i) % nl, loc(i) // nl))
        in_specs += [pl.BlockSpec((tm, d), row), pl.BlockSpec((tm, half), row), mb_spec,
                     pl.BlockSpec((None, tm, pd), lambda i, loc=loc: (layer, loc(i), 0))]
        operands += [h, mix_a, mix_b, p]
        out_specs.append(pl.BlockSpec((tm, d), row))
        out_shape.append(jax.ShapeDtypeStruct((t, d), F32))
    pd = groups[0][3].shape[-1]
    lw = lambda r, cdim: pl.BlockSpec((None, r, cdim), lambda i: (layer, 0, 0))
    return pl.pallas_call(
        functools.partial(_post_kernel, n_groups=len(groups), starts=tuple(starts), final=final,
                          ff_chunk=1024),
        grid=(starts[-1],),
        in_specs=in_specs + [_full((d, d)), _full((1, d)), lw(d, d_ff), lw(d_ff, d), _full((1, d)),
                             lw(d, d), lw(pd, d), _full((1, d))],
        out_specs=out_specs,
        out_shape=out_shape,
        compiler_params=_params(1, VMEM_LIMIT),
        name="post",
    )(*operands, wo, gff.reshape(1, d), w1, w2, gple.reshape(1, d), wg, wp, gfin.reshape(1, d))


def _mlstm_kernel(zq_ref, zk_ref, zv_ref, zo_ref, zg_ref, cw_ref, cb_ref, bg_ref, gn_ref,
                  conv0_ref, c0_ref, n0_ref, m0_ref,
                  out_ref, conv_ref, c_ref, n_ref, m_ref, ext_ref, u_ref, qk_ref, gates_ref,
                  *, c, group, single):
    j = pl.program_id(1)
    tail = CONV_W - 1

    c_src = c0_ref if single else c_ref

    @pl.when(j == 0)
    def _():
        if not single:
            c_ref[...] = c0_ref[...]
        n_ref[...] = n0_ref[...]
        m_ref[...] = m0_ref[...]
        for gi in range(group):
            ext_ref[gi, 0:SUBLANES, :] = jnp.zeros((SUBLANES, 2 * HW), F32)
            ext_ref[gi, SUBLANES - tail:SUBLANES, :] = conv0_ref[gi]

    ri = lax.broadcasted_iota(jnp.int32, (c, c), 0)
    ci = lax.broadcasted_iota(jnp.int32, (c, c), 1)
    eye = ri == ci
    tril = ri >= ci
    lane = lax.broadcasted_iota(jnp.int32, (c, LANES), 1)
    bg = bg_ref[...]

    def lsum(x):
        return jnp.broadcast_to(jnp.sum(x, axis=1, keepdims=True), (c, DH))

    def unit(gi, h):
        sl = slice(h * DH, (h + 1) * DH)
        gates = gates_ref[gi]
        i_col = lsum(jnp.where(lane == h, gates, 0.0))
        f_col = lsum(jnp.where(lane == HEADS + h, gates, 0.0))
        yield
        b_row = jnp.sum(jnp.where(ri <= ci, f_col[:, :c], 0.0), axis=0, keepdims=True)
        b_col = lsum(jnp.where(eye, b_row, 0.0))
        i_row = jnp.sum(jnp.where(eye, i_col[:, :c], 0.0), axis=0, keepdims=True)
        yield
        m_prev = m_ref[gi, h:h + 1, :]
        dmat = jnp.where(tril, b_col[:, :c] - b_row + i_row, NEG)
        inter = b_col + m_prev
        row_max = jnp.broadcast_to(jnp.max(dmat, axis=1, keepdims=True), (c, DH))
        qh = qk_ref[gi, :, sl]
        kh = qk_ref[gi, :, HW + h * DH:HW + (h + 1) * DH] * (DH ** -0.5)
        vh = zv_ref[gi, :, sl]
        c_h = c_src[gi, h]
        n_h = n_ref[gi, h:h + 1, :]
        s_raw = _bdot_nt(qh, kh)
        q_c = _bdot(qh, c_h)
        q_n = lsum(qh * n_h)
        yield
        m_t = jnp.maximum(inter, row_max)
        w_intra = jnp.exp(dmat - m_t[:, :c])
        w_inter = jnp.exp(inter - m_t)
        s = s_raw * w_intra
        s_v = _bdot(s, vh)
        s_sum = lsum(s)
        m_new = m_t[c - 1:c, :]
        b_last = b_col[c - 1:c, :]
        w_last = jnp.exp(b_last - b_col + i_col - m_new)
        decay = jnp.exp(b_last + m_prev - m_new)
        kw = w_last * kh
        kw_v = _bdot_tn(kw, vh)
        yield
        num = w_inter * q_c + s_v
        den = w_inter * q_n + s_sum
        hh = num / jnp.maximum(jnp.abs(den), jnp.exp(-m_t))
        c_ref[gi, h] = decay * c_h + kw_v
        n_ref[gi, h:h + 1, :] = decay * n_h + jnp.sum(kw, axis=0, keepdims=True)
        m_ref[gi, h:h + 1, :] = m_new
        hh = _sigmoid(zo_ref[gi, :, sl]) * hh
        out_ref[gi, :, sl] = _head_rms(hh) * gn_ref[:, sl]

    for gi in range(group):
        ext_ref[gi, SUBLANES:SUBLANES + c, 0:HW] = zq_ref[gi]
        ext_ref[gi, SUBLANES:SUBLANES + c, HW:2 * HW] = zk_ref[gi]
        assert CONV_W == 4
        x0 = ext_ref[gi, SUBLANES:SUBLANES + c, :]
        x2 = ext_ref[gi, SUBLANES - 2:SUBLANES - 2 + c, :]
        u_ref[gi, SUBLANES:SUBLANES + c, :] = cw_ref[2:3, :] * x0 + cw_ref[0:1, :] * x2
        u_ref[gi, SUBLANES - 1:SUBLANES, :] = (cw_ref[2:3, :] * ext_ref[gi, SUBLANES - 1:SUBLANES, :]
                                               + cw_ref[0:1, :] * ext_ref[gi, SUBLANES - 3:SUBLANES - 2, :])
        conv = (cb_ref[...] + cw_ref[3:4, :] * x0 + cw_ref[1:2, :] * x2
                + u_ref[gi, SUBLANES - 1:SUBLANES - 1 + c, :])
        ext_ref[gi, 0:SUBLANES, :] = ext_ref[gi, c:c + SUBLANES, :]
        conv_ref[gi] = ext_ref[gi, SUBLANES - tail:SUBLANES, :]
        qk_ref[gi] = conv * _sigmoid(conv)
        gb = zg_ref[gi] + bg
        gates_ref[gi] = jnp.where(lane < HEADS, gb, _log_sigmoid(gb))
    _round_robin([unit(gi, h) for gi in range(group) for h in range(HEADS)])


def _seq_group(b, c):
    rows = 512
    group = max(1, min(b, rows // c, 2 * SUBLANES))
    assert b % group == 0
    return group


def _mlstm(z_main, z_gate, conv_w, conv_b, b_gate, gn_a, conv0, c0, n0, m0, b, l):
    c = _chunk_len(l)
    nc = l // c
    grp = _seq_group(b, c)
    z3 = z_main.reshape(b, l, z_main.shape[-1])
    zspec = lambda col: pl.BlockSpec((grp, c, HW), lambda bi, j: (bi, j, col))
    st = lambda shape: pl.BlockSpec((grp,) + shape, lambda bi, j: (bi,) + (0,) * len(shape))
    m0b = jnp.broadcast_to(m0[:, :, None], (b, HEADS, DH))
    out, conv_new, c_new, n_new, m_new = pl.pallas_call(
        functools.partial(_mlstm_kernel, c=c, group=grp, single=nc == 1),
        grid=(b // grp, nc),
        in_specs=[zspec(0), zspec(1), zspec(2), zspec(3),
                  pl.BlockSpec((grp, c, LANES), lambda bi, j: (bi, j, 0)),
                  _full((CONV_W, 2 * HW)), _full((1, 2 * HW)), _full((1, LANES)), _full((1, HW)),
                  st((CONV_W - 1, 2 * HW)), st((HEADS, DH, DH)), st((HEADS, DH)), st((HEADS, DH))],
        out_specs=[pl.BlockSpec((grp, c, HW), lambda bi, j: (bi, j, 0)),
                   st((CONV_W - 1, 2 * HW)), st((HEADS, DH, DH)), st((HEADS, DH)), st((HEADS, DH))],
        out_shape=[jax.ShapeDtypeStruct((b, l, HW), F32),
                   jax.ShapeDtypeStruct((b, CONV_W - 1, 2 * HW), F32),
                   jax.ShapeDtypeStruct((b, HEADS, DH, DH), F32),
                   jax.ShapeDtypeStruct((b, HEADS, DH), F32),
                   jax.ShapeDtypeStruct((b, HEADS, DH), F32)],
        scratch_shapes=[pltpu.VMEM((grp, c + SUBLANES, 2 * HW), F32),
                        pltpu.VMEM((grp, c + SUBLANES, 2 * HW), F32),
                        pltpu.VMEM((grp, c, 2 * HW), F32), pltpu.VMEM((grp, c, LANES), F32)],
        compiler_params=_params(2),
        name="mlstm",
    )(z3, z3, z3, z3, z_gate.reshape(b, l, LANES), conv_w, conv_b.reshape(1, -1),
      jnp.pad(b_gate, (0, LANES - 2 * HEADS)).reshape(1, LANES), gn_a.reshape(1, -1),
      conv0, c0, n0, m0b)
    return out.reshape(b * l, HW), conv_new, c_new, n_new, m_new[:, :, 0]


def _rope_table_kernel(inv_ref, sign_ref, cos_ref, sin_ref, *, pos0, rows):
    i = pl.program_id(0)
    pos = (pos0 + i * rows + lax.broadcasted_iota(jnp.int32, (rows, LANES), 0)).astype(F32)
    ang = pos * inv_ref[...]
    cos_ref[...] = jnp.cos(ang)
    sin_ref[...] = jnp.sin(ang) * sign_ref[...]


def _rope_tables(l, pos0):
    half = DH // 2
    inv = ROPE_BASE ** (-jnp.arange(half, dtype=F32) / half)
    inv2 = jnp.concatenate([inv, inv]).reshape(1, DH)
    sign = jnp.concatenate([-jnp.ones((half,), F32), jnp.ones((half,), F32)]).reshape(1, DH)
    rows = min(l, 512)
    assert l % rows == 0
    return pl.pallas_call(
        functools.partial(_rope_table_kernel, pos0=pos0, rows=rows),
        grid=(l // rows,),
        in_specs=[_full((1, DH)), _full((1, DH))],
        out_specs=[pl.BlockSpec((rows, DH), lambda i: (i, 0))] * 2,
        out_shape=[jax.ShapeDtypeStruct((l, DH), F32)] * 2,
        compiler_params=_params(1),
        name="rope_table",
    )(inv2, sign)


def _ret_kernel(zq_ref, zk_ref, zv_ref, zg_ref, cos_ref, sin_ref, s0_ref, out_ref, s_ref,
                *, c, group, single):
    j = pl.program_id(1)
    s_src = s0_ref if single else s_ref

    if not single:
        @pl.when(j == 0)
        def _():
            s_ref[...] = s0_ref[...]

    cosf = cos_ref[...]
    sinf = sin_ref[...]
    ti = lax.broadcasted_iota(jnp.int32, (c, c), 0)
    si = lax.broadcasted_iota(jnp.int32, (c, c), 1)
    rel = jnp.maximum(ti - si, 0).astype(F32)
    tcol = lax.broadcasted_iota(jnp.int32, (c, 1), 0).astype(F32)

    def rope(x):
        return x * cosf + pltpu.roll(x, DH // 2, axis=1) * sinf

    def unit(gi, h, decay, inter, kdecay, cdecay):
        sl = slice(h * DH, (h + 1) * DH)
        qr = rope(zq_ref[gi, :, sl])
        yield
        kr = rope(zk_ref[gi, :, sl]) * (DH ** -0.5)
        yield
        vh = zv_ref[gi, :, sl]
        s_h = s_src[gi, h]
        qk = _bdot_nt(qr, kr)
        yield
        q_s = _bdot(qr, s_h)
        yield
        k_v = _bdot_tn(kr * kdecay, vh)
        yield
        o = q_s * inter + _bdot(qk * decay, vh)
        s_ref[gi, h] = cdecay * s_h + k_v
        yield
        gate = zg_ref[gi, :, sl]
        out_ref[gi, :, sl] = _head_rms(o) * (gate * _sigmoid(gate))

    units = []
    for h in range(HEADS):
        lg = math.log1p(-(2.0 ** (-5.0 - h)))
        decay = jnp.where(ti >= si, jnp.exp(rel * lg), 0.0)
        inter = jnp.exp((tcol + 1.0) * lg)
        kdecay = jnp.exp((c - 1.0 - tcol) * lg)
        cdecay = math.exp(c * lg)
        units += [unit(gi, h, decay, inter, kdecay, cdecay) for gi in range(group)]
    _round_robin(units)


def _retention(z_main, cos_t, sin_t, s0, b, l):
    c = _chunk_len(l)
    nc = l // c
    grp = _seq_group(b, c)
    z3 = z_main.reshape(b, l, z_main.shape[-1])
    zspec = lambda col: pl.BlockSpec((grp, c, HW), lambda bi, j: (bi, j, col))
    st = pl.BlockSpec((grp, HEADS, DH, DH), lambda bi, j: (bi, 0, 0, 0))
    tab = pl.BlockSpec((c, DH), lambda bi, j: (j, 0))
    out, s_new = pl.pallas_call(
        functools.partial(_ret_kernel, c=c, group=grp, single=nc == 1),
        grid=(b // grp, nc),
        in_specs=[zspec(4), zspec(5), zspec(6), zspec(7), tab, tab, st],
        out_specs=[pl.BlockSpec((grp, c, HW), lambda bi, j: (bi, j, 0)), st],
        out_shape=[jax.ShapeDtypeStruct((b, l, HW), F32),
                   jax.ShapeDtypeStruct((b, HEADS, DH, DH), F32)],
        compiler_params=_params(2),
        name="retention",
    )(z3, z3, z3, z3, cos_t, sin_t, s0)
    return out.reshape(b * l, HW), s_new


def _hgrn_kernel(zq_ref, zf_ref, zi_ref, zg_ref, lbl_ref, gn_ref, s0_ref, out_ref, s_ref,
                 kk_ref, bcum_ref, *, c, sc, layer, group, single):
    j = pl.program_id(1)
    s_src = s0_ref if single else s_ref

    if not single:
        @pl.when(j == 0)
        def _():
            s_ref[...] = s0_ref[...]

    lbl = lbl_ref[...]
    e = jnp.exp(lbl - jnp.max(lbl, axis=0, keepdims=True))
    sm = e / jnp.sum(e, axis=0, keepdims=True)
    cum = sm[0:1, :]
    for r in range(1, layer + 1):
        cum = cum + sm[r:r + 1, :]
    lb = cum - sm[0:1, :]

    oml = 1.0 - lb
    ri = lax.broadcasted_iota(jnp.int32, (c, c), 0)
    ci = lax.broadcasted_iota(jnp.int32, (c, c), 1)
    tril = jnp.where(ri >= ci, 1.0, 0.0).astype(BF16)
    row_s = lax.broadcasted_iota(jnp.int32, (sc, sc), 0)
    lane_s = lax.broadcasted_iota(jnp.int32, (sc, sc), 1)
    causal_col = jnp.where(row_s >= lane_s, lane_s, -1)
    keep = [causal_col == s for s in range(sc)]
    e_r = lax.broadcasted_iota(jnp.int32, (DH, DH), 0)
    e_c = lax.broadcasted_iota(jnp.int32, (DH, DH), 1)
    eye = e_r == e_c

    def unit(gi, h):
        sl = slice(h * DH, (h + 1) * DH)
        bh = bcum_ref[gi, :, sl]
        qh = zq_ref[gi, :, sl] * (DH ** -0.5)
        kh = kk_ref[gi, :, sl]
        vh = zi_ref[gi, :, sl]
        s_h = s_src[gi, h]
        b_last = bh[c - 1:c, :]
        o_inter = _bdot(qh * jnp.exp(bh), s_h)
        yield
        k_v = _bdot_tn(kh * jnp.exp(b_last - bh), vh)
        yield
        dec_col = jnp.sum(jnp.where(eye, jnp.exp(b_last), 0.0), axis=1, keepdims=True)
        yield
        s_ref[gi, h] = dec_col * s_h + k_v
        blocks = []
        for blk in range(c // sc):
            r0 = blk * sc
            b_i = bh[r0:r0 + sc]
            q_i = qh[r0:r0 + sc]
            k_i = kh[r0:r0 + sc]
            v_i = vh[r0:r0 + sc]
            att_prev = None
            if blk > 0:
                ref_row = bh[r0 - 1:r0, :]
                a_i = q_i * jnp.exp(b_i - ref_row)
                k_prev = kh[0:r0] * jnp.exp(ref_row - bh[0:r0])
                att_prev = _bdot_nt(a_i, k_prev)
                yield
            b2_i = b_i * LOG2_E
            c2_i = b2_i - jnp.log2(k_i)
            cols = [jnp.sum(q_i * jnp.exp2(b2_i - c2_i[s:s + 1, :]), axis=1, keepdims=True)
                    for s in range(sc)]
            yield
            att = jnp.zeros((sc, sc), F32)
            for s in range(sc):
                att = jnp.where(keep[s], cols[s], att)
            o_i = _bdot(att, v_i)
            if att_prev is not None:
                o_i = o_i + _bdot(att_prev, vh[0:r0])
            blocks.append(o_i)
        yield
        o = o_inter + (jnp.concatenate(blocks, axis=0) if len(blocks) > 1 else blocks[0])
        gate = zg_ref[gi, :, sl]
        out_ref[gi, :, sl] = _head_rms(o) * gn_ref[:, sl] * (gate * _sigmoid(gate))

    for gi in range(group):
        zf = zf_ref[gi]
        ez = jnp.exp(-jnp.abs(zf))
        big = 1.0 / (1.0 + ez)
        small = ez * big
        pos = zf >= 0.0
        logf = jnp.log(lb + oml * jnp.where(pos, big, small))
        kk_ref[gi] = oml * jnp.where(pos, small, big)

        p0 = logf.astype(BF16)
        r1 = logf - p0.astype(F32)
        p1 = r1.astype(BF16)
        p2 = (r1 - p1.astype(F32)).astype(BF16)
        bcum_ref[gi] = (jnp.dot(tril, p0, preferred_element_type=F32)
                        + jnp.dot(tril, p1, preferred_element_type=F32)
                        + jnp.dot(tril, p2, preferred_element_type=F32))
    _round_robin([unit(gi, h) for gi in range(group) for h in range(HEADS)])


def _hgrn(z_cd, lb_logits, gn_c, s0, b, l, layer):
    c = _chunk_len(l)
    sc = min(c, SUBLANES)
    nc = l // c
    grp = _seq_group(b, c)
    depth = lb_logits.shape[0]
    z3 = z_cd.reshape(b, l, z_cd.shape[-1])
    zspec = lambda col: pl.BlockSpec((grp, c, HW), lambda bi, j: (bi, j, col))
    st = pl.BlockSpec((grp, HEADS, DH, DH), lambda bi, j: (bi, 0, 0, 0))
    out, s_new = pl.pallas_call(
        functools.partial(_hgrn_kernel, c=c, sc=sc, layer=layer, group=grp, single=nc == 1),
        grid=(b // grp, nc),
        in_specs=[zspec(0), zspec(1), zspec(2), zspec(3), _full((depth, HW)), _full((1, HW)), st],
        out_specs=[pl.BlockSpec((grp, c, HW), lambda bi, j: (bi, j, 0)), st],
        out_shape=[jax.ShapeDtypeStruct((b, l, HW), F32),
                   jax.ShapeDtypeStruct((b, HEADS, DH, DH), F32)],
        scratch_shapes=[pltpu.VMEM((grp, c, HW), F32), pltpu.VMEM((grp, c, HW), F32)],
        compiler_params=_params(2),
        name="hgrn2",
    )(z3, z3, z3, z3, lb_logits, gn_c.reshape(1, -1), s0)
    return out.reshape(b * l, HW), s_new


def _s5_kernel(u_ref, are_ref, aim_ref, ldt_ref, bre_ref, bim_ref, cre_ref, cim_ref, d_ref,
               wglu_ref, bglu_ref, x0r_ref, x0i_ref,
               s_ref, xr_ref, xi_ref, ar_sc, ai_sc, bbr_sc, bbi_sc, bur_sc, bui_sc, *, ct):
    j = pl.program_id(1)
    ns = are_ref.shape[-1]
    nu = u_ref.shape[-1]
    hs = ns // 2
    hu = nu // 2
    bt = u_ref.shape[1]
    rows = ct * bt

    @pl.when((pl.program_id(0) == 0) & (j == 0))
    def _():
        a_re = are_ref[...]
        a_im = aim_ref[...]
        dt = jnp.exp(ldt_ref[...])
        mag = jnp.exp(dt * a_re)
        ar = mag * jnp.cos(dt * a_im)
        ai = mag * jnp.sin(dt * a_im)
        ar_sc[...] = jnp.broadcast_to(ar, (SUBLANES, ns))
        ai_sc[...] = jnp.broadcast_to(ai, (SUBLANES, ns))
        den = a_re * a_re + a_im * a_im
        nr = ar - 1.0
        zr = (nr * a_re + ai * a_im) / den
        zi = (ai * a_re - nr * a_im) / den
        for hg in range(2):
            us = slice(hg * hu, (hg + 1) * hu)
            ss = slice(hg * hs, (hg + 1) * hs)
            bbr_sc[us, :] = (zr[:, ss] * bre_ref[us, :] - zi[:, ss] * bim_ref[us, :]).astype(BF16)
            bbi_sc[us, :] = (zr[:, ss] * bim_ref[us, :] + zi[:, ss] * bre_ref[us, :]).astype(BF16)

    @pl.when(j == 0)
    def _():
        xr_ref[...] = x0r_ref[...]
        xi_ref[...] = x0i_ref[...]

    u = u_ref[...].reshape(rows, nu)
    ub = u.astype(BF16)
    for hg in range(2):
        us = slice(hg * hu, (hg + 1) * hu)
        ss = slice(hg * hs, (hg + 1) * hs)
        bur_sc[:, ss] = jnp.dot(ub[:, us], bbr_sc[us, :], preferred_element_type=F32)
        bui_sc[:, ss] = jnp.dot(ub[:, us], bbi_sc[us, :], preferred_element_type=F32)

    lane_chunk = 8 * LANES
    for lc in range(ns // lane_chunk):
        ls = slice(lc * lane_chunk, (lc + 1) * lane_chunk)
        ar = ar_sc[:, ls]
        ai = ai_sc[:, ls]
        for sg in range(bt // SUBLANES):
            srows = slice(sg * SUBLANES, (sg + 1) * SUBLANES)

            def step(t, carry, sg=sg, ls=ls, ar=ar, ai=ai):
                xr, xi = carry
                r0 = pl.multiple_of(t * bt + sg * SUBLANES, SUBLANES)
                nxr = ar * xr - ai * xi + bur_sc[pl.ds(r0, SUBLANES), ls]
                nxi = ar * xi + ai * xr + bui_sc[pl.ds(r0, SUBLANES), ls]
                bur_sc[pl.ds(r0, SUBLANES), ls] = nxr
                bui_sc[pl.ds(r0, SUBLANES), ls] = nxi
                return nxr, nxi

            xr, xi = lax.fori_loop(0, ct, step, (xr_ref[srows, ls], xi_ref[srows, ls]), unroll=SUBLANES)
            xr_ref[srows, ls] = xr
            xi_ref[srows, ls] = xi

    ys = []
    for hg in range(2):
        ss = slice(hg * hs, (hg + 1) * hs)
        ys.append(jnp.dot(bur_sc[:, ss].astype(BF16), cre_ref[ss, :], preferred_element_type=F32)
                  - jnp.dot(bui_sc[:, ss].astype(BF16), cim_ref[ss, :], preferred_element_type=F32))
    y = jnp.concatenate(ys, axis=1) + d_ref[...] * u
    a = 0.5 * y * (1.0 + jnp.tanh(math.sqrt(2.0 / math.pi) * (y + 0.044715 * (y * y * y))))
    s = a * _sigmoid(jnp.dot(a.astype(BF16), wglu_ref[...], preferred_element_type=F32) + bglu_ref[...])
    s_ref[...] = s.reshape(ct, bt, nu)


def _s5_block_diag(bmat, cmat):
    g, p, hgrp = bmat.shape
    gh = g // 2
    eye = jnp.eye(gh, dtype=F32)
    b4 = bmat.reshape(2, gh, p, hgrp)
    bc = jnp.einsum('agph,gk->aghkp', b4, eye).reshape(2 * gh * hgrp, gh * p)
    c4 = cmat.reshape(2, gh, hgrp, p)
    cc = jnp.einsum('aghp,gk->agpkh', c4, eye).reshape(2 * gh * p, gh * hgrp)
    return bc, cc


def _s5_operands(a_re, a_im, log_dt, b_re, b_im, c_re, c_im, d_skip, w_glu, b_glu):
    g, p = a_re.shape
    ns = g * p
    nu = d_skip.shape[-1]
    assert nu == g * S5_GROUP and (g // 2) * S5_GROUP == MXU_DIM
    bre_c, cre_c = _s5_block_diag(b_re, c_re)
    bim_c, cim_c = _s5_block_diag(b_im, c_im)
    ldt = jnp.broadcast_to(log_dt[:, None], (g, p)).reshape(1, ns)
    return (a_re.reshape(1, ns), a_im.reshape(1, ns), ldt, bre_c, bim_c, cre_c.astype(BF16),
            cim_c.astype(BF16), d_skip.reshape(1, nu), w_glu.astype(BF16), b_glu.reshape(1, nu))


def _s5(u_tm, operands, x0r, x0i):
    l, b, nu = u_tm.shape
    g, p = x0r.shape[1:]
    ns = g * p
    assert b % SUBLANES == 0
    ct = _chunk_len(l)
    nct = l // ct
    bt = min(b, max(SUBLANES, 512 // ct))
    assert b % bt == 0 and bt % SUBLANES == 0
    rows = ct * bt
    xspec = pl.BlockSpec((bt, ns), lambda bb, j: (bb, 0))
    s, xr, xi = pl.pallas_call(
        functools.partial(_s5_kernel, ct=ct),
        grid=(b // bt, nct),
        in_specs=[pl.BlockSpec((ct, bt, nu), lambda bb, j: (j, bb, 0)),
                  _full((1, ns)), _full((1, ns)), _full((1, ns)),
                  _full((nu, ns // 2)), _full((nu, ns // 2)),
                  _full((ns, nu // 2)), _full((ns, nu // 2)),
                  _full((1, nu)), _full((nu, nu)), _full((1, nu)), xspec, xspec],
        out_specs=[pl.BlockSpec((ct, bt, nu), lambda bb, j: (j, bb, 0)), xspec, xspec],
        out_shape=[jax.ShapeDtypeStruct((l, b, nu), F32),
                   jax.ShapeDtypeStruct((b, ns), F32), jax.ShapeDtypeStruct((b, ns), F32)],
        scratch_shapes=[pltpu.VMEM((SUBLANES, ns), F32), pltpu.VMEM((SUBLANES, ns), F32),
                        pltpu.VMEM((nu, ns // 2), BF16), pltpu.VMEM((nu, ns // 2), BF16),
                        pltpu.VMEM((rows, ns), F32), pltpu.VMEM((rows, ns), F32)],
        compiler_params=_params(2, VMEM_LIMIT),
        name="s5",
    )(u_tm, *operands, x0r.reshape(b, ns), x0i.reshape(b, ns))
    return s, xr.reshape(b, g, p), xi.reshape(b, g, p)


def _trunk(groups, prm):
    d = groups[0]['x'].shape[-1]
    tm = 512
    info = []
    for grp in groups:
        b, l, _ = grp['x'].shape
        assert l >= CONV_W - 1 and (b * l) % tm == 0
        info.append(dict(b=b, l=l, t=b * l, direct_tm=b == SUBLANES and l % tm == 0))
    hs = [grp['x'].reshape(-1, d) for grp in groups]
    pps = [grp['p'].reshape(grp['p'].shape[0], -1, grp['p'].shape[-1]) for grp in groups]

    zs = _inproj(hs, prm['norm_mix'][0], prm['w_ab'], (8 * HW, LANES), tm, [None] * len(groups))
    ab_new, post_in = [], []
    for grp, inf, h, pp, (z_main, z_gate) in zip(groups, info, hs, pps, zs):
        b, l = inf['b'], inf['l']
        conv0, c0, n0, m0, ret0 = grp['ab_state']
        hm, conv_new, c_new, n_new, m_new = _mlstm(z_main, z_gate, prm['conv_w_ab'][0], prm['conv_b_ab'][0],
                                         prm['b_gate_ab'][0], prm['gn_a'][0], conv0[0], c0[0], n0[0],
                                         m0[0], b, l)
        cos_t, sin_t = _rope_tables(l, grp['pos0'])
        hr, ret_new = _retention(z_main, cos_t, sin_t, ret0[0], b, l)
        ab_new.append((conv_new[None], c_new[None], n_new[None], m_new[None], ret_new[None]))
        post_in.append((h, hm, hr, pp, None))
    hs = _post(post_in, prm['w_out_ab'], prm['norm_ff'][0], prm['w_ff1'], prm['w_ff2'],
               prm['norm_ple'][0], prm['w_ple_gate'], prm['w_ple_proj'], prm['norm_final'],
               layer=0, final=False, tm=tm)

    tml = [(inf['b'], inf['l']) if inf['direct_tm'] else None for inf in info]
    zs = _inproj(hs, prm['norm_mix'][1], prm['w_cd'], (4 * HW, HW), tm, tml)
    cd_new, post_in = [], []
    for grp, inf, h, pp, (z_cd, su) in zip(groups, info, hs, pps, zs):
        b, l, t = inf['b'], inf['l'], inf['t']
        hg0, x0r, x0i = grp['cd_state']
        o, hg_new = _hgrn(z_cd, prm['lb_logits'], prm['gn_c'][0], hg0[0], b, l, layer=1)
        u_tm = (su.reshape(l, b, HW) if inf['direct_tm']
                else jnp.transpose(su.reshape(b, l, HW), (1, 0, 2)))
        s_tm, xr, xi = _s5(u_tm, prm['s5_operands'], x0r[0], x0i[0])
        s_in = (s_tm.reshape(l, b * HW) if inf['direct_tm']
                else jnp.transpose(s_tm, (1, 0, 2)).reshape(t, HW))
        cd_new.append((hg_new[None], xr[None], xi[None]))
        post_in.append((h, o, s_in, pp, (b, l) if inf['direct_tm'] else None))
    ys = _post(post_in, prm['w_out_cd'], prm['norm_ff'][1], prm['w_ff1'], prm['w_ff2'],
               prm['norm_ple'][1], prm['w_ple_gate'], prm['w_ple_proj'], prm['norm_final'],
               layer=1, final=True, tm=tm)
    ys = [y.reshape(grp['x'].shape) for y, grp in zip(ys, groups)]
    return ys, ab_new, cd_new


def kernel(x_prompt, x_sample, state_mlstm_conv, state_mlstm_C, state_mlstm_n, state_mlstm_m, state_ret, state_hgrn, state_s5_re, state_s5_im, p_prompt, p_sample, norm_mix, norm_ff, norm_ple, norm_final, w_in_ab, b_gate_ab, conv_w_ab, conv_b_ab, gn_a, w_out_ab, w_in_cd, lb_logits, gn_c, s5_A_re, s5_A_im, s5_log_dt, s5_B_re, s5_B_im, s5_C_re, s5_C_im, s5_D, w_glu, b_glu, w_out_cd, w_ff1, w_ff2, w_ple_proj, w_ple_gate):
    assert norm_mix.shape[0] == 2, "two layers: (mLSTM || retention), (HGRN2 || S5)"
    w_ab = w_in_ab[0]
    gate0 = 4 * HW
    w_ab = (w_ab[:, :gate0].astype(BF16), w_ab[:, gate0 + 2 * HEADS:].astype(BF16),
            jnp.pad(w_ab[:, gate0:gate0 + 2 * HEADS], ((0, 0), (0, LANES - 2 * HEADS))).astype(BF16))
    prm = dict(norm_mix=norm_mix, norm_ff=norm_ff, norm_ple=norm_ple, norm_final=norm_final,
               w_ab=w_ab, b_gate_ab=b_gate_ab, conv_w_ab=conv_w_ab, conv_b_ab=conv_b_ab,
               gn_a=gn_a, w_out_ab=w_out_ab[0].astype(BF16), w_cd=(w_in_cd[0].astype(BF16),),
               lb_logits=lb_logits, gn_c=gn_c,
               s5_operands=_s5_operands(s5_A_re[0], s5_A_im[0], s5_log_dt[0], s5_B_re[0], s5_B_im[0],
                                        s5_C_re[0], s5_C_im[0], s5_D[0], w_glu[0], b_glu[0]),
               w_out_cd=w_out_cd[0].astype(BF16), w_ff1=w_ff1.astype(BF16), w_ff2=w_ff2.astype(BF16),
               w_ple_proj=w_ple_proj.astype(BF16), w_ple_gate=w_ple_gate.astype(BF16))

    bp = x_prompt.shape[0]
    z = lambda *s: jnp.zeros(s, F32)
    zero_ab = (z(1, bp, CONV_W - 1, 2 * HW), z(1, bp, HEADS, DH, DH), z(1, bp, HEADS, DH),
               z(1, bp, HEADS), z(1, bp, HEADS, DH, DH))
    zero_cd = (z(1, bp, HEADS, DH, DH),) + (z(*((1, bp) + s5_A_re.shape[1:])),) * 2
    groups = [dict(x=x_prompt, p=p_prompt, pos0=0, ab_state=zero_ab, cd_state=zero_cd),
              dict(x=x_sample, p=p_sample, pos0=PAST_LEN,
                   ab_state=(state_mlstm_conv, state_mlstm_C, state_mlstm_n, state_mlstm_m, state_ret),
                   cd_state=(state_hgrn, state_s5_re, state_s5_im))]
    (y_p, y_s), (ab_p, ab_s), (cd_p, cd_s) = _trunk(groups, prm)
    return (y_p, y_s,
            ab_p[0], ab_s[0], ab_p[1], ab_s[1], ab_p[2], ab_s[2], ab_p[3], ab_s[3], ab_p[4], ab_s[4],
            cd_p[0], cd_s[0], cd_p[1], cd_s[1], cd_p[2], cd_s[2])
```
